```python
import math
import jax, jax.numpy as jnp
from jax import lax
import numpy as np

D_MODEL = 1024
BATCH = 4
SEQ = 8192
DEPTH = 2

MEM_LEN = 256
HEAD_DIM = 64
SELF_WIDTH = 3 * D_MODEL // 4
N_MOBA_HEADS = SELF_WIDTH // HEAD_DIM
N_DIFF_HEADS = SELF_WIDTH // (2 * HEAD_DIM)
N_MEM_HEADS = 4
MEM_WIDTH = N_MEM_HEADS * HEAD_DIM
A_IN = 3 * SELF_WIDTH + MEM_WIDTH
B_IN = SELF_WIDTH + MEM_WIDTH
D_FF = 4 * D_MODEL
MOBA_BLOCK = 256
MOBA_TOPK = 3
Q_BLOCK = 128
MOBA_Q_CHUNK = 32
ROPE_THETA = 10000.0
EPS = 1e-6
NEG = -1e30
N_A_LAYERS = DEPTH // 2
N_B_LAYERS = DEPTH - N_A_LAYERS

kernel_name = "yoco_diffattn_moba_hybrid"


def _rmsnorm(x, g):
    xf = x.astype(jnp.float32)
    y = xf * lax.rsqrt(jnp.mean(xf * xf, axis=-1, keepdims=True) + EPS)
    return (y * g.astype(jnp.float32)).astype(x.dtype)


def _rope_tables(n):
    half = HEAD_DIM // 2
    inv = 1.0 / (ROPE_THETA ** (jnp.arange(half, dtype=jnp.float32) / half))
    ang = jnp.arange(n, dtype=jnp.float32)[:, None] * inv[None, :]
    return jnp.cos(ang), jnp.sin(ang)


def _apply_rope(x, cos, sin):
    half = HEAD_DIM // 2
    shp = (x.shape[1],) + (1,) * (x.ndim - 3) + (half,)
    c = cos.reshape(shp)
    s = sin.reshape(shp)
    xf = x.astype(jnp.float32)
    x1, x2 = xf[..., :half], xf[..., half:]
    return jnp.concatenate([x1 * c - x2 * s, x2 * c + x1 * s], axis=-1).astype(x.dtype)


def _sq_relu_mlp(h, w_up, w_down):
    u = jax.nn.relu(h @ w_up)
    return (u * u) @ w_down


def _memory_attention(qm, mem, mem_norm, w_mem_kv):
    B, S = qm.shape[:2]
    M = mem.shape[1]
    kv = _rmsnorm(mem, mem_norm) @ w_mem_kv
    mk = kv[..., :MEM_WIDTH].reshape(B, M, N_MEM_HEADS, HEAD_DIM).astype(jnp.float32)
    mv = kv[..., MEM_WIDTH:].reshape(B, M, N_MEM_HEADS, HEAD_DIM).astype(jnp.float32)
    s = jnp.einsum('bshd,bmhd->bhsm', qm.astype(jnp.float32), mk) * (HEAD_DIM ** -0.5)
    p = jax.nn.softmax(s, axis=-1)
    o = jnp.einsum('bhsm,bmhd->bshd', p, mv)
    return o.reshape(B, S, MEM_WIDTH).astype(qm.dtype)


def _diff_attention(q, k, v, lam, lam_init, subln):
    B, S, H, _, d = q.shape
    nq = S // Q_BLOCK
    kf = k.astype(jnp.float32)
    vf = v.astype(jnp.float32)
    qb = q.astype(jnp.float32).reshape(B, nq, Q_BLOCK, H, 2, d).transpose(1, 0, 2, 3, 4, 5)
    kpos = jnp.arange(S)
    scale = d ** -0.5

    def block(args):
        qi, i = args
        s = jnp.einsum('bqhcd,bkhcd->bhcqk', qi, kf) * scale
        qpos = i * Q_BLOCK + jnp.arange(Q_BLOCK)
        s = jnp.where(kpos[None, :] <= qpos[:, None], s, NEG)
        p = jax.nn.softmax(s, axis=-1)
        w = p[:, :, 0] - lam * p[:, :, 1]
        return jnp.einsum('bhqk,bkhe->bqhe', w, vf)

    o = lax.map(block, (qb, jnp.arange(nq)))
    o = o.transpose(1, 0, 2, 3, 4).reshape(B, S, H, 2 * d)
    o = _rmsnorm(o, subln) * (1.0 - lam_init)
    return o.reshape(B, S, H * 2 * d).astype(v.dtype)


def _moba_attention(q, kbh, vbh, kmean):
    B, S, H, d = q.shape
    NB = kbh.shape[2]
    ksel = min(MOBA_TOPK, NB)
    C = MOBA_Q_CHUNK
    nc = S // C
    qc = q.astype(jnp.float32).transpose(0, 2, 1, 3).reshape(B, H, nc, C, d).transpose(2, 0, 1, 3, 4)
    b_ix = jnp.arange(B)[:, None, None, None]
    h_ix = jnp.arange(H)[None, :, None, None]
    scale = d ** -0.5

    def chunk(args):
        qi, i = args
        qpos = i * C + jnp.arange(C)
        cur = (i * C) // MOBA_BLOCK
        gate = jnp.einsum('bhqd,bhnd->bhqn', qi, kmean)
        gate = jnp.where(jnp.arange(NB) < cur, gate, NEG)
        _, idx = lax.top_k(gate, ksel)
        valid = idx < cur
        k_sel = kbh[b_ix, h_ix, idx]
        v_sel = vbh[b_ix, h_ix, idx]
        s_sel = jnp.einsum('bhqd,bhqnkd->bhqnk', qi, k_sel) * scale
        s_sel = jnp.where(valid[..., None], s_sel, NEG).reshape(B, H, C, ksel * MOBA_BLOCK)
        k_own = lax.dynamic_index_in_dim(kbh, cur, axis=2, keepdims=False)
        v_own = lax.dynamic_index_in_dim(vbh, cur, axis=2, keepdims=False)
        s_own = jnp.einsum('bhqd,bhkd->bhqk', qi, k_own) * scale
        kpos = cur * MOBA_BLOCK + jnp.arange(MOBA_BLOCK)
        s_own = jnp.where(kpos[None, :] <= qpos[:, None], s_own, NEG)
        p = jax.nn.softmax(jnp.concatenate([s_sel, s_own], axis=-1), axis=-1)
        p_sel = p[..., :ksel * MOBA_BLOCK].reshape(B, H, C, ksel, MOBA_BLOCK)
        p_own = p[..., ksel * MOBA_BLOCK:]
        return (jnp.einsum('bhqnk,bhqnkd->bhqd', p_sel, v_sel)
                + jnp.einsum('bhqk,bhkd->bhqd', p_own, v_own))

    o = lax.map(chunk, (qc, jnp.arange(nc)))
    return o.transpose(1, 0, 3, 2, 4).reshape(B, S, H * d).astype(q.dtype)


def setup_inputs(seed: int = 0) -> dict:
    key = jax.random.key(seed)
    ks = iter(jax.random.split(key, 40))

    def nrm(shape, scale):
        return jax.random.normal(next(ks), shape, jnp.float32) * scale

    def gain(shape):
        return 1.0 + nrm(shape, 0.02)

    NA, NBL, D = N_A_LAYERS, N_B_LAYERS, D_MODEL
    return {
        "x": nrm((BATCH, SEQ, D), 1.0),
        "mem": nrm((BATCH, MEM_LEN, D), 1.0),
        "a_norm_attn": gain((NA, D)),
        "a_w_in": nrm((NA, D, A_IN), D ** -0.5),
        "a_lambda": nrm((NA, 4, HEAD_DIM), 0.1),
        "a_subln": gain((NA, 2 * HEAD_DIM)),
        "a_mem_norm": gain((NA, D)),
        "a_w_mem_kv": nrm((NA, D, 2 * MEM_WIDTH), D ** -0.5),
        "a_w_out": nrm((NA, D, D), D ** -0.5),
        "a_norm_mlp": gain((NA, D)),
        "a_w_up": nrm((NA, D, D_FF), D ** -0.5),
        "a_w_down": nrm((NA, D_FF, D), D_FF ** -0.5),
        "kv_norm": gain((D,)),
        "w_kv": nrm((D, 2 * SELF_WIDTH), D ** -0.5),
        "b_norm_attn": gain((NBL, D)),
        "b_w_in": nrm((NBL, D, B_IN), D ** -0.5),
        "b_mem_norm": gain((NBL, D)),
        "b_w_mem_kv": nrm((NBL, D, 2 * MEM_WIDTH), D ** -0.5),
        "b_w_out": nrm((NBL, D, D), D ** -0.5),
        "b_norm_mlp": gain((NBL, D)),
        "b_w_up": nrm((NBL, D, D_FF), D ** -0.5),
        "b_w_down": nrm((NBL, D_FF, D), D_FF ** -0.5),
        "final_norm": gain((D,)),
    }


def reference(x, mem, a_norm_attn, a_w_in, a_lambda, a_subln, a_mem_norm, a_w_mem_kv,
              a_w_out, a_norm_mlp, a_w_up, a_w_down, kv_norm, w_kv, b_norm_attn, b_w_in,
              b_mem_norm, b_w_mem_kv, b_w_out, b_norm_mlp, b_w_up, b_w_down, final_norm):
    B, S, _ = x.shape
    cos, sin = _rope_tables(S)
    kbh = vbh = kmean = None
    for layer in range(DEPTH):
        if layer < N_A_LAYERS:
            i = layer
            h = _rmsnorm(x, a_norm_attn[i])
            z = h @ a_w_in[i]
            w = SELF_WIDTH
            q = z[..., :w].reshape(B, S, N_DIFF_HEADS, 2, HEAD_DIM)
            k = z[..., w:2 * w].reshape(B, S, N_DIFF_HEADS, 2, HEAD_DIM)
            v = z[..., 2 * w:3 * w].reshape(B, S, N_DIFF_HEADS, 2 * HEAD_DIM)
            qm = z[..., 3 * w:].reshape(B, S, N_MEM_HEADS, HEAD_DIM)
            q = _apply_rope(q, cos, sin)
            k = _apply_rope(k, cos, sin)
            lam_init = 0.8 - 0.6 * math.exp(-0.3 * layer)
            lp = a_lambda[i].astype(jnp.float32)
            lam = jnp.exp(jnp.sum(lp[0] * lp[1])) - jnp.exp(jnp.sum(lp[2] * lp[3])) + lam_init
            o_self = _diff_attention(q, k, v, lam, lam_init, a_subln[i])
            o_mem = _memory_attention(qm, mem, a_mem_norm[i], a_w_mem_kv[i])
            x = x + jnp.concatenate([o_self, o_mem], axis=-1) @ a_w_out[i]
            x = x + _sq_relu_mlp(_rmsnorm(x, a_norm_mlp[i]), a_w_up[i], a_w_down[i])
        else:
            j = layer - N_A_LAYERS
            if j == 0:
                hk = _rmsnorm(x, kv_norm)
                kv = hk @ w_kv
                ks_ = _apply_rope(kv[..., :SELF_WIDTH].reshape(B, S, N_MOBA_HEADS, HEAD_DIM), cos, sin)
                vs_ = kv[..., SELF_WIDTH:].reshape(B, S, N_MOBA_HEADS, HEAD_DIM)
                n_blk = -(-S // MOBA_BLOCK)
                pad = n_blk * MOBA_BLOCK - S
                padw = ((0, 0), (0, pad), (0, 0), (0, 0))
                kbh = jnp.pad(ks_.astype(jnp.float32), padw).reshape(
                    B, n_blk, MOBA_BLOCK, N_MOBA_HEADS, HEAD_DIM).transpose(0, 3, 1, 2, 4)
                vbh = jnp.pad(vs_.astype(jnp.float32), padw).reshape(
                    B, n_blk, MOBA_BLOCK, N_MOBA_HEADS, HEAD_DIM).transpose(0, 3, 1, 2, 4)
                kmean = jnp.mean(kbh, axis=3)
            h = _rmsnorm(x, b_norm_attn[j])
            z = h @ b_w_in[j]
            q = _apply_rope(z[..., :SELF_WIDTH].reshape(B, S, N_MOBA_HEADS, HEAD_DIM), cos, sin)
            qm = z[..., SELF_WIDTH:].reshape(B, S, N_MEM_HEADS, HEAD_DIM)
            o_self = _moba_attention(q, kbh, vbh, kmean).astype(x.dtype)
            o_mem = _memory_attention(qm, mem, b_mem_norm[j], b_w_mem_kv[j])
            x = x + jnp.concatenate([o_self, o_mem], axis=-1) @ b_w_out[j]
            x = x + _sq_relu_mlp(_rmsnorm(x, b_norm_mlp[j]), b_w_up[j], b_w_down[j])
    return _rmsnorm(x, final_norm)
```

```python
import functools
import math

import jax
import jax.numpy as jnp
from jax import lax
from jax.experimental import pallas as pl
from jax.experimental.pallas import tpu as pltpu

D_MODEL = 1024
HEAD_DIM = 64
SELF_WIDTH = 768
MEM_WIDTH = 256
N_MEM_HEADS = 4
D_FF = 4096
MOBA_BLOCK = 256
MOBA_TOPK = 3
ROPE_THETA = 10000.0
EPS = 1e-6
NEG = -1e30

LANES = 128
N_SELF_GROUPS = SELF_WIDTH // LANES
QK_SCALE = HEAD_DIM ** -0.5

F32 = jnp.float32
BF16 = jnp.bfloat16

_VMEM_LIMIT = 56 * 1024 * 1024


def _params(*sem):
    return pltpu.CompilerParams(dimension_semantics=sem, vmem_limit_bytes=_VMEM_LIMIT)


def _proj_kernel(x_ref, g_ref, w_ref, cos_ref, sin_ref, o_ref, *km_refs, n_rope, scaled, col_chunk):
    tm = x_ref.shape[0]
    n_out = w_ref.shape[1]
    x = x_ref[...]
    ms = jnp.mean(x * x, axis=-1, keepdims=True)
    h = (x * lax.rsqrt(ms + EPS) * g_ref[...]).astype(BF16)
    lane = lax.broadcasted_iota(jnp.int32, (tm, LANES), 1)
    first_half = (lane & (HEAD_DIM - 1)) < HEAD_DIM // 2
    if n_rope:
        cos = cos_ref[...]
        sin = sin_ref[...]
    for c0 in range(0, n_out, col_chunk):
        z = jnp.dot(h, w_ref[:, c0:c0 + col_chunk], preferred_element_type=F32)
        for jj in range(col_chunk // LANES):
            j = c0 // LANES + jj
            blk = z[:, jj * LANES:(jj + 1) * LANES]
            if j < n_rope:
                swap = jnp.where(first_half, pltpu.roll(blk, LANES - 32, 1), pltpu.roll(blk, 32, 1))
                blk = blk * cos + swap * sin
                if km_refs:
                    km_refs[0][0, :, j * LANES:(j + 1) * LANES] = jnp.mean(
                        blk.reshape(tm // MOBA_BLOCK, MOBA_BLOCK, LANES), axis=1)
            if j in scaled:
                blk = blk * QK_SCALE
            o_ref[:, j * LANES:(j + 1) * LANES] = blk.astype(o_ref.dtype)


def _proj(x2d, g, w, cos_t, sin_t, *, n_rope, scaled, with_kmean, seq, tm, name):
    t, d = x2d.shape
    n_out = w.shape[1]
    n_pos_blocks = seq // tm
    col_chunk = 512 if n_out % 512 == 0 else 256
    out_shape = [jax.ShapeDtypeStruct((t, n_out), BF16)]
    out_specs = [pl.BlockSpec((tm, n_out), lambda i: (i, 0))]
    if with_kmean:
        out_shape.append(jax.ShapeDtypeStruct((t // tm, tm // MOBA_BLOCK, n_rope * LANES), F32))
        out_specs.append(pl.BlockSpec((1, tm // MOBA_BLOCK, n_rope * LANES), lambda i: (i, 0, 0)))
    kern = functools.partial(_proj_kernel, n_rope=n_rope, scaled=frozenset(scaled), col_chunk=col_chunk)
    return pl.pallas_call(
        kern,
        out_shape=out_shape,
        grid=(t // tm,),
        in_specs=[
            pl.BlockSpec((tm, d), lambda i: (i, 0)),
            pl.BlockSpec((1, d), lambda i: (0, 0)),
            pl.BlockSpec((d, n_out), lambda i: (0, 0)),
            pl.BlockSpec((tm, LANES), lambda i: (i % n_pos_blocks, 0)),
            pl.BlockSpec((tm, LANES), lambda i: (i % n_pos_blocks, 0)),
        ],
        out_specs=out_specs,
        compiler_params=_params("parallel"),
        name=name,
    )(x2d, g.reshape(1, d), w, cos_t, sin_t)


def _stack_maps(q):
    tq = q.shape[0]
    lane = lax.broadcasted_iota(jnp.int32, (tq, LANES), 1)
    zero = jnp.zeros_like(q)
    return jnp.concatenate(
        [jnp.where(lane < HEAD_DIM, q, zero), jnp.where(lane >= HEAD_DIM, q, zero)], axis=0)


def _online_softmax_step(s, v, m_scr, l_scr, acc_scr):
    tk = s.shape[1]
    m_prev = m_scr[...]
    m_next = jnp.maximum(m_prev, jnp.max(s, axis=1, keepdims=True))
    alpha = jnp.exp(m_prev - m_next)
    p = jnp.exp(s - jnp.tile(m_next, (1, tk // LANES)))
    l_scr[...] = alpha * l_scr[...] + jnp.sum(p, axis=1, keepdims=True)
    acc_scr[...] = alpha * acc_scr[...] + jnp.dot(p.astype(BF16), v, preferred_element_type=F32)
    m_scr[...] = m_next


def _causal_mask(s, tq):
    r = lax.broadcasted_iota(jnp.int32, s.shape, 0) & (tq - 1)
    c = lax.broadcasted_iota(jnp.int32, s.shape, 1)
    return jnp.where(c <= r, s, NEG)


_NT = (((1,), (1,)), ((), ()))


def _diff_attn_kernel(lam_ref, g_ref, q_ref, k_ref, v_ref, o_ref, m_scr, l_scr, acc_scr, *, lam_init):
    tq = q_ref.shape[0]
    qi = pl.program_id(2)
    qs = _stack_maps(q_ref[...])
    m_scr[...] = jnp.full(m_scr.shape, NEG, F32)
    l_scr[...] = jnp.zeros(l_scr.shape, F32)
    acc_scr[...] = jnp.zeros(acc_scr.shape, F32)

    def chunk(j, causal):
        start = pl.multiple_of(j * tq, tq)
        k = k_ref[pl.ds(start, tq), :]
        v = v_ref[pl.ds(start, tq), :]
        s = lax.dot_general(qs, k, _NT, preferred_element_type=F32)
        if causal:
            s = _causal_mask(s, tq)
        _online_softmax_step(s, v, m_scr, l_scr, acc_scr)

    def body(j, carry):
        chunk(j, False)
        return carry

    lax.fori_loop(0, qi, body, 0)
    chunk(qi, True)

    o = acc_scr[...] / l_scr[...]
    lp = lam_ref[...]
    lam = (jnp.exp(jnp.sum(lp[0:1] * lp[1:2], axis=1, keepdims=True))
           - jnp.exp(jnp.sum(lp[2:3] * lp[3:4], axis=1, keepdims=True)) + lam_init)
    od = o[:tq] - lam * o[tq:]
    ms = jnp.mean(od * od, axis=-1, keepdims=True)
    y = od * lax.rsqrt(ms + EPS) * g_ref[...]
    o_ref[...] = (y * (1.0 - lam_init)).astype(o_ref.dtype)


def _diff_attention(z, lam_p, subln, *, batch, seq, tq, lam_init):
    t = z.shape[0]
    nq = seq // tq
    kern = functools.partial(_diff_attn_kernel, lam_init=lam_init)
    return pl.pallas_call(
        kern,
        out_shape=jax.ShapeDtypeStruct((t, SELF_WIDTH), BF16),
        grid=(batch, N_SELF_GROUPS, nq),
        in_specs=[
            pl.BlockSpec((4, HEAD_DIM), lambda b, h, i: (0, 0)),
            pl.BlockSpec((1, LANES), lambda b, h, i: (0, 0)),
            pl.BlockSpec((tq, LANES), lambda b, h, i: (b * nq + i, h)),
            pl.BlockSpec((seq, LANES), lambda b, h, i: (b, N_SELF_GROUPS + h)),
            pl.BlockSpec((seq, LANES), lambda b, h, i: (b, 2 * N_SELF_GROUPS + h)),
        ],
        out_specs=pl.BlockSpec((tq, LANES), lambda b, h, i: (b * nq + i, h)),
        scratch_shapes=[
            pltpu.VMEM((2 * tq, LANES), F32),
            pltpu.VMEM((2 * tq, LANES), F32),
            pltpu.VMEM((2 * tq, LANES), F32),
        ],
        compiler_params=_params("parallel", "parallel", "arbitrary"),
        name="diff_attn",
    )(lam_p, subln.reshape(1, LANES), z, z, z)


def _moba_attn_kernel(q_ref, k_ref, v_ref, km_ref, o_ref, m_scr, l_scr, acc_scr):
    tq = q_ref.shape[0]
    qi = pl.program_id(2)
    qs = _stack_maps(q_ref[...])
    m_scr[...] = jnp.full(m_scr.shape, NEG, F32)
    l_scr[...] = jnp.zeros(l_scr.shape, F32)
    acc_scr[...] = jnp.zeros(acc_scr.shape, F32)

    km = km_ref[...]
    km_pad = jnp.concatenate([km, jnp.zeros((LANES - km.shape[0], LANES), F32)], axis=0)
    gate = lax.dot_general(qs.astype(F32), km_pad, _NT, preferred_element_type=F32,
                           precision=lax.Precision.HIGHEST)
    blk = lax.broadcasted_iota(jnp.int32, gate.shape, 1)
    past = blk < qi
    gm = jnp.where(past, gate, NEG)
    sel = jnp.zeros(gate.shape, F32)
    for _ in range(MOBA_TOPK):
        mx = jnp.max(gm, axis=1, keepdims=True)
        first = jnp.min(jnp.where(gm == mx, blk, LANES), axis=1, keepdims=True)
        pick = blk == first
        sel = jnp.where(pick, 1.0, sel)
        gm = jnp.where(pick, -jnp.inf, gm)
    bias = jnp.where(past, jnp.where(sel > 0.5, 0.0, NEG), NEG).astype(BF16)
    qs_aug = jnp.concatenate([qs, bias], axis=1)

    start = pl.multiple_of(qi * tq, tq)
    s_own = lax.dot_general(qs, k_ref[pl.ds(start, tq), :], _NT, preferred_element_type=F32)
    _online_softmax_step(_causal_mask(s_own, tq), v_ref[pl.ds(start, tq), :], m_scr, l_scr, acc_scr)

    def body(j, carry):
        st = pl.multiple_of(j * tq, tq)
        k = k_ref[pl.ds(st, tq), :]
        v = v_ref[pl.ds(st, tq), :]
        kl = lax.broadcasted_iota(jnp.int32, k.shape, 1)
        hot = jnp.where(kl == j, 1.0, 0.0).astype(BF16)
        k_aug = jnp.concatenate([k, hot], axis=1)
        s = lax.dot_general(qs_aug, k_aug, _NT, preferred_element_type=F32)
        _online_softmax_step(s, v, m_scr, l_scr, acc_scr)
        return carry

    lax.fori_loop(0, qi, body, 0)

    o = acc_scr[...] / l_scr[...]
    lane = lax.broadcasted_iota(jnp.int32, (tq, LANES), 1)
    o_ref[...] = jnp.where(lane < HEAD_DIM, o[:tq], o[tq:]).astype(o_ref.dtype)


def _moba_attention(zq, zkv, kmean, *, batch, seq):
    t = zq.shape[0]
    tq = MOBA_BLOCK
    nq = seq // tq
    return pl.pallas_call(
        _moba_attn_kernel,
        out_shape=jax.ShapeDtypeStruct((t, SELF_WIDTH), BF16),
        grid=(batch, N_SELF_GROUPS, nq),
        in_specs=[
            pl.BlockSpec((tq, LANES), lambda b, h, i: (b * nq + i, h)),
            pl.BlockSpec((seq, LANES), lambda b, h, i: (b, h)),
            pl.BlockSpec((seq, LANES), lambda b, h, i: (b, N_SELF_GROUPS + h)),
            pl.BlockSpec((None, nq, LANES), lambda b, h, i: (b, 0, h)),
        ],
        out_specs=pl.BlockSpec((tq, LANES), lambda b, h, i: (b * nq + i, h)),
        scratch_shapes=[
            pltpu.VMEM((2 * tq, LANES), F32),
            pltpu.VMEM((2 * tq, LANES), F32),
            pltpu.VMEM((2 * tq, LANES), F32),
        ],
        compiler_params=_params("parallel", "parallel", "arbitrary"),
        name="moba_attn",
    )(zq, zkv, zkv, kmean)


def _attn_out_kernel(x_ref, os_ref, qm_ref, mkv_ref, w_ref, o_ref):
    tm = x_ref.shape[0]
    qm = qm_ref[...]
    mk = mkv_ref[:, :MEM_WIDTH]
    mv = mkv_ref[:, MEM_WIDTH:]
    lane = lax.broadcasted_iota(jnp.int32, (tm, MEM_WIDTH), 1)
    o_mem = jnp.zeros((tm, MEM_WIDTH), F32)
    for h in range(N_MEM_HEADS):
        in_head = (lane >= h * HEAD_DIM) & (lane < (h + 1) * HEAD_DIM)
        qh = jnp.where(in_head, qm, jnp.zeros_like(qm))
        s = lax.dot_general(qh, mk, _NT, preferred_element_type=F32)
        e = jnp.exp(s - jnp.max(s, axis=1, keepdims=True))
        p = e / jnp.sum(e, axis=1, keepdims=True)
        oh = jnp.dot(p.astype(BF16), mv, preferred_element_type=F32)
        o_mem = jnp.where(in_head, oh, o_mem)
    y = jnp.dot(os_ref[...], w_ref[:SELF_WIDTH, :], preferred_element_type=F32)
    y = y + jnp.dot(o_mem.astype(BF16), w_ref[SELF_WIDTH:, :], preferred_element_type=F32)
    o_ref[...] = x_ref[...] + y


def _attn_out(x2d, o_self, zq, qm_block, memkv, w_out, *, seq, mem_len, tm, name):
    t, d = x2d.shape
    per_batch = seq // tm
    return pl.pallas_call(
        _attn_out_kernel,
        out_shape=jax.ShapeDtypeStruct((t, d), F32),
        grid=(t // tm,),
        in_specs=[
            pl.BlockSpec((tm, d), lambda i: (i, 0)),
            pl.BlockSpec((tm, SELF_WIDTH), lambda i: (i, 0)),
            pl.BlockSpec((tm, MEM_WIDTH), lambda i: (i, qm_block)),
            pl.BlockSpec((mem_len, 2 * MEM_WIDTH), lambda i: (i // per_batch, 0)),
            pl.BlockSpec((d, d), lambda i: (0, 0)),
        ],
        out_specs=pl.BlockSpec((tm, d), lambda i: (i, 0)),
        compiler_params=_params("parallel"),
        name=name,
    )(x2d, o_self, zq, memkv, w_out)


def _mlp_kernel(x_ref, g_ref, wu_ref, wd_ref, gf_ref, o_ref, h_scr, acc_scr, *, final_norm):
    j = pl.program_id(1)

    @pl.when(j == 0)
    def _():
        x = x_ref[...]
        ms = jnp.mean(x * x, axis=-1, keepdims=True)
        h_scr[...] = (x * lax.rsqrt(ms + EPS) * g_ref[...]).astype(BF16)
        acc_scr[...] = jnp.zeros(acc_scr.shape, F32)

    u = jnp.maximum(jnp.dot(h_scr[...], wu_ref[...], preferred_element_type=F32), 0.0)
    acc_scr[...] += jnp.dot((u * u).astype(BF16), wd_ref[...], preferred_element_type=F32)

    @pl.when(j == pl.num_programs(1) - 1)
    def _():
        y = x_ref[...] + acc_scr[...]
        if final_norm:
            ms = jnp.mean(y * y, axis=-1, keepdims=True)
            y = y * lax.rsqrt(ms + EPS) * gf_ref[...]
        o_ref[...] = y


def _mlp(x2d, g, w_up, w_down, g_final, *, final_norm, tm, tf, name):
    t, d = x2d.shape
    dff = w_up.shape[1]
    kern = functools.partial(_mlp_kernel, final_norm=final_norm)
    return pl.pallas_call(
        kern,
        out_shape=jax.ShapeDtypeStruct((t, d), F32),
        grid=(t // tm, dff // tf),
        in_specs=[
            pl.BlockSpec((tm, d), lambda i, j: (i, 0)),
            pl.BlockSpec((1, d), lambda i, j: (0, 0)),
            pl.BlockSpec((d, tf), lambda i, j: (0, j)),
            pl.BlockSpec((tf, d), lambda i, j: (j, 0)),
            pl.BlockSpec((1, d), lambda i, j: (0, 0)),
        ],
        out_specs=pl.BlockSpec((tm, d), lambda i, j: (i, 0)),
        scratch_shapes=[pltpu.VMEM((tm, d), BF16), pltpu.VMEM((tm, d), F32)],
        compiler_params=_params("parallel", "arbitrary"),
        name=name,
    )(x2d, g.reshape(1, d), w_up, w_down, g_final.reshape(1, d))


def _rope_tables(seq):
    half = HEAD_DIM // 2
    inv = 1.0 / (ROPE_THETA ** (jnp.arange(half, dtype=F32) / half))
    ang = jnp.arange(seq, dtype=F32)[:, None] * inv[None, :]
    cos, sin = jnp.cos(ang), jnp.sin(ang)
    reps = LANES // half
    cos_t = jnp.tile(cos, (1, reps))
    sign = jnp.tile(jnp.concatenate([-jnp.ones((half,), F32), jnp.ones((half,), F32)]), LANES // HEAD_DIM)
    sin_t = jnp.tile(sin, (1, reps)) * sign[None, :]
    return cos_t, sin_t


def kernel(x, mem, a_norm_attn, a_w_in, a_lambda, a_subln, a_mem_norm, a_w_mem_kv, a_w_out, a_norm_mlp, a_w_up, a_w_down, kv_norm, w_kv, b_norm_attn, b_w_in, b_mem_norm, b_w_mem_kv, b_w_out, b_norm_mlp, b_w_up, b_w_down, final_norm):
    batch, seq, d = x.shape
    mem_len = mem.shape[1]
    t = batch * seq
    x2d = x.reshape(t, d)
    mem2d = mem.reshape(batch * mem_len, d)
    cos_t, sin_t = _rope_tables(seq)
    bf = lambda w: w.astype(BF16)
    tm = 512
    g6 = range(N_SELF_GROUPS)

    lam_init = 0.8 - 0.6 * math.exp(-0.3 * 0)
    qm_groups = (3 * N_SELF_GROUPS, 3 * N_SELF_GROUPS + 1)
    za, = _proj(x2d, a_norm_attn[0], bf(a_w_in[0]), cos_t, sin_t, n_rope=2 * N_SELF_GROUPS,
                scaled=tuple(g6) + qm_groups, with_kmean=False, seq=seq, tm=tm, name="a_proj")
    mkv_a, = _proj(mem2d, a_mem_norm[0], bf(a_w_mem_kv[0]), cos_t, sin_t, n_rope=0, scaled=(),
                   with_kmean=False, seq=mem_len, tm=mem_len, name="a_memkv")
    o_self = _diff_attention(za, a_lambda[0], a_subln[0], batch=batch, seq=seq, tq=256, lam_init=lam_init)
    x2d = _attn_out(x2d, o_self, za, 3 * SELF_WIDTH // MEM_WIDTH, mkv_a, bf(a_w_out[0]),
                    seq=seq, mem_len=mem_len, tm=tm, name="a_attn_out")
    x2d = _mlp(x2d, a_norm_mlp[0], bf(a_w_up[0]), bf(a_w_down[0]), final_norm,
               final_norm=False, tm=1024, tf=512, name="a_mlp")

    zkv, kmean = _proj(x2d, kv_norm, bf(w_kv), cos_t, sin_t, n_rope=N_SELF_GROUPS, scaled=(),
                       with_kmean=True, seq=seq, tm=tm, name="b_kvproj")
    kmean = kmean.reshape(batch, seq // MOBA_BLOCK, SELF_WIDTH)
    zb, = _proj(x2d, b_norm_attn[0], bf(b_w_in[0]), cos_t, sin_t, n_rope=N_SELF_GROUPS,
                scaled=tuple(g6) + (N_SELF_GROUPS, N_SELF_GROUPS + 1), with_kmean=False,
                seq=seq, tm=tm, name="b_qproj")
    mkv_b, = _proj(mem2d, b_mem_norm[0], bf(b_w_mem_kv[0]), cos_t, sin_t, n_rope=0, scaled=(),
                   with_kmean=False, seq=mem_len, tm=mem_len, name="b_memkv")
    o_self = _moba_attention(zb, zkv, kmean, batch=batch, seq=seq)
    x2d = _attn_out(x2d, o_self, zb, SELF_WIDTH // MEM_WIDTH, mkv_b, bf(b_w_out[0]),
                    seq=seq, mem_len=mem_len, tm=tm, name="b_attn_out")
    x2d = _mlp(x2d, b_norm_mlp[0], bf(b_w_up[0]), bf(b_w_down[0]), final_norm,
               final_norm=True, tm=1024, tf=512, name="b_mlp")
    return x2d.reshape(batch, seq, d)
```

```python
import functools
import math

import jax
import jax.numpy as jnp
from jax import lax
from jax.experimental import pallas as pl
from jax.experimental.pallas import tpu as pltpu

D_MODEL = 1024
HEAD_DIM = 64
SELF_WIDTH = 768
MEM_WIDTH = 256
N_MEM_HEADS = 4
D_FF = 4096
MOBA_BLOCK = 256
MOBA_TOPK = 3
ROPE_THETA = 10000.0
EPS = 1e-6
NEG = -1e30

LANES = 128
N_SELF_GROUPS = SELF_WIDTH // LANES
QK_SCALE = HEAD_DIM ** -0.5
LOG2E = math.log2(math.e)
KV_CHUNK = 256

F32 = jnp.float32
BF16 = jnp.bfloat16

_VMEM_LIMIT = 56 * 1024 * 1024


def _params(*sem):
    return pltpu.CompilerParams(dimension_semantics=sem, vmem_limit_bytes=_VMEM_LIMIT)


def _proj_kernel(x_ref, g_ref, w_ref, cos_ref, sin_ref, o_ref, *extra_refs,
                 n_rope, scales, col_chunk, with_kmean, vt_start):
    tm = x_ref.shape[0]
    n_out = w_ref.shape[1]
    extra = list(extra_refs)
    km_ref = extra.pop(0) if with_kmean else None
    vt_ref = extra.pop(0) if vt_start is not None else None
    x = x_ref[...]
    ms = jnp.mean(x * x, axis=-1, keepdims=True)
    h = (x * lax.rsqrt(ms + EPS) * g_ref[...]).astype(BF16)
    lane = lax.broadcasted_iota(jnp.int32, (tm, LANES), 1)
    first_half = (lane & (HEAD_DIM - 1)) < HEAD_DIM // 2
    if n_rope:
        cos = cos_ref[...]
        sin = sin_ref[...]
    for c0 in range(0, n_out, col_chunk):
        z = jnp.dot(h, w_ref[:, c0:c0 + col_chunk], preferred_element_type=F32)
        for jj in range(col_chunk // LANES):
            j = c0 // LANES + jj
            blk = z[:, jj * LANES:(jj + 1) * LANES]
            if j < n_rope:
                swap = jnp.where(first_half, pltpu.roll(blk, LANES - 32, 1), pltpu.roll(blk, 32, 1))
                blk = blk * cos + swap * sin
                if km_ref is not None:
                    km_ref[0, :, j * LANES:(j + 1) * LANES] = jnp.mean(
                        blk.reshape(tm // MOBA_BLOCK, MOBA_BLOCK, LANES), axis=1)
            if j in scales:
                blk = blk * scales[j]
            o_ref[:, j * LANES:(j + 1) * LANES] = blk.astype(o_ref.dtype)
            if vt_ref is not None and vt_start <= j < vt_start + N_SELF_GROUPS:
                g = j - vt_start
                for c in range(tm // KV_CHUNK):
                    vt_ref[c, g * LANES:(g + 1) * LANES, :] = (
                        blk[c * KV_CHUNK:(c + 1) * KV_CHUNK, :].T.astype(BF16))


def _proj(x2d, g, w, cos_t, sin_t, *, n_rope, scales, with_kmean, vt_start, seq, tm, name):
    t, d = x2d.shape
    n_out = w.shape[1]
    n_pos_blocks = seq // tm
    col_chunk = 512 if n_out % 512 == 0 else 256
    out_shape = [jax.ShapeDtypeStruct((t, n_out), BF16)]
    out_specs = [pl.BlockSpec((tm, n_out), lambda i: (i, 0))]
    if with_kmean:
        out_shape.append(jax.ShapeDtypeStruct((t // tm, tm // MOBA_BLOCK, n_rope * LANES), F32))
        out_specs.append(pl.BlockSpec((1, tm // MOBA_BLOCK, n_rope * LANES), lambda i: (i, 0, 0)))
    if vt_start is not None:
        out_shape.append(jax.ShapeDtypeStruct((t // KV_CHUNK, SELF_WIDTH, KV_CHUNK), BF16))
        out_specs.append(pl.BlockSpec((tm // KV_CHUNK, SELF_WIDTH, KV_CHUNK), lambda i: (i, 0, 0)))
    kern = functools.partial(_proj_kernel, n_rope=n_rope, scales=dict(scales), col_chunk=col_chunk,
                             with_kmean=with_kmean, vt_start=vt_start)
    return pl.pallas_call(
        kern,
        out_shape=out_shape,
        grid=(t // tm,),
        in_specs=[
            pl.BlockSpec((tm, d), lambda i: (i, 0)),
            pl.BlockSpec((1, d), lambda i: (0, 0)),
            pl.BlockSpec((d, n_out), lambda i: (0, 0)),
            pl.BlockSpec((tm, LANES), lambda i: (i % n_pos_blocks, 0)),
            pl.BlockSpec((tm, LANES), lambda i: (i % n_pos_blocks, 0)),
        ],
        out_specs=out_specs,
        compiler_params=_params("parallel"),
        name=name,
    )(x2d, g.reshape(1, d), w, cos_t, sin_t)


_NT = (((1,), (1,)), ((), ()))


def _stack_maps(q):
    tq = q.shape[0]
    lane = lax.broadcasted_iota(jnp.int32, (tq, LANES), 1)
    zero = jnp.zeros_like(q)
    return jnp.concatenate(
        [jnp.where(lane < HEAD_DIM, q, zero), jnp.where(lane >= HEAD_DIM, q, zero)], axis=0)


def _causal_mask(st, tq):
    key = lax.broadcasted_iota(jnp.int32, st.shape, 0)
    qry = lax.broadcasted_iota(jnp.int32, st.shape, 1) & (tq - 1)
    return jnp.where(key <= qry, st, NEG)


def _flash_pipeline(n_groups, n_past, tq, scores_fn, vt_fn, s_scr, acc_scr):
    n = 2 * tq

    def produce(j):
        col_max = []
        for g in range(n_groups):
            st = scores_fn(g, j)
            s_scr[g] = st
            col_max.append(jnp.max(st, axis=0, keepdims=True))
        return tuple(col_max)

    def consume(j, stats, col_max, diagonal):
        out = []
        for g in range(n_groups):
            m, l = stats[g]
            st = s_scr[g]
            if diagonal:
                st = _causal_mask(st, tq)
                m_cur = jnp.max(st, axis=0, keepdims=True)
            else:
                m_cur = col_max[g]
            m_next = jnp.maximum(m, m_cur)
            alpha = jnp.exp2(m - m_next)
            p = jnp.exp2(st - m_next)
            l_next = alpha * l + jnp.sum(p, axis=0, keepdims=True)
            acc_scr[g] = alpha * acc_scr[g] + jnp.dot(vt_fn(g, j), p.astype(BF16),
                                                      preferred_element_type=F32)
            out.append((m_next, l_next))
        return tuple(out)

    def body(j, carry):
        stats = consume(j, carry[0], carry[1], False)
        return stats, produce(j + 1)

    acc_scr[...] = jnp.zeros(acc_scr.shape, F32)
    init = tuple((jnp.full((1, n), NEG, F32), jnp.zeros((1, n), F32)) for _ in range(n_groups))
    stats, col_max = lax.fori_loop(0, n_past, body, (init, produce(0)))
    return consume(n_past, stats, col_max, True)


def _diff_attn_kernel(lam_ref, g_ref, q_ref, k_ref, vt_ref, o_ref, s_scr, acc_scr, *, lam_init):
    tq = q_ref.shape[0]
    n_groups = q_ref.shape[1] // LANES
    qi = pl.program_id(2)
    cols = [slice(g * LANES, (g + 1) * LANES) for g in range(n_groups)]
    qs = [_stack_maps(q_ref[:, c]) for c in cols]

    def scores(g, j):
        start = pl.multiple_of(j * tq, tq)
        return lax.dot_general(k_ref[pl.ds(start, tq), cols[g]], qs[g], _NT,
                               preferred_element_type=F32)

    stats = _flash_pipeline(n_groups, qi, tq, scores, lambda g, j: vt_ref[j, cols[g], :],
                            s_scr, acc_scr)

    lp = lam_ref[...]
    lam = (jnp.exp(jnp.sum(lp[0:1] * lp[1:2], axis=1, keepdims=True))
           - jnp.exp(jnp.sum(lp[2:3] * lp[3:4], axis=1, keepdims=True)) + lam_init)
    for g, c in enumerate(cols):
        o_t = acc_scr[g] * (1.0 / stats[g][1])
        od = (o_t[:, :tq] - lam * o_t[:, tq:]).T
        ms = jnp.mean(od * od, axis=-1, keepdims=True)
        y = od * lax.rsqrt(ms + EPS) * g_ref[...]
        o_ref[:, c] = (y * (1.0 - lam_init)).astype(o_ref.dtype)


def _diff_attention(z, vt, lam_p, subln, *, batch, seq, n_groups, lam_init):
    t = z.shape[0]
    tq = KV_CHUNK
    nq = seq // tq
    w = n_groups * LANES
    n_steps = N_SELF_GROUPS // n_groups
    kern = functools.partial(_diff_attn_kernel, lam_init=lam_init)
    return pl.pallas_call(
        kern,
        out_shape=jax.ShapeDtypeStruct((t, SELF_WIDTH), BF16),
        grid=(batch, n_steps, nq),
        in_specs=[
            pl.BlockSpec((4, HEAD_DIM), lambda b, h, i: (0, 0)),
            pl.BlockSpec((1, LANES), lambda b, h, i: (0, 0)),
            pl.BlockSpec((tq, w), lambda b, h, i: (b * nq + i, h)),
            pl.BlockSpec((seq, w), lambda b, h, i: (b, n_steps + h)),
            pl.BlockSpec((seq // KV_CHUNK, w, KV_CHUNK), lambda b, h, i: (b, h, 0)),
        ],
        out_specs=pl.BlockSpec((tq, w), lambda b, h, i: (b * nq + i, h)),
        scratch_shapes=[pltpu.VMEM((n_groups, tq, 2 * tq), F32),
                        pltpu.VMEM((n_groups, LANES, 2 * tq), F32)],
        compiler_params=_params("parallel", "parallel", "arbitrary"),
        name="diff_attn",
    )(lam_p, subln.reshape(1, LANES), z, z, vt)


def _moba_select_bias(qs, km, qi):
    km_pad = jnp.concatenate([km, jnp.zeros((LANES - km.shape[0], LANES), F32)], axis=0)
    gate = lax.dot_general(qs.astype(F32), km_pad, _NT, preferred_element_type=F32,
                           precision=lax.Precision.HIGHEST)
    blk = lax.broadcasted_iota(jnp.int32, gate.shape, 1)
    past = blk < qi
    gm = jnp.where(past, gate, NEG)
    sel = jnp.zeros(gate.shape, F32)
    for _ in range(MOBA_TOPK):
        mx = jnp.max(gm, axis=1, keepdims=True)
        first = jnp.min(jnp.where(gm == mx, blk, LANES), axis=1, keepdims=True)
        pick = blk == first
        sel = jnp.where(pick, 1.0, sel)
        gm = jnp.where(pick, -jnp.inf, gm)
    return jnp.where((past & (sel > 0.5)) | (blk == qi), 0.0, NEG).astype(BF16)


def _moba_attn_kernel(q_ref, k_ref, vt_ref, km_ref, o_ref, s_scr, acc_scr):
    tq = q_ref.shape[0]
    n_groups = q_ref.shape[1] // LANES
    qi = pl.program_id(2)
    cols = [slice(g * LANES, (g + 1) * LANES) for g in range(n_groups)]
    qs_aug = []
    for c in cols:
        qs = _stack_maps(q_ref[:, c])
        qs_aug.append(jnp.concatenate([qs, _moba_select_bias(qs, km_ref[:, c], qi)], axis=1))

    def scores(g, j):
        start = pl.multiple_of(j * tq, tq)
        k = k_ref[pl.ds(start, tq), cols[g]]
        hot = jnp.where(lax.broadcasted_iota(jnp.int32, k.shape, 1) == j, 1.0, 0.0).astype(BF16)
        return lax.dot_general(jnp.concatenate([k, hot], axis=1), qs_aug[g], _NT,
                               preferred_element_type=F32)

    stats = _flash_pipeline(n_groups, qi, tq, scores, lambda g, j: vt_ref[j, cols[g], :],
                            s_scr, acc_scr)

    row = lax.broadcasted_iota(jnp.int32, (LANES, tq), 0)
    for g, c in enumerate(cols):
        o_t = acc_scr[g] * (1.0 / stats[g][1])
        o_ref[:, c] = jnp.where(row < HEAD_DIM, o_t[:, :tq], o_t[:, tq:]).T.astype(o_ref.dtype)


def _moba_attention(zq, zkv, vt, kmean, *, batch, seq, n_groups):
    t = zq.shape[0]
    tq = MOBA_BLOCK
    nq = seq // tq
    w = n_groups * LANES
    n_steps = N_SELF_GROUPS // n_groups
    return pl.pallas_call(
        _moba_attn_kernel,
        out_shape=jax.ShapeDtypeStruct((t, SELF_WIDTH), BF16),
        grid=(batch, n_steps, nq),
        in_specs=[
            pl.BlockSpec((tq, w), lambda b, h, i: (b * nq + i, h)),
            pl.BlockSpec((seq, w), lambda b, h, i: (b, h)),
            pl.BlockSpec((seq // KV_CHUNK, w, KV_CHUNK), lambda b, h, i: (b, h, 0)),
            pl.BlockSpec((None, nq, w), lambda b, h, i: (b, 0, h)),
        ],
        out_specs=pl.BlockSpec((tq, w), lambda b, h, i: (b * nq + i, h)),
        scratch_shapes=[pltpu.VMEM((n_groups, tq, 2 * tq), F32),
                        pltpu.VMEM((n_groups, LANES, 2 * tq), F32)],
        compiler_params=_params("parallel", "parallel", "arbitrary"),
        name="moba_attn",
    )(zq, zkv, vt, kmean)


def _attn_out_kernel(x_ref, os_ref, qm_ref, mkv_ref, w_ref, o_ref):
    tm = x_ref.shape[0]
    qm = qm_ref[...]
    mk = mkv_ref[:, :MEM_WIDTH]
    mv = mkv_ref[:, MEM_WIDTH:]
    lane = lax.broadcasted_iota(jnp.int32, (tm, MEM_WIDTH), 1)
    o_mem = jnp.zeros((tm, MEM_WIDTH), F32)
    for h in range(N_MEM_HEADS):
        in_head = (lane >= h * HEAD_DIM) & (lane < (h + 1) * HEAD_DIM)
        qh = jnp.where(in_head, qm, jnp.zeros_like(qm))
        s = lax.dot_general(qh, mk, _NT, preferred_element_type=F32)
        e = jnp.exp(s - jnp.max(s, axis=1, keepdims=True))
        p = e / jnp.sum(e, axis=1, keepdims=True)
        oh = jnp.dot(p.astype(BF16), mv, preferred_element_type=F32)
        o_mem = jnp.where(in_head, oh, o_mem)
    y = jnp.dot(os_ref[...], w_ref[:SELF_WIDTH, :], preferred_element_type=F32)
    y = y + jnp.dot(o_mem.astype(BF16), w_ref[SELF_WIDTH:, :], preferred_element_type=F32)
    o_ref[...] = x_ref[...] + y


def _attn_out(x2d, o_self, zq, qm_block, memkv, w_out, *, seq, mem_len, tm, name):
    t, d = x2d.shape
    per_batch = seq // tm
    return pl.pallas_call(
        _attn_out_kernel,
        out_shape=jax.ShapeDtypeStruct((t, d), F32),
        grid=(t // tm,),
        in_specs=[
            pl.BlockSpec((tm, d), lambda i: (i, 0)),
            pl.BlockSpec((tm, SELF_WIDTH), lambda i: (i, 0)),
            pl.BlockSpec((tm, MEM_WIDTH), lambda i: (i, qm_block)),
            pl.BlockSpec((mem_len, 2 * MEM_WIDTH), lambda i: (i // per_batch, 0)),
            pl.BlockSpec((d, d), lambda i: (0, 0)),
        ],
        out_specs=pl.BlockSpec((tm, d), lambda i: (i, 0)),
        compiler_params=_params("parallel"),
        name=name,
    )(x2d, o_self, zq, memkv, w_out)


def _mlp_kernel(x_ref, g_ref, wu_ref, wd_ref, gf_ref, o_ref, h_scr, acc_scr, *, final_norm):
    j = pl.program_id(1)

    @pl.when(j == 0)
    def _():
        x = x_ref[...]
        ms = jnp.mean(x * x, axis=-1, keepdims=True)
        h_scr[...] = (x * lax.rsqrt(ms + EPS) * g_ref[...]).astype(BF16)
        acc_scr[...] = jnp.zeros(acc_scr.shape, F32)

    u = jnp.maximum(jnp.dot(h_scr[...], wu_ref[...], preferred_element_type=F32), 0.0)
    acc_scr[...] += jnp.dot((u * u).astype(BF16), wd_ref[...], preferred_element_type=F32)

    @pl.when(j == pl.num_programs(1) - 1)
    def _():
        y = x_ref[...] + acc_scr[...]
        if final_norm:
            ms = jnp.mean(y * y, axis=-1, keepdims=True)
            y = y * lax.rsqrt(ms + EPS) * gf_ref[...]
        o_ref[...] = y


def _mlp(x2d, g, w_up, w_down, g_final, *, final_norm, tm, tf, name):
    t, d = x2d.shape
    dff = w_up.shape[1]
    kern = functools.partial(_mlp_kernel, final_norm=final_norm)
    return pl.pallas_call(
        kern,
        out_shape=jax.ShapeDtypeStruct((t, d), F32),
        grid=(t // tm, dff // tf),
        in_specs=[
            pl.BlockSpec((tm, d), lambda i, j: (i, 0)),
            pl.BlockSpec((1, d), lambda i, j: (0, 0)),
            pl.BlockSpec((d, tf), lambda i, j: (0, j)),
            pl.BlockSpec((tf, d), lambda i, j: (j, 0)),
            pl.BlockSpec((1, d), lambda i, j: (0, 0)),
        ],
        out_specs=pl.BlockSpec((tm, d), lambda i, j: (i, 0)),
        scratch_shapes=[pltpu.VMEM((tm, d), BF16), pltpu.VMEM((tm, d), F32)],
        compiler_params=_params("parallel", "arbitrary"),
        name=name,
    )(x2d, g.reshape(1, d), w_up, w_down, g_final.reshape(1, d))


def _rope_tables(seq):
    half = HEAD_DIM // 2
    inv = 1.0 / (ROPE_THETA ** (jnp.arange(half, dtype=F32) / half))
    ang = jnp.arange(seq, dtype=F32)[:, None] * inv[None, :]
    cos, sin = jnp.cos(ang), jnp.sin(ang)
    reps = LANES // half
    cos_t = jnp.tile(cos, (1, reps))
    sign = jnp.tile(jnp.concatenate([-jnp.ones((half,), F32), jnp.ones((half,), F32)]), LANES // HEAD_DIM)
    sin_t = jnp.tile(sin, (1, reps)) * sign[None, :]
    return cos_t, sin_t


def kernel(x, mem, a_norm_attn, a_w_in, a_lambda, a_subln, a_mem_norm, a_w_mem_kv, a_w_out, a_norm_mlp, a_w_up, a_w_down, kv_norm, w_kv, b_norm_attn, b_w_in, b_mem_norm, b_w_mem_kv, b_w_out, b_norm_mlp, b_w_up, b_w_down, final_norm):
    batch, seq, d = x.shape
    mem_len = mem.shape[1]
    t = batch * seq
    x2d = x.reshape(t, d)
    mem2d = mem.reshape(batch * mem_len, d)
    cos_t, sin_t = _rope_tables(seq)
    bf = lambda w: w.astype(BF16)
    tm = 512
    ng = N_SELF_GROUPS
    self_q = {j: QK_SCALE * LOG2E for j in range(ng)}

    lam_init = 0.8 - 0.6 * math.exp(-0.3 * 0)
    za, vt_a = _proj(x2d, a_norm_attn[0], bf(a_w_in[0]), cos_t, sin_t, n_rope=2 * ng,
                     scales={**self_q, 3 * ng: QK_SCALE, 3 * ng + 1: QK_SCALE},
                     with_kmean=False, vt_start=2 * ng, seq=seq, tm=tm, name="a_proj")
    mkv_a, = _proj(mem2d, a_mem_norm[0], bf(a_w_mem_kv[0]), cos_t, sin_t, n_rope=0, scales={},
                   with_kmean=False, vt_start=None, seq=mem_len, tm=mem_len, name="a_memkv")
    o_self = _diff_attention(za, vt_a, a_lambda[0], a_subln[0], batch=batch, seq=seq,
                             n_groups=2, lam_init=lam_init)
    x2d = _attn_out(x2d, o_self, za, 3 * SELF_WIDTH // MEM_WIDTH, mkv_a, bf(a_w_out[0]),
                    seq=seq, mem_len=mem_len, tm=tm, name="a_attn_out")
    x2d = _mlp(x2d, a_norm_mlp[0], bf(a_w_up[0]), bf(a_w_down[0]), final_norm,
               final_norm=False, tm=1024, tf=512, name="a_mlp")

    zkv, kmean, vt_b = _proj(x2d, kv_norm, bf(w_kv), cos_t, sin_t, n_rope=ng, scales={},
                             with_kmean=True, vt_start=ng, seq=seq, tm=tm, name="b_kvproj")
    kmean = kmean.reshape(batch, seq // MOBA_BLOCK, SELF_WIDTH)
    zb, = _proj(x2d, b_norm_attn[0], bf(b_w_in[0]), cos_t, sin_t, n_rope=ng,
                scales={**self_q, ng: QK_SCALE, ng + 1: QK_SCALE}, with_kmean=False, vt_start=None,
                seq=seq, tm=tm, name="b_qproj")
    mkv_b, = _proj(mem2d, b_mem_norm[0], bf(b_w_mem_kv[0]), cos_t, sin_t, n_rope=0, scales={},
                   with_kmean=False, vt_start=None, seq=mem_len, tm=mem_len, name="b_memkv")
    o_self = _moba_attention(zb, zkv, vt_b, kmean, batch=batch, seq=seq, n_groups=2)
    x2d = _attn_out(x2d, o_self, zb, SELF_WIDTH // MEM_WIDTH, mkv_b, bf(b_w_out[0]),
                    seq=seq, mem_len=mem_len, tm=tm, name="b_attn_out")
    x2d = _mlp(x2d, b_norm_mlp[0], bf(b_w_up[0]), bf(b_w_down[0]), final_norm,
               final_norm=True, tm=1024, tf=512, name="b_mlp")
    return x2d.reshape(batch, seq, d)
```

```python
import functools
import math

import jax
import jax.numpy as jnp
from jax import lax
from jax.experimental import pallas as pl
from jax.experimental.pallas import tpu as pltpu

D_MODEL = 1024
HEAD_DIM = 64
SELF_WIDTH = 768
MEM_WIDTH = 256
N_MEM_HEADS = 4
D_FF = 4096
MOBA_BLOCK = 256
MOBA_TOPK = 3
ROPE_THETA = 10000.0
EPS = 1e-6
NEG = -1e30

LANES = 128
N_SELF_GROUPS = SELF_WIDTH // LANES
QK_SCALE = HEAD_DIM ** -0.5
LOG2E = math.log2(math.e)
KV_CHUNK = 256

F32 = jnp.float32
BF16 = jnp.bfloat16

_VMEM_LIMIT = 56 * 1024 * 1024


def _params(*sem):
    return pltpu.CompilerParams(dimension_semantics=sem, vmem_limit_bytes=_VMEM_LIMIT)


def _proj_kernel(x_ref, g_ref, w_ref, cos_ref, sin_ref, o_ref, *extra_refs,
                 n_rope, scales, col_chunk, with_kmean, vt_start):
    tm = x_ref.shape[0]
    n_out = w_ref.shape[1]
    extra = list(extra_refs)
    km_ref = extra.pop(0) if with_kmean else None
    vt_ref = extra.pop(0) if vt_start is not None else None
    x = x_ref[...]
    ms = jnp.mean(x * x, axis=-1, keepdims=True)
    h = (x * lax.rsqrt(ms + EPS) * g_ref[...]).astype(BF16)
    lane = lax.broadcasted_iota(jnp.int32, (tm, LANES), 1)
    first_half = (lane & (HEAD_DIM - 1)) < HEAD_DIM // 2
    if n_rope:
        cos = cos_ref[...]
        sin = sin_ref[...]
    for c0 in range(0, n_out, col_chunk):
        z = jnp.dot(h, w_ref[:, c0:c0 + col_chunk], preferred_element_type=F32)
        for jj in range(col_chunk // LANES):
            j = c0 // LANES + jj
            blk = z[:, jj * LANES:(jj + 1) * LANES]
            if j < n_rope:
                swap = jnp.where(first_half, pltpu.roll(blk, LANES - 32, 1), pltpu.roll(blk, 32, 1))
                blk = blk * cos + swap * sin
                if km_ref is not None:
                    km_ref[0, :, j * LANES:(j + 1) * LANES] = jnp.mean(
                        blk.reshape(tm // MOBA_BLOCK, MOBA_BLOCK, LANES), axis=1)
            if j in scales:
                blk = blk * scales[j]
            o_ref[:, j * LANES:(j + 1) * LANES] = blk.astype(o_ref.dtype)
            if vt_ref is not None and vt_start <= j < vt_start + N_SELF_GROUPS:
                g = j - vt_start
                for c in range(tm // KV_CHUNK):
                    vt_ref[c, g * LANES:(g + 1) * LANES, :] = (
                        blk[c * KV_CHUNK:(c + 1) * KV_CHUNK, :].T.astype(BF16))


def _proj(x2d, g, w, cos_t, sin_t, *, n_rope, scales, with_kmean, vt_start, seq, tm, name):
    t, d = x2d.shape
    n_out = w.shape[1]
    n_pos_blocks = seq // tm
    col_chunk = 512 if n_out % 512 == 0 else 256
    out_shape = [jax.ShapeDtypeStruct((t, n_out), BF16)]
    out_specs = [pl.BlockSpec((tm, n_out), lambda i: (i, 0))]
    if with_kmean:
        out_shape.append(jax.ShapeDtypeStruct((t // tm, tm // MOBA_BLOCK, n_rope * LANES), F32))
        out_specs.append(pl.BlockSpec((1, tm // MOBA_BLOCK, n_rope * LANES), lambda i: (i, 0, 0)))
    if vt_start is not None:
        out_shape.append(jax.ShapeDtypeStruct((t // KV_CHUNK, SELF_WIDTH, KV_CHUNK), BF16))
        out_specs.append(pl.BlockSpec((tm // KV_CHUNK, SELF_WIDTH, KV_CHUNK), lambda i: (i, 0, 0)))
    kern = functools.partial(_proj_kernel, n_rope=n_rope, scales=dict(scales), col_chunk=col_chunk,
                             with_kmean=with_kmean, vt_start=vt_start)
    return pl.pallas_call(
        kern,
        out_shape=out_shape,
        grid=(t // tm,),
        in_specs=[
            pl.BlockSpec((tm, d), lambda i: (i, 0)),
            pl.BlockSpec((1, d), lambda i: (0, 0)),
            pl.BlockSpec((d, n_out), lambda i: (0, 0)),
            pl.BlockSpec((tm, LANES), lambda i: (i % n_pos_blocks, 0)),
            pl.BlockSpec((tm, LANES), lambda i: (i % n_pos_blocks, 0)),
        ],
        out_specs=out_specs,
        compiler_params=_params("parallel"),
        name=name,
    )(x2d, g.reshape(1, d), w, cos_t, sin_t)


_NT = (((1,), (1,)), ((), ()))


def _stack_maps(q):
    tq = q.shape[0]
    lane = lax.broadcasted_iota(jnp.int32, (tq, LANES), 1)
    zero = jnp.zeros_like(q)
    return jnp.concatenate(
        [jnp.where(lane < HEAD_DIM, q, zero), jnp.where(lane >= HEAD_DIM, q, zero)], axis=0)


def _causal_mask(st, tq):
    key = lax.broadcasted_iota(jnp.int32, st.shape, 0)
    qry = lax.broadcasted_iota(jnp.int32, st.shape, 1) & (tq - 1)
    return jnp.where(key <= qry, st, NEG)


def _flash_pipeline(n_groups, n_past, tq, scores_fn, vt_fn, s_scr, acc_scr):
    n = 2 * tq

    def produce(j, buf):
        col_max = []
        for g in range(n_groups):
            st = scores_fn(g, j)
            s_scr[buf, g] = st
            col_max.append(jnp.max(st, axis=0, keepdims=True))
        return tuple(col_max)

    def consume(j, buf, stats, col_max, diagonal):
        out = []
        for g in range(n_groups):
            m, l = stats[g]
            st = s_scr[buf, g]
            if diagonal:
                st = _causal_mask(st, tq)
                m_cur = jnp.max(st, axis=0, keepdims=True)
            else:
                m_cur = col_max[g]
            m_next = jnp.maximum(m, m_cur)
            alpha = jnp.exp2(m - m_next)
            p = jnp.exp2(st - m_next)
            l_next = alpha * l + jnp.sum(p, axis=0, keepdims=True)
            acc_scr[g] = alpha * acc_scr[g] + jnp.dot(vt_fn(g, j), p.astype(BF16),
                                                      preferred_element_type=F32)
            out.append((m_next, l_next))
        return tuple(out)

    def step(j, buf, carry):
        nxt = produce(j + 1, 1 - buf)
        return consume(j, buf, carry[0], carry[1], False), nxt

    def pair(i, carry):
        return step(2 * i + 1, 1, step(2 * i, 0, carry))

    acc_scr[...] = jnp.zeros(acc_scr.shape, F32)
    init = tuple((jnp.full((1, n), NEG, F32), jnp.zeros((1, n), F32)) for _ in range(n_groups))
    carry = lax.fori_loop(0, n_past // 2, pair, (init, produce(0, 0)))

    def odd_tail(carry):
        stats, col_max = step(n_past - 1, 0, carry)
        return consume(n_past, 1, stats, col_max, True)

    def even_tail(carry):
        return consume(n_past, 0, carry[0], carry[1], True)

    return lax.cond(n_past % 2 == 1, odd_tail, even_tail, carry)


def _diff_attn_kernel(lam_ref, g_ref, q_ref, k_ref, vt_ref, o_ref, s_scr, acc_scr, *, lam_init):
    tq = q_ref.shape[0]
    n_groups = q_ref.shape[1] // LANES
    qi = pl.program_id(2)
    cols = [slice(g * LANES, (g + 1) * LANES) for g in range(n_groups)]
    qs = [_stack_maps(q_ref[:, c]) for c in cols]

    def scores(g, j):
        start = pl.multiple_of(j * tq, tq)
        return lax.dot_general(k_ref[pl.ds(start, tq), cols[g]], qs[g], _NT,
                               preferred_element_type=F32)

    stats = _flash_pipeline(n_groups, qi, tq, scores, lambda g, j: vt_ref[j, cols[g], :],
                            s_scr, acc_scr)

    lp = lam_ref[...]
    lam = (jnp.exp(jnp.sum(lp[0:1] * lp[1:2], axis=1, keepdims=True))
           - jnp.exp(jnp.sum(lp[2:3] * lp[3:4], axis=1, keepdims=True)) + lam_init)
    for g, c in enumerate(cols):
        o_t = acc_scr[g] * (1.0 / stats[g][1])
        od = (o_t[:, :tq] - lam * o_t[:, tq:]).T
        ms = jnp.mean(od * od, axis=-1, keepdims=True)
        y = od * lax.rsqrt(ms + EPS) * g_ref[...]
        o_ref[:, c] = (y * (1.0 - lam_init)).astype(o_ref.dtype)


def _diff_attention(z, vt, lam_p, subln, *, batch, seq, n_groups, lam_init):
    t = z.shape[0]
    tq = KV_CHUNK
    nq = seq // tq
    w = n_groups * LANES
    n_steps = N_SELF_GROUPS // n_groups
    kern = functools.partial(_diff_attn_kernel, lam_init=lam_init)
    return pl.pallas_call(
        kern,
        out_shape=jax.ShapeDtypeStruct((t, SELF_WIDTH), BF16),
        grid=(batch, n_steps, nq),
        in_specs=[
            pl.BlockSpec((4, HEAD_DIM), lambda b, h, i: (0, 0)),
            pl.BlockSpec((1, LANES), lambda b, h, i: (0, 0)),
            pl.BlockSpec((tq, w), lambda b, h, i: (b * nq + i, h)),
            pl.BlockSpec((seq, w), lambda b, h, i: (b, n_steps + h)),
            pl.BlockSpec((seq // KV_CHUNK, w, KV_CHUNK), lambda b, h, i: (b, h, 0)),
        ],
        out_specs=pl.BlockSpec((tq, w), lambda b, h, i: (b * nq + i, h)),
        scratch_shapes=[pltpu.VMEM((2, n_groups, tq, 2 * tq), F32),
                        pltpu.VMEM((n_groups, LANES, 2 * tq), F32)],
        compiler_params=_params("parallel", "parallel", "arbitrary"),
        name="diff_attn",
    )(lam_p, subln.reshape(1, LANES), z, z, vt)


def _moba_select_bias(qs, km, qi):
    km_pad = jnp.concatenate([km, jnp.zeros((LANES - km.shape[0], LANES), F32)], axis=0)
    gate = lax.dot_general(qs.astype(F32), km_pad, _NT, preferred_element_type=F32,
                           precision=lax.Precision.HIGHEST)
    blk = lax.broadcasted_iota(jnp.int32, gate.shape, 1)
    past = blk < qi
    gm = jnp.where(past, gate, NEG)
    sel = jnp.zeros(gate.shape, F32)
    for _ in range(MOBA_TOPK):
        mx = jnp.max(gm, axis=1, keepdims=True)
        first = jnp.min(jnp.where(gm == mx, blk, LANES), axis=1, keepdims=True)
        pick = blk == first
        sel = jnp.where(pick, 1.0, sel)
        gm = jnp.where(pick, -jnp.inf, gm)
    return jnp.where((past & (sel > 0.5)) | (blk == qi), 0.0, NEG).astype(BF16)


def _moba_attn_kernel(q_ref, k_ref, vt_ref, km_ref, o_ref, s_scr, acc_scr):
    tq = q_ref.shape[0]
    n_groups = q_ref.shape[1] // LANES
    qi = pl.program_id(2)
    cols = [slice(g * LANES, (g + 1) * LANES) for g in range(n_groups)]
    qs_aug = []
    for c in cols:
        qs = _stack_maps(q_ref[:, c])
        qs_aug.append(jnp.concatenate([qs, _moba_select_bias(qs, km_ref[:, c], qi)], axis=1))

    def scores(g, j):
        start = pl.multiple_of(j * tq, tq)
        k = k_ref[pl.ds(start, tq), cols[g]]
        hot = jnp.where(lax.broadcasted_iota(jnp.int32, k.shape, 1) == j, 1.0, 0.0).astype(BF16)
        return lax.dot_general(jnp.concatenate([k, hot], axis=1), qs_aug[g], _NT,
                               preferred_element_type=F32)

    stats = _flash_pipeline(n_groups, qi, tq, scores, lambda g, j: vt_ref[j, cols[g], :],
                            s_scr, acc_scr)

    row = lax.broadcasted_iota(jnp.int32, (LANES, tq), 0)
    for g, c in enumerate(cols):
        o_t = acc_scr[g] * (1.0 / stats[g][1])
        o_ref[:, c] = jnp.where(row < HEAD_DIM, o_t[:, :tq], o_t[:, tq:]).T.astype(o_ref.dtype)


def _moba_attention(zq, zkv, vt, kmean, *, batch, seq, n_groups):
    t = zq.shape[0]
    tq = MOBA_BLOCK
    nq = seq // tq
    w = n_groups * LANES
    n_steps = N_SELF_GROUPS // n_groups
    return pl.pallas_call(
        _moba_attn_kernel,
        out_shape=jax.ShapeDtypeStruct((t, SELF_WIDTH), BF16),
        grid=(batch, n_steps, nq),
        in_specs=[
            pl.BlockSpec((tq, w), lambda b, h, i: (b * nq + i, h)),
            pl.BlockSpec((seq, w), lambda b, h, i: (b, h)),
            pl.BlockSpec((seq // KV_CHUNK, w, KV_CHUNK), lambda b, h, i: (b, h, 0)),
            pl.BlockSpec((None, nq, w), lambda b, h, i: (b, 0, h)),
        ],
        out_specs=pl.BlockSpec((tq, w), lambda b, h, i: (b * nq + i, h)),
        scratch_shapes=[pltpu.VMEM((2, n_groups, tq, 2 * tq), F32),
                        pltpu.VMEM((n_groups, LANES, 2 * tq), F32)],
        compiler_params=_params("parallel", "parallel", "arbitrary"),
        name="moba_attn",
    )(zq, zkv, vt, kmean)


def _attn_out_kernel(x_ref, os_ref, qm_ref, mkv_ref, w_ref, o_ref):
    tm = x_ref.shape[0]
    qm = qm_ref[...]
    mk = mkv_ref[:, :MEM_WIDTH]
    mv = mkv_ref[:, MEM_WIDTH:]
    lane = lax.broadcasted_iota(jnp.int32, (tm, MEM_WIDTH), 1)
    o_mem = jnp.zeros((tm, MEM_WIDTH), F32)
    for h in range(N_MEM_HEADS):
        in_head = (lane >= h * HEAD_DIM) & (lane < (h + 1) * HEAD_DIM)
        qh = jnp.where(in_head, qm, jnp.zeros_like(qm))
        s = lax.dot_general(qh, mk, _NT, preferred_element_type=F32)
        e = jnp.exp(s - jnp.max(s, axis=1, keepdims=True))
        p = e / jnp.sum(e, axis=1, keepdims=True)
        oh = jnp.dot(p.astype(BF16), mv, preferred_element_type=F32)
        o_mem = jnp.where(in_head, oh, o_mem)
    y = jnp.dot(os_ref[...], w_ref[:SELF_WIDTH, :], preferred_element_type=F32)
    y = y + jnp.dot(o_mem.astype(BF16), w_ref[SELF_WIDTH:, :], preferred_element_type=F32)
    o_ref[...] = x_ref[...] + y


def _attn_out(x2d, o_self, zq, qm_block, memkv, w_out, *, seq, mem_len, tm, name):
    t, d = x2d.shape
    per_batch = seq // tm
    return pl.pallas_call(
        _attn_out_kernel,
        out_shape=jax.ShapeDtypeStruct((t, d), F32),
        grid=(t // tm,),
        in_specs=[
            pl.BlockSpec((tm, d), lambda i: (i, 0)),
            pl.BlockSpec((tm, SELF_WIDTH), lambda i: (i, 0)),
            pl.BlockSpec((tm, MEM_WIDTH), lambda i: (i, qm_block)),
            pl.BlockSpec((mem_len, 2 * MEM_WIDTH), lambda i: (i // per_batch, 0)),
            pl.BlockSpec((d, d), lambda i: (0, 0)),
        ],
        out_specs=pl.BlockSpec((tm, d), lambda i: (i, 0)),
        compiler_params=_params("parallel"),
        name=name,
    )(x2d, o_self, zq, memkv, w_out)


def _mlp_kernel(x_ref, g_ref, wu_ref, wd_ref, gf_ref, o_ref, h_scr, acc_scr, *, final_norm):
    j = pl.program_id(1)

    @pl.when(j == 0)
    def _():
        x = x_ref[...]
        ms = jnp.mean(x * x, axis=-1, keepdims=True)
        h_scr[...] = (x * lax.rsqrt(ms + EPS) * g_ref[...]).astype(BF16)
        acc_scr[...] = jnp.zeros(acc_scr.shape, F32)

    u = jnp.maximum(jnp.dot(h_scr[...], wu_ref[...], preferred_element_type=F32), 0.0)
    acc_scr[...] += jnp.dot((u * u).astype(BF16), wd_ref[...], preferred_element_type=F32)

    @pl.when(j == pl.num_programs(1) - 1)
    def _():
        y = x_ref[...] + acc_scr[...]
        if final_norm:
            ms = jnp.mean(y * y, axis=-1, keepdims=True)
            y = y * lax.rsqrt(ms + EPS) * gf_ref[...]
        o_ref[...] = y


def _mlp(x2d, g, w_up, w_down, g_final, *, final_norm, tm, tf, name):
    t, d = x2d.shape
    dff = w_up.shape[1]
    kern = functools.partial(_mlp_kernel, final_norm=final_norm)
    return pl.pallas_call(
        kern,
        out_shape=jax.ShapeDtypeStruct((t, d), F32),
        grid=(t // tm, dff // tf),
        in_specs=[
            pl.BlockSpec((tm, d), lambda i, j: (i, 0)),
            pl.BlockSpec((1, d), lambda i, j: (0, 0)),
            pl.BlockSpec((d, tf), lambda i, j: (0, j)),
            pl.BlockSpec((tf, d), lambda i, j: (j, 0)),
            pl.BlockSpec((1, d), lambda i, j: (0, 0)),
        ],
        out_specs=pl.BlockSpec((tm, d), lambda i, j: (i, 0)),
        scratch_shapes=[pltpu.VMEM((tm, d), BF16), pltpu.VMEM((tm, d), F32)],
        compiler_params=_params("parallel", "arbitrary"),
        name=name,
    )(x2d, g.reshape(1, d), w_up, w_down, g_final.reshape(1, d))


def _rope_tables(seq):
    half = HEAD_DIM // 2
    inv = 1.0 / (ROPE_THETA ** (jnp.arange(half, dtype=F32) / half))
    ang = jnp.arange(seq, dtype=F32)[:, None] * inv[None, :]
    cos, sin = jnp.cos(ang), jnp.sin(ang)
    reps = LANES // half
    cos_t = jnp.tile(cos, (1, reps))
    sign = jnp.tile(jnp.concatenate([-jnp.ones((half,), F32), jnp.ones((half,), F32)]), LANES // HEAD_DIM)
    sin_t = jnp.tile(sin, (1, reps)) * sign[None, :]
    return cos_t, sin_t


def kernel(x, mem, a_norm_attn, a_w_in, a_lambda, a_subln, a_mem_norm, a_w_mem_kv, a_w_out, a_norm_mlp, a_w_up, a_w_down, kv_norm, w_kv, b_norm_attn, b_w_in, b_mem_norm, b_w_mem_kv, b_w_out, b_norm_mlp, b_w_up, b_w_down, final_norm):
    batch, seq, d = x.shape
    mem_len = mem.shape[1]
    t = batch * seq
    x2d = x.reshape(t, d)
    mem2d = mem.reshape(batch * mem_len, d)
    cos_t, sin_t = _rope_tables(seq)
    bf = lambda w: w.astype(BF16)
    tm = 512
    ng = N_SELF_GROUPS
    self_q = {j: QK_SCALE * LOG2E for j in range(ng)}

    lam_init = 0.8 - 0.6 * math.exp(-0.3 * 0)
    za, vt_a = _proj(x2d, a_norm_attn[0], bf(a_w_in[0]), cos_t, sin_t, n_rope=2 * ng,
                     scales={**self_q, 3 * ng: QK_SCALE, 3 * ng + 1: QK_SCALE},
                     with_kmean=False, vt_start=2 * ng, seq=seq, tm=tm, name="a_proj")
    mkv_a, = _proj(mem2d, a_mem_norm[0], bf(a_w_mem_kv[0]), cos_t, sin_t, n_rope=0, scales={},
                   with_kmean=False, vt_start=None, seq=mem_len, tm=mem_len, name="a_memkv")
    o_self = _diff_attention(za, vt_a, a_lambda[0], a_subln[0], batch=batch, seq=seq,
                             n_groups=3, lam_init=lam_init)
    x2d = _attn_out(x2d, o_self, za, 3 * SELF_WIDTH // MEM_WIDTH, mkv_a, bf(a_w_out[0]),
                    seq=seq, mem_len=mem_len, tm=tm, name="a_attn_out")
    x2d = _mlp(x2d, a_norm_mlp[0], bf(a_w_up[0]), bf(a_w_down[0]), final_norm,
               final_norm=False, tm=1024, tf=512, name="a_mlp")

    zkv, kmean, vt_b = _proj(x2d, kv_norm, bf(w_kv), cos_t, sin_t, n_rope=ng, scales={},
                             with_kmean=True, vt_start=ng, seq=seq, tm=tm, name="b_kvproj")
    kmean = kmean.reshape(batch, seq // MOBA_BLOCK, SELF_WIDTH)
    zb, = _proj(x2d, b_norm_attn[0], bf(b_w_in[0]), cos_t, sin_t, n_rope=ng,
                scales={**self_q, ng: QK_SCALE, ng + 1: QK_SCALE}, with_kmean=False, vt_start=None,
                seq=seq, tm=tm, name="b_qproj")
    mkv_b, = _proj(mem2d, b_mem_norm[0], bf(b_w_mem_kv[0]), cos_t, sin_t, n_rope=0, scales={},
                   with_kmean=False, vt_start=None, seq=mem_len, tm=mem_len, name="b_memkv")
    o_self = _moba_attention(zb, zkv, vt_b, kmean, batch=batch, seq=seq, n_groups=2)
    x2d = _attn_out(x2d, o_self, zb, SELF_WIDTH // MEM_WIDTH, mkv_b, bf(b_w_out[0]),
                    seq=seq, mem_len=mem_len, tm=tm, name="b_attn_out")
    x2d = _mlp(x2d, b_norm_mlp[0], bf(b_w_up[0]), bf(b_w_down[0]), final_norm,
               final_norm=True, tm=1024, tf=512, name="b_mlp")
    return x2d.reshape(batch, seq, d)
```

```python
import functools
import math

import jax
import jax.numpy as jnp
from jax import lax
from jax.experimental import pallas as pl
from jax.experimental.pallas import tpu as pltpu

D_MODEL = 1024
HEAD_DIM = 64
SELF_WIDTH = 768
MEM_WIDTH = 256
N_MEM_HEADS = 4
D_FF = 4096
MOBA_BLOCK = 256
MOBA_TOPK = 3
ROPE_THETA = 10000.0
EPS = 1e-6
NEG = -1e30

LANES = 128
N_SELF_GROUPS = SELF_WIDTH // LANES
QK_SCALE = HEAD_DIM ** -0.5
LOG2E = math.log2(math.e)
KV_CHUNK = 256
SUM_ROWS = 16

F32 = jnp.float32
BF16 = jnp.bfloat16

_VMEM_LIMIT = 56 * 1024 * 1024


def _params(*sem):
    return pltpu.CompilerParams(dimension_semantics=sem, vmem_limit_bytes=_VMEM_LIMIT)


def _proj_kernel(x_ref, g_ref, w_ref, cos_ref, sin_ref, o_ref, *extra_refs,
                 n_rope, scales, col_chunk, with_kmean, vt_start):
    tm = x_ref.shape[0]
    n_out = w_ref.shape[1]
    extra = list(extra_refs)
    km_ref = extra.pop(0) if with_kmean else None
    vt_ref = extra.pop(0) if vt_start is not None else None
    x = x_ref[...]
    ms = jnp.mean(x * x, axis=-1, keepdims=True)
    h = (x * lax.rsqrt(ms + EPS) * g_ref[...]).astype(BF16)
    lane = lax.broadcasted_iota(jnp.int32, (tm, LANES), 1)
    first_half = (lane & (HEAD_DIM - 1)) < HEAD_DIM // 2
    if n_rope:
        cos = cos_ref[...]
        sin = sin_ref[...]
    for c0 in range(0, n_out, col_chunk):
        z = jnp.dot(h, w_ref[:, c0:c0 + col_chunk], preferred_element_type=F32)
        for jj in range(col_chunk // LANES):
            j = c0 // LANES + jj
            blk = z[:, jj * LANES:(jj + 1) * LANES]
            if j < n_rope:
                swap = jnp.where(first_half, pltpu.roll(blk, LANES - 32, 1), pltpu.roll(blk, 32, 1))
                blk = blk * cos + swap * sin
                if km_ref is not None:
                    km_ref[0, :, j * LANES:(j + 1) * LANES] = jnp.mean(
                        blk.reshape(tm // MOBA_BLOCK, MOBA_BLOCK, LANES), axis=1)
            if j in scales:
                blk = blk * scales[j]
            o_ref[:, j * LANES:(j + 1) * LANES] = blk.astype(o_ref.dtype)
            if vt_ref is not None and vt_start <= j < vt_start + N_SELF_GROUPS:
                g = j - vt_start
                for c in range(tm // KV_CHUNK):
                    vt_ref[c, g * LANES:(g + 1) * LANES, :] = (
                        blk[c * KV_CHUNK:(c + 1) * KV_CHUNK, :].T.astype(BF16))


def _proj(x2d, g, w, cos_t, sin_t, *, n_rope, scales, with_kmean, vt_start, seq, tm, name):
    t, d = x2d.shape
    n_out = w.shape[1]
    n_pos_blocks = seq // tm
    col_chunk = 512 if n_out % 512 == 0 else 256
    out_shape = [jax.ShapeDtypeStruct((t, n_out), BF16)]
    out_specs = [pl.BlockSpec((tm, n_out), lambda i: (i, 0))]
    if with_kmean:
        out_shape.append(jax.ShapeDtypeStruct((t // tm, tm // MOBA_BLOCK, n_rope * LANES), F32))
        out_specs.append(pl.BlockSpec((1, tm // MOBA_BLOCK, n_rope * LANES), lambda i: (i, 0, 0)))
    if vt_start is not None:
        out_shape.append(jax.ShapeDtypeStruct((t // KV_CHUNK, SELF_WIDTH, KV_CHUNK), BF16))
        out_specs.append(pl.BlockSpec((tm // KV_CHUNK, SELF_WIDTH, KV_CHUNK), lambda i: (i, 0, 0)))
    kern = functools.partial(_proj_kernel, n_rope=n_rope, scales=dict(scales), col_chunk=col_chunk,
                             with_kmean=with_kmean, vt_start=vt_start)
    return pl.pallas_call(
        kern,
        out_shape=out_shape,
        grid=(t // tm,),
        in_specs=[
            pl.BlockSpec((tm, d), lambda i: (i, 0)),
            pl.BlockSpec((1, d), lambda i: (0, 0)),
            pl.BlockSpec((d, n_out), lambda i: (0, 0)),
            pl.BlockSpec((tm, LANES), lambda i: (i % n_pos_blocks, 0)),
            pl.BlockSpec((tm, LANES), lambda i: (i % n_pos_blocks, 0)),
        ],
        out_specs=out_specs,
        compiler_params=_params("parallel"),
        name=name,
    )(x2d, g.reshape(1, d), w, cos_t, sin_t)


_NT = (((1,), (1,)), ((), ()))


def _stack_maps(q):
    tq = q.shape[0]
    lane = lax.broadcasted_iota(jnp.int32, (tq, LANES), 1)
    zero = jnp.zeros_like(q)
    return jnp.concatenate(
        [jnp.where(lane < HEAD_DIM, q, zero), jnp.where(lane >= HEAD_DIM, q, zero)], axis=0)


def _causal_mask(st, tq):
    key = lax.broadcasted_iota(jnp.int32, st.shape, 0)
    qry = lax.broadcasted_iota(jnp.int32, st.shape, 1) & (tq - 1)
    return jnp.where(key <= qry, st, NEG)


def _flash_pipeline(n_groups, n_past, tq, scores_fn, vt_fn, s_scr, acc_scr, split_heads):
    n = 2 * tq

    def produce(j, buf):
        col_max = []
        for g in range(n_groups):
            st = scores_fn(g, j)
            s_scr[buf, g] = st
            col_max.append(jnp.max(st, axis=0, keepdims=True))
        return tuple(col_max)

    ones_rows = jnp.ones((SUM_ROWS, tq), BF16)

    def consume(j, buf, stats, col_max, diagonal):
        out = []
        for g in range(n_groups):
            st = s_scr[buf, g]
            if diagonal:
                st = _causal_mask(st, tq)
                m_cur = jnp.max(st, axis=0, keepdims=True)
            else:
                m_cur = col_max[g]
            m_next = jnp.maximum(stats[g], m_cur)
            alpha = jnp.exp2(stats[g] - m_next)
            p = jnp.exp2((st - m_next).astype(BF16))
            vt = vt_fn(g, j)
            if split_heads:
                for h in range(2):
                    vt_h = jnp.concatenate([vt[h * HEAD_DIM:(h + 1) * HEAD_DIM], ones_rows], axis=0)
                    lanes = slice(h * tq, (h + 1) * tq)
                    acc_scr[g, h] = alpha[:, lanes] * acc_scr[g, h] + jnp.dot(
                        vt_h, p[:, lanes], preferred_element_type=F32)
            else:
                acc_scr[g] = alpha * acc_scr[g] + jnp.dot(
                    jnp.concatenate([vt, ones_rows], axis=0), p, preferred_element_type=F32)
            out.append(m_next)
        return tuple(out)

    def step(j, buf, carry):
        nxt = produce(j + 1, 1 - buf)
        return consume(j, buf, carry[0], carry[1], False), nxt

    def pair(i, carry):
        return step(2 * i + 1, 1, step(2 * i, 0, carry))

    def quad(i, carry):
        return pair(2 * i + 1, pair(2 * i, carry))

    acc_scr[...] = jnp.zeros(acc_scr.shape, F32)
    init = tuple(jnp.full((1, n), NEG, F32) for _ in range(n_groups))
    n_quads = n_past // 4
    carry = lax.fori_loop(0, n_quads, quad, (init, produce(0, 0)))
    carry = lax.fori_loop(2 * n_quads, n_past // 2, pair, carry)

    def odd_tail(carry):
        stats, col_max = step(n_past - 1, 0, carry)
        return consume(n_past, 1, stats, col_max, True)

    def even_tail(carry):
        return consume(n_past, 0, carry[0], carry[1], True)

    return lax.cond(n_past % 2 == 1, odd_tail, even_tail, carry)


def _diff_attn_kernel(lam_ref, g_ref, q_ref, k_ref, vt_ref, o_ref, s_scr, acc_scr, *, lam_init):
    tq = q_ref.shape[0]
    n_groups = q_ref.shape[1] // LANES
    qi = pl.program_id(2)
    cols = [slice(g * LANES, (g + 1) * LANES) for g in range(n_groups)]
    qs = [_stack_maps(q_ref[:, c]) for c in cols]

    def scores(g, j):
        start = pl.multiple_of(j * tq, tq)
        return lax.dot_general(k_ref[pl.ds(start, tq), cols[g]], qs[g], _NT,
                               preferred_element_type=F32)

    _flash_pipeline(n_groups, qi, tq, scores, lambda g, j: vt_ref[j, cols[g], :], s_scr, acc_scr,
                    split_heads=False)

    lp = lam_ref[...]
    lam = (jnp.exp(jnp.sum(lp[0:1] * lp[1:2], axis=1, keepdims=True))
           - jnp.exp(jnp.sum(lp[2:3] * lp[3:4], axis=1, keepdims=True)) + lam_init)
    for g, c in enumerate(cols):
        o_t = acc_scr[g, :LANES, :] * (1.0 / acc_scr[g, LANES:LANES + 1, :])
        od = (o_t[:, :tq] - lam * o_t[:, tq:]).T
        ms = jnp.mean(od * od, axis=-1, keepdims=True)
        y = od * lax.rsqrt(ms + EPS) * g_ref[...]
        o_ref[:, c] = (y * (1.0 - lam_init)).astype(o_ref.dtype)


def _diff_attention(z, vt, lam_p, subln, *, batch, seq, n_groups, lam_init):
    t = z.shape[0]
    tq = KV_CHUNK
    nq = seq // tq
    w = n_groups * LANES
    n_steps = N_SELF_GROUPS // n_groups
    kern = functools.partial(_diff_attn_kernel, lam_init=lam_init)
    return pl.pallas_call(
        kern,
        out_shape=jax.ShapeDtypeStruct((t, SELF_WIDTH), BF16),
        grid=(batch, n_steps, nq),
        in_specs=[
            pl.BlockSpec((4, HEAD_DIM), lambda b, h, i: (0, 0)),
            pl.BlockSpec((1, LANES), lambda b, h, i: (0, 0)),
            pl.BlockSpec((tq, w), lambda b, h, i: (b * nq + i, h)),
            pl.BlockSpec((seq, w), lambda b, h, i: (b, n_steps + h)),
            pl.BlockSpec((seq // KV_CHUNK, w, KV_CHUNK), lambda b, h, i: (b, h, 0)),
        ],
        out_specs=pl.BlockSpec((tq, w), lambda b, h, i: (b * nq + i, h)),
        scratch_shapes=[pltpu.VMEM((2, n_groups, tq, 2 * tq), F32),
                        pltpu.VMEM((n_groups, LANES + SUM_ROWS, 2 * tq), F32)],
        compiler_params=_params("parallel", "parallel", "arbitrary"),
        name="diff_attn",
    )(lam_p, subln.reshape(1, LANES), z, z, vt)


def _moba_select_bias(qs, km, qi):
    km_pad = jnp.concatenate([km, jnp.zeros((LANES - km.shape[0], LANES), F32)], axis=0)
    gate = lax.dot_general(qs.astype(F32), km_pad, _NT, preferred_element_type=F32,
                           precision=lax.Precision.HIGHEST)
    blk = lax.broadcasted_iota(jnp.int32, gate.shape, 1)
    past = blk < qi
    gm = jnp.where(past, gate, NEG)
    sel = jnp.zeros(gate.shape, F32)
    for _ in range(MOBA_TOPK):
        mx = jnp.max(gm, axis=1, keepdims=True)
        first = jnp.min(jnp.where(gm == mx, blk, LANES), axis=1, keepdims=True)
        pick = blk == first
        sel = jnp.where(pick, 1.0, sel)
        gm = jnp.where(pick, -jnp.inf, gm)
    return jnp.where((past & (sel > 0.5)) | (blk == qi), 0.0, NEG).astype(BF16)


def _moba_attn_kernel(q_ref, k_ref, vt_ref, km_ref, o_ref, s_scr, acc_scr):
    tq = q_ref.shape[0]
    n_groups = q_ref.shape[1] // LANES
    qi = pl.program_id(2)
    cols = [slice(g * LANES, (g + 1) * LANES) for g in range(n_groups)]
    qs_aug = []
    for c in cols:
        qs = _stack_maps(q_ref[:, c])
        qs_aug.append(jnp.concatenate([qs, _moba_select_bias(qs, km_ref[:, c], qi)], axis=1))

    def scores(g, j):
        start = pl.multiple_of(j * tq, tq)
        k = k_ref[pl.ds(start, tq), cols[g]]
        hot = jnp.where(lax.broadcasted_iota(jnp.int32, k.shape, 1) == j, 1.0, 0.0).astype(BF16)
        return lax.dot_general(jnp.concatenate([k, hot], axis=1), qs_aug[g], _NT,
                               preferred_element_type=F32)

    _flash_pipeline(n_groups, qi, tq, scores, lambda g, j: vt_ref[j, cols[g], :], s_scr, acc_scr,
                    split_heads=True)

    for g, c in enumerate(cols):
        heads = [acc_scr[g, h, :HEAD_DIM, :] * (1.0 / acc_scr[g, h, HEAD_DIM:HEAD_DIM + 1, :])
                 for h in range(2)]
        o_ref[:, c] = jnp.concatenate(heads, axis=0).T.astype(o_ref.dtype)


def _moba_attention(zq, zkv, vt, kmean, *, batch, seq, n_groups):
    t = zq.shape[0]
    tq = MOBA_BLOCK
    nq = seq // tq
    w = n_groups * LANES
    n_steps = N_SELF_GROUPS // n_groups
    return pl.pallas_call(
        _moba_attn_kernel,
        out_shape=jax.ShapeDtypeStruct((t, SELF_WIDTH), BF16),
        grid=(batch, n_steps, nq),
        in_specs=[
            pl.BlockSpec((tq, w), lambda b, h, i: (b * nq + i, h)),
            pl.BlockSpec((seq, w), lambda b, h, i: (b, h)),
            pl.BlockSpec((seq // KV_CHUNK, w, KV_CHUNK), lambda b, h, i: (b, h, 0)),
            pl.BlockSpec((None, nq, w), lambda b, h, i: (b, 0, h)),
        ],
        out_specs=pl.BlockSpec((tq, w), lambda b, h, i: (b * nq + i, h)),
        scratch_shapes=[pltpu.VMEM((2, n_groups, tq, 2 * tq), F32),
                        pltpu.VMEM((n_groups, 2, HEAD_DIM + SUM_ROWS, tq), F32)],
        compiler_params=_params("parallel", "parallel", "arbitrary"),
        name="moba_attn",
    )(zq, zkv, vt, kmean)


def _attn_out_kernel(x_ref, os_ref, qm_ref, mkv_ref, w_ref, o_ref):
    tm = x_ref.shape[0]
    qm = qm_ref[...]
    mk = mkv_ref[:, :MEM_WIDTH]
    mv = mkv_ref[:, MEM_WIDTH:]
    lane = lax.broadcasted_iota(jnp.int32, (tm, MEM_WIDTH), 1)
    o_mem = jnp.zeros((tm, MEM_WIDTH), F32)
    for h in range(N_MEM_HEADS):
        in_head = (lane >= h * HEAD_DIM) & (lane < (h + 1) * HEAD_DIM)
        qh = jnp.where(in_head, qm, jnp.zeros_like(qm))
        s = lax.dot_general(qh, mk, _NT, preferred_element_type=F32)
        e = jnp.exp(s - jnp.max(s, axis=1, keepdims=True))
        p = e / jnp.sum(e, axis=1, keepdims=True)
        oh = jnp.dot(p.astype(BF16), mv, preferred_element_type=F32)
        o_mem = jnp.where(in_head, oh, o_mem)
    y = jnp.dot(os_ref[...], w_ref[:SELF_WIDTH, :], preferred_element_type=F32)
    y = y + jnp.dot(o_mem.astype(BF16), w_ref[SELF_WIDTH:, :], preferred_element_type=F32)
    o_ref[...] = x_ref[...] + y


def _attn_out(x2d, o_self, zq, qm_block, memkv, w_out, *, seq, mem_len, tm, name):
    t, d = x2d.shape
    per_batch = seq // tm
    return pl.pallas_call(
        _attn_out_kernel,
        out_shape=jax.ShapeDtypeStruct((t, d), F32),
        grid=(t // tm,),
        in_specs=[
            pl.BlockSpec((tm, d), lambda i: (i, 0)),
            pl.BlockSpec((tm, SELF_WIDTH), lambda i: (i, 0)),
            pl.BlockSpec((tm, MEM_WIDTH), lambda i: (i, qm_block)),
            pl.BlockSpec((mem_len, 2 * MEM_WIDTH), lambda i: (i // per_batch, 0)),
            pl.BlockSpec((d, d), lambda i: (0, 0)),
        ],
        out_specs=pl.BlockSpec((tm, d), lambda i: (i, 0)),
        compiler_params=_params("parallel"),
        name=name,
    )(x2d, o_self, zq, memkv, w_out)


def _mlp_kernel(x_ref, g_ref, wu_ref, wd_ref, gf_ref, o_ref, h_scr, acc_scr, *, final_norm):
    j = pl.program_id(1)

    @pl.when(j == 0)
    def _():
        x = x_ref[...]
        ms = jnp.mean(x * x, axis=-1, keepdims=True)
        h_scr[...] = (x * lax.rsqrt(ms + EPS) * g_ref[...]).astype(BF16)
        acc_scr[...] = jnp.zeros(acc_scr.shape, F32)

    u = jnp.maximum(jnp.dot(h_scr[...], wu_ref[...], preferred_element_type=F32), 0.0)
    acc_scr[...] += jnp.dot((u * u).astype(BF16), wd_ref[...], preferred_element_type=F32)

    @pl.when(j == pl.num_programs(1) - 1)
    def _():
        y = x_ref[...] + acc_scr[...]
        if final_norm:
            ms = jnp.mean(y * y, axis=-1, keepdims=True)
            y = y * lax.rsqrt(ms + EPS) * gf_ref[...]
        o_ref[...] = y


def _mlp(x2d, g, w_up, w_down, g_final, *, final_norm, tm, tf, name):
    t, d = x2d.shape
    dff = w_up.shape[1]
    kern = functools.partial(_mlp_kernel, final_norm=final_norm)
    return pl.pallas_call(
        kern,
        out_shape=jax.ShapeDtypeStruct((t, d), F32),
        grid=(t // tm, dff // tf),
        in_specs=[
            pl.BlockSpec((tm, d), lambda i, j: (i, 0)),
            pl.BlockSpec((1, d), lambda i, j: (0, 0)),
            pl.BlockSpec((d, tf), lambda i, j: (0, j)),
            pl.BlockSpec((tf, d), lambda i, j: (j, 0)),
            pl.BlockSpec((1, d), lambda i, j: (0, 0)),
        ],
        out_specs=pl.BlockSpec((tm, d), lambda i, j: (i, 0)),
        scratch_shapes=[pltpu.VMEM((tm, d), BF16), pltpu.VMEM((tm, d), F32)],
        compiler_params=_params("parallel", "arbitrary"),
        name=name,
    )(x2d, g.reshape(1, d), w_up, w_down, g_final.reshape(1, d))


def _rope_tables(seq):
    half = HEAD_DIM // 2
    inv = 1.0 / (ROPE_THETA ** (jnp.arange(half, dtype=F32) / half))
    ang = jnp.arange(seq, dtype=F32)[:, None] * inv[None, :]
    cos, sin = jnp.cos(ang), jnp.sin(ang)
    reps = LANES // half
    cos_t = jnp.tile(cos, (1, reps))
    sign = jnp.tile(jnp.concatenate([-jnp.ones((half,), F32), jnp.ones((half,), F32)]), LANES // HEAD_DIM)
    sin_t = jnp.tile(sin, (1, reps)) * sign[None, :]
    return cos_t, sin_t


def kernel(x, mem, a_norm_attn, a_w_in, a_lambda, a_subln, a_mem_norm, a_w_mem_kv, a_w_out, a_norm_mlp, a_w_up, a_w_down, kv_norm, w_kv, b_norm_attn, b_w_in, b_mem_norm, b_w_mem_kv, b_w_out, b_norm_mlp, b_w_up, b_w_down, final_norm):
    batch, seq, d = x.shape
    mem_len = mem.shape[1]
    t = batch * seq
    x2d = x.reshape(t, d)
    mem2d = mem.reshape(batch * mem_len, d)
    cos_t, sin_t = _rope_tables(seq)
    bf = lambda w: w.astype(BF16)
    tm = 512
    ng = N_SELF_GROUPS
    self_q = {j: QK_SCALE * LOG2E for j in range(ng)}

    lam_init = 0.8 - 0.6 * math.exp(-0.3 * 0)
    za, vt_a = _proj(x2d, a_norm_attn[0], bf(a_w_in[0]), cos_t, sin_t, n_rope=2 * ng,
                     scales={**self_q, 3 * ng: QK_SCALE, 3 * ng + 1: QK_SCALE},
                     with_kmean=False, vt_start=2 * ng, seq=seq, tm=tm, name="a_proj")
    mkv_a, = _proj(mem2d, a_mem_norm[0], bf(a_w_mem_kv[0]), cos_t, sin_t, n_rope=0, scales={},
                   with_kmean=False, vt_start=None, seq=mem_len, tm=mem_len, name="a_memkv")
    o_self = _diff_attention(za, vt_a, a_lambda[0], a_subln[0], batch=batch, seq=seq,
                             n_groups=2, lam_init=lam_init)
    x2d = _attn_out(x2d, o_self, za, 3 * SELF_WIDTH // MEM_WIDTH, mkv_a, bf(a_w_out[0]),
                    seq=seq, mem_len=mem_len, tm=tm, name="a_attn_out")
    x2d = _mlp(x2d, a_norm_mlp[0], bf(a_w_up[0]), bf(a_w_down[0]), final_norm,
               final_norm=False, tm=1024, tf=512, name="a_mlp")

    zkv, kmean, vt_b = _proj(x2d, kv_norm, bf(w_kv), cos_t, sin_t, n_rope=ng, scales={},
                             with_kmean=True, vt_start=ng, seq=seq, tm=tm, name="b_kvproj")
    kmean = kmean.reshape(batch, seq // MOBA_BLOCK, SELF_WIDTH)
    zb, = _proj(x2d, b_norm_attn[0], bf(b_w_in[0]), cos_t, sin_t, n_rope=ng,
                scales={**self_q, ng: QK_SCALE, ng + 1: QK_SCALE}, with_kmean=False, vt_start=None,
                seq=seq, tm=tm, name="b_qproj")
    mkv_b, = _proj(mem2d, b_mem_norm[0], bf(b_w_mem_kv[0]), cos_t, sin_t, n_rope=0, scales={},
                   with_kmean=False, vt_start=None, seq=mem_len, tm=mem_len, name="b_memkv")
    o_self = _moba_attention(zb, zkv, vt_b, kmean, batch=batch, seq=seq, n_groups=2)
    x2d = _attn_out(x2d, o_self, zb, SELF_WIDTH // MEM_WIDTH, mkv_b, bf(b_w_out[0]),
                    seq=seq, mem_len=mem_len, tm=tm, name="b_attn_out")
    x2d = _mlp(x2d, b_norm_mlp[0], bf(b_w_up[0]), bf(b_w_down[0]), final_norm,
               final_norm=True, tm=1024, tf=512, name="b_mlp")
    return x2d.reshape(batch, seq, d)
```

```python
import functools
import math

import jax
import jax.numpy as jnp
from jax import lax
from jax.experimental import pallas as pl
from jax.experimental.pallas import tpu as pltpu

D_MODEL = 1024
HEAD_DIM = 64
SELF_WIDTH = 768
MEM_WIDTH = 256
N_MEM_HEADS = 4
D_FF = 4096
MOBA_BLOCK = 256
MOBA_TOPK = 3
ROPE_THETA = 10000.0
EPS = 1e-6
NEG = -1e30

LANES = 128
N_SELF_GROUPS = SELF_WIDTH // LANES
QK_SCALE = HEAD_DIM ** -0.5
LOG2E = math.log2(math.e)
KV_CHUNK = 256
SUM_ROWS = 16

F32 = jnp.float32
BF16 = jnp.bfloat16

_VMEM_LIMIT = 56 * 1024 * 1024


def _params(*sem):
    return pltpu.CompilerParams(dimension_semantics=sem, vmem_limit_bytes=_VMEM_LIMIT)


def _proj_kernel(x_ref, g_ref, w_ref, cos_ref, sin_ref, o_ref, *extra_refs,
                 n_rope, scales, col_chunk, with_kmean, vt_start):
    tm = x_ref.shape[0]
    n_out = w_ref.shape[1]
    extra = list(extra_refs)
    km_ref = extra.pop(0) if with_kmean else None
    vt_ref = extra.pop(0) if vt_start is not None else None
    x = x_ref[...]
    ms = jnp.mean(x * x, axis=-1, keepdims=True)
    h = (x * lax.rsqrt(ms + EPS) * g_ref[...]).astype(BF16)
    lane = lax.broadcasted_iota(jnp.int32, (tm, LANES), 1)
    first_half = (lane & (HEAD_DIM - 1)) < HEAD_DIM // 2
    if n_rope:
        cos = cos_ref[...]
        sin = sin_ref[...]
    for c0 in range(0, n_out, col_chunk):
        z = jnp.dot(h, w_ref[:, c0:c0 + col_chunk], preferred_element_type=F32)
        for jj in range(col_chunk // LANES):
            j = c0 // LANES + jj
            blk = z[:, jj * LANES:(jj + 1) * LANES]
            if j < n_rope:
                swap = jnp.where(first_half, pltpu.roll(blk, LANES - 32, 1), pltpu.roll(blk, 32, 1))
                blk = blk * cos + swap * sin
                if km_ref is not None:
                    km_ref[0, :, j * LANES:(j + 1) * LANES] = jnp.mean(
                        blk.reshape(tm // MOBA_BLOCK, MOBA_BLOCK, LANES), axis=1)
            if j in scales:
                blk = blk * scales[j]
            o_ref[:, j * LANES:(j + 1) * LANES] = blk.astype(o_ref.dtype)
            if vt_ref is not None and vt_start <= j < vt_start + N_SELF_GROUPS:
                g = j - vt_start
                for c in range(tm // KV_CHUNK):
                    vt_ref[c, g * LANES:(g + 1) * LANES, :] = (
                        blk[c * KV_CHUNK:(c + 1) * KV_CHUNK, :].T.astype(BF16))


def _proj(x2d, g, w, cos_t, sin_t, *, n_rope, scales, with_kmean, vt_start, seq, tm, name):
    t, d = x2d.shape
    n_out = w.shape[1]
    n_pos_blocks = seq // tm
    col_chunk = 512 if n_out % 512 == 0 else 256
    out_shape = [jax.ShapeDtypeStruct((t, n_out), BF16)]
    out_specs = [pl.BlockSpec((tm, n_out), lambda i: (i, 0))]
    if with_kmean:
        out_shape.append(jax.ShapeDtypeStruct((t // tm, tm // MOBA_BLOCK, n_rope * LANES), F32))
        out_specs.append(pl.BlockSpec((1, tm // MOBA_BLOCK, n_rope * LANES), lambda i: (i, 0, 0)))
    if vt_start is not None:
        out_shape.append(jax.ShapeDtypeStruct((t // KV_CHUNK, SELF_WIDTH, KV_CHUNK), BF16))
        out_specs.append(pl.BlockSpec((tm // KV_CHUNK, SELF_WIDTH, KV_CHUNK), lambda i: (i, 0, 0)))
    kern = functools.partial(_proj_kernel, n_rope=n_rope, scales=dict(scales), col_chunk=col_chunk,
                             with_kmean=with_kmean, vt_start=vt_start)
    return pl.pallas_call(
        kern,
        out_shape=out_shape,
        grid=(t // tm,),
        in_specs=[
            pl.BlockSpec((tm, d), lambda i: (i, 0)),
            pl.BlockSpec((1, d), lambda i: (0, 0)),
            pl.BlockSpec((d, n_out), lambda i: (0, 0)),
            pl.BlockSpec((tm, LANES), lambda i: (i % n_pos_blocks, 0)),
            pl.BlockSpec((tm, LANES), lambda i: (i % n_pos_blocks, 0)),
        ],
        out_specs=out_specs,
        compiler_params=_params("parallel"),
        name=name,
    )(x2d, g.reshape(1, d), w, cos_t, sin_t)


_NT = (((1,), (1,)), ((), ()))


def _stack_maps(q):
    tq = q.shape[0]
    lane = lax.broadcasted_iota(jnp.int32, (tq, LANES), 1)
    zero = jnp.zeros_like(q)
    return jnp.concatenate(
        [jnp.where(lane < HEAD_DIM, q, zero), jnp.where(lane >= HEAD_DIM, q, zero)], axis=0)


def _causal_mask(st, tq):
    key = lax.broadcasted_iota(jnp.int32, st.shape, 0)
    qry = lax.broadcasted_iota(jnp.int32, st.shape, 1) & (tq - 1)
    return jnp.where(key <= qry, st, NEG)


def _flash_pipeline(n_groups, n_past, tq, scores_fn, vt_fn, s_scr, acc_scr, split_heads):
    n = 2 * tq

    def produce(j, buf):
        col_max = []
        for g in range(n_groups):
            st = scores_fn(g, j)
            s_scr[buf, g] = st
            col_max.append(jnp.max(st, axis=0, keepdims=True))
        return tuple(col_max)

    ones_rows = jnp.ones((SUM_ROWS, tq), BF16)

    def consume(j, buf, stats, col_max, diagonal):
        out = []
        for g in range(n_groups):
            st = s_scr[buf, g]
            if diagonal:
                st = _causal_mask(st, tq)
                m_cur = jnp.max(st, axis=0, keepdims=True)
            else:
                m_cur = col_max[g]
            m_next = jnp.maximum(stats[g], m_cur)
            alpha = jnp.exp2(stats[g] - m_next)
            p = jnp.exp2((st - m_next).astype(BF16))
            vt = vt_fn(g, j)
            if split_heads:
                for h in range(2):
                    vt_h = jnp.concatenate([vt[h * HEAD_DIM:(h + 1) * HEAD_DIM], ones_rows], axis=0)
                    lanes = slice(h * tq, (h + 1) * tq)
                    acc_scr[g, h] = alpha[:, lanes] * acc_scr[g, h] + jnp.dot(
                        vt_h, p[:, lanes], preferred_element_type=F32)
            else:
                acc_scr[g] = alpha * acc_scr[g] + jnp.dot(
                    jnp.concatenate([vt, ones_rows], axis=0), p, preferred_element_type=F32)
            out.append(m_next)
        return tuple(out)

    def step(j, buf, carry):
        nxt = produce(j + 1, 1 - buf)
        return consume(j, buf, carry[0], carry[1], False), nxt

    def pair(i, carry):
        return step(2 * i + 1, 1, step(2 * i, 0, carry))

    def quad(i, carry):
        return pair(2 * i + 1, pair(2 * i, carry))

    acc_scr[...] = jnp.zeros(acc_scr.shape, F32)
    init = tuple(jnp.full((1, n), NEG, F32) for _ in range(n_groups))
    n_quads = n_past // 4
    carry = lax.fori_loop(0, n_quads, quad, (init, produce(0, 0)))
    carry = lax.fori_loop(2 * n_quads, n_past // 2, pair, carry)

    def odd_tail(carry):
        stats, col_max = step(n_past - 1, 0, carry)
        return consume(n_past, 1, stats, col_max, True)

    def even_tail(carry):
        return consume(n_past, 0, carry[0], carry[1], True)

    return lax.cond(n_past % 2 == 1, odd_tail, even_tail, carry)


def _diff_attn_kernel(lam_ref, g_ref, q_ref, k_ref, vt_ref, o_ref, s_scr, acc_scr, *, lam_init):
    tq = q_ref.shape[0]
    n_groups = q_ref.shape[1] // LANES
    qi = pl.program_id(2)
    cols = [slice(g * LANES, (g + 1) * LANES) for g in range(n_groups)]
    qs = [_stack_maps(q_ref[:, c]) for c in cols]

    def scores(g, j):
        start = pl.multiple_of(j * tq, tq)
        return lax.dot_general(k_ref[pl.ds(start, tq), cols[g]], qs[g], _NT,
                               preferred_element_type=F32)

    _flash_pipeline(n_groups, qi, tq, scores, lambda g, j: vt_ref[j, cols[g], :], s_scr, acc_scr,
                    split_heads=False)

    lp = lam_ref[...]
    lam = (jnp.exp(jnp.sum(lp[0:1] * lp[1:2], axis=1, keepdims=True))
           - jnp.exp(jnp.sum(lp[2:3] * lp[3:4], axis=1, keepdims=True)) + lam_init)
    for g, c in enumerate(cols):
        o_t = acc_scr[g, :LANES, :] * (1.0 / acc_scr[g, LANES:LANES + 1, :])
        od = (o_t[:, :tq] - lam * o_t[:, tq:]).T
        ms = jnp.mean(od * od, axis=-1, keepdims=True)
        y = od * lax.rsqrt(ms + EPS) * g_ref[...]
        o_ref[:, c] = (y * (1.0 - lam_init)).astype(o_ref.dtype)


def _diff_attention(z, vt, lam_p, subln, *, batch, seq, n_groups, lam_init):
    t = z.shape[0]
    tq = KV_CHUNK
    nq = seq // tq
    w = n_groups * LANES
    n_steps = N_SELF_GROUPS // n_groups
    kern = functools.partial(_diff_attn_kernel, lam_init=lam_init)
    return pl.pallas_call(
        kern,
        out_shape=jax.ShapeDtypeStruct((t, SELF_WIDTH), BF16),
        grid=(batch, n_steps, nq),
        in_specs=[
            pl.BlockSpec((4, HEAD_DIM), lambda b, h, i: (0, 0)),
            pl.BlockSpec((1, LANES), lambda b, h, i: (0, 0)),
            pl.BlockSpec((tq, w), lambda b, h, i: (b * nq + i, h)),
            pl.BlockSpec((seq, w), lambda b, h, i: (b, n_steps + h)),
            pl.BlockSpec((seq // KV_CHUNK, w, KV_CHUNK), lambda b, h, i: (b, h, 0)),
        ],
        out_specs=pl.BlockSpec((tq, w), lambda b, h, i: (b * nq + i, h)),
        scratch_shapes=[pltpu.VMEM((2, n_groups, tq, 2 * tq), F32),
                        pltpu.VMEM((n_groups, LANES + SUM_ROWS, 2 * tq), F32)],
        compiler_params=_params("parallel", "parallel", "arbitrary"),
        name="diff_attn",
    )(lam_p, subln.reshape(1, LANES), z, z, vt)


def _moba_select_bias(qs, km, qi):
    n_blk = km.shape[0]
    parts, rest = [], km
    for _ in range(3):
        part = rest.astype(BF16)
        parts.append(part)
        rest = rest - part.astype(F32)
    terms = lax.dot_general(jnp.concatenate(parts, axis=0), qs, _NT, preferred_element_type=F32)
    gate = (terms[2 * n_blk:] + terms[n_blk:2 * n_blk]) + terms[:n_blk]
    blk = lax.broadcasted_iota(jnp.int32, gate.shape, 0)
    past = blk < qi
    gm = jnp.where(past, gate, NEG)
    sel = jnp.zeros(gate.shape, F32)
    for _ in range(MOBA_TOPK):
        mx = jnp.max(gm, axis=0, keepdims=True)
        first = jnp.min(jnp.where(gm == mx, blk, n_blk), axis=0, keepdims=True)
        pick = blk == first
        sel = jnp.where(pick, 1.0, sel)
        gm = jnp.where(pick, -jnp.inf, gm)
    bias_t = jnp.where((past & (sel > 0.5)) | (blk == qi), 0.0, NEG)
    bias_t = jnp.concatenate([bias_t, jnp.zeros((LANES - n_blk, gate.shape[1]), F32)], axis=0)
    return bias_t.T.astype(BF16)


def _moba_attn_kernel(q_ref, k_ref, vt_ref, km_ref, o_ref, s_scr, acc_scr):
    tq = q_ref.shape[0]
    n_groups = q_ref.shape[1] // LANES
    qi = pl.program_id(2)
    cols = [slice(g * LANES, (g + 1) * LANES) for g in range(n_groups)]
    qs_aug = []
    for c in cols:
        qs = _stack_maps(q_ref[:, c])
        qs_aug.append(jnp.concatenate([qs, _moba_select_bias(qs, km_ref[:, c], qi)], axis=1))

    def scores(g, j):
        start = pl.multiple_of(j * tq, tq)
        k = k_ref[pl.ds(start, tq), cols[g]]
        hot = jnp.where(lax.broadcasted_iota(jnp.int32, k.shape, 1) == j, 1.0, 0.0).astype(BF16)
        return lax.dot_general(jnp.concatenate([k, hot], axis=1), qs_aug[g], _NT,
                               preferred_element_type=F32)

    _flash_pipeline(n_groups, qi, tq, scores, lambda g, j: vt_ref[j, cols[g], :], s_scr, acc_scr,
                    split_heads=True)

    for g, c in enumerate(cols):
        heads = [acc_scr[g, h, :HEAD_DIM, :] * (1.0 / acc_scr[g, h, HEAD_DIM:HEAD_DIM + 1, :])
                 for h in range(2)]
        o_ref[:, c] = jnp.concatenate(heads, axis=0).T.astype(o_ref.dtype)


def _moba_attention(zq, zkv, vt, kmean, *, batch, seq, n_groups):
    t = zq.shape[0]
    tq = MOBA_BLOCK
    nq = seq // tq
    w = n_groups * LANES
    n_steps = N_SELF_GROUPS // n_groups
    return pl.pallas_call(
        _moba_attn_kernel,
        out_shape=jax.ShapeDtypeStruct((t, SELF_WIDTH), BF16),
        grid=(batch, n_steps, nq),
        in_specs=[
            pl.BlockSpec((tq, w), lambda b, h, i: (b * nq + i, h)),
            pl.BlockSpec((seq, w), lambda b, h, i: (b, h)),
            pl.BlockSpec((seq // KV_CHUNK, w, KV_CHUNK), lambda b, h, i: (b, h, 0)),
            pl.BlockSpec((None, nq, w), lambda b, h, i: (b, 0, h)),
        ],
        out_specs=pl.BlockSpec((tq, w), lambda b, h, i: (b * nq + i, h)),
        scratch_shapes=[pltpu.VMEM((2, n_groups, tq, 2 * tq), F32),
                        pltpu.VMEM((n_groups, 2, HEAD_DIM + SUM_ROWS, tq), F32)],
        compiler_params=_params("parallel", "parallel", "arbitrary"),
        name="moba_attn",
    )(zq, zkv, vt, kmean)


def _attn_out_kernel(x_ref, os_ref, qm_ref, mkv_ref, w_ref, o_ref):
    tm = x_ref.shape[0]
    qm = qm_ref[...]
    mk = mkv_ref[:, :MEM_WIDTH]
    mv = mkv_ref[:, MEM_WIDTH:]
    lane = lax.broadcasted_iota(jnp.int32, (tm, MEM_WIDTH), 1)
    o_mem = jnp.zeros((tm, MEM_WIDTH), F32)
    for h in range(N_MEM_HEADS):
        in_head = (lane >= h * HEAD_DIM) & (lane < (h + 1) * HEAD_DIM)
        qh = jnp.where(in_head, qm, jnp.zeros_like(qm))
        s = lax.dot_general(qh, mk, _NT, preferred_element_type=F32)
        e = jnp.exp(s - jnp.max(s, axis=1, keepdims=True))
        p = e / jnp.sum(e, axis=1, keepdims=True)
        oh = jnp.dot(p.astype(BF16), mv, preferred_element_type=F32)
        o_mem = jnp.where(in_head, oh, o_mem)
    y = jnp.dot(os_ref[...], w_ref[:SELF_WIDTH, :], preferred_element_type=F32)
    y = y + jnp.dot(o_mem.astype(BF16), w_ref[SELF_WIDTH:, :], preferred_element_type=F32)
    o_ref[...] = x_ref[...] + y


def _attn_out(x2d, o_self, zq, qm_block, memkv, w_out, *, seq, mem_len, tm, name):
    t, d = x2d.shape
    per_batch = seq // tm
    return pl.pallas_call(
        _attn_out_kernel,
        out_shape=jax.ShapeDtypeStruct((t, d), F32),
        grid=(t // tm,),
        in_specs=[
            pl.BlockSpec((tm, d), lambda i: (i, 0)),
            pl.BlockSpec((tm, SELF_WIDTH), lambda i: (i, 0)),
            pl.BlockSpec((tm, MEM_WIDTH), lambda i: (i, qm_block)),
            pl.BlockSpec((mem_len, 2 * MEM_WIDTH), lambda i: (i // per_batch, 0)),
            pl.BlockSpec((d, d), lambda i: (0, 0)),
        ],
        out_specs=pl.BlockSpec((tm, d), lambda i: (i, 0)),
        compiler_params=_params("parallel"),
        name=name,
    )(x2d, o_self, zq, memkv, w_out)


def _mlp_kernel(x_ref, g_ref, wu_ref, wd_ref, gf_ref, o_ref, h_scr, acc_scr, *, final_norm):
    j = pl.program_id(1)

    @pl.when(j == 0)
    def _():
        x = x_ref[...]
        ms = jnp.mean(x * x, axis=-1, keepdims=True)
        h_scr[...] = (x * lax.rsqrt(ms + EPS) * g_ref[...]).astype(BF16)
        acc_scr[...] = jnp.zeros(acc_scr.shape, F32)

    u = jnp.maximum(jnp.dot(h_scr[...], wu_ref[...], preferred_element_type=F32), 0.0)
    acc_scr[...] += jnp.dot((u * u).astype(BF16), wd_ref[...], preferred_element_type=F32)

    @pl.when(j == pl.num_programs(1) - 1)
    def _():
        y = x_ref[...] + acc_scr[...]
        if final_norm:
            ms = jnp.mean(y * y, axis=-1, keepdims=True)
            y = y * lax.rsqrt(ms + EPS) * gf_ref[...]
        o_ref[...] = y


def _mlp(x2d, g, w_up, w_down, g_final, *, final_norm, tm, tf, name):
    t, d = x2d.shape
    dff = w_up.shape[1]
    kern = functools.partial(_mlp_kernel, final_norm=final_norm)
    return pl.pallas_call(
        kern,
        out_shape=jax.ShapeDtypeStruct((t, d), F32),
        grid=(t // tm, dff // tf),
        in_specs=[
            pl.BlockSpec((tm, d), lambda i, j: (i, 0)),
            pl.BlockSpec((1, d), lambda i, j: (0, 0)),
            pl.BlockSpec((d, tf), lambda i, j: (0, j)),
            pl.BlockSpec((tf, d), lambda i, j: (j, 0)),
            pl.BlockSpec((1, d), lambda i, j: (0, 0)),
        ],
        out_specs=pl.BlockSpec((tm, d), lambda i, j: (i, 0)),
        scratch_shapes=[pltpu.VMEM((tm, d), BF16), pltpu.VMEM((tm, d), F32)],
        compiler_params=_params("parallel", "arbitrary"),
        name=name,
    )(x2d, g.reshape(1, d), w_up, w_down, g_final.reshape(1, d))


def _rope_tables(seq):
    half = HEAD_DIM // 2
    inv = 1.0 / (ROPE_THETA ** (jnp.arange(half, dtype=F32) / half))
    ang = jnp.arange(seq, dtype=F32)[:, None] * inv[None, :]
    cos, sin = jnp.cos(ang), jnp.sin(ang)
    reps = LANES // half
    cos_t = jnp.tile(cos, (1, reps))
    sign = jnp.tile(jnp.concatenate([-jnp.ones((half,), F32), jnp.ones((half,), F32)]), LANES // HEAD_DIM)
    sin_t = jnp.tile(sin, (1, reps)) * sign[None, :]
    return cos_t, sin_t


def kernel(x, mem, a_norm_attn, a_w_in, a_lambda, a_subln, a_mem_norm, a_w_mem_kv, a_w_out, a_norm_mlp, a_w_up, a_w_down, kv_norm, w_kv, b_norm_attn, b_w_in, b_mem_norm, b_w_mem_kv, b_w_out, b_norm_mlp, b_w_up, b_w_down, final_norm):
    batch, seq, d = x.shape
    mem_len = mem.shape[1]
    t = batch * seq
    x2d = x.reshape(t, d)
    mem2d = mem.reshape(batch * mem_len, d)
    cos_t, sin_t = _rope_tables(seq)
    bf = lambda w: w.astype(BF16)
    tm = 512
    ng = N_SELF_GROUPS
    self_q = {j: QK_SCALE * LOG2E for j in range(ng)}

    lam_init = 0.8 - 0.6 * math.exp(-0.3 * 0)
    za, vt_a = _proj(x2d, a_norm_attn[0], bf(a_w_in[0]), cos_t, sin_t, n_rope=2 * ng,
                     scales={**self_q, 3 * ng: QK_SCALE, 3 * ng + 1: QK_SCALE},
                     with_kmean=False, vt_start=2 * ng, seq=seq, tm=tm, name="a_proj")
    mkv_a, = _proj(mem2d, a_mem_norm[0], bf(a_w_mem_kv[0]), cos_t, sin_t, n_rope=0, scales={},
                   with_kmean=False, vt_start=None, seq=mem_len, tm=mem_len, name="a_memkv")
    o_self = _diff_attention(za, vt_a, a_lambda[0], a_subln[0], batch=batch, seq=seq,
                             n_groups=2, lam_init=lam_init)
    x2d = _attn_out(x2d, o_self, za, 3 * SELF_WIDTH // MEM_WIDTH, mkv_a, bf(a_w_out[0]),
                    seq=seq, mem_len=mem_len, tm=tm, name="a_attn_out")
    x2d = _mlp(x2d, a_norm_mlp[0], bf(a_w_up[0]), bf(a_w_down[0]), final_norm,
               final_norm=False, tm=1024, tf=512, name="a_mlp")

    zkv, kmean, vt_b = _proj(x2d, kv_norm, bf(w_kv), cos_t, sin_t, n_rope=ng, scales={},
                             with_kmean=True, vt_start=ng, seq=seq, tm=tm, name="b_kvproj")
    kmean = kmean.reshape(batch, seq // MOBA_BLOCK, SELF_WIDTH)
    zb, = _proj(x2d, b_norm_attn[0], bf(b_w_in[0]), cos_t, sin_t, n_rope=ng,
                scales={**self_q, ng: QK_SCALE, ng + 1: QK_SCALE}, with_kmean=False, vt_start=None,
                seq=seq, tm=tm, name="b_qproj")
    mkv_b, = _proj(mem2d, b_mem_norm[0], bf(b_w_mem_kv[0]), cos_t, sin_t, n_rope=0, scales={},
                   with_kmean=False, vt_start=None, seq=mem_len, tm=mem_len, name="b_memkv")
    o_self = _moba_attention(zb, zkv, vt_b, kmean, batch=batch, seq=seq, n_groups=2)
    x2d = _attn_out(x2d, o_self, zb, SELF_WIDTH // MEM_WIDTH, mkv_b, bf(b_w_out[0]),
                    seq=seq, mem_len=mem_len, tm=tm, name="b_attn_out")
    x2d = _mlp(x2d, b_norm_mlp[0], bf(b_w_up[0]), bf(b_w_down[0]), final_norm,
               final_norm=True, tm=1024, tf=512, name="b_mlp")
    return x2d.reshape(batch, seq, d)
```

```python
import functools
import math

import jax
import jax.numpy as jnp
from jax import lax
from jax.experimental import pallas as pl
from jax.experimental.pallas import tpu as pltpu

D_MODEL = 1024
HEAD_DIM = 64
SELF_WIDTH = 768
MEM_WIDTH = 256
N_MEM_HEADS = 4
D_FF = 4096
MOBA_BLOCK = 256
MOBA_TOPK = 3
ROPE_THETA = 10000.0
EPS = 1e-6
NEG = -1e30

LANES = 128
N_SELF_GROUPS = SELF_WIDTH // LANES
QK_SCALE = HEAD_DIM ** -0.5
LOG2E = math.log2(math.e)
KV_CHUNK = 256
SUM_ROWS = 16

F32 = jnp.float32
BF16 = jnp.bfloat16

_VMEM_LIMIT = 56 * 1024 * 1024


def _params(*sem):
    return pltpu.CompilerParams(dimension_semantics=sem, vmem_limit_bytes=_VMEM_LIMIT)


def _proj_kernel(x_ref, g_ref, w_ref, cos_ref, sin_ref, o_ref, *extra_refs,
                 n_rope, scales, col_chunk, with_kmean, vt_start):
    tm = x_ref.shape[0]
    n_out = w_ref.shape[1]
    extra = list(extra_refs)
    km_ref = extra.pop(0) if with_kmean else None
    vt_ref = extra.pop(0) if vt_start is not None else None
    x = x_ref[...]
    ms = jnp.mean(x * x, axis=-1, keepdims=True)
    h = (x * lax.rsqrt(ms + EPS) * g_ref[...]).astype(BF16)
    lane = lax.broadcasted_iota(jnp.int32, (tm, LANES), 1)
    first_half = (lane & (HEAD_DIM - 1)) < HEAD_DIM // 2
    if n_rope:
        cos = cos_ref[...]
        sin = sin_ref[...]
    for c0 in range(0, n_out, col_chunk):
        z = jnp.dot(h, w_ref[:, c0:c0 + col_chunk], preferred_element_type=F32)
        for jj in range(col_chunk // LANES):
            j = c0 // LANES + jj
            blk = z[:, jj * LANES:(jj + 1) * LANES]
            if j < n_rope:
                swap = jnp.where(first_half, pltpu.roll(blk, LANES - 32, 1), pltpu.roll(blk, 32, 1))
                blk = blk * cos + swap * sin
                if km_ref is not None:
                    km_ref[0, :, j * LANES:(j + 1) * LANES] = jnp.mean(
                        blk.reshape(tm // MOBA_BLOCK, MOBA_BLOCK, LANES), axis=1)
            if j in scales:
                blk = blk * scales[j]
            o_ref[:, j * LANES:(j + 1) * LANES] = blk.astype(o_ref.dtype)
            if vt_ref is not None and vt_start <= j < vt_start + N_SELF_GROUPS:
                g = j - vt_start
                for c in range(tm // KV_CHUNK):
                    vt_ref[c, g * LANES:(g + 1) * LANES, :] = (
                        blk[c * KV_CHUNK:(c + 1) * KV_CHUNK, :].T.astype(BF16))


def _proj(x2d, g, w, cos_t, sin_t, *, n_rope, scales, with_kmean, vt_start, seq, tm, name):
    t, d = x2d.shape
    n_out = w.shape[1]
    n_pos_blocks = seq // tm
    col_chunk = 512 if n_out % 512 == 0 else 256
    out_shape = [jax.ShapeDtypeStruct((t, n_out), BF16)]
    out_specs = [pl.BlockSpec((tm, n_out), lambda i: (i, 0))]
    if with_kmean:
        out_shape.append(jax.ShapeDtypeStruct((t // tm, tm // MOBA_BLOCK, n_rope * LANES), F32))
        out_specs.append(pl.BlockSpec((1, tm // MOBA_BLOCK, n_rope * LANES), lambda i: (i, 0, 0)))
    if vt_start is not None:
        out_shape.append(jax.ShapeDtypeStruct((t // KV_CHUNK, SELF_WIDTH, KV_CHUNK), BF16))
        out_specs.append(pl.BlockSpec((tm // KV_CHUNK, SELF_WIDTH, KV_CHUNK), lambda i: (i, 0, 0)))
    kern = functools.partial(_proj_kernel, n_rope=n_rope, scales=dict(scales), col_chunk=col_chunk,
                             with_kmean=with_kmean, vt_start=vt_start)
    return pl.pallas_call(
        kern,
        out_shape=out_shape,
        grid=(t // tm,),
        in_specs=[
            pl.BlockSpec((tm, d), lambda i: (i, 0)),
            pl.BlockSpec((1, d), lambda i: (0, 0)),
            pl.BlockSpec((d, n_out), lambda i: (0, 0)),
            pl.BlockSpec((tm, LANES), lambda i: (i % n_pos_blocks, 0)),
            pl.BlockSpec((tm, LANES), lambda i: (i % n_pos_blocks, 0)),
        ],
        out_specs=out_specs,
        compiler_params=_params("parallel"),
        name=name,
    )(x2d, g.reshape(1, d), w, cos_t, sin_t)


_NT = (((1,), (1,)), ((), ()))


def _stack_maps(q):
    tq = q.shape[0]
    lane = lax.broadcasted_iota(jnp.int32, (tq, LANES), 1)
    zero = jnp.zeros_like(q)
    return jnp.concatenate(
        [jnp.where(lane < HEAD_DIM, q, zero), jnp.where(lane >= HEAD_DIM, q, zero)], axis=0)


def _causal_mask(st, tq):
    key = lax.broadcasted_iota(jnp.int32, st.shape, 0)
    qry = lax.broadcasted_iota(jnp.int32, st.shape, 1) & (tq - 1)
    return jnp.where(key <= qry, st, NEG)


def _flash_pipeline(n_groups, n_past, tq, scores_fn, vt_fn, bias_fn, s_scr, acc_scr, split_heads):
    n = 2 * tq

    def produce(j, buf):
        col_max = []
        for g in range(n_groups):
            st = scores_fn(g, j)
            s_scr[buf, g] = st
            col_max.append(jnp.max(st, axis=0, keepdims=True))
        return tuple(col_max)

    ones_rows = jnp.ones((SUM_ROWS, tq), BF16)

    def consume(j, buf, stats, col_max, diagonal):
        out = []
        for g in range(n_groups):
            st = s_scr[buf, g]
            if diagonal:
                st = _causal_mask(st, tq)
                m_cur = jnp.max(st, axis=0, keepdims=True)
            else:
                m_cur = col_max[g]
            if bias_fn is not None:
                bias = bias_fn(g, j)
                m_cur = m_cur + bias
            m_next = jnp.maximum(stats[g], m_cur)
            alpha = jnp.exp2(stats[g] - m_next)
            shift = m_next if bias_fn is None else m_next - bias
            p = jnp.exp2((st - shift).astype(BF16))
            vt = vt_fn(g, j)
            if split_heads:
                for h in range(2):
                    vt_h = jnp.concatenate([vt[h * HEAD_DIM:(h + 1) * HEAD_DIM], ones_rows], axis=0)
                    lanes = slice(h * tq, (h + 1) * tq)
                    acc_scr[g, h] = alpha[:, lanes] * acc_scr[g, h] + jnp.dot(
                        vt_h, p[:, lanes], preferred_element_type=F32)
            else:
                acc_scr[g] = alpha * acc_scr[g] + jnp.dot(
                    jnp.concatenate([vt, ones_rows], axis=0), p, preferred_element_type=F32)
            out.append(m_next)
        return tuple(out)

    def step(j, buf, carry):
        nxt = produce(j + 1, 1 - buf)
        return consume(j, buf, carry[0], carry[1], False), nxt

    def pair(i, carry):
        return step(2 * i + 1, 1, step(2 * i, 0, carry))

    def quad(i, carry):
        return pair(2 * i + 1, pair(2 * i, carry))

    acc_scr[...] = jnp.zeros(acc_scr.shape, F32)
    init = tuple(jnp.full((1, n), NEG, F32) for _ in range(n_groups))
    n_quads = n_past // 4
    carry = lax.fori_loop(0, n_quads, quad, (init, produce(0, 0)))
    carry = lax.fori_loop(2 * n_quads, n_past // 2, pair, carry)

    def odd_tail(carry):
        stats, col_max = step(n_past - 1, 0, carry)
        return consume(n_past, 1, stats, col_max, True)

    def even_tail(carry):
        return consume(n_past, 0, carry[0], carry[1], True)

    return lax.cond(n_past % 2 == 1, odd_tail, even_tail, carry)


def _diff_attn_kernel(lam_ref, g_ref, q_ref, k_ref, vt_ref, o_ref, s_scr, acc_scr, *, lam_init):
    tq = q_ref.shape[0]
    n_groups = q_ref.shape[1] // LANES
    qi = pl.program_id(2)
    cols = [slice(g * LANES, (g + 1) * LANES) for g in range(n_groups)]
    qs = [_stack_maps(q_ref[:, c]) for c in cols]

    def scores(g, j):
        start = pl.multiple_of(j * tq, tq)
        return lax.dot_general(k_ref[pl.ds(start, tq), cols[g]], qs[g], _NT,
                               preferred_element_type=F32)

    _flash_pipeline(n_groups, qi, tq, scores, lambda g, j: vt_ref[j, cols[g], :], None,
                    s_scr, acc_scr, split_heads=False)

    lp = lam_ref[...]
    lam = (jnp.exp(jnp.sum(lp[0:1] * lp[1:2], axis=1, keepdims=True))
           - jnp.exp(jnp.sum(lp[2:3] * lp[3:4], axis=1, keepdims=True)) + lam_init)
    for g, c in enumerate(cols):
        o_t = acc_scr[g, :LANES, :] * (1.0 / acc_scr[g, LANES:LANES + 1, :])
        od = (o_t[:, :tq] - lam * o_t[:, tq:]).T
        ms = jnp.mean(od * od, axis=-1, keepdims=True)
        y = od * lax.rsqrt(ms + EPS) * g_ref[...]
        o_ref[:, c] = (y * (1.0 - lam_init)).astype(o_ref.dtype)


def _diff_attention(z, vt, lam_p, subln, *, batch, seq, n_groups, lam_init):
    t = z.shape[0]
    tq = KV_CHUNK
    nq = seq // tq
    w = n_groups * LANES
    n_steps = N_SELF_GROUPS // n_groups
    kern = functools.partial(_diff_attn_kernel, lam_init=lam_init)
    return pl.pallas_call(
        kern,
        out_shape=jax.ShapeDtypeStruct((t, SELF_WIDTH), BF16),
        grid=(batch, n_steps, nq),
        in_specs=[
            pl.BlockSpec((4, HEAD_DIM), lambda b, h, i: (0, 0)),
            pl.BlockSpec((1, LANES), lambda b, h, i: (0, 0)),
            pl.BlockSpec((tq, w), lambda b, h, i: (b * nq + i, h)),
            pl.BlockSpec((seq, w), lambda b, h, i: (b, n_steps + h)),
            pl.BlockSpec((seq // KV_CHUNK, w, KV_CHUNK), lambda b, h, i: (b, h, 0)),
        ],
        out_specs=pl.BlockSpec((tq, w), lambda b, h, i: (b * nq + i, h)),
        scratch_shapes=[pltpu.VMEM((2, n_groups, tq, 2 * tq), F32),
                        pltpu.VMEM((n_groups, LANES + SUM_ROWS, 2 * tq), F32)],
        compiler_params=_params("parallel", "parallel", "arbitrary"),
        name="diff_attn",
    )(lam_p, subln.reshape(1, LANES), z, z, vt)


def _moba_select_bias(qs, km, qi):
    n_blk = km.shape[0]
    parts, rest = [], km
    for _ in range(3):
        part = rest.astype(BF16)
        parts.append(part)
        rest = rest - part.astype(F32)
    terms = lax.dot_general(jnp.concatenate(parts, axis=0), qs, _NT, preferred_element_type=F32)
    gate = (terms[2 * n_blk:] + terms[n_blk:2 * n_blk]) + terms[:n_blk]
    blk = lax.broadcasted_iota(jnp.int32, gate.shape, 0)
    past = blk < qi
    gm = jnp.where(past, gate, NEG)
    sel = jnp.zeros(gate.shape, F32)
    for _ in range(MOBA_TOPK):
        mx = jnp.max(gm, axis=0, keepdims=True)
        first = jnp.min(jnp.where(gm == mx, blk, n_blk), axis=0, keepdims=True)
        pick = blk == first
        sel = jnp.where(pick, 1.0, sel)
        gm = jnp.where(pick, -jnp.inf, gm)
    return jnp.where((past & (sel > 0.5)) | (blk == qi), 0.0, 2.0 * NEG)


def _moba_attn_kernel(q_ref, k_ref, vt_ref, km_ref, o_ref, s_scr, acc_scr, bias_scr):
    tq = q_ref.shape[0]
    n_groups = q_ref.shape[1] // LANES
    qi = pl.program_id(2)
    cols = [slice(g * LANES, (g + 1) * LANES) for g in range(n_groups)]
    qs = [_stack_maps(q_ref[:, c]) for c in cols]
    for g, c in enumerate(cols):
        bias_scr[g] = _moba_select_bias(qs[g], km_ref[:, c], qi)

    def scores(g, j):
        start = pl.multiple_of(j * tq, tq)
        return lax.dot_general(k_ref[pl.ds(start, tq), cols[g]], qs[g], _NT,
                               preferred_element_type=F32)

    _flash_pipeline(n_groups, qi, tq, scores, lambda g, j: vt_ref[j, cols[g], :],
                    lambda g, j: bias_scr[g, pl.ds(j, 1), :], s_scr, acc_scr, split_heads=True)

    for g, c in enumerate(cols):
        heads = [acc_scr[g, h, :HEAD_DIM, :] * (1.0 / acc_scr[g, h, HEAD_DIM:HEAD_DIM + 1, :])
                 for h in range(2)]
        o_ref[:, c] = jnp.concatenate(heads, axis=0).T.astype(o_ref.dtype)


def _moba_attention(zq, zkv, vt, kmean, *, batch, seq, n_groups):
    t = zq.shape[0]
    tq = MOBA_BLOCK
    nq = seq // tq
    w = n_groups * LANES
    n_steps = N_SELF_GROUPS // n_groups
    return pl.pallas_call(
        _moba_attn_kernel,
        out_shape=jax.ShapeDtypeStruct((t, SELF_WIDTH), BF16),
        grid=(batch, n_steps, nq),
        in_specs=[
            pl.BlockSpec((tq, w), lambda b, h, i: (b * nq + i, h)),
            pl.BlockSpec((seq, w), lambda b, h, i: (b, h)),
            pl.BlockSpec((seq // KV_CHUNK, w, KV_CHUNK), lambda b, h, i: (b, h, 0)),
            pl.BlockSpec((None, nq, w), lambda b, h, i: (b, 0, h)),
        ],
        out_specs=pl.BlockSpec((tq, w), lambda b, h, i: (b * nq + i, h)),
        scratch_shapes=[pltpu.VMEM((2, n_groups, tq, 2 * tq), F32),
                        pltpu.VMEM((n_groups, 2, HEAD_DIM + SUM_ROWS, tq), F32),
                        pltpu.VMEM((n_groups, nq, 2 * tq), F32)],
        compiler_params=_params("parallel", "parallel", "arbitrary"),
        name="moba_attn",
    )(zq, zkv, vt, kmean)


def _attn_out_kernel(x_ref, os_ref, qm_ref, mkv_ref, w_ref, o_ref):
    tm = x_ref.shape[0]
    qm = qm_ref[...]
    mk = mkv_ref[:, :MEM_WIDTH]
    mv = mkv_ref[:, MEM_WIDTH:]
    lane = lax.broadcasted_iota(jnp.int32, (tm, MEM_WIDTH), 1)
    o_mem = jnp.zeros((tm, MEM_WIDTH), F32)
    for h in range(N_MEM_HEADS):
        in_head = (lane >= h * HEAD_DIM) & (lane < (h + 1) * HEAD_DIM)
        qh = jnp.where(in_head, qm, jnp.zeros_like(qm))
        s = lax.dot_general(qh, mk, _NT, preferred_element_type=F32)
        e = jnp.exp(s - jnp.max(s, axis=1, keepdims=True))
        p = e / jnp.sum(e, axis=1, keepdims=True)
        oh = jnp.dot(p.astype(BF16), mv, preferred_element_type=F32)
        o_mem = jnp.where(in_head, oh, o_mem)
    y = jnp.dot(os_ref[...], w_ref[:SELF_WIDTH, :], preferred_element_type=F32)
    y = y + jnp.dot(o_mem.astype(BF16), w_ref[SELF_WIDTH:, :], preferred_element_type=F32)
    o_ref[...] = x_ref[...] + y


def _attn_out(x2d, o_self, zq, qm_block, memkv, w_out, *, seq, mem_len, tm, name):
    t, d = x2d.shape
    per_batch = seq // tm
    return pl.pallas_call(
        _attn_out_kernel,
        out_shape=jax.ShapeDtypeStruct((t, d), F32),
        grid=(t // tm,),
        in_specs=[
            pl.BlockSpec((tm, d), lambda i: (i, 0)),
            pl.BlockSpec((tm, SELF_WIDTH), lambda i: (i, 0)),
            pl.BlockSpec((tm, MEM_WIDTH), lambda i: (i, qm_block)),
            pl.BlockSpec((mem_len, 2 * MEM_WIDTH), lambda i: (i // per_batch, 0)),
            pl.BlockSpec((d, d), lambda i: (0, 0)),
        ],
        out_specs=pl.BlockSpec((tm, d), lambda i: (i, 0)),
        compiler_params=_params("parallel"),
        name=name,
    )(x2d, o_self, zq, memkv, w_out)


def _mlp_kernel(x_ref, g_ref, wu_ref, wd_ref, gf_ref, o_ref, h_scr, acc_scr, *, final_norm):
    j = pl.program_id(1)

    @pl.when(j == 0)
    def _():
        x = x_ref[...]
        ms = jnp.mean(x * x, axis=-1, keepdims=True)
        h_scr[...] = (x * lax.rsqrt(ms + EPS) * g_ref[...]).astype(BF16)
        acc_scr[...] = jnp.zeros(acc_scr.shape, F32)

    u = jnp.maximum(jnp.dot(h_scr[...], wu_ref[...], preferred_element_type=F32), 0.0)
    acc_scr[...] += jnp.dot((u * u).astype(BF16), wd_ref[...], preferred_element_type=F32)

    @pl.when(j == pl.num_programs(1) - 1)
    def _():
        y = x_ref[...] + acc_scr[...]
        if final_norm:
            ms = jnp.mean(y * y, axis=-1, keepdims=True)
            y = y * lax.rsqrt(ms + EPS) * gf_ref[...]
        o_ref[...] = y


def _mlp(x2d, g, w_up, w_down, g_final, *, final_norm, tm, tf, name):
    t, d = x2d.shape
    dff = w_up.shape[1]
    kern = functools.partial(_mlp_kernel, final_norm=final_norm)
    return pl.pallas_call(
        kern,
        out_shape=jax.ShapeDtypeStruct((t, d), F32),
        grid=(t // tm, dff // tf),
        in_specs=[
            pl.BlockSpec((tm, d), lambda i, j: (i, 0)),
            pl.BlockSpec((1, d), lambda i, j: (0, 0)),
            pl.BlockSpec((d, tf), lambda i, j: (0, j)),
            pl.BlockSpec((tf, d), lambda i, j: (j, 0)),
            pl.BlockSpec((1, d), lambda i, j: (0, 0)),
        ],
        out_specs=pl.BlockSpec((tm, d), lambda i, j: (i, 0)),
        scratch_shapes=[pltpu.VMEM((tm, d), BF16), pltpu.VMEM((tm, d), F32)],
        compiler_params=_params("parallel", "arbitrary"),
        name=name,
    )(x2d, g.reshape(1, d), w_up, w_down, g_final.reshape(1, d))


def _rope_tables(seq):
    half = HEAD_DIM // 2
    inv = 1.0 / (ROPE_THETA ** (jnp.arange(half, dtype=F32) / half))
    ang = jnp.arange(seq, dtype=F32)[:, None] * inv[None, :]
    cos, sin = jnp.cos(ang), jnp.sin(ang)
    reps = LANES // half
    cos_t = jnp.tile(cos, (1, reps))
    sign = jnp.tile(jnp.concatenate([-jnp.ones((half,), F32), jnp.ones((half,), F32)]), LANES // HEAD_DIM)
    sin_t = jnp.tile(sin, (1, reps)) * sign[None, :]
    return cos_t, sin_t


def kernel(x, mem, a_norm_attn, a_w_in, a_lambda, a_subln, a_mem_norm, a_w_mem_kv, a_w_out, a_norm_mlp, a_w_up, a_w_down, kv_norm, w_kv, b_norm_attn, b_w_in, b_mem_norm, b_w_mem_kv, b_w_out, b_norm_mlp, b_w_up, b_w_down, final_norm):
    batch, seq, d = x.shape
    mem_len = mem.shape[1]
    t = batch * seq
    x2d = x.reshape(t, d)
    mem2d = mem.reshape(batch * mem_len, d)
    cos_t, sin_t = _rope_tables(seq)
    bf = lambda w: w.astype(BF16)
    tm = 512
    ng = N_SELF_GROUPS
    self_q = {j: QK_SCALE * LOG2E for j in range(ng)}

    lam_init = 0.8 - 0.6 * math.exp(-0.3 * 0)
    za, vt_a = _proj(x2d, a_norm_attn[0], bf(a_w_in[0]), cos_t, sin_t, n_rope=2 * ng,
                     scales={**self_q, 3 * ng: QK_SCALE, 3 * ng + 1: QK_SCALE},
                     with_kmean=False, vt_start=2 * ng, seq=seq, tm=tm, name="a_proj")
    mkv_a, = _proj(mem2d, a_mem_norm[0], bf(a_w_mem_kv[0]), cos_t, sin_t, n_rope=0, scales={},
                   with_kmean=False, vt_start=None, seq=mem_len, tm=mem_len, name="a_memkv")
    o_self = _diff_attention(za, vt_a, a_lambda[0], a_subln[0], batch=batch, seq=seq,
                             n_groups=2, lam_init=lam_init)
    x2d = _attn_out(x2d, o_self, za, 3 * SELF_WIDTH // MEM_WIDTH, mkv_a, bf(a_w_out[0]),
                    seq=seq, mem_len=mem_len, tm=tm, name="a_attn_out")
    x2d = _mlp(x2d, a_norm_mlp[0], bf(a_w_up[0]), bf(a_w_down[0]), final_norm,
               final_norm=False, tm=1024, tf=512, name="a_mlp")

    zkv, kmean, vt_b = _proj(x2d, kv_norm, bf(w_kv), cos_t, sin_t, n_rope=ng, scales={},
                             with_kmean=True, vt_start=ng, seq=seq, tm=tm, name="b_kvproj")
    kmean = kmean.reshape(batch, seq // MOBA_BLOCK, SELF_WIDTH)
    zb, = _proj(x2d, b_norm_attn[0], bf(b_w_in[0]), cos_t, sin_t, n_rope=ng,
                scales={**self_q, ng: QK_SCALE, ng + 1: QK_SCALE}, with_kmean=False, vt_start=None,
                seq=seq, tm=tm, name="b_qproj")
    mkv_b, = _proj(mem2d, b_mem_norm[0], bf(b_w_mem_kv[0]), cos_t, sin_t, n_rope=0, scales={},
                   with_kmean=False, vt_start=None, seq=mem_len, tm=mem_len, name="b_memkv")
    o_self = _moba_attention(zb, zkv, vt_b, kmean, batch=batch, seq=seq, n_groups=2)
    x2d = _attn_out(x2d, o_self, zb, SELF_WIDTH // MEM_WIDTH, mkv_b, bf(b_w_out[0]),
                    seq=seq, mem_len=mem_len, tm=tm, name="b_attn_out")
    x2d = _mlp(x2d, b_norm_mlp[0], bf(b_w_up[0]), bf(b_w_down[0]), final_norm,
               final_norm=True, tm=1024, tf=512, name="b_mlp")
    return x2d.reshape(batch, seq, d)
```

```python
import functools
import math

import jax
import jax.numpy as jnp
from jax import lax
from jax.experimental import pallas as pl
from jax.experimental.pallas import tpu as pltpu

D_MODEL = 1024
HEAD_DIM = 64
SELF_WIDTH = 768
MEM_WIDTH = 256
N_MEM_HEADS = 4
D_FF = 4096
MOBA_BLOCK = 256
MOBA_TOPK = 3
ROPE_THETA = 10000.0
EPS = 1e-6
NEG = -1e30

LANES = 128
N_SELF_GROUPS = SELF_WIDTH // LANES
QK_SCALE = HEAD_DIM ** -0.5
LOG2E = math.log2(math.e)
KV_CHUNK = 256
SUM_ROWS = 16

F32 = jnp.float32
BF16 = jnp.bfloat16

_VMEM_LIMIT = 56 * 1024 * 1024


def _params(*sem):
    return pltpu.CompilerParams(dimension_semantics=sem, vmem_limit_bytes=_VMEM_LIMIT)


def _proj_kernel(x_ref, g_ref, w_ref, cos_ref, sin_ref, o_ref, *extra_refs,
                 n_rope, scales, col_chunk, with_kmean, vt_start):
    tm = x_ref.shape[0]
    n_out = w_ref.shape[1]
    extra = list(extra_refs)
    km_ref = extra.pop(0) if with_kmean else None
    vt_ref = extra.pop(0) if vt_start is not None else None
    x = x_ref[...]
    ms = jnp.mean(x * x, axis=-1, keepdims=True)
    h = (x * lax.rsqrt(ms + EPS) * g_ref[...]).astype(BF16)
    lane = lax.broadcasted_iota(jnp.int32, (tm, LANES), 1)
    first_half = (lane & (HEAD_DIM - 1)) < HEAD_DIM // 2
    if n_rope:
        cos = cos_ref[...]
        sin = sin_ref[...]
    for c0 in range(0, n_out, col_chunk):
        z = jnp.dot(h, w_ref[:, c0:c0 + col_chunk], preferred_element_type=F32)
        for jj in range(col_chunk // LANES):
            j = c0 // LANES + jj
            blk = z[:, jj * LANES:(jj + 1) * LANES]
            if j < n_rope:
                swap = jnp.where(first_half, pltpu.roll(blk, LANES - 32, 1), pltpu.roll(blk, 32, 1))
                blk = blk * cos + swap * sin
                if km_ref is not None:
                    km_ref[0, :, j * LANES:(j + 1) * LANES] = jnp.mean(
                        blk.reshape(tm // MOBA_BLOCK, MOBA_BLOCK, LANES), axis=1)
            if j in scales:
                blk = blk * scales[j]
            o_ref[:, j * LANES:(j + 1) * LANES] = blk.astype(o_ref.dtype)
            if vt_ref is not None and vt_start <= j < vt_start + N_SELF_GROUPS:
                g = j - vt_start
                for c in range(tm // KV_CHUNK):
                    vt_ref[c, g * LANES:(g + 1) * LANES, :] = (
                        blk[c * KV_CHUNK:(c + 1) * KV_CHUNK, :].T.astype(BF16))


def _proj(x2d, g, w, cos_t, sin_t, *, n_rope, scales, with_kmean, vt_start, seq, tm, name):
    t, d = x2d.shape
    n_out = w.shape[1]
    n_pos_blocks = seq // tm
    col_chunk = 512 if n_out % 512 == 0 else 256
    out_shape = [jax.ShapeDtypeStruct((t, n_out), BF16)]
    out_specs = [pl.BlockSpec((tm, n_out), lambda i: (i, 0))]
    if with_kmean:
        out_shape.append(jax.ShapeDtypeStruct((t // tm, tm // MOBA_BLOCK, n_rope * LANES), F32))
        out_specs.append(pl.BlockSpec((1, tm // MOBA_BLOCK, n_rope * LANES), lambda i: (i, 0, 0)))
    if vt_start is not None:
        out_shape.append(jax.ShapeDtypeStruct((t // KV_CHUNK, SELF_WIDTH, KV_CHUNK), BF16))
        out_specs.append(pl.BlockSpec((tm // KV_CHUNK, SELF_WIDTH, KV_CHUNK), lambda i: (i, 0, 0)))
    kern = functools.partial(_proj_kernel, n_rope=n_rope, scales=dict(scales), col_chunk=col_chunk,
                             with_kmean=with_kmean, vt_start=vt_start)
    return pl.pallas_call(
        kern,
        out_shape=out_shape,
        grid=(t // tm,),
        in_specs=[
            pl.BlockSpec((tm, d), lambda i: (i, 0)),
            pl.BlockSpec((1, d), lambda i: (0, 0)),
            pl.BlockSpec((d, n_out), lambda i: (0, 0)),
            pl.BlockSpec((tm, LANES), lambda i: (i % n_pos_blocks, 0)),
            pl.BlockSpec((tm, LANES), lambda i: (i % n_pos_blocks, 0)),
        ],
        out_specs=out_specs,
        compiler_params=_params("parallel"),
        name=name,
    )(x2d, g.reshape(1, d), w, cos_t, sin_t)


_NT = (((1,), (1,)), ((), ()))


def _stack_maps(q):
    tq = q.shape[0]
    lane = lax.broadcasted_iota(jnp.int32, (tq, LANES), 1)
    zero = jnp.zeros_like(q)
    return jnp.concatenate(
        [jnp.where(lane < HEAD_DIM, q, zero), jnp.where(lane >= HEAD_DIM, q, zero)], axis=0)


def _causal_mask(st, tq):
    key = lax.broadcasted_iota(jnp.int32, st.shape, 0)
    qry = lax.broadcasted_iota(jnp.int32, st.shape, 1) & (tq - 1)
    return jnp.where(key <= qry, st, NEG)


def _flash_pipeline(n_groups, n_past, tq, scores_fn, vt_fn, bias_fn, s_scr, acc_scr, split_heads):
    n = 2 * tq

    def produce(j, buf):
        col_max = []
        for g in range(n_groups):
            st = scores_fn(g, j)
            s_scr[buf, g] = st
            col_max.append(jnp.max(st, axis=0, keepdims=True))
        return tuple(col_max)

    ones_rows = jnp.ones((SUM_ROWS, tq), BF16)

    def consume(j, buf, stats, col_max, diagonal):
        out = []
        for g in range(n_groups):
            st = s_scr[buf, g]
            if diagonal:
                st = _causal_mask(st, tq)
                m_cur = jnp.max(st, axis=0, keepdims=True)
            else:
                m_cur = col_max[g]
            if bias_fn is not None:
                bias = bias_fn(g, j)
                m_cur = m_cur + bias
            m_next = jnp.maximum(stats[g], m_cur)
            alpha = jnp.exp2(stats[g] - m_next)
            shift = m_next if bias_fn is None else m_next - bias
            p = jnp.exp2((st - shift).astype(BF16))
            vt = vt_fn(g, j)
            if split_heads:
                for h in range(2):
                    vt_h = jnp.concatenate([vt[h * HEAD_DIM:(h + 1) * HEAD_DIM], ones_rows], axis=0)
                    lanes = slice(h * tq, (h + 1) * tq)
                    acc_scr[g, h] = alpha[:, lanes] * acc_scr[g, h] + jnp.dot(
                        vt_h, p[:, lanes], preferred_element_type=F32)
            else:
                acc_scr[g] = alpha * acc_scr[g] + jnp.dot(
                    jnp.concatenate([vt, ones_rows], axis=0), p, preferred_element_type=F32)
            out.append(m_next)
        return tuple(out)

    def step(j, buf, carry):
        nxt = produce(j + 1, 1 - buf)
        return consume(j, buf, carry[0], carry[1], False), nxt

    def pair(i, carry):
        return step(2 * i + 1, 1, step(2 * i, 0, carry))

    def quad(i, carry):
        return pair(2 * i + 1, pair(2 * i, carry))

    def octet(i, carry):
        return quad(2 * i + 1, quad(2 * i, carry))

    acc_scr[...] = jnp.zeros(acc_scr.shape, F32)
    init = tuple(jnp.full((1, n), NEG, F32) for _ in range(n_groups))
    n_octets = n_past // 8
    carry = lax.fori_loop(0, n_octets, octet, (init, produce(0, 0)))
    carry = lax.fori_loop(2 * n_octets, n_past // 4, quad, carry)
    carry = lax.fori_loop(2 * (n_past // 4), n_past // 2, pair, carry)

    def odd_tail(carry):
        stats, col_max = step(n_past - 1, 0, carry)
        return consume(n_past, 1, stats, col_max, True)

    def even_tail(carry):
        return consume(n_past, 0, carry[0], carry[1], True)

    return lax.cond(n_past % 2 == 1, odd_tail, even_tail, carry)


def _diff_attn_kernel(lam_ref, g_ref, q_ref, k_ref, vt_ref, o_ref, s_scr, acc_scr, *, lam_init):
    tq = q_ref.shape[0]
    n_groups = q_ref.shape[1] // LANES
    qi = pl.program_id(2)
    cols = [slice(g * LANES, (g + 1) * LANES) for g in range(n_groups)]
    qs = [_stack_maps(q_ref[:, c]) for c in cols]

    def scores(g, j):
        start = pl.multiple_of(j * tq, tq)
        return lax.dot_general(k_ref[pl.ds(start, tq), cols[g]], qs[g], _NT,
                               preferred_element_type=F32)

    _flash_pipeline(n_groups, qi, tq, scores, lambda g, j: vt_ref[j, cols[g], :], None,
                    s_scr, acc_scr, split_heads=False)

    lp = lam_ref[...]
    lam = (jnp.exp(jnp.sum(lp[0:1] * lp[1:2], axis=1, keepdims=True))
           - jnp.exp(jnp.sum(lp[2:3] * lp[3:4], axis=1, keepdims=True)) + lam_init)
    for g, c in enumerate(cols):
        o_t = acc_scr[g, :LANES, :] * (1.0 / acc_scr[g, LANES:LANES + 1, :])
        od = (o_t[:, :tq] - lam * o_t[:, tq:]).T
        ms = jnp.mean(od * od, axis=-1, keepdims=True)
        y = od * lax.rsqrt(ms + EPS) * g_ref[...]
        o_ref[:, c] = (y * (1.0 - lam_init)).astype(o_ref.dtype)


def _diff_attention(z, vt, lam_p, subln, *, batch, seq, n_groups, lam_init):
    t = z.shape[0]
    tq = KV_CHUNK
    nq = seq // tq
    w = n_groups * LANES
    n_steps = N_SELF_GROUPS // n_groups
    kern = functools.partial(_diff_attn_kernel, lam_init=lam_init)
    return pl.pallas_call(
        kern,
        out_shape=jax.ShapeDtypeStruct((t, SELF_WIDTH), BF16),
        grid=(batch, n_steps, nq),
        in_specs=[
            pl.BlockSpec((4, HEAD_DIM), lambda b, h, i: (0, 0)),
            pl.BlockSpec((1, LANES), lambda b, h, i: (0, 0)),
            pl.BlockSpec((tq, w), lambda b, h, i: (b * nq + i, h)),
            pl.BlockSpec((seq, w), lambda b, h, i: (b, n_steps + h)),
            pl.BlockSpec((seq // KV_CHUNK, w, KV_CHUNK), lambda b, h, i: (b, h, 0)),
        ],
        out_specs=pl.BlockSpec((tq, w), lambda b, h, i: (b * nq + i, h)),
        scratch_shapes=[pltpu.VMEM((2, n_groups, tq, 2 * tq), F32),
                        pltpu.VMEM((n_groups, LANES + SUM_ROWS, 2 * tq), F32)],
        compiler_params=_params("parallel", "parallel", "arbitrary"),
        name="diff_attn",
    )(lam_p, subln.reshape(1, LANES), z, z, vt)


def _moba_select_bias(qs, km, qi):
    n_blk = km.shape[0]
    parts, rest = [], km
    for _ in range(3):
        part = rest.astype(BF16)
        parts.append(part)
        rest = rest - part.astype(F32)
    terms = lax.dot_general(jnp.concatenate(parts, axis=0), qs, _NT, preferred_element_type=F32)
    gate = (terms[2 * n_blk:] + terms[n_blk:2 * n_blk]) + terms[:n_blk]
    blk = lax.broadcasted_iota(jnp.int32, gate.shape, 0)
    past = blk < qi
    gm = jnp.where(past, gate, NEG)
    sel = jnp.zeros(gate.shape, F32)
    for _ in range(MOBA_TOPK):
        mx = jnp.max(gm, axis=0, keepdims=True)
        first = jnp.min(jnp.where(gm == mx, blk, n_blk), axis=0, keepdims=True)
        pick = blk == first
        sel = jnp.where(pick, 1.0, sel)
        gm = jnp.where(pick, -jnp.inf, gm)
    return jnp.where((past & (sel > 0.5)) | (blk == qi), 0.0, 2.0 * NEG)


def _moba_attn_kernel(q_ref, k_ref, vt_ref, km_ref, o_ref, s_scr, acc_scr, bias_scr):
    tq = q_ref.shape[0]
    n_groups = q_ref.shape[1] // LANES
    qi = pl.program_id(2)
    cols = [slice(g * LANES, (g + 1) * LANES) for g in range(n_groups)]
    qs = [_stack_maps(q_ref[:, c]) for c in cols]
    for g, c in enumerate(cols):
        bias_scr[g] = _moba_select_bias(qs[g], km_ref[:, c], qi)

    def scores(g, j):
        start = pl.multiple_of(j * tq, tq)
        return lax.dot_general(k_ref[pl.ds(start, tq), cols[g]], qs[g], _NT,
                               preferred_element_type=F32)

    _flash_pipeline(n_groups, qi, tq, scores, lambda g, j: vt_ref[j, cols[g], :],
                    lambda g, j: bias_scr[g, pl.ds(j, 1), :], s_scr, acc_scr, split_heads=True)

    for g, c in enumerate(cols):
        heads = [acc_scr[g, h, :HEAD_DIM, :] * (1.0 / acc_scr[g, h, HEAD_DIM:HEAD_DIM + 1, :])
                 for h in range(2)]
        o_ref[:, c] = jnp.concatenate(heads, axis=0).T.astype(o_ref.dtype)


def _moba_attention(zq, zkv, vt, kmean, *, batch, seq, n_groups):
    t = zq.shape[0]
    tq = MOBA_BLOCK
    nq = seq // tq
    w = n_groups * LANES
    n_steps = N_SELF_GROUPS // n_groups
    return pl.pallas_call(
        _moba_attn_kernel,
        out_shape=jax.ShapeDtypeStruct((t, SELF_WIDTH), BF16),
        grid=(batch, n_steps, nq),
        in_specs=[
            pl.BlockSpec((tq, w), lambda b, h, i: (b * nq + i, h)),
            pl.BlockSpec((seq, w), lambda b, h, i: (b, h)),
            pl.BlockSpec((seq // KV_CHUNK, w, KV_CHUNK), lambda b, h, i: (b, h, 0)),
            pl.BlockSpec((None, nq, w), lambda b, h, i: (b, 0, h)),
        ],
        out_specs=pl.BlockSpec((tq, w), lambda b, h, i: (b * nq + i, h)),
        scratch_shapes=[pltpu.VMEM((2, n_groups, tq, 2 * tq), F32),
                        pltpu.VMEM((n_groups, 2, HEAD_DIM + SUM_ROWS, tq), F32),
                        pltpu.VMEM((n_groups, nq, 2 * tq), F32)],
        compiler_params=_params("parallel", "parallel", "arbitrary"),
        name="moba_attn",
    )(zq, zkv, vt, kmean)


def _attn_out_kernel(x_ref, os_ref, qm_ref, mkv_ref, w_ref, o_ref):
    tm = x_ref.shape[0]
    qm = qm_ref[...]
    mk = mkv_ref[:, :MEM_WIDTH]
    mv = mkv_ref[:, MEM_WIDTH:]
    lane = lax.broadcasted_iota(jnp.int32, (tm, MEM_WIDTH), 1)
    o_mem = jnp.zeros((tm, MEM_WIDTH), F32)
    for h in range(N_MEM_HEADS):
        in_head = (lane >= h * HEAD_DIM) & (lane < (h + 1) * HEAD_DIM)
        qh = jnp.where(in_head, qm, jnp.zeros_like(qm))
        s = lax.dot_general(qh, mk, _NT, preferred_element_type=F32)
        e = jnp.exp(s - jnp.max(s, axis=1, keepdims=True))
        p = e / jnp.sum(e, axis=1, keepdims=True)
        oh = jnp.dot(p.astype(BF16), mv, preferred_element_type=F32)
        o_mem = jnp.where(in_head, oh, o_mem)
    y = jnp.dot(os_ref[...], w_ref[:SELF_WIDTH, :], preferred_element_type=F32)
    y = y + jnp.dot(o_mem.astype(BF16), w_ref[SELF_WIDTH:, :], preferred_element_type=F32)
    o_ref[...] = x_ref[...] + y


def _attn_out(x2d, o_self, zq, qm_block, memkv, w_out, *, seq, mem_len, tm, name):
    t, d = x2d.shape
    per_batch = seq // tm
    return pl.pallas_call(
        _attn_out_kernel,
        out_shape=jax.ShapeDtypeStruct((t, d), F32),
        grid=(t // tm,),
        in_specs=[
            pl.BlockSpec((tm, d), lambda i: (i, 0)),
            pl.BlockSpec((tm, SELF_WIDTH), lambda i: (i, 0)),
            pl.BlockSpec((tm, MEM_WIDTH), lambda i: (i, qm_block)),
            pl.BlockSpec((mem_len, 2 * MEM_WIDTH), lambda i: (i // per_batch, 0)),
            pl.BlockSpec((d, d), lambda i: (0, 0)),
        ],
        out_specs=pl.BlockSpec((tm, d), lambda i: (i, 0)),
        compiler_params=_params("parallel"),
        name=name,
    )(x2d, o_self, zq, memkv, w_out)


def _mlp_kernel(x_ref, g_ref, wu_ref, wd_ref, gf_ref, o_ref, h_scr, acc_scr, *, final_norm):
    j = pl.program_id(1)

    @pl.when(j == 0)
    def _():
        x = x_ref[...]
        ms = jnp.mean(x * x, axis=-1, keepdims=True)
        h_scr[...] = (x * lax.rsqrt(ms + EPS) * g_ref[...]).astype(BF16)
        acc_scr[...] = jnp.zeros(acc_scr.shape, F32)

    u = jnp.maximum(jnp.dot(h_scr[...], wu_ref[...], preferred_element_type=F32), 0.0)
    acc_scr[...] += jnp.dot((u * u).astype(BF16), wd_ref[...], preferred_element_type=F32)

    @pl.when(j == pl.num_programs(1) - 1)
    def _():
        y = x_ref[...] + acc_scr[...]
        if final_norm:
            ms = jnp.mean(y * y, axis=-1, keepdims=True)
            y = y * lax.rsqrt(ms + EPS) * gf_ref[...]
        o_ref[...] = y


def _mlp(x2d, g, w_up, w_down, g_final, *, final_norm, tm, tf, name):
    t, d = x2d.shape
    dff = w_up.shape[1]
    kern = functools.partial(_mlp_kernel, final_norm=final_norm)
    return pl.pallas_call(
        kern,
        out_shape=jax.ShapeDtypeStruct((t, d), F32),
        grid=(t // tm, dff // tf),
        in_specs=[
            pl.BlockSpec((tm, d), lambda i, j: (i, 0)),
            pl.BlockSpec((1, d), lambda i, j: (0, 0)),
            pl.BlockSpec((d, tf), lambda i, j: (0, j)),
            pl.BlockSpec((tf, d), lambda i, j: (j, 0)),
            pl.BlockSpec((1, d), lambda i, j: (0, 0)),
        ],
        out_specs=pl.BlockSpec((tm, d), lambda i, j: (i, 0)),
        scratch_shapes=[pltpu.VMEM((tm, d), BF16), pltpu.VMEM((tm, d), F32)],
        compiler_params=_params("parallel", "arbitrary"),
        name=name,
    )(x2d, g.reshape(1, d), w_up, w_down, g_final.reshape(1, d))


def _rope_tables(seq):
    half = HEAD_DIM // 2
    inv = 1.0 / (ROPE_THETA ** (jnp.arange(half, dtype=F32) / half))
    ang = jnp.arange(seq, dtype=F32)[:, None] * inv[None, :]
    cos, sin = jnp.cos(ang), jnp.sin(ang)
    reps = LANES // half
    cos_t = jnp.tile(cos, (1, reps))
    sign = jnp.tile(jnp.concatenate([-jnp.ones((half,), F32), jnp.ones((half,), F32)]), LANES // HEAD_DIM)
    sin_t = jnp.tile(sin, (1, reps)) * sign[None, :]
    return cos_t, sin_t


def kernel(x, mem, a_norm_attn, a_w_in, a_lambda, a_subln, a_mem_norm, a_w_mem_kv, a_w_out, a_norm_mlp, a_w_up, a_w_down, kv_norm, w_kv, b_norm_attn, b_w_in, b_mem_norm, b_w_mem_kv, b_w_out, b_norm_mlp, b_w_up, b_w_down, final_norm):
    batch, seq, d = x.shape
    mem_len = mem.shape[1]
    t = batch * seq
    x2d = x.reshape(t, d)
    mem2d = mem.reshape(batch * mem_len, d)
    cos_t, sin_t = _rope_tables(seq)
    bf = lambda w: w.astype(BF16)
    tm = 512
    ng = N_SELF_GROUPS
    self_q = {j: QK_SCALE * LOG2E for j in range(ng)}

    lam_init = 0.8 - 0.6 * math.exp(-0.3 * 0)
    za, vt_a = _proj(x2d, a_norm_attn[0], bf(a_w_in[0]), cos_t, sin_t, n_rope=2 * ng,
                     scales={**self_q, 3 * ng: QK_SCALE, 3 * ng + 1: QK_SCALE},
                     with_kmean=False, vt_start=2 * ng, seq=seq, tm=tm, name="a_proj")
    mkv_a, = _proj(mem2d, a_mem_norm[0], bf(a_w_mem_kv[0]), cos_t, sin_t, n_rope=0, scales={},
                   with_kmean=False, vt_start=None, seq=mem_len, tm=mem_len, name="a_memkv")
    o_self = _diff_attention(za, vt_a, a_lambda[0], a_subln[0], batch=batch, seq=seq,
                             n_groups=2, lam_init=lam_init)
    x2d = _attn_out(x2d, o_self, za, 3 * SELF_WIDTH // MEM_WIDTH, mkv_a, bf(a_w_out[0]),
                    seq=seq, mem_len=mem_len, tm=tm, name="a_attn_out")
    x2d = _mlp(x2d, a_norm_mlp[0], bf(a_w_up[0]), bf(a_w_down[0]), final_norm,
               final_norm=False, tm=1024, tf=512, name="a_mlp")

    zkv, kmean, vt_b = _proj(x2d, kv_norm, bf(w_kv), cos_t, sin_t, n_rope=ng, scales={},
                             with_kmean=True, vt_start=ng, seq=seq, tm=tm, name="b_kvproj")
    kmean = kmean.reshape(batch, seq // MOBA_BLOCK, SELF_WIDTH)
    zb, = _proj(x2d, b_norm_attn[0], bf(b_w_in[0]), cos_t, sin_t, n_rope=ng,
                scales={**self_q, ng: QK_SCALE, ng + 1: QK_SCALE}, with_kmean=False, vt_start=None,
                seq=seq, tm=tm, name="b_qproj")
    mkv_b, = _proj(mem2d, b_mem_norm[0], bf(b_w_mem_kv[0]), cos_t, sin_t, n_rope=0, scales={},
                   with_kmean=False, vt_start=None, seq=mem_len, tm=mem_len, name="b_memkv")
    o_self = _moba_attention(zb, zkv, vt_b, kmean, batch=batch, seq=seq, n_groups=2)
    x2d = _attn_out(x2d, o_self, zb, SELF_WIDTH // MEM_WIDTH, mkv_b, bf(b_w_out[0]),
                    seq=seq, mem_len=mem_len, tm=tm, name="b_attn_out")
    x2d = _mlp(x2d, b_norm_mlp[0], bf(b_w_up[0]), bf(b_w_down[0]), final_norm,
               final_norm=True, tm=1024, tf=512, name="b_mlp")
    return x2d.reshape(batch, seq, d)
```

```python
import functools
import math

import jax
import jax.numpy as jnp
from jax import lax
from jax.experimental import pallas as pl
from jax.experimental.pallas import tpu as pltpu

D_MODEL = 1024
HEAD_DIM = 64
SELF_WIDTH = 768
MEM_WIDTH = 256
N_MEM_HEADS = 4
D_FF = 4096
MOBA_BLOCK = 256
MOBA_TOPK = 3
ROPE_THETA = 10000.0
EPS = 1e-6
NEG = -1e30

LANES = 128
N_SELF_GROUPS = SELF_WIDTH // LANES
QK_SCALE = HEAD_DIM ** -0.5
LOG2E = math.log2(math.e)
KV_CHUNK = 256
SUM_ROWS = 16

F32 = jnp.float32
BF16 = jnp.bfloat16

_VMEM_LIMIT = 56 * 1024 * 1024


def _params(*sem):
    return pltpu.CompilerParams(dimension_semantics=sem, vmem_limit_bytes=_VMEM_LIMIT)


def _proj_kernel(x_ref, g_ref, w_ref, cos_ref, sin_ref, o_ref, *extra_refs,
                 n_rope, scales, col_chunk, with_kmean, vt_start):
    tm = x_ref.shape[0]
    n_out = w_ref.shape[1]
    extra = list(extra_refs)
    km_ref = extra.pop(0) if with_kmean else None
    vt_ref = extra.pop(0) if vt_start is not None else None
    x = x_ref[...]
    ms = jnp.mean(x * x, axis=-1, keepdims=True)
    h = (x * lax.rsqrt(ms + EPS) * g_ref[...]).astype(BF16)
    lane = lax.broadcasted_iota(jnp.int32, (tm, LANES), 1)
    first_half = (lane & (HEAD_DIM - 1)) < HEAD_DIM // 2
    if n_rope:
        cos = cos_ref[...]
        sin = sin_ref[...]
    for c0 in range(0, n_out, col_chunk):
        z = jnp.dot(h, w_ref[:, c0:c0 + col_chunk], preferred_element_type=F32)
        for jj in range(col_chunk // LANES):
            j = c0 // LANES + jj
            blk = z[:, jj * LANES:(jj + 1) * LANES]
            if j < n_rope:
                swap = jnp.where(first_half, pltpu.roll(blk, LANES - 32, 1), pltpu.roll(blk, 32, 1))
                blk = blk * cos + swap * sin
                if km_ref is not None:
                    km_ref[0, :, j * LANES:(j + 1) * LANES] = jnp.mean(
                        blk.reshape(tm // MOBA_BLOCK, MOBA_BLOCK, LANES), axis=1)
            if j in scales:
                blk = blk * scales[j]
            o_ref[:, j * LANES:(j + 1) * LANES] = blk.astype(o_ref.dtype)
            if vt_ref is not None and vt_start <= j < vt_start + N_SELF_GROUPS:
                g = j - vt_start
                for c in range(tm // KV_CHUNK):
                    vt_ref[c, g * LANES:(g + 1) * LANES, :] = (
                        blk[c * KV_CHUNK:(c + 1) * KV_CHUNK, :].T.astype(BF16))


def _proj(x2d, g, w, cos_t, sin_t, *, n_rope, scales, with_kmean, vt_start, seq, tm, name):
    t, d = x2d.shape
    n_out = w.shape[1]
    n_pos_blocks = seq // tm
    col_chunk = 512 if n_out % 512 == 0 else 256
    out_shape = [jax.ShapeDtypeStruct((t, n_out), BF16)]
    out_specs = [pl.BlockSpec((tm, n_out), lambda i: (i, 0))]
    if with_kmean:
        out_shape.append(jax.ShapeDtypeStruct((t // tm, tm // MOBA_BLOCK, n_rope * LANES), F32))
        out_specs.append(pl.BlockSpec((1, tm // MOBA_BLOCK, n_rope * LANES), lambda i: (i, 0, 0)))
    if vt_start is not None:
        out_shape.append(jax.ShapeDtypeStruct((t // KV_CHUNK, SELF_WIDTH, KV_CHUNK), BF16))
        out_specs.append(pl.BlockSpec((tm // KV_CHUNK, SELF_WIDTH, KV_CHUNK), lambda i: (i, 0, 0)))
    kern = functools.partial(_proj_kernel, n_rope=n_rope, scales=dict(scales), col_chunk=col_chunk,
                             with_kmean=with_kmean, vt_start=vt_start)
    return pl.pallas_call(
        kern,
        out_shape=out_shape,
        grid=(t // tm,),
        in_specs=[
            pl.BlockSpec((tm, d), lambda i: (i, 0)),
            pl.BlockSpec((1, d), lambda i: (0, 0)),
            pl.BlockSpec((d, n_out), lambda i: (0, 0)),
            pl.BlockSpec((tm, LANES), lambda i: (i % n_pos_blocks, 0)),
            pl.BlockSpec((tm, LANES), lambda i: (i % n_pos_blocks, 0)),
        ],
        out_specs=out_specs,
        compiler_params=_params("parallel"),
        name=name,
    )(x2d, g.reshape(1, d), w, cos_t, sin_t)


_NT = (((1,), (1,)), ((), ()))


def _stack_maps(q):
    tq = q.shape[0]
    lane = lax.broadcasted_iota(jnp.int32, (tq, LANES), 1)
    zero = jnp.zeros_like(q)
    return jnp.concatenate(
        [jnp.where(lane < HEAD_DIM, q, zero), jnp.where(lane >= HEAD_DIM, q, zero)], axis=0)


def _causal_mask(st, tq):
    key = lax.broadcasted_iota(jnp.int32, st.shape, 0)
    qry = lax.broadcasted_iota(jnp.int32, st.shape, 1) & (tq - 1)
    return jnp.where(key <= qry, st, NEG)


def _flash_pipeline(n_groups, n_past, tq, scores_fn, vt_fn, bias_fn, s_scr, acc_scr, split_heads):
    n = 2 * tq

    def produce(j, buf):
        col_max = []
        for g in range(n_groups):
            st = scores_fn(g, j)
            s_scr[buf, g] = st
            col_max.append(jnp.max(st, axis=0, keepdims=True))
        return tuple(col_max)

    ones_rows = jnp.ones((SUM_ROWS, tq), BF16)

    def consume(j, buf, stats, col_max, diagonal):
        out = []
        for g in range(n_groups):
            st = s_scr[buf, g]
            if diagonal:
                st = _causal_mask(st, tq)
                m_cur = jnp.max(st, axis=0, keepdims=True)
            else:
                m_cur = col_max[g]
            if bias_fn is not None:
                bias = bias_fn(g, j)
                m_cur = m_cur + bias
            m_next = jnp.maximum(stats[g], m_cur)
            alpha = jnp.exp2(stats[g] - m_next)
            shift = m_next if bias_fn is None else m_next - bias
            p = jnp.exp2((st - shift).astype(BF16))
            vt = vt_fn(g, j)
            if split_heads:
                for h in range(2):
                    vt_h = jnp.concatenate([vt[h * HEAD_DIM:(h + 1) * HEAD_DIM], ones_rows], axis=0)
                    lanes = slice(h * tq, (h + 1) * tq)
                    acc_scr[g, h] = alpha[:, lanes] * acc_scr[g, h] + jnp.dot(
                        vt_h, p[:, lanes], preferred_element_type=F32)
            else:
                acc_scr[g] = alpha * acc_scr[g] + jnp.dot(
                    jnp.concatenate([vt, ones_rows], axis=0), p, preferred_element_type=F32)
            out.append(m_next)
        return tuple(out)

    def step(j, buf, carry):
        nxt = produce(j + 1, 1 - buf)
        return consume(j, buf, carry[0], carry[1], False), nxt

    def pair(i, carry):
        return step(2 * i + 1, 1, step(2 * i, 0, carry))

    def quad(i, carry):
        return pair(2 * i + 1, pair(2 * i, carry))

    def octet(i, carry):
        return quad(2 * i + 1, quad(2 * i, carry))

    acc_scr[...] = jnp.zeros(acc_scr.shape, F32)
    init = tuple(jnp.full((1, n), NEG, F32) for _ in range(n_groups))
    n_octets = n_past // 8
    carry = lax.fori_loop(0, n_octets, octet, (init, produce(0, 0)))
    carry = lax.fori_loop(2 * n_octets, n_past // 4, quad, carry)
    carry = lax.fori_loop(2 * (n_past // 4), n_past // 2, pair, carry)

    def odd_tail(carry):
        stats, col_max = step(n_past - 1, 0, carry)
        return consume(n_past, 1, stats, col_max, True)

    def even_tail(carry):
        return consume(n_past, 0, carry[0], carry[1], True)

    return lax.cond(n_past % 2 == 1, odd_tail, even_tail, carry)


def _diff_attn_kernel(lam_ref, g_ref, q_ref, k_ref, vt_ref, o_ref, s_scr, acc_scr, *, lam_init):
    tq = q_ref.shape[0]
    n_groups = q_ref.shape[1] // LANES
    qi = pl.program_id(2)
    cols = [slice(g * LANES, (g + 1) * LANES) for g in range(n_groups)]
    qs = [_stack_maps(q_ref[:, c]) for c in cols]

    def scores(g, j):
        start = pl.multiple_of(j * tq, tq)
        return lax.dot_general(k_ref[pl.ds(start, tq), cols[g]], qs[g], _NT,
                               preferred_element_type=F32)

    _flash_pipeline(n_groups, qi, tq, scores, lambda g, j: vt_ref[j, cols[g], :], None,
                    s_scr, acc_scr, split_heads=False)

    lp = lam_ref[...]
    lam = (jnp.exp(jnp.sum(lp[0:1] * lp[1:2], axis=1, keepdims=True))
           - jnp.exp(jnp.sum(lp[2:3] * lp[3:4], axis=1, keepdims=True)) + lam_init)
    for g, c in enumerate(cols):
        o_t = acc_scr[g, :LANES, :] * (1.0 / acc_scr[g, LANES:LANES + 1, :])
        od = (o_t[:, :tq] - lam * o_t[:, tq:]).T
        ms = jnp.mean(od * od, axis=-1, keepdims=True)
        y = od * lax.rsqrt(ms + EPS) * g_ref[...]
        o_ref[:, c] = (y * (1.0 - lam_init)).astype(o_ref.dtype)


def _diff_attention(z, vt, lam_p, subln, *, batch, seq, n_groups, lam_init):
    t = z.shape[0]
    tq = KV_CHUNK
    nq = seq // tq
    w = n_groups * LANES
    n_steps = N_SELF_GROUPS // n_groups
    kern = functools.partial(_diff_attn_kernel, lam_init=lam_init)
    return pl.pallas_call(
        kern,
        out_shape=jax.ShapeDtypeStruct((t, SELF_WIDTH), BF16),
        grid=(batch, n_steps, nq),
        in_specs=[
            pl.BlockSpec((4, HEAD_DIM), lambda b, h, i: (0, 0)),
            pl.BlockSpec((1, LANES), lambda b, h, i: (0, 0)),
            pl.BlockSpec((tq, w), lambda b, h, i: (b * nq + i, h)),
            pl.BlockSpec((seq, w), lambda b, h, i: (b, n_steps + h)),
            pl.BlockSpec((seq // KV_CHUNK, w, KV_CHUNK), lambda b, h, i: (b, h, 0)),
        ],
        out_specs=pl.BlockSpec((tq, w), lambda b, h, i: (b * nq + i, h)),
        scratch_shapes=[pltpu.VMEM((2, n_groups, tq, 2 * tq), F32),
                        pltpu.VMEM((n_groups, LANES + SUM_ROWS, 2 * tq), F32)],
        compiler_params=_params("parallel", "parallel", "arbitrary"),
        name="diff_attn",
    )(lam_p, subln.reshape(1, LANES), z, z, vt)


def _moba_select_bias(qs, km, qi):
    n_blk = km.shape[0]
    parts, rest = [], km
    for _ in range(3):
        part = rest.astype(BF16)
        parts.append(part)
        rest = rest - part.astype(F32)
    terms = lax.dot_general(jnp.concatenate(parts, axis=0), qs, _NT, preferred_element_type=F32)
    gate = (terms[2 * n_blk:] + terms[n_blk:2 * n_blk]) + terms[:n_blk]
    blk = lax.broadcasted_iota(jnp.int32, gate.shape, 0)
    past = blk < qi
    gm = jnp.where(past, gate, NEG)
    sel = jnp.zeros(gate.shape, F32)
    for _ in range(MOBA_TOPK):
        mx = jnp.max(gm, axis=0, keepdims=True)
        first = jnp.min(jnp.where(gm == mx, blk, n_blk), axis=0, keepdims=True)
        pick = blk == first
        sel = jnp.where(pick, 1.0, sel)
        gm = jnp.where(pick, -jnp.inf, gm)
    return jnp.where((past & (sel > 0.5)) | (blk == qi), 0.0, 2.0 * NEG)


def _moba_attn_kernel(q_ref, k_ref, vt_ref, km_ref, o_ref, s_scr, acc_scr, bias_scr):
    tq = q_ref.shape[0]
    n_groups = q_ref.shape[1] // LANES
    qi = pl.program_id(2)
    cols = [slice(g * LANES, (g + 1) * LANES) for g in range(n_groups)]
    qs = [_stack_maps(q_ref[:, c]) for c in cols]
    for g, c in enumerate(cols):
        bias_scr[g] = _moba_select_bias(qs[g], km_ref[:, c], qi)

    def scores(g, j):
        start = pl.multiple_of(j * tq, tq)
        return lax.dot_general(k_ref[pl.ds(start, tq), cols[g]], qs[g], _NT,
                               preferred_element_type=F32)

    _flash_pipeline(n_groups, qi, tq, scores, lambda g, j: vt_ref[j, cols[g], :],
                    lambda g, j: bias_scr[g, pl.ds(j, 1), :], s_scr, acc_scr, split_heads=True)

    for g, c in enumerate(cols):
        heads = [acc_scr[g, h, :HEAD_DIM, :] * (1.0 / acc_scr[g, h, HEAD_DIM:HEAD_DIM + 1, :])
                 for h in range(2)]
        o_ref[:, c] = jnp.concatenate(heads, axis=0).T.astype(o_ref.dtype)


def _moba_attention(zq, zkv, vt, kmean, *, batch, seq, n_groups):
    t = zq.shape[0]
    tq = MOBA_BLOCK
    nq = seq // tq
    w = n_groups * LANES
    n_steps = N_SELF_GROUPS // n_groups
    return pl.pallas_call(
        _moba_attn_kernel,
        out_shape=jax.ShapeDtypeStruct((t, SELF_WIDTH), BF16),
        grid=(batch, n_steps, nq),
        in_specs=[
            pl.BlockSpec((tq, w), lambda b, h, i: (b * nq + i, h)),
            pl.BlockSpec((seq, w), lambda b, h, i: (b, h)),
            pl.BlockSpec((seq // KV_CHUNK, w, KV_CHUNK), lambda b, h, i: (b, h, 0)),
            pl.BlockSpec((None, nq, w), lambda b, h, i: (b, 0, h)),
        ],
        out_specs=pl.BlockSpec((tq, w), lambda b, h, i: (b * nq + i, h)),
        scratch_shapes=[pltpu.VMEM((2, n_groups, tq, 2 * tq), F32),
                        pltpu.VMEM((n_groups, 2, HEAD_DIM + SUM_ROWS, tq), F32),
                        pltpu.VMEM((n_groups, nq, 2 * tq), F32)],
        compiler_params=_params("parallel", "parallel", "arbitrary"),
        name="moba_attn",
    )(zq, zkv, vt, kmean)


def _attn_out_kernel(x_ref, os_ref, qm_ref, mkv_ref, w_ref, o_ref):
    tm = x_ref.shape[0]
    qm = qm_ref[...]
    mk = mkv_ref[:, :MEM_WIDTH]
    mv = mkv_ref[:, MEM_WIDTH:]
    lane = lax.broadcasted_iota(jnp.int32, (tm, MEM_WIDTH), 1)
    o_mem = jnp.zeros((tm, MEM_WIDTH), F32)
    for h in range(N_MEM_HEADS):
        in_head = (lane >= h * HEAD_DIM) & (lane < (h + 1) * HEAD_DIM)
        qh = jnp.where(in_head, qm, jnp.zeros_like(qm))
        s = lax.dot_general(qh, mk, _NT, preferred_element_type=F32)
        e = jnp.exp(s - jnp.max(s, axis=1, keepdims=True))
        p = e / jnp.sum(e, axis=1, keepdims=True)
        oh = jnp.dot(p.astype(BF16), mv, preferred_element_type=F32)
        o_mem = jnp.where(in_head, oh, o_mem)
    y = jnp.dot(os_ref[...], w_ref[:SELF_WIDTH, :], preferred_element_type=F32)
    y = y + jnp.dot(o_mem.astype(BF16), w_ref[SELF_WIDTH:, :], preferred_element_type=F32)
    o_ref[...] = x_ref[...] + y


def _attn_out(x2d, o_self, zq, qm_block, memkv, w_out, *, seq, mem_len, tm, name):
    t, d = x2d.shape
    per_batch = seq // tm
    return pl.pallas_call(
        _attn_out_kernel,
        out_shape=jax.ShapeDtypeStruct((t, d), F32),
        grid=(t // tm,),
        in_specs=[
            pl.BlockSpec((tm, d), lambda i: (i, 0)),
            pl.BlockSpec((tm, SELF_WIDTH), lambda i: (i, 0)),
            pl.BlockSpec((tm, MEM_WIDTH), lambda i: (i, qm_block)),
            pl.BlockSpec((mem_len, 2 * MEM_WIDTH), lambda i: (i // per_batch, 0)),
            pl.BlockSpec((d, d), lambda i: (0, 0)),
        ],
        out_specs=pl.BlockSpec((tm, d), lambda i: (i, 0)),
        compiler_params=_params("parallel"),
        name=name,
    )(x2d, o_self, zq, memkv, w_out)


def _mlp_kernel(x_ref, g_ref, wu_ref, wd_ref, gf_ref, o_ref, u_scr, *, final_norm, tf):
    x = x_ref[...]
    ms = jnp.mean(x * x, axis=-1, keepdims=True)
    h = (x * lax.rsqrt(ms + EPS) * g_ref[...]).astype(BF16)
    for c0 in range(0, wu_ref.shape[1], tf):
        u = jnp.maximum(jnp.dot(h, wu_ref[:, c0:c0 + tf], preferred_element_type=F32), 0.0)
        u_scr[:, c0:c0 + tf] = (u * u).astype(BF16)
    y = x + jnp.dot(u_scr[...], wd_ref[...], preferred_element_type=F32)
    if final_norm:
        ms = jnp.mean(y * y, axis=-1, keepdims=True)
        y = y * lax.rsqrt(ms + EPS) * gf_ref[...]
    o_ref[...] = y


def _mlp(x2d, g, w_up, w_down, g_final, *, final_norm, tm, tf, name):
    t, d = x2d.shape
    dff = w_up.shape[1]
    kern = functools.partial(_mlp_kernel, final_norm=final_norm, tf=tf)
    resident = dict(pipeline_mode=pl.Buffered(1))
    return pl.pallas_call(
        kern,
        out_shape=jax.ShapeDtypeStruct((t, d), F32),
        grid=(t // tm,),
        in_specs=[
            pl.BlockSpec((tm, d), lambda i: (i, 0)),
            pl.BlockSpec((1, d), lambda i: (0, 0)),
            pl.BlockSpec((d, dff), lambda i: (0, 0), **resident),
            pl.BlockSpec((dff, d), lambda i: (0, 0), **resident),
            pl.BlockSpec((1, d), lambda i: (0, 0)),
        ],
        out_specs=pl.BlockSpec((tm, d), lambda i: (i, 0)),
        scratch_shapes=[pltpu.VMEM((tm, dff), BF16)],
        compiler_params=_params("parallel"),
        name=name,
    )(x2d, g.reshape(1, d), w_up, w_down, g_final.reshape(1, d))


def _rope_tables(seq):
    half = HEAD_DIM // 2
    inv = 1.0 / (ROPE_THETA ** (jnp.arange(half, dtype=F32) / half))
    ang = jnp.arange(seq, dtype=F32)[:, None] * inv[None, :]
    cos, sin = jnp.cos(ang), jnp.sin(ang)
    reps = LANES // half
    cos_t = jnp.tile(cos, (1, reps))
    sign = jnp.tile(jnp.concatenate([-jnp.ones((half,), F32), jnp.ones((half,), F32)]), LANES // HEAD_DIM)
    sin_t = jnp.tile(sin, (1, reps)) * sign[None, :]
    return cos_t, sin_t


def kernel(x, mem, a_norm_attn, a_w_in, a_lambda, a_subln, a_mem_norm, a_w_mem_kv, a_w_out, a_norm_mlp, a_w_up, a_w_down, kv_norm, w_kv, b_norm_attn, b_w_in, b_mem_norm, b_w_mem_kv, b_w_out, b_norm_mlp, b_w_up, b_w_down, final_norm):
    batch, seq, d = x.shape
    mem_len = mem.shape[1]
    t = batch * seq
    x2d = x.reshape(t, d)
    mem2d = mem.reshape(batch * mem_len, d)
    cos_t, sin_t = _rope_tables(seq)
    bf = lambda w: w.astype(BF16)
    tm = 512
    ng = N_SELF_GROUPS
    self_q = {j: QK_SCALE * LOG2E for j in range(ng)}

    lam_init = 0.8 - 0.6 * math.exp(-0.3 * 0)
    za, vt_a = _proj(x2d, a_norm_attn[0], bf(a_w_in[0]), cos_t, sin_t, n_rope=2 * ng,
                     scales={**self_q, 3 * ng: QK_SCALE, 3 * ng + 1: QK_SCALE},
                     with_kmean=False, vt_start=2 * ng, seq=seq, tm=tm, name="a_proj")
    mkv_a, = _proj(mem2d, a_mem_norm[0], bf(a_w_mem_kv[0]), cos_t, sin_t, n_rope=0, scales={},
                   with_kmean=False, vt_start=None, seq=mem_len, tm=mem_len, name="a_memkv")
    o_self = _diff_attention(za, vt_a, a_lambda[0], a_subln[0], batch=batch, seq=seq,
                             n_groups=2, lam_init=lam_init)
    x2d = _attn_out(x2d, o_self, za, 3 * SELF_WIDTH // MEM_WIDTH, mkv_a, bf(a_w_out[0]),
                    seq=seq, mem_len=mem_len, tm=tm, name="a_attn_out")
    x2d = _mlp(x2d, a_norm_mlp[0], bf(a_w_up[0]), bf(a_w_down[0]), final_norm,
               final_norm=False, tm=512, tf=512, name="a_mlp")

    zkv, kmean, vt_b = _proj(x2d, kv_norm, bf(w_kv), cos_t, sin_t, n_rope=ng, scales={},
                             with_kmean=True, vt_start=ng, seq=seq, tm=tm, name="b_kvproj")
    kmean = kmean.reshape(batch, seq // MOBA_BLOCK, SELF_WIDTH)
    zb, = _proj(x2d, b_norm_attn[0], bf(b_w_in[0]), cos_t, sin_t, n_rope=ng,
                scales={**self_q, ng: QK_SCALE, ng + 1: QK_SCALE}, with_kmean=False, vt_start=None,
                seq=seq, tm=tm, name="b_qproj")
    mkv_b, = _proj(mem2d, b_mem_norm[0], bf(b_w_mem_kv[0]), cos_t, sin_t, n_rope=0, scales={},
                   with_kmean=False, vt_start=None, seq=mem_len, tm=mem_len, name="b_memkv")
    o_self = _moba_attention(zb, zkv, vt_b, kmean, batch=batch, seq=seq, n_groups=2)
    x2d = _attn_out(x2d, o_self, zb, SELF_WIDTH // MEM_WIDTH, mkv_b, bf(b_w_out[0]),
                    seq=seq, mem_len=mem_len, tm=tm, name="b_attn_out")
    x2d = _mlp(x2d, b_norm_mlp[0], bf(b_w_up[0]), bf(b_w_down[0]), final_norm,
               final_norm=True, tm=512, tf=512, name="b_mlp")
    return x2d.reshape(batch, seq, d)
```

```python
import functools
import math

import jax
import jax.numpy as jnp
from jax import lax
from jax.experimental import pallas as pl
from jax.experimental.pallas import tpu as pltpu

D_MODEL = 1024
HEAD_DIM = 64
SELF_WIDTH = 768
MEM_WIDTH = 256
N_MEM_HEADS = 4
D_FF = 4096
MOBA_BLOCK = 256
MOBA_TOPK = 3
ROPE_THETA = 10000.0
EPS = 1e-6
NEG = -1e30

LANES = 128
N_SELF_GROUPS = SELF_WIDTH // LANES
QK_SCALE = HEAD_DIM ** -0.5
LOG2E = math.log2(math.e)
KV_CHUNK = 256
SUM_ROWS = 16

F32 = jnp.float32
BF16 = jnp.bfloat16

_VMEM_LIMIT = 56 * 1024 * 1024


def _params(*sem):
    return pltpu.CompilerParams(dimension_semantics=sem, vmem_limit_bytes=_VMEM_LIMIT)


def _proj_kernel(x_ref, g_ref, w_ref, cos_ref, sin_ref, o_ref, *extra_refs,
                 n_rope, scales, col_chunk, with_kmean, vt_start):
    tm = x_ref.shape[0]
    n_out = w_ref.shape[1]
    extra = list(extra_refs)
    km_ref = extra.pop(0) if with_kmean else None
    vt_ref = extra.pop(0) if vt_start is not None else None
    x = x_ref[...]
    ms = jnp.mean(x * x, axis=-1, keepdims=True)
    h = (x * lax.rsqrt(ms + EPS) * g_ref[...]).astype(BF16)
    lane = lax.broadcasted_iota(jnp.int32, (tm, LANES), 1)
    first_half = (lane & (HEAD_DIM - 1)) < HEAD_DIM // 2
    if n_rope:
        cos = cos_ref[...]
        sin = sin_ref[...]
    for c0 in range(0, n_out, col_chunk):
        z = jnp.dot(h, w_ref[:, c0:c0 + col_chunk], preferred_element_type=F32)
        for jj in range(col_chunk // LANES):
            j = c0 // LANES + jj
            blk = z[:, jj * LANES:(jj + 1) * LANES]
            if j < n_rope:
                swap = jnp.where(first_half, pltpu.roll(blk, LANES - 32, 1), pltpu.roll(blk, 32, 1))
                blk = blk * cos + swap * sin
                if km_ref is not None:
                    km_ref[0, :, j * LANES:(j + 1) * LANES] = jnp.mean(
                        blk.reshape(tm // MOBA_BLOCK, MOBA_BLOCK, LANES), axis=1)
            if j in scales:
                blk = blk * scales[j]
            o_ref[:, j * LANES:(j + 1) * LANES] = blk.astype(o_ref.dtype)
            if vt_ref is not None and vt_start <= j < vt_start + vt_ref.shape[1] // LANES:
                g = j - vt_start
                for c in range(tm // KV_CHUNK):
                    vt_ref[c, g * LANES:(g + 1) * LANES, :] = (
                        blk[c * KV_CHUNK:(c + 1) * KV_CHUNK, :].T.astype(BF16))


def _proj(x2d, g, w, cos_t, sin_t, *, n_rope, scales, with_kmean, vt_start, seq, tm, name,
          vt_width=SELF_WIDTH):
    t, d = x2d.shape
    n_out = w.shape[1]
    n_pos_blocks = seq // tm
    col_chunk = 512 if n_out % 512 == 0 else 256
    out_shape = [jax.ShapeDtypeStruct((t, n_out), BF16)]
    out_specs = [pl.BlockSpec((tm, n_out), lambda i: (i, 0))]
    if with_kmean:
        out_shape.append(jax.ShapeDtypeStruct((t // tm, tm // MOBA_BLOCK, n_rope * LANES), F32))
        out_specs.append(pl.BlockSpec((1, tm // MOBA_BLOCK, n_rope * LANES), lambda i: (i, 0, 0)))
    if vt_start is not None:
        out_shape.append(jax.ShapeDtypeStruct((t // KV_CHUNK, vt_width, KV_CHUNK), BF16))
        out_specs.append(pl.BlockSpec((tm // KV_CHUNK, vt_width, KV_CHUNK), lambda i: (i, 0, 0)))
    kern = functools.partial(_proj_kernel, n_rope=n_rope, scales=dict(scales), col_chunk=col_chunk,
                             with_kmean=with_kmean, vt_start=vt_start)
    return pl.pallas_call(
        kern,
        out_shape=out_shape,
        grid=(t // tm,),
        in_specs=[
            pl.BlockSpec((tm, d), lambda i: (i, 0)),
            pl.BlockSpec((1, d), lambda i: (0, 0)),
            pl.BlockSpec((d, n_out), lambda i: (0, 0)),
            pl.BlockSpec((tm, LANES), lambda i: (i % n_pos_blocks, 0)),
            pl.BlockSpec((tm, LANES), lambda i: (i % n_pos_blocks, 0)),
        ],
        out_specs=out_specs,
        compiler_params=_params("parallel"),
        name=name,
    )(x2d, g.reshape(1, d), w, cos_t, sin_t)


_NT = (((1,), (1,)), ((), ()))


def _stack_maps(q):
    tq = q.shape[0]
    lane = lax.broadcasted_iota(jnp.int32, (tq, LANES), 1)
    zero = jnp.zeros_like(q)
    return jnp.concatenate(
        [jnp.where(lane < HEAD_DIM, q, zero), jnp.where(lane >= HEAD_DIM, q, zero)], axis=0)


def _causal_mask(st, tq):
    key = lax.broadcasted_iota(jnp.int32, st.shape, 0)
    qry = lax.broadcasted_iota(jnp.int32, st.shape, 1) & (tq - 1)
    return jnp.where(key <= qry, st, NEG)


def _flash_pipeline(n_groups, n_past, tq, scores_fn, vt_fn, bias_fn, s_scr, acc_scr, split_heads):
    n = 2 * tq

    def produce(j, buf):
        col_max = []
        for g in range(n_groups):
            st = scores_fn(g, j)
            s_scr[buf, g] = st
            col_max.append(jnp.max(st, axis=0, keepdims=True))
        return tuple(col_max)

    ones_rows = jnp.ones((SUM_ROWS, tq), BF16)

    def consume(j, buf, stats, col_max, diagonal):
        out = []
        for g in range(n_groups):
            st = s_scr[buf, g]
            if diagonal:
                st = _causal_mask(st, tq)
                m_cur = jnp.max(st, axis=0, keepdims=True)
            else:
                m_cur = col_max[g]
            if bias_fn is not None:
                bias = bias_fn(g, j)
                m_cur = m_cur + bias
            m_next = jnp.maximum(stats[g], m_cur)
            alpha = jnp.exp2(stats[g] - m_next)
            shift = m_next if bias_fn is None else m_next - bias
            p = jnp.exp2((st - shift).astype(BF16))
            vt = vt_fn(g, j)
            if split_heads:
                for h in range(2):
                    vt_h = jnp.concatenate([vt[h * HEAD_DIM:(h + 1) * HEAD_DIM], ones_rows], axis=0)
                    lanes = slice(h * tq, (h + 1) * tq)
                    acc_scr[g, h] = alpha[:, lanes] * acc_scr[g, h] + jnp.dot(
                        vt_h, p[:, lanes], preferred_element_type=F32)
            else:
                acc_scr[g] = alpha * acc_scr[g] + jnp.dot(
                    jnp.concatenate([vt, ones_rows], axis=0), p, preferred_element_type=F32)
            out.append(m_next)
        return tuple(out)

    def step(j, buf, carry):
        nxt = produce(j + 1, 1 - buf)
        return consume(j, buf, carry[0], carry[1], False), nxt

    def pair(i, carry):
        return step(2 * i + 1, 1, step(2 * i, 0, carry))

    def quad(i, carry):
        return pair(2 * i + 1, pair(2 * i, carry))

    def octet(i, carry):
        return quad(2 * i + 1, quad(2 * i, carry))

    acc_scr[...] = jnp.zeros(acc_scr.shape, F32)
    init = tuple(jnp.full((1, n), NEG, F32) for _ in range(n_groups))
    n_octets = n_past // 8
    carry = lax.fori_loop(0, n_octets, octet, (init, produce(0, 0)))
    carry = lax.fori_loop(2 * n_octets, n_past // 4, quad, carry)
    carry = lax.fori_loop(2 * (n_past // 4), n_past // 2, pair, carry)

    def odd_tail(carry):
        stats, col_max = step(n_past - 1, 0, carry)
        return consume(n_past, 1, stats, col_max, True)

    def even_tail(carry):
        return consume(n_past, 0, carry[0], carry[1], True)

    return lax.cond(n_past % 2 == 1, odd_tail, even_tail, carry)


def _diff_attn_kernel(lam_ref, g_ref, q_ref, k_ref, vt_ref, o_ref, s_scr, acc_scr, *, lam_init):
    tq = q_ref.shape[0]
    n_groups = q_ref.shape[1] // LANES
    qi = pl.program_id(2)
    cols = [slice(g * LANES, (g + 1) * LANES) for g in range(n_groups)]
    qs = [_stack_maps(q_ref[:, c]) for c in cols]

    def scores(g, j):
        start = pl.multiple_of(j * tq, tq)
        return lax.dot_general(k_ref[pl.ds(start, tq), cols[g]], qs[g], _NT,
                               preferred_element_type=F32)

    _flash_pipeline(n_groups, qi, tq, scores, lambda g, j: vt_ref[j, cols[g], :], None,
                    s_scr, acc_scr, split_heads=False)

    lp = lam_ref[...]
    lam = (jnp.exp(jnp.sum(lp[0:1] * lp[1:2], axis=1, keepdims=True))
           - jnp.exp(jnp.sum(lp[2:3] * lp[3:4], axis=1, keepdims=True)) + lam_init)
    for g, c in enumerate(cols):
        o_t = acc_scr[g, :LANES, :] * (1.0 / acc_scr[g, LANES:LANES + 1, :])
        od = (o_t[:, :tq] - lam * o_t[:, tq:]).T
        ms = jnp.mean(od * od, axis=-1, keepdims=True)
        y = od * lax.rsqrt(ms + EPS) * g_ref[...]
        o_ref[:, c] = (y * (1.0 - lam_init)).astype(o_ref.dtype)


def _diff_attention(z, vt, lam_p, subln, *, batch, seq, n_groups, lam_init):
    t = z.shape[0]
    tq = KV_CHUNK
    nq = seq // tq
    w = n_groups * LANES
    n_steps = N_SELF_GROUPS // n_groups
    kern = functools.partial(_diff_attn_kernel, lam_init=lam_init)
    return pl.pallas_call(
        kern,
        out_shape=jax.ShapeDtypeStruct((t, SELF_WIDTH), BF16),
        grid=(batch, n_steps, nq),
        in_specs=[
            pl.BlockSpec((4, HEAD_DIM), lambda b, h, i: (0, 0)),
            pl.BlockSpec((1, LANES), lambda b, h, i: (0, 0)),
            pl.BlockSpec((tq, w), lambda b, h, i: (b * nq + i, h)),
            pl.BlockSpec((seq, w), lambda b, h, i: (b, n_steps + h)),
            pl.BlockSpec((seq // KV_CHUNK, w, KV_CHUNK), lambda b, h, i: (b, h, 0)),
        ],
        out_specs=pl.BlockSpec((tq, w), lambda b, h, i: (b * nq + i, h)),
        scratch_shapes=[pltpu.VMEM((2, n_groups, tq, 2 * tq), F32),
                        pltpu.VMEM((n_groups, LANES + SUM_ROWS, 2 * tq), F32)],
        compiler_params=_params("parallel", "parallel", "arbitrary"),
        name="diff_attn",
    )(lam_p, subln.reshape(1, LANES), z, z, vt)


def _moba_select_bias(qs, km, qi):
    n_blk = km.shape[0]
    parts, rest = [], km
    for _ in range(3):
        part = rest.astype(BF16)
        parts.append(part)
        rest = rest - part.astype(F32)
    terms = lax.dot_general(jnp.concatenate(parts, axis=0), qs, _NT, preferred_element_type=F32)
    gate = (terms[2 * n_blk:] + terms[n_blk:2 * n_blk]) + terms[:n_blk]
    blk = lax.broadcasted_iota(jnp.int32, gate.shape, 0)
    past = blk < qi
    gm = jnp.where(past, gate, NEG)
    sel = jnp.zeros(gate.shape, F32)
    for _ in range(MOBA_TOPK):
        mx = jnp.max(gm, axis=0, keepdims=True)
        first = jnp.min(jnp.where(gm == mx, blk, n_blk), axis=0, keepdims=True)
        pick = blk == first
        sel = jnp.where(pick, 1.0, sel)
        gm = jnp.where(pick, -jnp.inf, gm)
    return jnp.where((past & (sel > 0.5)) | (blk == qi), 0.0, 2.0 * NEG)


def _moba_attn_kernel(q_ref, k_ref, vt_ref, km_ref, o_ref, s_scr, acc_scr, bias_scr):
    tq = q_ref.shape[0]
    n_groups = q_ref.shape[1] // LANES
    qi = pl.program_id(2)
    cols = [slice(g * LANES, (g + 1) * LANES) for g in range(n_groups)]
    qs = [_stack_maps(q_ref[:, c]) for c in cols]
    for g, c in enumerate(cols):
        bias_scr[g] = _moba_select_bias(qs[g], km_ref[:, c], qi)

    def scores(g, j):
        start = pl.multiple_of(j * tq, tq)
        return lax.dot_general(k_ref[pl.ds(start, tq), cols[g]], qs[g], _NT,
                               preferred_element_type=F32)

    _flash_pipeline(n_groups, qi, tq, scores, lambda g, j: vt_ref[j, cols[g], :],
                    lambda g, j: bias_scr[g, pl.ds(j, 1), :], s_scr, acc_scr, split_heads=True)

    for g, c in enumerate(cols):
        heads = [acc_scr[g, h, :HEAD_DIM, :] * (1.0 / acc_scr[g, h, HEAD_DIM:HEAD_DIM + 1, :])
                 for h in range(2)]
        o_ref[:, c] = jnp.concatenate(heads, axis=0).T.astype(o_ref.dtype)


def _moba_attention(zq, zkv, vt, kmean, *, batch, seq, n_groups):
    t = zq.shape[0]
    tq = MOBA_BLOCK
    nq = seq // tq
    w = n_groups * LANES
    n_steps = N_SELF_GROUPS // n_groups
    return pl.pallas_call(
        _moba_attn_kernel,
        out_shape=jax.ShapeDtypeStruct((t, SELF_WIDTH), BF16),
        grid=(batch, n_steps, nq),
        in_specs=[
            pl.BlockSpec((tq, w), lambda b, h, i: (b * nq + i, h)),
            pl.BlockSpec((seq, w), lambda b, h, i: (b, h)),
            pl.BlockSpec((seq // KV_CHUNK, w, KV_CHUNK), lambda b, h, i: (b, h, 0)),
            pl.BlockSpec((None, nq, w), lambda b, h, i: (b, 0, h)),
        ],
        out_specs=pl.BlockSpec((tq, w), lambda b, h, i: (b * nq + i, h)),
        scratch_shapes=[pltpu.VMEM((2, n_groups, tq, 2 * tq), F32),
                        pltpu.VMEM((n_groups, 2, HEAD_DIM + SUM_ROWS, tq), F32),
                        pltpu.VMEM((n_groups, nq, 2 * tq), F32)],
        compiler_params=_params("parallel", "parallel", "arbitrary"),
        name="moba_attn",
    )(zq, zkv, vt, kmean)


def _attn_out_kernel(x_ref, os_ref, qm_ref, mk_ref, mvt_ref, w_ref, o_ref):
    mem_len = mk_ref.shape[0]
    mk = mk_ref[...]
    lane = lax.broadcasted_iota(jnp.int32, mk.shape, 1)
    zero = jnp.zeros_like(mk)
    mk_heads = jnp.concatenate(
        [jnp.where((lane >= h * HEAD_DIM) & (lane < (h + 1) * HEAD_DIM), mk, zero)
         for h in range(N_MEM_HEADS)], axis=0)
    s = lax.dot_general(mk_heads, qm_ref[...], _NT, preferred_element_type=F32)
    ones_rows = jnp.ones((SUM_ROWS, mem_len), BF16)
    heads = []
    for h in range(N_MEM_HEADS):
        sh = s[h * mem_len:(h + 1) * mem_len]
        p = jnp.exp2((sh - jnp.max(sh, axis=0, keepdims=True)).astype(BF16))
        vt_ones = jnp.concatenate([mvt_ref[h * HEAD_DIM:(h + 1) * HEAD_DIM, :], ones_rows], axis=0)
        oh = jnp.dot(vt_ones, p, preferred_element_type=F32)
        heads.append(oh[:HEAD_DIM] * (1.0 / oh[HEAD_DIM:HEAD_DIM + 1]))
    o_mem = jnp.concatenate(heads, axis=0).T.astype(BF16)
    y = jnp.dot(os_ref[...], w_ref[:SELF_WIDTH, :], preferred_element_type=F32)
    y = y + jnp.dot(o_mem, w_ref[SELF_WIDTH:, :], preferred_element_type=F32)
    o_ref[...] = x_ref[...] + y


def _attn_out(x2d, o_self, zq, qm_block, memkv, memvt, w_out, *, seq, mem_len, tm, name):
    t, d = x2d.shape
    per_batch = seq // tm
    return pl.pallas_call(
        _attn_out_kernel,
        out_shape=jax.ShapeDtypeStruct((t, d), F32),
        grid=(t // tm,),
        in_specs=[
            pl.BlockSpec((tm, d), lambda i: (i, 0)),
            pl.BlockSpec((tm, SELF_WIDTH), lambda i: (i, 0)),
            pl.BlockSpec((tm, MEM_WIDTH), lambda i: (i, qm_block)),
            pl.BlockSpec((mem_len, MEM_WIDTH), lambda i: (i // per_batch, 0)),
            pl.BlockSpec((None, MEM_WIDTH, mem_len), lambda i: (i // per_batch, 0, 0)),
            pl.BlockSpec((d, d), lambda i: (0, 0)),
        ],
        out_specs=pl.BlockSpec((tm, d), lambda i: (i, 0)),
        compiler_params=_params("parallel"),
        name=name,
    )(x2d, o_self, zq, memkv, memvt, w_out)


def _mlp_kernel(x_ref, g_ref, wu_ref, wd_ref, gf_ref, o_ref, u_scr, *, final_norm, tf):
    x = x_ref[...]
    ms = jnp.mean(x * x, axis=-1, keepdims=True)
    h = (x * lax.rsqrt(ms + EPS) * g_ref[...]).astype(BF16)
    for c0 in range(0, wu_ref.shape[1], tf):
        u = jnp.maximum(jnp.dot(h, wu_ref[:, c0:c0 + tf], preferred_element_type=F32), 0.0)
        u_scr[:, c0:c0 + tf] = (u * u).astype(BF16)
    y = x + jnp.dot(u_scr[...], wd_ref[...], preferred_element_type=F32)
    if final_norm:
        ms = jnp.mean(y * y, axis=-1, keepdims=True)
        y = y * lax.rsqrt(ms + EPS) * gf_ref[...]
    o_ref[...] = y


def _mlp(x2d, g, w_up, w_down, g_final, *, final_norm, tm, tf, name):
    t, d = x2d.shape
    dff = w_up.shape[1]
    kern = functools.partial(_mlp_kernel, final_norm=final_norm, tf=tf)
    resident = dict(pipeline_mode=pl.Buffered(1))
    return pl.pallas_call(
        kern,
        out_shape=jax.ShapeDtypeStruct((t, d), F32),
        grid=(t // tm,),
        in_specs=[
            pl.BlockSpec((tm, d), lambda i: (i, 0)),
            pl.BlockSpec((1, d), lambda i: (0, 0)),
            pl.BlockSpec((d, dff), lambda i: (0, 0), **resident),
            pl.BlockSpec((dff, d), lambda i: (0, 0), **resident),
            pl.BlockSpec((1, d), lambda i: (0, 0)),
        ],
        out_specs=pl.BlockSpec((tm, d), lambda i: (i, 0)),
        scratch_shapes=[pltpu.VMEM((tm, dff), BF16)],
        compiler_params=_params("parallel"),
        name=name,
    )(x2d, g.reshape(1, d), w_up, w_down, g_final.reshape(1, d))


def _rope_tables(seq):
    half = HEAD_DIM // 2
    inv = 1.0 / (ROPE_THETA ** (jnp.arange(half, dtype=F32) / half))
    ang = jnp.arange(seq, dtype=F32)[:, None] * inv[None, :]
    cos, sin = jnp.cos(ang), jnp.sin(ang)
    reps = LANES // half
    cos_t = jnp.tile(cos, (1, reps))
    sign = jnp.tile(jnp.concatenate([-jnp.ones((half,), F32), jnp.ones((half,), F32)]), LANES // HEAD_DIM)
    sin_t = jnp.tile(sin, (1, reps)) * sign[None, :]
    return cos_t, sin_t


def kernel(x, mem, a_norm_attn, a_w_in, a_lambda, a_subln, a_mem_norm, a_w_mem_kv, a_w_out, a_norm_mlp, a_w_up, a_w_down, kv_norm, w_kv, b_norm_attn, b_w_in, b_mem_norm, b_w_mem_kv, b_w_out, b_norm_mlp, b_w_up, b_w_down, final_norm):
    batch, seq, d = x.shape
    mem_len = mem.shape[1]
    t = batch * seq
    x2d = x.reshape(t, d)
    mem2d = mem.reshape(batch * mem_len, d)
    cos_t, sin_t = _rope_tables(seq)
    bf = lambda w: w.astype(BF16)
    tm = 512
    ng = N_SELF_GROUPS
    q_scale = QK_SCALE * LOG2E
    mem_groups = MEM_WIDTH // LANES

    def memkv(g, w, name):
        return _proj(mem2d, g, bf(w), cos_t, sin_t, n_rope=0, scales={}, with_kmean=False,
                     vt_start=mem_groups, vt_width=MEM_WIDTH, seq=mem_len, tm=mem_len, name=name)

    lam_init = 0.8 - 0.6 * math.exp(-0.3 * 0)
    za, vt_a = _proj(x2d, a_norm_attn[0], bf(a_w_in[0]), cos_t, sin_t, n_rope=2 * ng,
                     scales={j: q_scale for j in (*range(ng), 3 * ng, 3 * ng + 1)},
                     with_kmean=False, vt_start=2 * ng, seq=seq, tm=tm, name="a_proj")
    mkv_a, mvt_a = memkv(a_mem_norm[0], a_w_mem_kv[0], "a_memkv")
    o_self = _diff_attention(za, vt_a, a_lambda[0], a_subln[0], batch=batch, seq=seq,
                             n_groups=2, lam_init=lam_init)
    x2d = _attn_out(x2d, o_self, za, 3 * SELF_WIDTH // MEM_WIDTH, mkv_a, mvt_a, bf(a_w_out[0]),
                    seq=seq, mem_len=mem_len, tm=tm, name="a_attn_out")
    x2d = _mlp(x2d, a_norm_mlp[0], bf(a_w_up[0]), bf(a_w_down[0]), final_norm,
               final_norm=False, tm=512, tf=512, name="a_mlp")

    zkv, kmean, vt_b = _proj(x2d, kv_norm, bf(w_kv), cos_t, sin_t, n_rope=ng, scales={},
                             with_kmean=True, vt_start=ng, seq=seq, tm=tm, name="b_kvproj")
    kmean = kmean.reshape(batch, seq // MOBA_BLOCK, SELF_WIDTH)
    zb, = _proj(x2d, b_norm_attn[0], bf(b_w_in[0]), cos_t, sin_t, n_rope=ng,
                scales={j: q_scale for j in range(ng + mem_groups)}, with_kmean=False, vt_start=None,
                seq=seq, tm=tm, name="b_qproj")
    mkv_b, mvt_b = memkv(b_mem_norm[0], b_w_mem_kv[0], "b_memkv")
    o_self = _moba_attention(zb, zkv, vt_b, kmean, batch=batch, seq=seq, n_groups=2)
    x2d = _attn_out(x2d, o_self, zb, SELF_WIDTH // MEM_WIDTH, mkv_b, mvt_b, bf(b_w_out[0]),
                    seq=seq, mem_len=mem_len, tm=tm, name="b_attn_out")
    x2d = _mlp(x2d, b_norm_mlp[0], bf(b_w_up[0]), bf(b_w_down[0]), final_norm,
               final_norm=True, tm=512, tf=512, name="b_mlp")
    return x2d.reshape(batch, seq, d)
```

```python
import functools
import math

import jax
import jax.numpy as jnp
from jax import lax
from jax.experimental import pallas as pl
from jax.experimental.pallas import tpu as pltpu

D_MODEL = 1024
HEAD_DIM = 64
SELF_WIDTH = 768
MEM_WIDTH = 256
N_MEM_HEADS = 4
D_FF = 4096
MOBA_BLOCK = 256
MOBA_TOPK = 3
ROPE_THETA = 10000.0
EPS = 1e-6
NEG = -1e30

LANES = 128
N_SELF_GROUPS = SELF_WIDTH // LANES
QK_SCALE = HEAD_DIM ** -0.5
LOG2E = math.log2(math.e)
KV_CHUNK = 256
SUM_ROWS = 16

F32 = jnp.float32
BF16 = jnp.bfloat16

_VMEM_LIMIT = 56 * 1024 * 1024


def _params(*sem):
    return pltpu.CompilerParams(dimension_semantics=sem, vmem_limit_bytes=_VMEM_LIMIT)


def _proj_kernel(x_ref, g_ref, w_ref, cos_ref, sin_ref, o_ref, *extra_refs,
                 n_rope, scales, col_chunk, with_kmean, vt_start):
    tm = x_ref.shape[0]
    n_out = w_ref.shape[1]
    extra = list(extra_refs)
    km_ref = extra.pop(0) if with_kmean else None
    vt_ref = extra.pop(0) if vt_start is not None else None
    x = x_ref[...]
    ms = jnp.mean(x * x, axis=-1, keepdims=True)
    h = (x * lax.rsqrt(ms + EPS) * g_ref[...]).astype(BF16)
    lane = lax.broadcasted_iota(jnp.int32, (tm, LANES), 1)
    first_half = (lane & (HEAD_DIM - 1)) < HEAD_DIM // 2
    if n_rope:
        cos = cos_ref[...]
        sin = sin_ref[...]
    for c0 in range(0, n_out, col_chunk):
        z = jnp.dot(h, w_ref[:, c0:c0 + col_chunk], preferred_element_type=F32)
        for jj in range(col_chunk // LANES):
            j = c0 // LANES + jj
            blk = z[:, jj * LANES:(jj + 1) * LANES]
            if j < n_rope:
                swap = jnp.where(first_half, pltpu.roll(blk, LANES - 32, 1), pltpu.roll(blk, 32, 1))
                blk = blk * cos + swap * sin
                if km_ref is not None:
                    km_ref[0, :, j * LANES:(j + 1) * LANES] = jnp.mean(
                        blk.reshape(tm // MOBA_BLOCK, MOBA_BLOCK, LANES), axis=1)
            if j in scales:
                blk = blk * scales[j]
            o_ref[:, j * LANES:(j + 1) * LANES] = blk.astype(o_ref.dtype)
            if vt_ref is not None and vt_start <= j < vt_start + vt_ref.shape[1] // LANES:
                g = j - vt_start
                for c in range(tm // KV_CHUNK):
                    vt_ref[c, g * LANES:(g + 1) * LANES, :] = (
                        blk[c * KV_CHUNK:(c + 1) * KV_CHUNK, :].T.astype(BF16))


def _proj(x2d, g, w, cos_t, sin_t, *, n_rope, scales, with_kmean, vt_start, seq, tm, name,
          vt_width=SELF_WIDTH):
    t, d = x2d.shape
    n_out = w.shape[1]
    n_pos_blocks = seq // tm
    col_chunk = 512 if n_out % 512 == 0 else 256
    out_shape = [jax.ShapeDtypeStruct((t, n_out), BF16)]
    out_specs = [pl.BlockSpec((tm, n_out), lambda i: (i, 0))]
    if with_kmean:
        out_shape.append(jax.ShapeDtypeStruct((t // tm, tm // MOBA_BLOCK, n_rope * LANES), F32))
        out_specs.append(pl.BlockSpec((1, tm // MOBA_BLOCK, n_rope * LANES), lambda i: (i, 0, 0)))
    if vt_start is not None:
        out_shape.append(jax.ShapeDtypeStruct((t // KV_CHUNK, vt_width, KV_CHUNK), BF16))
        out_specs.append(pl.BlockSpec((tm // KV_CHUNK, vt_width, KV_CHUNK), lambda i: (i, 0, 0)))
    kern = functools.partial(_proj_kernel, n_rope=n_rope, scales=dict(scales), col_chunk=col_chunk,
                             with_kmean=with_kmean, vt_start=vt_start)
    return pl.pallas_call(
        kern,
        out_shape=out_shape,
        grid=(t // tm,),
        in_specs=[
            pl.BlockSpec((tm, d), lambda i: (i, 0)),
            pl.BlockSpec((1, d), lambda i: (0, 0)),
            pl.BlockSpec((d, n_out), lambda i: (0, 0)),
            pl.BlockSpec((tm, LANES), lambda i: (i % n_pos_blocks, 0)),
            pl.BlockSpec((tm, LANES), lambda i: (i % n_pos_blocks, 0)),
        ],
        out_specs=out_specs,
        compiler_params=_params("parallel"),
        name=name,
    )(x2d, g.reshape(1, d), w, cos_t, sin_t)


_NT = (((1,), (1,)), ((), ()))


def _stack_maps(q):
    tq = q.shape[0]
    lane = lax.broadcasted_iota(jnp.int32, (tq, LANES), 1)
    zero = jnp.zeros_like(q)
    return jnp.concatenate(
        [jnp.where(lane < HEAD_DIM, q, zero), jnp.where(lane >= HEAD_DIM, q, zero)], axis=0)


def _causal_mask(st, tq):
    key = lax.broadcasted_iota(jnp.int32, st.shape, 0)
    qry = lax.broadcasted_iota(jnp.int32, st.shape, 1) & (tq - 1)
    return jnp.where(key <= qry, st, NEG)


def _flash_pipeline(n_groups, n_past, tq, scores_fn, vt_fn, bias_fn, finalize_fn, s_scr, acc_scr,
                    split_heads):
    n = 2 * tq

    def produce(j, buf):
        col_max = []
        for g in range(n_groups):
            st = scores_fn(g, j)
            s_scr[buf, g] = st
            col_max.append(jnp.max(st, axis=0, keepdims=True))
        return tuple(col_max)

    ones_rows = jnp.ones((SUM_ROWS, tq), BF16)

    def consume(j, buf, stats, col_max, diagonal):
        out = []
        for g in range(n_groups):
            st = s_scr[buf, g]
            if diagonal:
                st = _causal_mask(st, tq)
                m_cur = jnp.max(st, axis=0, keepdims=True)
            else:
                m_cur = col_max[g]
            if bias_fn is not None:
                bias = bias_fn(g, j)
                m_cur = m_cur + bias
            m_next = jnp.maximum(stats[g], m_cur)
            alpha = jnp.exp2(stats[g] - m_next)
            shift = m_next if bias_fn is None else m_next - bias
            p = jnp.exp2((st - shift).astype(BF16))
            vt = vt_fn(g, j)
            if split_heads:
                acc = []
                for h in range(2):
                    vt_h = jnp.concatenate([vt[h * HEAD_DIM:(h + 1) * HEAD_DIM], ones_rows], axis=0)
                    lanes = slice(h * tq, (h + 1) * tq)
                    acc.append(alpha[:, lanes] * acc_scr[g, h] + jnp.dot(
                        vt_h, p[:, lanes], preferred_element_type=F32))
            else:
                acc = alpha * acc_scr[g] + jnp.dot(
                    jnp.concatenate([vt, ones_rows], axis=0), p, preferred_element_type=F32)
            if diagonal:
                finalize_fn(g, acc)
            elif split_heads:
                acc_scr[g, 0], acc_scr[g, 1] = acc
            else:
                acc_scr[g] = acc
            out.append(m_next)
        return tuple(out)

    def step(j, buf, carry):
        nxt = produce(j + 1, 1 - buf)
        return consume(j, buf, carry[0], carry[1], False), nxt

    def pair(i, carry):
        return step(2 * i + 1, 1, step(2 * i, 0, carry))

    def quad(i, carry):
        return pair(2 * i + 1, pair(2 * i, carry))

    def octet(i, carry):
        return quad(2 * i + 1, quad(2 * i, carry))

    acc_scr[...] = jnp.zeros(acc_scr.shape, F32)
    init = tuple(jnp.full((1, n), NEG, F32) for _ in range(n_groups))
    n_octets = n_past // 8
    carry = lax.fori_loop(0, n_octets, octet, (init, produce(0, 0)))
    carry = lax.fori_loop(2 * n_octets, n_past // 4, quad, carry)
    carry = lax.fori_loop(2 * (n_past // 4), n_past // 2, pair, carry)

    def odd_tail(carry):
        stats, col_max = step(n_past - 1, 0, carry)
        return consume(n_past, 1, stats, col_max, True)

    def even_tail(carry):
        return consume(n_past, 0, carry[0], carry[1], True)

    return lax.cond(n_past % 2 == 1, odd_tail, even_tail, carry)


def _diff_attn_kernel(lam_ref, g_ref, q_ref, k_ref, vt_ref, o_ref, s_scr, acc_scr, *, lam_init):
    tq = q_ref.shape[0]
    n_groups = q_ref.shape[1] // LANES
    qi = pl.program_id(2)
    cols = [slice(g * LANES, (g + 1) * LANES) for g in range(n_groups)]
    qs = [_stack_maps(q_ref[:, c]) for c in cols]

    def scores(g, j):
        start = pl.multiple_of(j * tq, tq)
        return lax.dot_general(k_ref[pl.ds(start, tq), cols[g]], qs[g], _NT,
                               preferred_element_type=F32)

    lp = lam_ref[...]
    lam = (jnp.exp(jnp.sum(lp[0:1] * lp[1:2], axis=1, keepdims=True))
           - jnp.exp(jnp.sum(lp[2:3] * lp[3:4], axis=1, keepdims=True)) + lam_init)

    def finalize(g, acc):
        o_t = acc[:LANES] * (1.0 / acc[LANES:LANES + 1])
        od_t = o_t[:, :tq] - lam * o_t[:, tq:]
        ms = jnp.mean(od_t * od_t, axis=0, keepdims=True)
        y = (od_t * lax.rsqrt(ms + EPS)).T * g_ref[...]
        o_ref[:, cols[g]] = (y * (1.0 - lam_init)).astype(o_ref.dtype)

    _flash_pipeline(n_groups, qi, tq, scores, lambda g, j: vt_ref[j, cols[g], :], None, finalize,
                    s_scr, acc_scr, split_heads=False)


def _diff_attention(z, vt, lam_p, subln, *, batch, seq, n_groups, lam_init):
    t = z.shape[0]
    tq = KV_CHUNK
    nq = seq // tq
    w = n_groups * LANES
    n_steps = N_SELF_GROUPS // n_groups
    kern = functools.partial(_diff_attn_kernel, lam_init=lam_init)
    return pl.pallas_call(
        kern,
        out_shape=jax.ShapeDtypeStruct((t, SELF_WIDTH), BF16),
        grid=(batch, n_steps, nq),
        in_specs=[
            pl.BlockSpec((4, HEAD_DIM), lambda b, h, i: (0, 0)),
            pl.BlockSpec((1, LANES), lambda b, h, i: (0, 0)),
            pl.BlockSpec((tq, w), lambda b, h, i: (b * nq + i, h)),
            pl.BlockSpec((seq, w), lambda b, h, i: (b, n_steps + h)),
            pl.BlockSpec((seq // KV_CHUNK, w, KV_CHUNK), lambda b, h, i: (b, h, 0)),
        ],
        out_specs=pl.BlockSpec((tq, w), lambda b, h, i: (b * nq + i, h)),
        scratch_shapes=[pltpu.VMEM((2, n_groups, tq, 2 * tq), F32),
                        pltpu.VMEM((n_groups, LANES + SUM_ROWS, 2 * tq), F32)],
        compiler_params=_params("parallel", "parallel", "arbitrary"),
        name="diff_attn",
    )(lam_p, subln.reshape(1, LANES), z, z, vt)


def _moba_select_bias(qs, km, qi):
    n_blk = km.shape[0]
    parts, rest = [], km
    for _ in range(3):
        part = rest.astype(BF16)
        parts.append(part)
        rest = rest - part.astype(F32)
    terms = lax.dot_general(jnp.concatenate(parts, axis=0), qs, _NT, preferred_element_type=F32)
    gate = (terms[2 * n_blk:] + terms[n_blk:2 * n_blk]) + terms[:n_blk]
    blk = lax.broadcasted_iota(jnp.int32, gate.shape, 0)
    past = blk < qi
    gm = jnp.where(past, gate, NEG)
    sel = jnp.zeros(gate.shape, F32)
    for _ in range(MOBA_TOPK):
        mx = jnp.max(gm, axis=0, keepdims=True)
        first = jnp.min(jnp.where(gm == mx, blk, n_blk), axis=0, keepdims=True)
        pick = blk == first
        sel = jnp.where(pick, 1.0, sel)
        gm = jnp.where(pick, -jnp.inf, gm)
    return jnp.where((past & (sel > 0.5)) | (blk == qi), 0.0, 2.0 * NEG)


def _moba_attn_kernel(q_ref, k_ref, vt_ref, km_ref, o_ref, s_scr, acc_scr, bias_scr):
    tq = q_ref.shape[0]
    n_groups = q_ref.shape[1] // LANES
    qi = pl.program_id(2)
    cols = [slice(g * LANES, (g + 1) * LANES) for g in range(n_groups)]
    qs = [_stack_maps(q_ref[:, c]) for c in cols]
    for g, c in enumerate(cols):
        bias_scr[g] = _moba_select_bias(qs[g], km_ref[:, c], qi)

    def scores(g, j):
        start = pl.multiple_of(j * tq, tq)
        return lax.dot_general(k_ref[pl.ds(start, tq), cols[g]], qs[g], _NT,
                               preferred_element_type=F32)

    def finalize(g, acc):
        heads = [a[:HEAD_DIM] * (1.0 / a[HEAD_DIM:HEAD_DIM + 1]) for a in acc]
        o_ref[:, cols[g]] = jnp.concatenate(heads, axis=0).T.astype(o_ref.dtype)

    _flash_pipeline(n_groups, qi, tq, scores, lambda g, j: vt_ref[j, cols[g], :],
                    lambda g, j: bias_scr[g, pl.ds(j, 1), :], finalize, s_scr, acc_scr,
                    split_heads=True)


def _moba_attention(zq, zkv, vt, kmean, *, batch, seq, n_groups):
    t = zq.shape[0]
    tq = MOBA_BLOCK
    nq = seq // tq
    w = n_groups * LANES
    n_steps = N_SELF_GROUPS // n_groups
    return pl.pallas_call(
        _moba_attn_kernel,
        out_shape=jax.ShapeDtypeStruct((t, SELF_WIDTH), BF16),
        grid=(batch, n_steps, nq),
        in_specs=[
            pl.BlockSpec((tq, w), lambda b, h, i: (b * nq + i, h)),
            pl.BlockSpec((seq, w), lambda b, h, i: (b, h)),
            pl.BlockSpec((seq // KV_CHUNK, w, KV_CHUNK), lambda b, h, i: (b, h, 0)),
            pl.BlockSpec((None, nq, w), lambda b, h, i: (b, 0, h)),
        ],
        out_specs=pl.BlockSpec((tq, w), lambda b, h, i: (b * nq + i, h)),
        scratch_shapes=[pltpu.VMEM((2, n_groups, tq, 2 * tq), F32),
                        pltpu.VMEM((n_groups, 2, HEAD_DIM + SUM_ROWS, tq), F32),
                        pltpu.VMEM((n_groups, nq, 2 * tq), F32)],
        compiler_params=_params("parallel", "parallel", "arbitrary"),
        name="moba_attn",
    )(zq, zkv, vt, kmean)


def _attn_out_kernel(x_ref, os_ref, qm_ref, mk_ref, mvt_ref, w_ref, o_ref):
    mem_len = mk_ref.shape[0]
    mk = mk_ref[...]
    lane = lax.broadcasted_iota(jnp.int32, mk.shape, 1)
    zero = jnp.zeros_like(mk)
    mk_heads = jnp.concatenate(
        [jnp.where((lane >= h * HEAD_DIM) & (lane < (h + 1) * HEAD_DIM), mk, zero)
         for h in range(N_MEM_HEADS)], axis=0)
    s = lax.dot_general(mk_heads, qm_ref[...], _NT, preferred_element_type=F32)
    ones_rows = jnp.ones((SUM_ROWS, mem_len), BF16)
    heads = []
    for h in range(N_MEM_HEADS):
        sh = s[h * mem_len:(h + 1) * mem_len]
        p = jnp.exp2((sh - jnp.max(sh, axis=0, keepdims=True)).astype(BF16))
        vt_ones = jnp.concatenate([mvt_ref[h * HEAD_DIM:(h + 1) * HEAD_DIM, :], ones_rows], axis=0)
        oh = jnp.dot(vt_ones, p, preferred_element_type=F32)
        heads.append(oh[:HEAD_DIM] * (1.0 / oh[HEAD_DIM:HEAD_DIM + 1]))
    o_mem = jnp.concatenate(heads, axis=0).T.astype(BF16)
    y = jnp.dot(os_ref[...], w_ref[:SELF_WIDTH, :], preferred_element_type=F32)
    y = y + jnp.dot(o_mem, w_ref[SELF_WIDTH:, :], preferred_element_type=F32)
    o_ref[...] = x_ref[...] + y


def _attn_out(x2d, o_self, zq, qm_block, memkv, memvt, w_out, *, seq, mem_len, tm, name):
    t, d = x2d.shape
    per_batch = seq // tm
    return pl.pallas_call(
        _attn_out_kernel,
        out_shape=jax.ShapeDtypeStruct((t, d), F32),
        grid=(t // tm,),
        in_specs=[
            pl.BlockSpec((tm, d), lambda i: (i, 0)),
            pl.BlockSpec((tm, SELF_WIDTH), lambda i: (i, 0)),
            pl.BlockSpec((tm, MEM_WIDTH), lambda i: (i, qm_block)),
            pl.BlockSpec((mem_len, MEM_WIDTH), lambda i: (i // per_batch, 0)),
            pl.BlockSpec((None, MEM_WIDTH, mem_len), lambda i: (i // per_batch, 0, 0)),
            pl.BlockSpec((d, d), lambda i: (0, 0)),
        ],
        out_specs=pl.BlockSpec((tm, d), lambda i: (i, 0)),
        compiler_params=_params("parallel"),
        name=name,
    )(x2d, o_self, zq, memkv, memvt, w_out)


def _mlp_kernel(x_ref, g_ref, wu_ref, wd_ref, gf_ref, o_ref, u_scr, *, final_norm, tf):
    x = x_ref[...]
    ms = jnp.mean(x * x, axis=-1, keepdims=True)
    h = (x * lax.rsqrt(ms + EPS) * g_ref[...]).astype(BF16)
    for c0 in range(0, wu_ref.shape[1], tf):
        u = jnp.maximum(jnp.dot(h, wu_ref[:, c0:c0 + tf], preferred_element_type=F32), 0.0)
        u_scr[:, c0:c0 + tf] = (u * u).astype(BF16)
    y = x + jnp.dot(u_scr[...], wd_ref[...], preferred_element_type=F32)
    if final_norm:
        ms = jnp.mean(y * y, axis=-1, keepdims=True)
        y = y * lax.rsqrt(ms + EPS) * gf_ref[...]
    o_ref[...] = y


def _mlp(x2d, g, w_up, w_down, g_final, *, final_norm, tm, tf, name):
    t, d = x2d.shape
    dff = w_up.shape[1]
    kern = functools.partial(_mlp_kernel, final_norm=final_norm, tf=tf)
    resident = dict(pipeline_mode=pl.Buffered(1))
    return pl.pallas_call(
        kern,
        out_shape=jax.ShapeDtypeStruct((t, d), F32),
        grid=(t // tm,),
        in_specs=[
            pl.BlockSpec((tm, d), lambda i: (i, 0)),
            pl.BlockSpec((1, d), lambda i: (0, 0)),
            pl.BlockSpec((d, dff), lambda i: (0, 0), **resident),
            pl.BlockSpec((dff, d), lambda i: (0, 0), **resident),
            pl.BlockSpec((1, d), lambda i: (0, 0)),
        ],
        out_specs=pl.BlockSpec((tm, d), lambda i: (i, 0)),
        scratch_shapes=[pltpu.VMEM((tm, dff), BF16)],
        compiler_params=_params("parallel"),
        name=name,
    )(x2d, g.reshape(1, d), w_up, w_down, g_final.reshape(1, d))


def _rope_tables(seq):
    half = HEAD_DIM // 2
    inv = 1.0 / (ROPE_THETA ** (jnp.arange(half, dtype=F32) / half))
    ang = jnp.arange(seq, dtype=F32)[:, None] * inv[None, :]
    cos, sin = jnp.cos(ang), jnp.sin(ang)
    reps = LANES // half
    cos_t = jnp.tile(cos, (1, reps))
    sign = jnp.tile(jnp.concatenate([-jnp.ones((half,), F32), jnp.ones((half,), F32)]), LANES // HEAD_DIM)
    sin_t = jnp.tile(sin, (1, reps)) * sign[None, :]
    return cos_t, sin_t


def kernel(x, mem, a_norm_attn, a_w_in, a_lambda, a_subln, a_mem_norm, a_w_mem_kv, a_w_out, a_norm_mlp, a_w_up, a_w_down, kv_norm, w_kv, b_norm_attn, b_w_in, b_mem_norm, b_w_mem_kv, b_w_out, b_norm_mlp, b_w_up, b_w_down, final_norm):
    batch, seq, d = x.shape
    mem_len = mem.shape[1]
    t = batch * seq
    x2d = x.reshape(t, d)
    mem2d = mem.reshape(batch * mem_len, d)
    cos_t, sin_t = _rope_tables(seq)
    bf = lambda w: w.astype(BF16)
    tm = 512
    ng = N_SELF_GROUPS
    q_scale = QK_SCALE * LOG2E
    mem_groups = MEM_WIDTH // LANES

    def memkv(g, w, name):
        return _proj(mem2d, g, bf(w), cos_t, sin_t, n_rope=0, scales={}, with_kmean=False,
                     vt_start=mem_groups, vt_width=MEM_WIDTH, seq=mem_len, tm=mem_len, name=name)

    lam_init = 0.8 - 0.6 * math.exp(-0.3 * 0)
    za, vt_a = _proj(x2d, a_norm_attn[0], bf(a_w_in[0]), cos_t, sin_t, n_rope=2 * ng,
                     scales={j: q_scale for j in (*range(ng), 3 * ng, 3 * ng + 1)},
                     with_kmean=False, vt_start=2 * ng, seq=seq, tm=tm, name="a_proj")
    mkv_a, mvt_a = memkv(a_mem_norm[0], a_w_mem_kv[0], "a_memkv")
    o_self = _diff_attention(za, vt_a, a_lambda[0], a_subln[0], batch=batch, seq=seq,
                             n_groups=2, lam_init=lam_init)
    x2d = _attn_out(x2d, o_self, za, 3 * SELF_WIDTH // MEM_WIDTH, mkv_a, mvt_a, bf(a_w_out[0]),
                    seq=seq, mem_len=mem_len, tm=tm, name="a_attn_out")
    x2d = _mlp(x2d, a_norm_mlp[0], bf(a_w_up[0]), bf(a_w_down[0]), final_norm,
               final_norm=False, tm=512, tf=512, name="a_mlp")

    zkv, kmean, vt_b = _proj(x2d, kv_norm, bf(w_kv), cos_t, sin_t, n_rope=ng, scales={},
                             with_kmean=True, vt_start=ng, seq=seq, tm=tm, name="b_kvproj")
    kmean = kmean.reshape(batch, seq // MOBA_BLOCK, SELF_WIDTH)
    zb, = _proj(x2d, b_norm_attn[0], bf(b_w_in[0]), cos_t, sin_t, n_rope=ng,
                scales={j: q_scale for j in range(ng + mem_groups)}, with_kmean=False, vt_start=None,
                seq=seq, tm=tm, name="b_qproj")
    mkv_b, mvt_b = memkv(b_mem_norm[0], b_w_mem_kv[0], "b_memkv")
    o_self = _moba_attention(zb, zkv, vt_b, kmean, batch=batch, seq=seq, n_groups=2)
    x2d = _attn_out(x2d, o_self, zb, SELF_WIDTH // MEM_WIDTH, mkv_b, mvt_b, bf(b_w_out[0]),
                    seq=seq, mem_len=mem_len, tm=tm, name="b_attn_out")
    x2d = _mlp(x2d, b_norm_mlp[0], bf(b_w_up[0]), bf(b_w_down[0]), final_norm,
               final_norm=True, tm=512, tf=512, name="b_mlp")
    return x2d.reshape(batch, seq, d)
```

```python
import functools
import math

import jax
import jax.numpy as jnp
from jax import lax
from jax.experimental import pallas as pl
from jax.experimental.pallas import tpu as pltpu

D_MODEL = 1024
HEAD_DIM = 64
SELF_WIDTH = 768
MEM_WIDTH = 256
N_MEM_HEADS = 4
D_FF = 4096
MOBA_BLOCK = 256
MOBA_TOPK = 3
ROPE_THETA = 10000.0
EPS = 1e-6
NEG = -1e30

LANES = 128
N_SELF_GROUPS = SELF_WIDTH // LANES
QK_SCALE = HEAD_DIM ** -0.5
LOG2E = math.log2(math.e)
KV_CHUNK = 256
SUM_ROWS = 16

F32 = jnp.float32
BF16 = jnp.bfloat16

_VMEM_LIMIT = 56 * 1024 * 1024


def _params(*sem):
    return pltpu.CompilerParams(dimension_semantics=sem, vmem_limit_bytes=_VMEM_LIMIT)


def _proj_kernel(x_ref, g_ref, w_ref, cos_ref, sin_ref, o_ref, *extra_refs,
                 n_rope, scales, col_chunk, with_kmean, vt_start):
    tm = x_ref.shape[0]
    n_out = w_ref.shape[1]
    extra = list(extra_refs)
    km_ref = extra.pop(0) if with_kmean else None
    vt_ref = extra.pop(0) if vt_start is not None else None
    x = x_ref[...]
    ms = jnp.mean(x * x, axis=-1, keepdims=True)
    h = (x * lax.rsqrt(ms + EPS) * g_ref[...]).astype(BF16)
    lane = lax.broadcasted_iota(jnp.int32, (tm, LANES), 1)
    first_half = (lane & (HEAD_DIM - 1)) < HEAD_DIM // 2
    if n_rope:
        cos = cos_ref[...]
        sin = sin_ref[...]
    for c0 in range(0, n_out, col_chunk):
        z = jnp.dot(h, w_ref[:, c0:c0 + col_chunk], preferred_element_type=F32)
        for jj in range(col_chunk // LANES):
            j = c0 // LANES + jj
            blk = z[:, jj * LANES:(jj + 1) * LANES]
            if j < n_rope:
                swap = jnp.where(first_half, pltpu.roll(blk, LANES - 32, 1), pltpu.roll(blk, 32, 1))
                blk = blk * cos + swap * sin
                if km_ref is not None:
                    km_ref[0, :, j * LANES:(j + 1) * LANES] = jnp.mean(
                        blk.reshape(tm // MOBA_BLOCK, MOBA_BLOCK, LANES), axis=1)
            if j in scales:
                blk = blk * scales[j]
            o_ref[:, j * LANES:(j + 1) * LANES] = blk.astype(o_ref.dtype)
            if vt_ref is not None and vt_start <= j < vt_start + vt_ref.shape[1] // LANES:
                g = j - vt_start
                for c in range(tm // KV_CHUNK):
                    vt_ref[c, g * LANES:(g + 1) * LANES, :] = (
                        blk[c * KV_CHUNK:(c + 1) * KV_CHUNK, :].T.astype(BF16))


def _proj(x2d, g, w, cos_t, sin_t, *, n_rope, scales, with_kmean, vt_start, seq, tm, name,
          vt_width=SELF_WIDTH):
    t, d = x2d.shape
    n_out = w.shape[1]
    n_pos_blocks = seq // tm
    col_chunk = 512 if n_out % 512 == 0 else 256
    out_shape = [jax.ShapeDtypeStruct((t, n_out), BF16)]
    out_specs = [pl.BlockSpec((tm, n_out), lambda i: (i, 0))]
    if with_kmean:
        out_shape.append(jax.ShapeDtypeStruct((t // tm, tm // MOBA_BLOCK, n_rope * LANES), F32))
        out_specs.append(pl.BlockSpec((1, tm // MOBA_BLOCK, n_rope * LANES), lambda i: (i, 0, 0)))
    if vt_start is not None:
        out_shape.append(jax.ShapeDtypeStruct((t // KV_CHUNK, vt_width, KV_CHUNK), BF16))
        out_specs.append(pl.BlockSpec((tm // KV_CHUNK, vt_width, KV_CHUNK), lambda i: (i, 0, 0)))
    kern = functools.partial(_proj_kernel, n_rope=n_rope, scales=dict(scales), col_chunk=col_chunk,
                             with_kmean=with_kmean, vt_start=vt_start)
    return pl.pallas_call(
        kern,
        out_shape=out_shape,
        grid=(t // tm,),
        in_specs=[
            pl.BlockSpec((tm, d), lambda i: (i, 0)),
            pl.BlockSpec((1, d), lambda i: (0, 0)),
            pl.BlockSpec((d, n_out), lambda i: (0, 0)),
            pl.BlockSpec((tm, LANES), lambda i: (i % n_pos_blocks, 0)),
            pl.BlockSpec((tm, LANES), lambda i: (i % n_pos_blocks, 0)),
        ],
        out_specs=out_specs,
        compiler_params=_params("parallel"),
        name=name,
    )(x2d, g.reshape(1, d), w, cos_t, sin_t)


_NT = (((1,), (1,)), ((), ()))


def _stack_maps(q):
    tq = q.shape[0]
    lane = lax.broadcasted_iota(jnp.int32, (tq, LANES), 1)
    zero = jnp.zeros_like(q)
    return jnp.concatenate(
        [jnp.where(lane < HEAD_DIM, q, zero), jnp.where(lane >= HEAD_DIM, q, zero)], axis=0)


def _causal_mask(st, tq):
    key = lax.broadcasted_iota(jnp.int32, st.shape, 0)
    qry = lax.broadcasted_iota(jnp.int32, st.shape, 1) & (tq - 1)
    return jnp.where(key <= qry, st, NEG)


def _flash_pipeline(n_groups, n_past, tq, scores_fn, vt_fn, bias_fn, finalize_fn, s_scr, acc_scr,
                    split_heads):
    n = 2 * tq

    def produce(j, buf):
        col_max = []
        for g in range(n_groups):
            st = scores_fn(g, j)
            s_scr[buf, g] = st
            col_max.append(jnp.max(st, axis=0, keepdims=True))
        return tuple(col_max)

    ones_rows = jnp.ones((SUM_ROWS, tq), BF16)

    def consume(j, buf, stats, col_max, diagonal):
        out = []
        for g in range(n_groups):
            st = s_scr[buf, g]
            if diagonal:
                st = _causal_mask(st, tq)
                m_cur = jnp.max(st, axis=0, keepdims=True)
            else:
                m_cur = col_max[g]
            if bias_fn is not None:
                bias = bias_fn(g, j)
                m_cur = m_cur + bias
            m_next = jnp.maximum(stats[g], m_cur)
            alpha = jnp.exp2(stats[g] - m_next)
            shift = m_next if bias_fn is None else m_next - bias
            p = jnp.exp2((st - shift).astype(BF16))
            vt = vt_fn(g, j)
            if split_heads:
                acc = []
                for h in range(2):
                    vt_h = jnp.concatenate([vt[h * HEAD_DIM:(h + 1) * HEAD_DIM], ones_rows], axis=0)
                    lanes = slice(h * tq, (h + 1) * tq)
                    acc.append(alpha[:, lanes] * acc_scr[g, h] + jnp.dot(
                        vt_h, p[:, lanes], preferred_element_type=F32))
            else:
                acc = alpha * acc_scr[g] + jnp.dot(
                    jnp.concatenate([vt, ones_rows], axis=0), p, preferred_element_type=F32)
            if diagonal:
                finalize_fn(g, acc)
            elif split_heads:
                acc_scr[g, 0], acc_scr[g, 1] = acc
            else:
                acc_scr[g] = acc
            out.append(m_next)
        return tuple(out)

    def step(j, buf, carry):
        nxt = produce(j + 1, 1 - buf)
        return consume(j, buf, carry[0], carry[1], False), nxt

    def pair(i, carry):
        return step(2 * i + 1, 1, step(2 * i, 0, carry))

    def quad(i, carry):
        return pair(2 * i + 1, pair(2 * i, carry))

    def octet(i, carry):
        return quad(2 * i + 1, quad(2 * i, carry))

    acc_scr[...] = jnp.zeros(acc_scr.shape, F32)
    init = tuple(jnp.full((1, n), NEG, F32) for _ in range(n_groups))
    n_octets = n_past // 8
    carry = lax.fori_loop(0, n_octets, octet, (init, produce(0, 0)))
    carry = lax.fori_loop(2 * n_octets, n_past // 4, quad, carry)
    carry = lax.fori_loop(2 * (n_past // 4), n_past // 2, pair, carry)

    def odd_tail(carry):
        stats, col_max = step(n_past - 1, 0, carry)
        return consume(n_past, 1, stats, col_max, True)

    def even_tail(carry):
        return consume(n_past, 0, carry[0], carry[1], True)

    return lax.cond(n_past % 2 == 1, odd_tail, even_tail, carry)


def _diff_attn_kernel(lam_ref, g_ref, q_ref, k_ref, vt_ref, o_ref, s_scr, acc_scr, *, lam_init):
    tq = KV_CHUNK
    n_groups = q_ref.shape[1] // LANES
    cols = [slice(g * LANES, (g + 1) * LANES) for g in range(n_groups)]
    lp = lam_ref[...]
    lam = (jnp.exp(jnp.sum(lp[0:1] * lp[1:2], axis=1, keepdims=True))
           - jnp.exp(jnp.sum(lp[2:3] * lp[3:4], axis=1, keepdims=True)) + lam_init)

    def query_tile(qi, carry):
        rows = pl.ds(pl.multiple_of(qi * tq, tq), tq)
        qs = [_stack_maps(q_ref[rows, c]) for c in cols]

        def scores(g, j):
            start = pl.multiple_of(j * tq, tq)
            return lax.dot_general(k_ref[pl.ds(start, tq), cols[g]], qs[g], _NT,
                                   preferred_element_type=F32)

        def finalize(g, acc):
            o_t = acc[:LANES] * (1.0 / acc[LANES:LANES + 1])
            od_t = o_t[:, :tq] - lam * o_t[:, tq:]
            ms = jnp.mean(od_t * od_t, axis=0, keepdims=True)
            y = (od_t * lax.rsqrt(ms + EPS)).T * g_ref[...]
            o_ref[rows, cols[g]] = (y * (1.0 - lam_init)).astype(o_ref.dtype)

        _flash_pipeline(n_groups, qi, tq, scores, lambda g, j: vt_ref[j, cols[g], :], None,
                        finalize, s_scr, acc_scr, split_heads=False)
        return carry

    lax.fori_loop(0, q_ref.shape[0] // tq, query_tile, 0)


def _diff_attention(z, vt, lam_p, subln, *, batch, seq, n_groups, lam_init):
    t = z.shape[0]
    tq = KV_CHUNK
    nq = seq // tq
    w = n_groups * LANES
    n_steps = N_SELF_GROUPS // n_groups
    kern = functools.partial(_diff_attn_kernel, lam_init=lam_init)
    return pl.pallas_call(
        kern,
        out_shape=jax.ShapeDtypeStruct((t, SELF_WIDTH), BF16),
        grid=(batch, n_steps),
        in_specs=[
            pl.BlockSpec((4, HEAD_DIM), lambda b, h: (0, 0)),
            pl.BlockSpec((1, LANES), lambda b, h: (0, 0)),
            pl.BlockSpec((seq, w), lambda b, h: (b, h)),
            pl.BlockSpec((seq, w), lambda b, h: (b, n_steps + h)),
            pl.BlockSpec((seq // KV_CHUNK, w, KV_CHUNK), lambda b, h: (b, h, 0)),
        ],
        out_specs=pl.BlockSpec((seq, w), lambda b, h: (b, h)),
        scratch_shapes=[pltpu.VMEM((2, n_groups, tq, 2 * tq), F32),
                        pltpu.VMEM((n_groups, LANES + SUM_ROWS, 2 * tq), F32)],
        compiler_params=_params("parallel", "parallel"),
        name="diff_attn",
    )(lam_p, subln.reshape(1, LANES), z, z, vt)


def _moba_select_bias(qs, km, qi):
    n_blk = km.shape[0]
    parts, rest = [], km
    for _ in range(3):
        part = rest.astype(BF16)
        parts.append(part)
        rest = rest - part.astype(F32)
    terms = lax.dot_general(jnp.concatenate(parts, axis=0), qs, _NT, preferred_element_type=F32)
    gate = (terms[2 * n_blk:] + terms[n_blk:2 * n_blk]) + terms[:n_blk]
    blk = lax.broadcasted_iota(jnp.int32, gate.shape, 0)
    past = blk < qi
    gm = jnp.where(past, gate, NEG)
    sel = jnp.zeros(gate.shape, F32)
    for _ in range(MOBA_TOPK):
        mx = jnp.max(gm, axis=0, keepdims=True)
        first = jnp.min(jnp.where(gm == mx, blk, n_blk), axis=0, keepdims=True)
        pick = blk == first
        sel = jnp.where(pick, 1.0, sel)
        gm = jnp.where(pick, -jnp.inf, gm)
    return jnp.where((past & (sel > 0.5)) | (blk == qi), 0.0, 2.0 * NEG)


def _moba_attn_kernel(q_ref, k_ref, vt_ref, km_ref, o_ref, s_scr, acc_scr, bias_scr):
    tq = MOBA_BLOCK
    n_groups = q_ref.shape[1] // LANES
    cols = [slice(g * LANES, (g + 1) * LANES) for g in range(n_groups)]

    def query_tile(qi, carry):
        rows = pl.ds(pl.multiple_of(qi * tq, tq), tq)
        qs = [_stack_maps(q_ref[rows, c]) for c in cols]
        for g, c in enumerate(cols):
            bias_scr[g] = _moba_select_bias(qs[g], km_ref[:, c], qi)

        def scores(g, j):
            start = pl.multiple_of(j * tq, tq)
            return lax.dot_general(k_ref[pl.ds(start, tq), cols[g]], qs[g], _NT,
                                   preferred_element_type=F32)

        def finalize(g, acc):
            heads = [a[:HEAD_DIM] * (1.0 / a[HEAD_DIM:HEAD_DIM + 1]) for a in acc]
            o_ref[rows, cols[g]] = jnp.concatenate(heads, axis=0).T.astype(o_ref.dtype)

        _flash_pipeline(n_groups, qi, tq, scores, lambda g, j: vt_ref[j, cols[g], :],
                        lambda g, j: bias_scr[g, pl.ds(j, 1), :], finalize, s_scr, acc_scr,
                        split_heads=True)
        return carry

    lax.fori_loop(0, q_ref.shape[0] // tq, query_tile, 0)


def _moba_attention(zq, zkv, vt, kmean, *, batch, seq, n_groups):
    t = zq.shape[0]
    tq = MOBA_BLOCK
    nq = seq // tq
    w = n_groups * LANES
    n_steps = N_SELF_GROUPS // n_groups
    return pl.pallas_call(
        _moba_attn_kernel,
        out_shape=jax.ShapeDtypeStruct((t, SELF_WIDTH), BF16),
        grid=(batch, n_steps),
        in_specs=[
            pl.BlockSpec((seq, w), lambda b, h: (b, h)),
            pl.BlockSpec((seq, w), lambda b, h: (b, h)),
            pl.BlockSpec((seq // KV_CHUNK, w, KV_CHUNK), lambda b, h: (b, h, 0)),
            pl.BlockSpec((None, nq, w), lambda b, h: (b, 0, h)),
        ],
        out_specs=pl.BlockSpec((seq, w), lambda b, h: (b, h)),
        scratch_shapes=[pltpu.VMEM((2, n_groups, tq, 2 * tq), F32),
                        pltpu.VMEM((n_groups, 2, HEAD_DIM + SUM_ROWS, tq), F32),
                        pltpu.VMEM((n_groups, nq, 2 * tq), F32)],
        compiler_params=_params("parallel", "parallel"),
        name="moba_attn",
    )(zq, zkv, vt, kmean)


def _attn_out_kernel(x_ref, os_ref, qm_ref, mk_ref, mvt_ref, w_ref, o_ref):
    mem_len = mk_ref.shape[0]
    mk = mk_ref[...]
    lane = lax.broadcasted_iota(jnp.int32, mk.shape, 1)
    zero = jnp.zeros_like(mk)
    mk_heads = jnp.concatenate(
        [jnp.where((lane >= h * HEAD_DIM) & (lane < (h + 1) * HEAD_DIM), mk, zero)
         for h in range(N_MEM_HEADS)], axis=0)
    s = lax.dot_general(mk_heads, qm_ref[...], _NT, preferred_element_type=F32)
    ones_rows = jnp.ones((SUM_ROWS, mem_len), BF16)
    heads = []
    for h in range(N_MEM_HEADS):
        sh = s[h * mem_len:(h + 1) * mem_len]
        p = jnp.exp2((sh - jnp.max(sh, axis=0, keepdims=True)).astype(BF16))
        vt_ones = jnp.concatenate([mvt_ref[h * HEAD_DIM:(h + 1) * HEAD_DIM, :], ones_rows], axis=0)
        oh = jnp.dot(vt_ones, p, preferred_element_type=F32)
        heads.append(oh[:HEAD_DIM] * (1.0 / oh[HEAD_DIM:HEAD_DIM + 1]))
    o_mem = jnp.concatenate(heads, axis=0).T.astype(BF16)
    y = jnp.dot(os_ref[...], w_ref[:SELF_WIDTH, :], preferred_element_type=F32)
    y = y + jnp.dot(o_mem, w_ref[SELF_WIDTH:, :], preferred_element_type=F32)
    o_ref[...] = x_ref[...] + y


def _attn_out(x2d, o_self, zq, qm_block, memkv, memvt, w_out, *, seq, mem_len, tm, name):
    t, d = x2d.shape
    per_batch = seq // tm
    return pl.pallas_call(
        _attn_out_kernel,
        out_shape=jax.ShapeDtypeStruct((t, d), F32),
        grid=(t // tm,),
        in_specs=[
            pl.BlockSpec((tm, d), lambda i: (i, 0)),
            pl.BlockSpec((tm, SELF_WIDTH), lambda i: (i, 0)),
            pl.BlockSpec((tm, MEM_WIDTH), lambda i: (i, qm_block)),
            pl.BlockSpec((mem_len, MEM_WIDTH), lambda i: (i // per_batch, 0)),
            pl.BlockSpec((None, MEM_WIDTH, mem_len), lambda i: (i // per_batch, 0, 0)),
            pl.BlockSpec((d, d), lambda i: (0, 0)),
        ],
        out_specs=pl.BlockSpec((tm, d), lambda i: (i, 0)),
        compiler_params=_params("parallel"),
        name=name,
    )(x2d, o_self, zq, memkv, memvt, w_out)


def _mlp_kernel(x_ref, g_ref, wu_ref, wd_ref, gf_ref, o_ref, u_scr, *, final_norm, tf):
    x = x_ref[...]
    ms = jnp.mean(x * x, axis=-1, keepdims=True)
    h = (x * lax.rsqrt(ms + EPS) * g_ref[...]).astype(BF16)
    for c0 in range(0, wu_ref.shape[1], tf):
        u = jnp.maximum(jnp.dot(h, wu_ref[:, c0:c0 + tf], preferred_element_type=F32), 0.0)
        u_scr[:, c0:c0 + tf] = (u * u).astype(BF16)
    y = x + jnp.dot(u_scr[...], wd_ref[...], preferred_element_type=F32)
    if final_norm:
        ms = jnp.mean(y * y, axis=-1, keepdims=True)
        y = y * lax.rsqrt(ms + EPS) * gf_ref[...]
    o_ref[...] = y


def _mlp(x2d, g, w_up, w_down, g_final, *, final_norm, tm, tf, name):
    t, d = x2d.shape
    dff = w_up.shape[1]
    kern = functools.partial(_mlp_kernel, final_norm=final_norm, tf=tf)
    resident = dict(pipeline_mode=pl.Buffered(1))
    return pl.pallas_call(
        kern,
        out_shape=jax.ShapeDtypeStruct((t, d), F32),
        grid=(t // tm,),
        in_specs=[
            pl.BlockSpec((tm, d), lambda i: (i, 0)),
            pl.BlockSpec((1, d), lambda i: (0, 0)),
            pl.BlockSpec((d, dff), lambda i: (0, 0), **resident),
            pl.BlockSpec((dff, d), lambda i: (0, 0), **resident),
            pl.BlockSpec((1, d), lambda i: (0, 0)),
        ],
        out_specs=pl.BlockSpec((tm, d), lambda i: (i, 0)),
        scratch_shapes=[pltpu.VMEM((tm, dff), BF16)],
        compiler_params=_params("parallel"),
        name=name,
    )(x2d, g.reshape(1, d), w_up, w_down, g_final.reshape(1, d))


def _rope_tables(seq):
    half = HEAD_DIM // 2
    inv = 1.0 / (ROPE_THETA ** (jnp.arange(half, dtype=F32) / half))
    ang = jnp.arange(seq, dtype=F32)[:, None] * inv[None, :]
    cos, sin = jnp.cos(ang), jnp.sin(ang)
    reps = LANES // half
    cos_t = jnp.tile(cos, (1, reps))
    sign = jnp.tile(jnp.concatenate([-jnp.ones((half,), F32), jnp.ones((half,), F32)]), LANES // HEAD_DIM)
    sin_t = jnp.tile(sin, (1, reps)) * sign[None, :]
    return cos_t, sin_t


def kernel(x, mem, a_norm_attn, a_w_in, a_lambda, a_subln, a_mem_norm, a_w_mem_kv, a_w_out, a_norm_mlp, a_w_up, a_w_down, kv_norm, w_kv, b_norm_attn, b_w_in, b_mem_norm, b_w_mem_kv, b_w_out, b_norm_mlp, b_w_up, b_w_down, final_norm):
    batch, seq, d = x.shape
    mem_len = mem.shape[1]
    t = batch * seq
    x2d = x.reshape(t, d)
    mem2d = mem.reshape(batch * mem_len, d)
    cos_t, sin_t = _rope_tables(seq)
    bf = lambda w: w.astype(BF16)
    tm = 512
    ng = N_SELF_GROUPS
    q_scale = QK_SCALE * LOG2E
    mem_groups = MEM_WIDTH // LANES

    def memkv(g, w, name):
        return _proj(mem2d, g, bf(w), cos_t, sin_t, n_rope=0, scales={}, with_kmean=False,
                     vt_start=mem_groups, vt_width=MEM_WIDTH, seq=mem_len, tm=mem_len, name=name)

    lam_init = 0.8 - 0.6 * math.exp(-0.3 * 0)
    za, vt_a = _proj(x2d, a_norm_attn[0], bf(a_w_in[0]), cos_t, sin_t, n_rope=2 * ng,
                     scales={j: q_scale for j in (*range(ng), 3 * ng, 3 * ng + 1)},
                     with_kmean=False, vt_start=2 * ng, seq=seq, tm=tm, name="a_proj")
    mkv_a, mvt_a = memkv(a_mem_norm[0], a_w_mem_kv[0], "a_memkv")
    o_self = _diff_attention(za, vt_a, a_lambda[0], a_subln[0], batch=batch, seq=seq,
                             n_groups=2, lam_init=lam_init)
    x2d = _attn_out(x2d, o_self, za, 3 * SELF_WIDTH // MEM_WIDTH, mkv_a, mvt_a, bf(a_w_out[0]),
                    seq=seq, mem_len=mem_len, tm=tm, name="a_attn_out")
    x2d = _mlp(x2d, a_norm_mlp[0], bf(a_w_up[0]), bf(a_w_down[0]), final_norm,
               final_norm=False, tm=512, tf=512, name="a_mlp")

    zkv, kmean, vt_b = _proj(x2d, kv_norm, bf(w_kv), cos_t, sin_t, n_rope=ng, scales={},
                             with_kmean=True, vt_start=ng, seq=seq, tm=tm, name="b_kvproj")
    kmean = kmean.reshape(batch, seq // MOBA_BLOCK, SELF_WIDTH)
    zb, = _proj(x2d, b_norm_attn[0], bf(b_w_in[0]), cos_t, sin_t, n_rope=ng,
                scales={j: q_scale for j in range(ng + mem_groups)}, with_kmean=False, vt_start=None,
                seq=seq, tm=tm, name="b_qproj")
    mkv_b, mvt_b = memkv(b_mem_norm[0], b_w_mem_kv[0], "b_memkv")
    o_self = _moba_attention(zb, zkv, vt_b, kmean, batch=batch, seq=seq, n_groups=2)
    x2d = _attn_out(x2d, o_self, zb, SELF_WIDTH // MEM_WIDTH, mkv_b, mvt_b, bf(b_w_out[0]),
                    seq=seq, mem_len=mem_len, tm=tm, name="b_attn_out")
    x2d = _mlp(x2d, b_norm_mlp[0], bf(b_w_up[0]), bf(b_w_down[0]), final_norm,
               final_norm=True, tm=512, tf=512, name="b_mlp")
    return x2d.reshape(batch, seq, d)
```

```python
import functools
import math

import jax
import jax.numpy as jnp
from jax import lax
from jax.experimental import pallas as pl
from jax.experimental.pallas import tpu as pltpu

D_MODEL = 1024
HEAD_DIM = 64
SELF_WIDTH = 768
MEM_WIDTH = 256
N_MEM_HEADS = 4
D_FF = 4096
MOBA_BLOCK = 256
MOBA_TOPK = 3
ROPE_THETA = 10000.0
EPS = 1e-6
NEG = -1e30

LANES = 128
N_SELF_GROUPS = SELF_WIDTH // LANES
QK_SCALE = HEAD_DIM ** -0.5
LOG2E = math.log2(math.e)
KV_CHUNK = 256
SUM_ROWS = 16

F32 = jnp.float32
BF16 = jnp.bfloat16

_VMEM_LIMIT = 56 * 1024 * 1024


def _params(*sem):
    return pltpu.CompilerParams(dimension_semantics=sem, vmem_limit_bytes=_VMEM_LIMIT)


def _proj_kernel(x_ref, g_ref, w_ref, cos_ref, sin_ref, o_ref, *extra_refs,
                 n_rope, scales, col_chunk, with_kmean, vt_start):
    tm = x_ref.shape[0]
    n_out = w_ref.shape[1]
    extra = list(extra_refs)
    km_ref = extra.pop(0) if with_kmean else None
    vt_ref = extra.pop(0) if vt_start is not None else None
    x = x_ref[...]
    ms = jnp.mean(x * x, axis=-1, keepdims=True)
    h = (x * lax.rsqrt(ms + EPS) * g_ref[...]).astype(BF16)
    lane = lax.broadcasted_iota(jnp.int32, (tm, LANES), 1)
    first_half = (lane & (HEAD_DIM - 1)) < HEAD_DIM // 2
    if n_rope:
        cos = cos_ref[...]
        sin = sin_ref[...]
    for c0 in range(0, n_out, col_chunk):
        z = jnp.dot(h, w_ref[:, c0:c0 + col_chunk], preferred_element_type=F32)
        for jj in range(col_chunk // LANES):
            j = c0 // LANES + jj
            blk = z[:, jj * LANES:(jj + 1) * LANES]
            if j < n_rope:
                swap = jnp.where(first_half, pltpu.roll(blk, LANES - 32, 1), pltpu.roll(blk, 32, 1))
                blk = blk * cos + swap * sin
                if km_ref is not None:
                    km_ref[0, :, j * LANES:(j + 1) * LANES] = jnp.mean(
                        blk.reshape(tm // MOBA_BLOCK, MOBA_BLOCK, LANES), axis=1)
            if j in scales:
                blk = blk * scales[j]
            o_ref[:, j * LANES:(j + 1) * LANES] = blk.astype(o_ref.dtype)
            if vt_ref is not None and vt_start <= j < vt_start + vt_ref.shape[1] // LANES:
                g = j - vt_start
                for c in range(tm // KV_CHUNK):
                    vt_ref[c, g * LANES:(g + 1) * LANES, :] = (
                        blk[c * KV_CHUNK:(c + 1) * KV_CHUNK, :].T.astype(BF16))


def _proj(x2d, g, w, cos_t, sin_t, *, n_rope, scales, with_kmean, vt_start, seq, tm, name,
          vt_width=SELF_WIDTH):
    t, d = x2d.shape
    n_out = w.shape[1]
    n_pos_blocks = seq // tm
    col_chunk = 512 if n_out % 512 == 0 else 256
    out_shape = [jax.ShapeDtypeStruct((t, n_out), BF16)]
    out_specs = [pl.BlockSpec((tm, n_out), lambda i: (i, 0))]
    if with_kmean:
        out_shape.append(jax.ShapeDtypeStruct((t // tm, tm // MOBA_BLOCK, n_rope * LANES), F32))
        out_specs.append(pl.BlockSpec((1, tm // MOBA_BLOCK, n_rope * LANES), lambda i: (i, 0, 0)))
    if vt_start is not None:
        out_shape.append(jax.ShapeDtypeStruct((t // KV_CHUNK, vt_width, KV_CHUNK), BF16))
        out_specs.append(pl.BlockSpec((tm // KV_CHUNK, vt_width, KV_CHUNK), lambda i: (i, 0, 0)))
    kern = functools.partial(_proj_kernel, n_rope=n_rope, scales=dict(scales), col_chunk=col_chunk,
                             with_kmean=with_kmean, vt_start=vt_start)
    return pl.pallas_call(
        kern,
        out_shape=out_shape,
        grid=(t // tm,),
        in_specs=[
            pl.BlockSpec((tm, d), lambda i: (i, 0)),
            pl.BlockSpec((1, d), lambda i: (0, 0)),
            pl.BlockSpec((d, n_out), lambda i: (0, 0)),
            pl.BlockSpec((tm, LANES), lambda i: (i % n_pos_blocks, 0)),
            pl.BlockSpec((tm, LANES), lambda i: (i % n_pos_blocks, 0)),
        ],
        out_specs=out_specs,
        compiler_params=_params("parallel"),
        name=name,
    )(x2d, g.reshape(1, d), w, cos_t, sin_t)


_NT = (((1,), (1,)), ((), ()))


def _stack_maps(q):
    tq = q.shape[0]
    lane = lax.broadcasted_iota(jnp.int32, (tq, LANES), 1)
    zero = jnp.zeros_like(q)
    return jnp.concatenate(
        [jnp.where(lane < HEAD_DIM, q, zero), jnp.where(lane >= HEAD_DIM, q, zero)], axis=0)


def _causal_mask(st, tq):
    key = lax.broadcasted_iota(jnp.int32, st.shape, 0)
    qry = lax.broadcasted_iota(jnp.int32, st.shape, 1) & (tq - 1)
    return jnp.where(key <= qry, st, NEG)


FIRST = 2


def _flash_pipeline(n_groups, n_past, tq, scores_fn, next_scores_fn, vt_fn, bias_fn, finalize_fn,
                    first_max, s_scr, acc_scr, split_heads):
    n = 2 * tq

    def produce(j, buf):
        col_max = []
        for g in range(n_groups):
            st = scores_fn(g, j)
            s_scr[buf, g] = st
            col_max.append(jnp.max(st, axis=0, keepdims=True))
        return tuple(col_max)

    ones_rows = jnp.ones((SUM_ROWS, tq), BF16)

    def consume(j, buf, stats, col_max, diagonal):
        out = []
        for g in range(n_groups):
            st = s_scr[buf, g]
            if diagonal:
                st = _causal_mask(st, tq)
                m_cur = jnp.max(st, axis=0, keepdims=True)
            else:
                m_cur = col_max[g]
            if bias_fn is not None:
                bias = bias_fn(g, j)
                m_cur = m_cur + bias
            m_next = jnp.maximum(stats[g], m_cur)
            alpha = jnp.exp2(stats[g] - m_next)
            shift = m_next if bias_fn is None else m_next - bias
            p = jnp.exp2((st - shift).astype(BF16))
            vt = vt_fn(g, j)
            if split_heads:
                acc = []
                for h in range(2):
                    vt_h = jnp.concatenate([vt[h * HEAD_DIM:(h + 1) * HEAD_DIM], ones_rows], axis=0)
                    lanes = slice(h * tq, (h + 1) * tq)
                    acc.append(alpha[:, lanes] * acc_scr[g, h] + jnp.dot(
                        vt_h, p[:, lanes], preferred_element_type=F32))
            else:
                acc = alpha * acc_scr[g] + jnp.dot(
                    jnp.concatenate([vt, ones_rows], axis=0), p, preferred_element_type=F32)
            if diagonal:
                finalize_fn(g, acc)
            elif split_heads:
                acc_scr[g, 0], acc_scr[g, 1] = acc
            else:
                acc_scr[g] = acc
            out.append(m_next)
        return tuple(out)

    def produce_next():
        col_max = []
        for g in range(n_groups):
            st = next_scores_fn(g)
            s_scr[FIRST, g] = st
            col_max.append(jnp.max(st, axis=0, keepdims=True))
        return tuple(col_max)

    def step(j, buf, carry):
        nxt = produce(j + 1, 1 - buf)
        return consume(j, buf, carry[0], carry[1], False), nxt

    def pair(i, carry):
        return step(2 * i + 2, 0, step(2 * i + 1, 1, carry))

    def quad(i, carry):
        return pair(2 * i + 1, pair(2 * i, carry))

    def octet(i, carry):
        return quad(2 * i + 1, quad(2 * i, carry))

    acc_scr[...] = jnp.zeros(acc_scr.shape, F32)
    init = tuple(jnp.full((1, n), NEG, F32) for _ in range(n_groups))

    def only_diagonal(_):
        consume(0, FIRST, init, first_max, True)
        return produce_next()

    def with_past(_):
        nxt = produce(1, 1)
        carry = (consume(0, FIRST, init, first_max, False), nxt)
        n_pairs = (n_past - 1) // 2
        carry = lax.fori_loop(0, n_pairs // 4, octet, carry)
        carry = lax.fori_loop(2 * (n_pairs // 4), n_pairs // 2, quad, carry)
        carry = lax.fori_loop(2 * (n_pairs // 2), n_pairs, pair, carry)

        def even_tail(carry):
            stats, col_max = step(n_past - 1, 1, carry)
            nxt = produce_next()
            consume(n_past, 0, stats, col_max, True)
            return nxt

        def odd_tail(carry):
            nxt = produce_next()
            consume(n_past, 1, carry[0], carry[1], True)
            return nxt

        return lax.cond(n_past % 2 == 0, even_tail, odd_tail, carry)

    return lax.cond(n_past == 0, only_diagonal, with_past, 0)


def _qk_scores(k_ref, cols, qs, tq):
    def scores(g, j):
        start = j * tq if isinstance(j, int) else pl.multiple_of(j * tq, tq)
        return lax.dot_general(k_ref[pl.ds(start, tq), cols[g]], qs[g], _NT,
                               preferred_element_type=F32)
    return scores


def _stacked_queries(q_ref, cols, tile, tq):
    start = tile * tq if isinstance(tile, int) else pl.multiple_of(tile * tq, tq)
    return [_stack_maps(q_ref[pl.ds(start, tq), c]) for c in cols]


def _tile_scores(q_ref, k_ref, cols, qi, n_tiles, tq):
    qs = _stacked_queries(q_ref, cols, qi, tq)
    nxt = _qk_scores(k_ref, cols, _stacked_queries(q_ref, cols, jnp.minimum(qi + 1, n_tiles - 1), tq), tq)
    return qs, _qk_scores(k_ref, cols, qs, tq), lambda g: nxt(g, 0)


def _first_chunk(q_ref, k_ref, cols, tq, s_scr):
    scores = _qk_scores(k_ref, cols, _stacked_queries(q_ref, cols, 0, tq), tq)
    col_max = []
    for g in range(len(cols)):
        st = scores(g, 0)
        s_scr[FIRST, g] = st
        col_max.append(jnp.max(st, axis=0, keepdims=True))
    return tuple(col_max)


def _diff_attn_kernel(lam_ref, g_ref, q_ref, k_ref, vt_ref, o_ref, s_scr, acc_scr, *, lam_init):
    tq = KV_CHUNK
    n_groups = q_ref.shape[1] // LANES
    cols = [slice(g * LANES, (g + 1) * LANES) for g in range(n_groups)]
    lp = lam_ref[...]
    lam = (jnp.exp(jnp.sum(lp[0:1] * lp[1:2], axis=1, keepdims=True))
           - jnp.exp(jnp.sum(lp[2:3] * lp[3:4], axis=1, keepdims=True)) + lam_init)

    n_tiles = q_ref.shape[0] // tq

    def query_tile(qi, first_max):
        rows = pl.ds(pl.multiple_of(qi * tq, tq), tq)
        _, scores, next_scores = _tile_scores(q_ref, k_ref, cols, qi, n_tiles, tq)

        def finalize(g, acc):
            o_t = acc[:LANES] * (1.0 / acc[LANES:LANES + 1])
            od_t = o_t[:, :tq] - lam * o_t[:, tq:]
            ms = jnp.mean(od_t * od_t, axis=0, keepdims=True)
            y = (od_t * lax.rsqrt(ms + EPS)).T * g_ref[...]
            o_ref[rows, cols[g]] = (y * (1.0 - lam_init)).astype(o_ref.dtype)

        return _flash_pipeline(n_groups, qi, tq, scores, next_scores,
                               lambda g, j: vt_ref[j, cols[g], :], None, finalize, first_max,
                               s_scr, acc_scr, split_heads=False)

    lax.fori_loop(0, n_tiles, query_tile, _first_chunk(q_ref, k_ref, cols, tq, s_scr))


def _diff_attention(z, vt, lam_p, subln, *, batch, seq, n_groups, lam_init):
    t = z.shape[0]
    tq = KV_CHUNK
    nq = seq // tq
    w = n_groups * LANES
    n_steps = N_SELF_GROUPS // n_groups
    kern = functools.partial(_diff_attn_kernel, lam_init=lam_init)
    return pl.pallas_call(
        kern,
        out_shape=jax.ShapeDtypeStruct((t, SELF_WIDTH), BF16),
        grid=(batch, n_steps),
        in_specs=[
            pl.BlockSpec((4, HEAD_DIM), lambda b, h: (0, 0)),
            pl.BlockSpec((1, LANES), lambda b, h: (0, 0)),
            pl.BlockSpec((seq, w), lambda b, h: (b, h)),
            pl.BlockSpec((seq, w), lambda b, h: (b, n_steps + h)),
            pl.BlockSpec((seq // KV_CHUNK, w, KV_CHUNK), lambda b, h: (b, h, 0)),
        ],
        out_specs=pl.BlockSpec((seq, w), lambda b, h: (b, h)),
        scratch_shapes=[pltpu.VMEM((3, n_groups, tq, 2 * tq), F32),
                        pltpu.VMEM((n_groups, LANES + SUM_ROWS, 2 * tq), F32)],
        compiler_params=_params("parallel", "parallel"),
        name="diff_attn",
    )(lam_p, subln.reshape(1, LANES), z, z, vt)


def _moba_select_bias(qs, km, qi):
    n_blk = km.shape[0]
    parts, rest = [], km
    for _ in range(3):
        part = rest.astype(BF16)
        parts.append(part)
        rest = rest - part.astype(F32)
    terms = lax.dot_general(jnp.concatenate(parts, axis=0), qs, _NT, preferred_element_type=F32)
    gate = (terms[2 * n_blk:] + terms[n_blk:2 * n_blk]) + terms[:n_blk]
    blk = lax.broadcasted_iota(jnp.int32, gate.shape, 0)
    past = blk < qi
    gm = jnp.where(past, gate, NEG)
    sel = jnp.zeros(gate.shape, F32)
    for _ in range(MOBA_TOPK):
        mx = jnp.max(gm, axis=0, keepdims=True)
        first = jnp.min(jnp.where(gm == mx, blk, n_blk), axis=0, keepdims=True)
        pick = blk == first
        sel = jnp.where(pick, 1.0, sel)
        gm = jnp.where(pick, -jnp.inf, gm)
    return jnp.where((past & (sel > 0.5)) | (blk == qi), 0.0, 2.0 * NEG)


def _moba_attn_kernel(q_ref, k_ref, vt_ref, km_ref, o_ref, s_scr, acc_scr, bias_scr):
    tq = MOBA_BLOCK
    n_groups = q_ref.shape[1] // LANES
    cols = [slice(g * LANES, (g + 1) * LANES) for g in range(n_groups)]

    n_tiles = q_ref.shape[0] // tq

    def query_tile(qi, first_max):
        rows = pl.ds(pl.multiple_of(qi * tq, tq), tq)
        qs, scores, next_scores = _tile_scores(q_ref, k_ref, cols, qi, n_tiles, tq)
        for g, c in enumerate(cols):
            bias_scr[g] = _moba_select_bias(qs[g], km_ref[:, c], qi)

        def finalize(g, acc):
            heads = [a[:HEAD_DIM] * (1.0 / a[HEAD_DIM:HEAD_DIM + 1]) for a in acc]
            o_ref[rows, cols[g]] = jnp.concatenate(heads, axis=0).T.astype(o_ref.dtype)

        return _flash_pipeline(n_groups, qi, tq, scores, next_scores,
                               lambda g, j: vt_ref[j, cols[g], :],
                               lambda g, j: bias_scr[g, pl.ds(j, 1), :], finalize, first_max,
                               s_scr, acc_scr, split_heads=True)

    lax.fori_loop(0, n_tiles, query_tile, _first_chunk(q_ref, k_ref, cols, tq, s_scr))


def _moba_attention(zq, zkv, vt, kmean, *, batch, seq, n_groups):
    t = zq.shape[0]
    tq = MOBA_BLOCK
    nq = seq // tq
    w = n_groups * LANES
    n_steps = N_SELF_GROUPS // n_groups
    return pl.pallas_call(
        _moba_attn_kernel,
        out_shape=jax.ShapeDtypeStruct((t, SELF_WIDTH), BF16),
        grid=(batch, n_steps),
        in_specs=[
            pl.BlockSpec((seq, w), lambda b, h: (b, h)),
            pl.BlockSpec((seq, w), lambda b, h: (b, h)),
            pl.BlockSpec((seq // KV_CHUNK, w, KV_CHUNK), lambda b, h: (b, h, 0)),
            pl.BlockSpec((None, nq, w), lambda b, h: (b, 0, h)),
        ],
        out_specs=pl.BlockSpec((seq, w), lambda b, h: (b, h)),
        scratch_shapes=[pltpu.VMEM((3, n_groups, tq, 2 * tq), F32),
                        pltpu.VMEM((n_groups, 2, HEAD_DIM + SUM_ROWS, tq), F32),
                        pltpu.VMEM((n_groups, nq, 2 * tq), F32)],
        compiler_params=_params("parallel", "parallel"),
        name="moba_attn",
    )(zq, zkv, vt, kmean)


def _attn_out_kernel(x_ref, os_ref, qm_ref, mk_ref, mvt_ref, w_ref, o_ref):
    mem_len = mk_ref.shape[0]
    mk = mk_ref[...]
    lane = lax.broadcasted_iota(jnp.int32, mk.shape, 1)
    zero = jnp.zeros_like(mk)
    mk_heads = jnp.concatenate(
        [jnp.where((lane >= h * HEAD_DIM) & (lane < (h + 1) * HEAD_DIM), mk, zero)
         for h in range(N_MEM_HEADS)], axis=0)
    s = lax.dot_general(mk_heads, qm_ref[...], _NT, preferred_element_type=F32)
    ones_rows = jnp.ones((SUM_ROWS, mem_len), BF16)
    heads = []
    for h in range(N_MEM_HEADS):
        sh = s[h * mem_len:(h + 1) * mem_len]
        p = jnp.exp2((sh - jnp.max(sh, axis=0, keepdims=True)).astype(BF16))
        vt_ones = jnp.concatenate([mvt_ref[h * HEAD_DIM:(h + 1) * HEAD_DIM, :], ones_rows], axis=0)
        oh = jnp.dot(vt_ones, p, preferred_element_type=F32)
        heads.append(oh[:HEAD_DIM] * (1.0 / oh[HEAD_DIM:HEAD_DIM + 1]))
    o_mem = jnp.concatenate(heads, axis=0).T.astype(BF16)
    y = jnp.dot(os_ref[...], w_ref[:SELF_WIDTH, :], preferred_element_type=F32)
    y = y + jnp.dot(o_mem, w_ref[SELF_WIDTH:, :], preferred_element_type=F32)
    o_ref[...] = x_ref[...] + y


def _attn_out(x2d, o_self, zq, qm_block, memkv, memvt, w_out, *, seq, mem_len, tm, name):
    t, d = x2d.shape
    per_batch = seq // tm
    return pl.pallas_call(
        _attn_out_kernel,
        out_shape=jax.ShapeDtypeStruct((t, d), F32),
        grid=(t // tm,),
        in_specs=[
            pl.BlockSpec((tm, d), lambda i: (i, 0)),
            pl.BlockSpec((tm, SELF_WIDTH), lambda i: (i, 0)),
            pl.BlockSpec((tm, MEM_WIDTH), lambda i: (i, qm_block)),
            pl.BlockSpec((mem_len, MEM_WIDTH), lambda i: (i // per_batch, 0)),
            pl.BlockSpec((None, MEM_WIDTH, mem_len), lambda i: (i // per_batch, 0, 0)),
            pl.BlockSpec((d, d), lambda i: (0, 0)),
        ],
        out_specs=pl.BlockSpec((tm, d), lambda i: (i, 0)),
        compiler_params=_params("parallel"),
        name=name,
    )(x2d, o_self, zq, memkv, memvt, w_out)


def _mlp_kernel(x_ref, g_ref, wu_ref, wd_ref, gf_ref, o_ref, u_scr, *, final_norm, tf):
    x = x_ref[...]
    ms = jnp.mean(x * x, axis=-1, keepdims=True)
    h = (x * lax.rsqrt(ms + EPS) * g_ref[...]).astype(BF16)
    for c0 in range(0, wu_ref.shape[1], tf):
        u = jnp.maximum(jnp.dot(h, wu_ref[:, c0:c0 + tf], preferred_element_type=F32), 0.0)
        u_scr[:, c0:c0 + tf] = (u * u).astype(BF16)
    y = x + jnp.dot(u_scr[...], wd_ref[...], preferred_element_type=F32)
    if final_norm:
        ms = jnp.mean(y * y, axis=-1, keepdims=True)
        y = y * lax.rsqrt(ms + EPS) * gf_ref[...]
    o_ref[...] = y


def _mlp(x2d, g, w_up, w_down, g_final, *, final_norm, tm, tf, name):
    t, d = x2d.shape
    dff = w_up.shape[1]
    kern = functools.partial(_mlp_kernel, final_norm=final_norm, tf=tf)
    resident = dict(pipeline_mode=pl.Buffered(1))
    return pl.pallas_call(
        kern,
        out_shape=jax.ShapeDtypeStruct((t, d), F32),
        grid=(t // tm,),
        in_specs=[
            pl.BlockSpec((tm, d), lambda i: (i, 0)),
            pl.BlockSpec((1, d), lambda i: (0, 0)),
            pl.BlockSpec((d, dff), lambda i: (0, 0), **resident),
            pl.BlockSpec((dff, d), lambda i: (0, 0), **resident),
            pl.BlockSpec((1, d), lambda i: (0, 0)),
        ],
        out_specs=pl.BlockSpec((tm, d), lambda i: (i, 0)),
        scratch_shapes=[pltpu.VMEM((tm, dff), BF16)],
        compiler_params=_params("parallel"),
        name=name,
    )(x2d, g.reshape(1, d), w_up, w_down, g_final.reshape(1, d))


def _rope_tables(seq):
    half = HEAD_DIM // 2
    inv = 1.0 / (ROPE_THETA ** (jnp.arange(half, dtype=F32) / half))
    ang = jnp.arange(seq, dtype=F32)[:, None] * inv[None, :]
    cos, sin = jnp.cos(ang), jnp.sin(ang)
    reps = LANES // half
    cos_t = jnp.tile(cos, (1, reps))
    sign = jnp.tile(jnp.concatenate([-jnp.ones((half,), F32), jnp.ones((half,), F32)]), LANES // HEAD_DIM)
    sin_t = jnp.tile(sin, (1, reps)) * sign[None, :]
    return cos_t, sin_t


def kernel(x, mem, a_norm_attn, a_w_in, a_lambda, a_subln, a_mem_norm, a_w_mem_kv, a_w_out, a_norm_mlp, a_w_up, a_w_down, kv_norm, w_kv, b_norm_attn, b_w_in, b_mem_norm, b_w_mem_kv, b_w_out, b_norm_mlp, b_w_up, b_w_down, final_norm):
    batch, seq, d = x.shape
    mem_len = mem.shape[1]
    t = batch * seq
    x2d = x.reshape(t, d)
    mem2d = mem.reshape(batch * mem_len, d)
    cos_t, sin_t = _rope_tables(seq)
    bf = lambda w: w.astype(BF16)
    tm = 512
    ng = N_SELF_GROUPS
    q_scale = QK_SCALE * LOG2E
    mem_groups = MEM_WIDTH // LANES

    def memkv(g, w, name):
        return _proj(mem2d, g, bf(w), cos_t, sin_t, n_rope=0, scales={}, with_kmean=False,
                     vt_start=mem_groups, vt_width=MEM_WIDTH, seq=mem_len, tm=mem_len, name=name)

    lam_init = 0.8 - 0.6 * math.exp(-0.3 * 0)
    za, vt_a = _proj(x2d, a_norm_attn[0], bf(a_w_in[0]), cos_t, sin_t, n_rope=2 * ng,
                     scales={j: q_scale for j in (*range(ng), 3 * ng, 3 * ng + 1)},
                     with_kmean=False, vt_start=2 * ng, seq=seq, tm=tm, name="a_proj")
    mkv_a, mvt_a = memkv(a_mem_norm[0], a_w_mem_kv[0], "a_memkv")
    o_self = _diff_attention(za, vt_a, a_lambda[0], a_subln[0], batch=batch, seq=seq,
                             n_groups=2, lam_init=lam_init)
    x2d = _attn_out(x2d, o_self, za, 3 * SELF_WIDTH // MEM_WIDTH, mkv_a, mvt_a, bf(a_w_out[0]),
                    seq=seq, mem_len=mem_len, tm=tm, name="a_attn_out")
    x2d = _mlp(x2d, a_norm_mlp[0], bf(a_w_up[0]), bf(a_w_down[0]), final_norm,
               final_norm=False, tm=512, tf=512, name="a_mlp")

    zkv, kmean, vt_b = _proj(x2d, kv_norm, bf(w_kv), cos_t, sin_t, n_rope=ng, scales={},
                             with_kmean=True, vt_start=ng, seq=seq, tm=tm, name="b_kvproj")
    kmean = kmean.reshape(batch, seq // MOBA_BLOCK, SELF_WIDTH)
    zb, = _proj(x2d, b_norm_attn[0], bf(b_w_in[0]), cos_t, sin_t, n_rope=ng,
                scales={j: q_scale for j in range(ng + mem_groups)}, with_kmean=False, vt_start=None,
                seq=seq, tm=tm, name="b_qproj")
    mkv_b, mvt_b = memkv(b_mem_norm[0], b_w_mem_kv[0], "b_memkv")
    o_self = _moba_attention(zb, zkv, vt_b, kmean, batch=batch, seq=seq, n_groups=2)
    x2d = _attn_out(x2d, o_self, zb, SELF_WIDTH // MEM_WIDTH, mkv_b, mvt_b, bf(b_w_out[0]),
                    seq=seq, mem_len=mem_len, tm=tm, name="b_attn_out")
    x2d = _mlp(x2d, b_norm_mlp[0], bf(b_w_up[0]), bf(b_w_down[0]), final_norm,
               final_norm=True, tm=512, tf=512, name="b_mlp")
    return x2d.reshape(batch, seq, d)
```

```python
import functools
import math

import jax
import jax.numpy as jnp
from jax import lax
from jax.experimental import pallas as pl
from jax.experimental.pallas import tpu as pltpu

D_MODEL = 1024
HEAD_DIM = 64
SELF_WIDTH = 768
MEM_WIDTH = 256
N_MEM_HEADS = 4
D_FF = 4096
MOBA_BLOCK = 256
MOBA_TOPK = 3
ROPE_THETA = 10000.0
EPS = 1e-6
NEG = -1e30

LANES = 128
N_SELF_GROUPS = SELF_WIDTH // LANES
QK_SCALE = HEAD_DIM ** -0.5
LOG2E = math.log2(math.e)
KV_CHUNK = 256
SUM_ROWS = 16

F32 = jnp.float32
BF16 = jnp.bfloat16

_VMEM_LIMIT = 56 * 1024 * 1024


def _params(*sem):
    return pltpu.CompilerParams(dimension_semantics=sem, vmem_limit_bytes=_VMEM_LIMIT)


def _proj_kernel(x_ref, g_ref, w_ref, cos_ref, sin_ref, o_ref, *extra_refs,
                 n_rope, scales, col_chunk, with_kmean, vt_start):
    tm = x_ref.shape[0]
    n_out = w_ref.shape[1]
    extra = list(extra_refs)
    km_ref = extra.pop(0) if with_kmean else None
    vt_ref = extra.pop(0) if vt_start is not None else None
    x = x_ref[...]
    ms = jnp.mean(x * x, axis=-1, keepdims=True)
    h = (x * lax.rsqrt(ms + EPS) * g_ref[...]).astype(BF16)
    lane = lax.broadcasted_iota(jnp.int32, (tm, LANES), 1)
    first_half = (lane & (HEAD_DIM - 1)) < HEAD_DIM // 2
    if n_rope:
        cos = cos_ref[...]
        sin = sin_ref[...]
    for c0 in range(0, n_out, col_chunk):
        z = jnp.dot(h, w_ref[:, c0:c0 + col_chunk], preferred_element_type=F32)
        for jj in range(col_chunk // LANES):
            j = c0 // LANES + jj
            blk = z[:, jj * LANES:(jj + 1) * LANES]
            if j < n_rope:
                swap = jnp.where(first_half, pltpu.roll(blk, LANES - 32, 1), pltpu.roll(blk, 32, 1))
                blk = blk * cos + swap * sin
                if km_ref is not None:
                    km_ref[0, :, j * LANES:(j + 1) * LANES] = jnp.mean(
                        blk.reshape(tm // MOBA_BLOCK, MOBA_BLOCK, LANES), axis=1)
            if j in scales:
                blk = blk * scales[j]
            o_ref[:, j * LANES:(j + 1) * LANES] = blk.astype(o_ref.dtype)
            if vt_ref is not None and vt_start <= j < vt_start + vt_ref.shape[1] // LANES:
                g = j - vt_start
                for c in range(tm // KV_CHUNK):
                    vt_ref[c, g * LANES:(g + 1) * LANES, :] = (
                        blk[c * KV_CHUNK:(c + 1) * KV_CHUNK, :].T.astype(BF16))


def _proj(x2d, g, w, cos_t, sin_t, *, n_rope, scales, with_kmean, vt_start, seq, tm, name,
          vt_width=SELF_WIDTH):
    t, d = x2d.shape
    n_out = w.shape[1]
    n_pos_blocks = seq // tm
    col_chunk = 512 if n_out % 512 == 0 else 256
    out_shape = [jax.ShapeDtypeStruct((t, n_out), BF16)]
    out_specs = [pl.BlockSpec((tm, n_out), lambda i: (i, 0))]
    if with_kmean:
        out_shape.append(jax.ShapeDtypeStruct((t // tm, tm // MOBA_BLOCK, n_rope * LANES), F32))
        out_specs.append(pl.BlockSpec((1, tm // MOBA_BLOCK, n_rope * LANES), lambda i: (i, 0, 0)))
    if vt_start is not None:
        out_shape.append(jax.ShapeDtypeStruct((t // KV_CHUNK, vt_width, KV_CHUNK), BF16))
        out_specs.append(pl.BlockSpec((tm // KV_CHUNK, vt_width, KV_CHUNK), lambda i: (i, 0, 0)))
    kern = functools.partial(_proj_kernel, n_rope=n_rope, scales=dict(scales), col_chunk=col_chunk,
                             with_kmean=with_kmean, vt_start=vt_start)
    return pl.pallas_call(
        kern,
        out_shape=out_shape,
        grid=(t // tm,),
        in_specs=[
            pl.BlockSpec((tm, d), lambda i: (i, 0)),
            pl.BlockSpec((1, d), lambda i: (0, 0)),
            pl.BlockSpec((d, n_out), lambda i: (0, 0)),
            pl.BlockSpec((tm, LANES), lambda i: (i % n_pos_blocks, 0)),
            pl.BlockSpec((tm, LANES), lambda i: (i % n_pos_blocks, 0)),
        ],
        out_specs=out_specs,
        compiler_params=_params("parallel"),
        name=name,
    )(x2d, g.reshape(1, d), w, cos_t, sin_t)


_NT = (((1,), (1,)), ((), ()))


def _split_maps(q):
    lane = lax.broadcasted_iota(jnp.int32, q.shape, 1)
    zero = jnp.zeros_like(q)
    return [jnp.where(lane < HEAD_DIM, q, zero), jnp.where(lane >= HEAD_DIM, q, zero)]


def _causal_mask(st):
    key = lax.broadcasted_iota(jnp.int32, st.shape, 0)
    qry = lax.broadcasted_iota(jnp.int32, st.shape, 1)
    return jnp.where(key <= qry, st, NEG)


def _flash_pipeline(n_groups, n_past, tq, scores_fn, vt_fn, bias_fn, finalize_fn, s_scr, acc_scr):
    streams = [(g, h) for g in range(n_groups) for h in range(2)]

    def produce(j, buf):
        col_max = []
        for g, h in streams:
            st = scores_fn(g, h, j)
            s_scr[buf, g, h] = st
            col_max.append(jnp.max(st, axis=0, keepdims=True))
        return tuple(col_max)

    ones_rows = jnp.ones((SUM_ROWS, tq), BF16)

    def consume(j, buf, stats, col_max, diagonal):
        out, accs = [], []
        for i, (g, h) in enumerate(streams):
            st = s_scr[buf, g, h]
            if diagonal:
                st = _causal_mask(st)
                m_cur = jnp.max(st, axis=0, keepdims=True)
            else:
                m_cur = col_max[i]
            if bias_fn is not None:
                bias = bias_fn(g, h, j)
                m_cur = m_cur + bias
            m_next = jnp.maximum(stats[i], m_cur)
            alpha = jnp.exp2(stats[i] - m_next)
            shift = m_next if bias_fn is None else m_next - bias
            p = jnp.exp2((st - shift).astype(BF16))
            acc = alpha * acc_scr[g, h] + jnp.dot(
                jnp.concatenate([vt_fn(g, h, j), ones_rows], axis=0), p, preferred_element_type=F32)
            if diagonal:
                accs.append(acc)
                if h == 1:
                    finalize_fn(g, accs[-2:])
            else:
                acc_scr[g, h] = acc
            out.append(m_next)
        return tuple(out)

    def step(j, buf, carry):
        nxt = produce(j + 1, 1 - buf)
        return consume(j, buf, carry[0], carry[1], False), nxt

    def pair(i, carry):
        return step(2 * i + 1, 1, step(2 * i, 0, carry))

    def quad(i, carry):
        return pair(2 * i + 1, pair(2 * i, carry))

    def octet(i, carry):
        return quad(2 * i + 1, quad(2 * i, carry))

    acc_scr[...] = jnp.zeros(acc_scr.shape, F32)
    init = tuple(jnp.full((1, tq), NEG, F32) for _ in streams)
    n_octets = n_past // 8
    carry = lax.fori_loop(0, n_octets, octet, (init, produce(0, 0)))
    carry = lax.fori_loop(2 * n_octets, n_past // 4, quad, carry)
    carry = lax.fori_loop(2 * (n_past // 4), n_past // 2, pair, carry)

    def odd_tail(carry):
        stats, col_max = step(n_past - 1, 0, carry)
        return consume(n_past, 1, stats, col_max, True)

    def even_tail(carry):
        return consume(n_past, 0, carry[0], carry[1], True)

    return lax.cond(n_past % 2 == 1, odd_tail, even_tail, carry)


def _diff_attn_kernel(lam_ref, g_ref, q_ref, k_ref, vt_ref, o_ref, s_scr, acc_scr, *, lam_init):
    tq = KV_CHUNK
    n_groups = q_ref.shape[1] // LANES
    cols = [slice(g * LANES, (g + 1) * LANES) for g in range(n_groups)]
    lp = lam_ref[...]
    lam = (jnp.exp(jnp.sum(lp[0:1] * lp[1:2], axis=1, keepdims=True))
           - jnp.exp(jnp.sum(lp[2:3] * lp[3:4], axis=1, keepdims=True)) + lam_init)

    def query_tile(qi, carry):
        rows = pl.ds(pl.multiple_of(qi * tq, tq), tq)
        qs = [_split_maps(q_ref[rows, c]) for c in cols]

        def scores(g, h, j):
            start = pl.multiple_of(j * tq, tq)
            return lax.dot_general(k_ref[pl.ds(start, tq), cols[g]], qs[g][h], _NT,
                                   preferred_element_type=F32)

        def finalize(g, accs):
            o1, o2 = [a[:LANES] * (1.0 / a[LANES:LANES + 1]) for a in accs]
            od_t = o1 - lam * o2
            ms = jnp.mean(od_t * od_t, axis=0, keepdims=True)
            y = (od_t * lax.rsqrt(ms + EPS)).T * g_ref[...]
            o_ref[rows, cols[g]] = (y * (1.0 - lam_init)).astype(o_ref.dtype)

        _flash_pipeline(n_groups, qi, tq, scores, lambda g, h, j: vt_ref[j, cols[g], :], None,
                        finalize, s_scr, acc_scr)
        return carry

    lax.fori_loop(0, q_ref.shape[0] // tq, query_tile, 0)


def _diff_attention(z, vt, lam_p, subln, *, batch, seq, n_groups, lam_init):
    t = z.shape[0]
    tq = KV_CHUNK
    nq = seq // tq
    w = n_groups * LANES
    n_steps = N_SELF_GROUPS // n_groups
    kern = functools.partial(_diff_attn_kernel, lam_init=lam_init)
    return pl.pallas_call(
        kern,
        out_shape=jax.ShapeDtypeStruct((t, SELF_WIDTH), BF16),
        grid=(batch, n_steps),
        in_specs=[
            pl.BlockSpec((4, HEAD_DIM), lambda b, h: (0, 0)),
            pl.BlockSpec((1, LANES), lambda b, h: (0, 0)),
            pl.BlockSpec((seq, w), lambda b, h: (b, h)),
            pl.BlockSpec((seq, w), lambda b, h: (b, n_steps + h)),
            pl.BlockSpec((seq // KV_CHUNK, w, KV_CHUNK), lambda b, h: (b, h, 0)),
        ],
        out_specs=pl.BlockSpec((seq, w), lambda b, h: (b, h)),
        scratch_shapes=[pltpu.VMEM((2, n_groups, 2, tq, tq), F32),
                        pltpu.VMEM((n_groups, 2, LANES + SUM_ROWS, tq), F32)],
        compiler_params=_params("parallel", "parallel"),
        name="diff_attn",
    )(lam_p, subln.reshape(1, LANES), z, z, vt)


def _moba_select_bias(qs, km, qi):
    n_blk = km.shape[0]
    parts, rest = [], km
    for _ in range(3):
        part = rest.astype(BF16)
        parts.append(part)
        rest = rest - part.astype(F32)
    terms = lax.dot_general(jnp.concatenate(parts, axis=0), qs, _NT, preferred_element_type=F32)
    gate = (terms[2 * n_blk:] + terms[n_blk:2 * n_blk]) + terms[:n_blk]
    blk = lax.broadcasted_iota(jnp.int32, gate.shape, 0)
    past = blk < qi
    gm = jnp.where(past, gate, NEG)
    sel = jnp.zeros(gate.shape, F32)
    for _ in range(MOBA_TOPK):
        mx = jnp.max(gm, axis=0, keepdims=True)
        first = jnp.min(jnp.where(gm == mx, blk, n_blk), axis=0, keepdims=True)
        pick = blk == first
        sel = jnp.where(pick, 1.0, sel)
        gm = jnp.where(pick, -jnp.inf, gm)
    return jnp.where((past & (sel > 0.5)) | (blk == qi), 0.0, 2.0 * NEG)


def _moba_attn_kernel(q_ref, k_ref, vt_ref, km_ref, o_ref, s_scr, acc_scr, bias_scr):
    tq = MOBA_BLOCK
    n_groups = q_ref.shape[1] // LANES
    cols = [slice(g * LANES, (g + 1) * LANES) for g in range(n_groups)]

    def query_tile(qi, carry):
        rows = pl.ds(pl.multiple_of(qi * tq, tq), tq)
        qs = [_split_maps(q_ref[rows, c]) for c in cols]
        for g, c in enumerate(cols):
            for h in range(2):
                bias_scr[g, h] = _moba_select_bias(qs[g][h], km_ref[:, c], qi)

        def scores(g, h, j):
            start = pl.multiple_of(j * tq, tq)
            return lax.dot_general(k_ref[pl.ds(start, tq), cols[g]], qs[g][h], _NT,
                                   preferred_element_type=F32)

        def values_t(g, h, j):
            return vt_ref[j, pl.ds(g * LANES + h * HEAD_DIM, HEAD_DIM), :]

        def finalize(g, accs):
            heads = [a[:HEAD_DIM] * (1.0 / a[HEAD_DIM:HEAD_DIM + 1]) for a in accs]
            o_ref[rows, cols[g]] = jnp.concatenate(heads, axis=0).T.astype(o_ref.dtype)

        _flash_pipeline(n_groups, qi, tq, scores, values_t,
                        lambda g, h, j: bias_scr[g, h, pl.ds(j, 1), :], finalize, s_scr, acc_scr)
        return carry

    lax.fori_loop(0, q_ref.shape[0] // tq, query_tile, 0)


def _moba_attention(zq, zkv, vt, kmean, *, batch, seq, n_groups):
    t = zq.shape[0]
    tq = MOBA_BLOCK
    nq = seq // tq
    w = n_groups * LANES
    n_steps = N_SELF_GROUPS // n_groups
    return pl.pallas_call(
        _moba_attn_kernel,
        out_shape=jax.ShapeDtypeStruct((t, SELF_WIDTH), BF16),
        grid=(batch, n_steps),
        in_specs=[
            pl.BlockSpec((seq, w), lambda b, h: (b, h)),
            pl.BlockSpec((seq, w), lambda b, h: (b, h)),
            pl.BlockSpec((seq // KV_CHUNK, w, KV_CHUNK), lambda b, h: (b, h, 0)),
            pl.BlockSpec((None, nq, w), lambda b, h: (b, 0, h)),
        ],
        out_specs=pl.BlockSpec((seq, w), lambda b, h: (b, h)),
        scratch_shapes=[pltpu.VMEM((2, n_groups, 2, tq, tq), F32),
                        pltpu.VMEM((n_groups, 2, HEAD_DIM + SUM_ROWS, tq), F32),
                        pltpu.VMEM((n_groups, 2, nq, tq), F32)],
        compiler_params=_params("parallel", "parallel"),
        name="moba_attn",
    )(zq, zkv, vt, kmean)


def _attn_out_kernel(x_ref, os_ref, qm_ref, mk_ref, mvt_ref, w_ref, o_ref):
    mem_len = mk_ref.shape[0]
    mk = mk_ref[...]
    lane = lax.broadcasted_iota(jnp.int32, mk.shape, 1)
    zero = jnp.zeros_like(mk)
    mk_heads = jnp.concatenate(
        [jnp.where((lane >= h * HEAD_DIM) & (lane < (h + 1) * HEAD_DIM), mk, zero)
         for h in range(N_MEM_HEADS)], axis=0)
    s = lax.dot_general(mk_heads, qm_ref[...], _NT, preferred_element_type=F32)
    ones_rows = jnp.ones((SUM_ROWS, mem_len), BF16)
    heads = []
    for h in range(N_MEM_HEADS):
        sh = s[h * mem_len:(h + 1) * mem_len]
        p = jnp.exp2((sh - jnp.max(sh, axis=0, keepdims=True)).astype(BF16))
        vt_ones = jnp.concatenate([mvt_ref[h * HEAD_DIM:(h + 1) * HEAD_DIM, :], ones_rows], axis=0)
        oh = jnp.dot(vt_ones, p, preferred_element_type=F32)
        heads.append(oh[:HEAD_DIM] * (1.0 / oh[HEAD_DIM:HEAD_DIM + 1]))
    o_mem = jnp.concatenate(heads, axis=0).T.astype(BF16)
    y = jnp.dot(os_ref[...], w_ref[:SELF_WIDTH, :], preferred_element_type=F32)
    y = y + jnp.dot(o_mem, w_ref[SELF_WIDTH:, :], preferred_element_type=F32)
    o_ref[...] = x_ref[...] + y


def _attn_out(x2d, o_self, zq, qm_block, memkv, memvt, w_out, *, seq, mem_len, tm, name):
    t, d = x2d.shape
    per_batch = seq // tm
    return pl.pallas_call(
        _attn_out_kernel,
        out_shape=jax.ShapeDtypeStruct((t, d), F32),
        grid=(t // tm,),
        in_specs=[
            pl.BlockSpec((tm, d), lambda i: (i, 0)),
            pl.BlockSpec((tm, SELF_WIDTH), lambda i: (i, 0)),
            pl.BlockSpec((tm, MEM_WIDTH), lambda i: (i, qm_block)),
            pl.BlockSpec((mem_len, MEM_WIDTH), lambda i: (i // per_batch, 0)),
            pl.BlockSpec((None, MEM_WIDTH, mem_len), lambda i: (i // per_batch, 0, 0)),
            pl.BlockSpec((d, d), lambda i: (0, 0)),
        ],
        out_specs=pl.BlockSpec((tm, d), lambda i: (i, 0)),
        compiler_params=_params("parallel"),
        name=name,
    )(x2d, o_self, zq, memkv, memvt, w_out)


def _mlp_kernel(x_ref, g_ref, wu_ref, wd_ref, gf_ref, o_ref, u_scr, *, final_norm, tf):
    x = x_ref[...]
    ms = jnp.mean(x * x, axis=-1, keepdims=True)
    h = (x * lax.rsqrt(ms + EPS) * g_ref[...]).astype(BF16)
    for c0 in range(0, wu_ref.shape[1], tf):
        u = jnp.maximum(jnp.dot(h, wu_ref[:, c0:c0 + tf], preferred_element_type=F32), 0.0)
        u_scr[:, c0:c0 + tf] = (u * u).astype(BF16)
    y = x + jnp.dot(u_scr[...], wd_ref[...], preferred_element_type=F32)
    if final_norm:
        ms = jnp.mean(y * y, axis=-1, keepdims=True)
        y = y * lax.rsqrt(ms + EPS) * gf_ref[...]
    o_ref[...] = y


def _mlp(x2d, g, w_up, w_down, g_final, *, final_norm, tm, tf, name):
    t, d = x2d.shape
    dff = w_up.shape[1]
    kern = functools.partial(_mlp_kernel, final_norm=final_norm, tf=tf)
    resident = dict(pipeline_mode=pl.Buffered(1))
    return pl.pallas_call(
        kern,
        out_shape=jax.ShapeDtypeStruct((t, d), F32),
        grid=(t // tm,),
        in_specs=[
            pl.BlockSpec((tm, d), lambda i: (i, 0)),
            pl.BlockSpec((1, d), lambda i: (0, 0)),
            pl.BlockSpec((d, dff), lambda i: (0, 0), **resident),
            pl.BlockSpec((dff, d), lambda i: (0, 0), **resident),
            pl.BlockSpec((1, d), lambda i: (0, 0)),
        ],
        out_specs=pl.BlockSpec((tm, d), lambda i: (i, 0)),
        scratch_shapes=[pltpu.VMEM((tm, dff), BF16)],
        compiler_params=_params("parallel"),
        name=name,
    )(x2d, g.reshape(1, d), w_up, w_down, g_final.reshape(1, d))


def _rope_tables(seq):
    half = HEAD_DIM // 2
    inv = 1.0 / (ROPE_THETA ** (jnp.arange(half, dtype=F32) / half))
    ang = jnp.arange(seq, dtype=F32)[:, None] * inv[None, :]
    cos, sin = jnp.cos(ang), jnp.sin(ang)
    reps = LANES // half
    cos_t = jnp.tile(cos, (1, reps))
    sign = jnp.tile(jnp.concatenate([-jnp.ones((half,), F32), jnp.ones((half,), F32)]), LANES // HEAD_DIM)
    sin_t = jnp.tile(sin, (1, reps)) * sign[None, :]
    return cos_t, sin_t


def kernel(x, mem, a_norm_attn, a_w_in, a_lambda, a_subln, a_mem_norm, a_w_mem_kv, a_w_out, a_norm_mlp, a_w_up, a_w_down, kv_norm, w_kv, b_norm_attn, b_w_in, b_mem_norm, b_w_mem_kv, b_w_out, b_norm_mlp, b_w_up, b_w_down, final_norm):
    batch, seq, d = x.shape
    mem_len = mem.shape[1]
    t = batch * seq
    x2d = x.reshape(t, d)
    mem2d = mem.reshape(batch * mem_len, d)
    cos_t, sin_t = _rope_tables(seq)
    bf = lambda w: w.astype(BF16)
    tm = 512
    ng = N_SELF_GROUPS
    q_scale = QK_SCALE * LOG2E
    mem_groups = MEM_WIDTH // LANES

    def memkv(g, w, name):
        return _proj(mem2d, g, bf(w), cos_t, sin_t, n_rope=0, scales={}, with_kmean=False,
                     vt_start=mem_groups, vt_width=MEM_WIDTH, seq=mem_len, tm=mem_len, name=name)

    lam_init = 0.8 - 0.6 * math.exp(-0.3 * 0)
    za, vt_a = _proj(x2d, a_norm_attn[0], bf(a_w_in[0]), cos_t, sin_t, n_rope=2 * ng,
                     scales={j: q_scale for j in (*range(ng), 3 * ng, 3 * ng + 1)},
                     with_kmean=False, vt_start=2 * ng, seq=seq, tm=tm, name="a_proj")
    mkv_a, mvt_a = memkv(a_mem_norm[0], a_w_mem_kv[0], "a_memkv")
    o_self = _diff_attention(za, vt_a, a_lambda[0], a_subln[0], batch=batch, seq=seq,
                             n_groups=2, lam_init=lam_init)
    x2d = _attn_out(x2d, o_self, za, 3 * SELF_WIDTH // MEM_WIDTH, mkv_a, mvt_a, bf(a_w_out[0]),
                    seq=seq, mem_len=mem_len, tm=tm, name="a_attn_out")
    x2d = _mlp(x2d, a_norm_mlp[0], bf(a_w_up[0]), bf(a_w_down[0]), final_norm,
               final_norm=False, tm=512, tf=512, name="a_mlp")

    zkv, kmean, vt_b = _proj(x2d, kv_norm, bf(w_kv), cos_t, sin_t, n_rope=ng, scales={},
                             with_kmean=True, vt_start=ng, seq=seq, tm=tm, name="b_kvproj")
    kmean = kmean.reshape(batch, seq // MOBA_BLOCK, SELF_WIDTH)
    zb, = _proj(x2d, b_norm_attn[0], bf(b_w_in[0]), cos_t, sin_t, n_rope=ng,
                scales={j: q_scale for j in range(ng + mem_groups)}, with_kmean=False, vt_start=None,
                seq=seq, tm=tm, name="b_qproj")
    mkv_b, mvt_b = memkv(b_mem_norm[0], b_w_mem_kv[0], "b_memkv")
    o_self = _moba_attention(zb, zkv, vt_b, kmean, batch=batch, seq=seq, n_groups=2)
    x2d = _attn_out(x2d, o_self, zb, SELF_WIDTH // MEM_WIDTH, mkv_b, mvt_b, bf(b_w_out[0]),
                    seq=seq, mem_len=mem_len, tm=tm, name="b_attn_out")
    x2d = _mlp(x2d, b_norm_mlp[0], bf(b_w_up[0]), bf(b_w_down[0]), final_norm,
               final_norm=True, tm=512, tf=512, name="b_mlp")
    return x2d.reshape(batch, seq, d)
```

```python
import functools
import math

import jax
import jax.numpy as jnp
from jax import lax
from jax.experimental import pallas as pl
from jax.experimental.pallas import tpu as pltpu

D_MODEL = 1024
HEAD_DIM = 64
SELF_WIDTH = 768
MEM_WIDTH = 256
N_MEM_HEADS = 4
D_FF = 4096
MOBA_BLOCK = 256
MOBA_TOPK = 3
ROPE_THETA = 10000.0
EPS = 1e-6
NEG = -1e30

LANES = 128
N_SELF_GROUPS = SELF_WIDTH // LANES
QK_SCALE = HEAD_DIM ** -0.5
LOG2E = math.log2(math.e)
KV_CHUNK = 256
SUM_ROWS = 16

F32 = jnp.float32
BF16 = jnp.bfloat16

_VMEM_LIMIT = 56 * 1024 * 1024


def _params(*sem):
    return pltpu.CompilerParams(dimension_semantics=sem, vmem_limit_bytes=_VMEM_LIMIT)


def _proj_kernel(x_ref, g_ref, w_ref, cos_ref, sin_ref, o_ref, *extra_refs,
                 n_rope, scales, col_chunk, with_kmean, vt_start):
    tm = x_ref.shape[0]
    n_out = w_ref.shape[1]
    extra = list(extra_refs)
    km_ref = extra.pop(0) if with_kmean else None
    vt_ref = extra.pop(0) if vt_start is not None else None
    x = x_ref[...]
    ms = jnp.mean(x * x, axis=-1, keepdims=True)
    h = (x * lax.rsqrt(ms + EPS) * g_ref[...]).astype(BF16)
    lane = lax.broadcasted_iota(jnp.int32, (tm, LANES), 1)
    first_half = (lane & (HEAD_DIM - 1)) < HEAD_DIM // 2
    if n_rope:
        cos = cos_ref[...]
        sin = sin_ref[...]
    for c0 in range(0, n_out, col_chunk):
        z = jnp.dot(h, w_ref[:, c0:c0 + col_chunk], preferred_element_type=F32)
        for jj in range(col_chunk // LANES):
            j = c0 // LANES + jj
            blk = z[:, jj * LANES:(jj + 1) * LANES]
            if j < n_rope:
                swap = jnp.where(first_half, pltpu.roll(blk, LANES - 32, 1), pltpu.roll(blk, 32, 1))
                blk = blk * cos + swap * sin
                if km_ref is not None:
                    km_ref[0, :, j * LANES:(j + 1) * LANES] = jnp.mean(
                        blk.reshape(tm // MOBA_BLOCK, MOBA_BLOCK, LANES), axis=1)
            if j in scales:
                blk = blk * scales[j]
            o_ref[:, j * LANES:(j + 1) * LANES] = blk.astype(o_ref.dtype)
            if vt_ref is not None and vt_start <= j < vt_start + vt_ref.shape[1] // LANES:
                g = j - vt_start
                for c in range(tm // KV_CHUNK):
                    vt_ref[c, g * LANES:(g + 1) * LANES, :] = (
                        blk[c * KV_CHUNK:(c + 1) * KV_CHUNK, :].T.astype(BF16))


def _proj(x2d, g, w, cos_t, sin_t, *, n_rope, scales, with_kmean, vt_start, seq, tm, name,
          vt_width=SELF_WIDTH):
    t, d = x2d.shape
    n_out = w.shape[1]
    n_pos_blocks = seq // tm
    col_chunk = 512 if n_out % 512 == 0 else 256
    out_shape = [jax.ShapeDtypeStruct((t, n_out), BF16)]
    out_specs = [pl.BlockSpec((tm, n_out), lambda i: (i, 0))]
    if with_kmean:
        out_shape.append(jax.ShapeDtypeStruct((t // tm, tm // MOBA_BLOCK, n_rope * LANES), F32))
        out_specs.append(pl.BlockSpec((1, tm // MOBA_BLOCK, n_rope * LANES), lambda i: (i, 0, 0)))
    if vt_start is not None:
        out_shape.append(jax.ShapeDtypeStruct((t // KV_CHUNK, vt_width, KV_CHUNK), BF16))
        out_specs.append(pl.BlockSpec((tm // KV_CHUNK, vt_width, KV_CHUNK), lambda i: (i, 0, 0)))
    kern = functools.partial(_proj_kernel, n_rope=n_rope, scales=dict(scales), col_chunk=col_chunk,
                             with_kmean=with_kmean, vt_start=vt_start)
    return pl.pallas_call(
        kern,
        out_shape=out_shape,
        grid=(t // tm,),
        in_specs=[
            pl.BlockSpec((tm, d), lambda i: (i, 0)),
            pl.BlockSpec((1, d), lambda i: (0, 0)),
            pl.BlockSpec((d, n_out), lambda i: (0, 0)),
            pl.BlockSpec((tm, LANES), lambda i: (i % n_pos_blocks, 0)),
            pl.BlockSpec((tm, LANES), lambda i: (i % n_pos_blocks, 0)),
        ],
        out_specs=out_specs,
        compiler_params=_params("parallel"),
        name=name,
    )(x2d, g.reshape(1, d), w, cos_t, sin_t)


_NT = (((1,), (1,)), ((), ()))


def _split_maps(q):
    lane = lax.broadcasted_iota(jnp.int32, q.shape, 1)
    zero = jnp.zeros_like(q)
    return [jnp.where(lane < HEAD_DIM, q, zero), jnp.where(lane >= HEAD_DIM, q, zero)]


def _causal_mask(st):
    key = lax.broadcasted_iota(jnp.int32, st.shape, 0)
    qry = lax.broadcasted_iota(jnp.int32, st.shape, 1)
    return jnp.where(key <= qry, st, NEG)


def _flash_pipeline(n_pairs, tq, scores_fn, vt_fn, bias_fn, finalize_fn, s_scr, acc_scr):
    streams = [(u, h) for u in range(2) for h in range(2)]
    both = (0, 1)

    def produce(j, buf, subtiles, old):
        col_max = list(old)
        for i, (u, h) in enumerate(streams):
            if u in subtiles:
                st = scores_fn(u, h, j)
                s_scr[buf, u, h] = st
                col_max[i] = jnp.max(st, axis=0, keepdims=True)
        return tuple(col_max)

    ones_rows = jnp.ones((SUM_ROWS, tq), BF16)

    def consume(j, buf, stats, col_max, subtiles, diagonal):
        out, accs = list(stats), []
        for i, (u, h) in enumerate(streams):
            if u not in subtiles:
                continue
            st = s_scr[buf, u, h]
            if u == diagonal:
                st = _causal_mask(st)
                m_cur = jnp.max(st, axis=0, keepdims=True)
            else:
                m_cur = col_max[i]
            if bias_fn is not None:
                bias = bias_fn(u, h, j)
                m_cur = m_cur + bias
            m_next = jnp.maximum(stats[i], m_cur)
            alpha = jnp.exp2(stats[i] - m_next)
            shift = m_next if bias_fn is None else m_next - bias
            p = jnp.exp2((st - shift).astype(BF16))
            acc = alpha * acc_scr[u, h] + jnp.dot(
                jnp.concatenate([vt_fn(u, h, j), ones_rows], axis=0), p, preferred_element_type=F32)
            if u == diagonal:
                accs.append(acc)
                if h == 1:
                    finalize_fn(u, accs)
            else:
                acc_scr[u, h] = acc
            out[i] = m_next
        return tuple(out)

    def step(j, buf, carry):
        nxt = produce(j + 1, 1 - buf, both, carry[1])
        return consume(j, buf, carry[0], carry[1], both, None), nxt

    def pair(i, carry):
        return step(2 * i + 1, 1, step(2 * i, 0, carry))

    def quad(i, carry):
        return pair(2 * i + 1, pair(2 * i, carry))

    def octet(i, carry):
        return quad(2 * i + 1, quad(2 * i, carry))

    acc_scr[...] = jnp.zeros(acc_scr.shape, F32)
    init = tuple(jnp.full((1, tq), NEG, F32) for _ in streams)
    carry = (init, produce(0, 0, both, init))
    carry = lax.fori_loop(0, n_pairs // 4, octet, carry)
    carry = lax.fori_loop(2 * (n_pairs // 4), n_pairs // 2, quad, carry)
    carry = lax.fori_loop(2 * (n_pairs // 2), n_pairs, pair, carry)

    stats, col_max = carry
    last = produce(2 * n_pairs + 1, 1, (1,), col_max)
    stats = consume(2 * n_pairs, 0, stats, col_max, both, 0)
    consume(2 * n_pairs + 1, 1, stats, last, (1,), 1)


def _diff_attn_kernel(lam_ref, g_ref, q_ref, k_ref, vt_ref, o_ref, s_scr, acc_scr, *, lam_init):
    tq = KV_CHUNK
    lp = lam_ref[...]
    lam = (jnp.exp(jnp.sum(lp[0:1] * lp[1:2], axis=1, keepdims=True))
           - jnp.exp(jnp.sum(lp[2:3] * lp[3:4], axis=1, keepdims=True)) + lam_init)

    def query_tile(ti, carry):
        rows = [pl.ds(pl.multiple_of((2 * ti + u) * tq, tq), tq) for u in range(2)]
        qs = [_split_maps(q_ref[r, :]) for r in rows]

        def scores(u, h, j):
            start = pl.multiple_of(j * tq, tq)
            return lax.dot_general(k_ref[pl.ds(start, tq), :], qs[u][h], _NT,
                                   preferred_element_type=F32)

        def finalize(u, accs):
            o1, o2 = [a[:LANES] * (1.0 / a[LANES:LANES + 1]) for a in accs]
            od_t = o1 - lam * o2
            ms = jnp.mean(od_t * od_t, axis=0, keepdims=True)
            y = (od_t * lax.rsqrt(ms + EPS)).T * g_ref[...]
            o_ref[rows[u], :] = (y * (1.0 - lam_init)).astype(o_ref.dtype)

        _flash_pipeline(ti, tq, scores, lambda u, h, j: vt_ref[j], None, finalize, s_scr, acc_scr)
        return carry

    lax.fori_loop(0, q_ref.shape[0] // (2 * tq), query_tile, 0)


def _diff_attention(z, vt, lam_p, subln, *, batch, seq, lam_init):
    t = z.shape[0]
    tq = KV_CHUNK
    ng = N_SELF_GROUPS
    kern = functools.partial(_diff_attn_kernel, lam_init=lam_init)
    return pl.pallas_call(
        kern,
        out_shape=jax.ShapeDtypeStruct((t, SELF_WIDTH), BF16),
        grid=(batch, ng),
        in_specs=[
            pl.BlockSpec((4, HEAD_DIM), lambda b, h: (0, 0)),
            pl.BlockSpec((1, LANES), lambda b, h: (0, 0)),
            pl.BlockSpec((seq, LANES), lambda b, h: (b, h)),
            pl.BlockSpec((seq, LANES), lambda b, h: (b, ng + h)),
            pl.BlockSpec((seq // KV_CHUNK, LANES, KV_CHUNK), lambda b, h: (b, h, 0)),
        ],
        out_specs=pl.BlockSpec((seq, LANES), lambda b, h: (b, h)),
        scratch_shapes=[pltpu.VMEM((2, 2, 2, tq, tq), F32),
                        pltpu.VMEM((2, 2, LANES + SUM_ROWS, tq), F32)],
        compiler_params=_params("parallel", "parallel"),
        name="diff_attn",
    )(lam_p, subln.reshape(1, LANES), z, z, vt)


def _moba_select_bias(qs, km, qi):
    n_blk = km.shape[0]
    parts, rest = [], km
    for _ in range(3):
        part = rest.astype(BF16)
        parts.append(part)
        rest = rest - part.astype(F32)
    terms = lax.dot_general(jnp.concatenate(parts, axis=0), qs, _NT, preferred_element_type=F32)
    gate = (terms[2 * n_blk:] + terms[n_blk:2 * n_blk]) + terms[:n_blk]
    blk = lax.broadcasted_iota(jnp.int32, gate.shape, 0)
    past = blk < qi
    gm = jnp.where(past, gate, NEG)
    sel = jnp.zeros(gate.shape, F32)
    for _ in range(MOBA_TOPK):
        mx = jnp.max(gm, axis=0, keepdims=True)
        first = jnp.min(jnp.where(gm == mx, blk, n_blk), axis=0, keepdims=True)
        pick = blk == first
        sel = jnp.where(pick, 1.0, sel)
        gm = jnp.where(pick, -jnp.inf, gm)
    return jnp.where((past & (sel > 0.5)) | (blk == qi), 0.0, 2.0 * NEG)


def _moba_attn_kernel(q_ref, k_ref, vt_ref, km_ref, o_ref, s_scr, acc_scr, bias_scr):
    tq = MOBA_BLOCK

    def query_tile(ti, carry):
        blocks = [2 * ti + u for u in range(2)]
        rows = [pl.ds(pl.multiple_of(b * tq, tq), tq) for b in blocks]
        qs = [_split_maps(q_ref[r, :]) for r in rows]
        for u in range(2):
            for h in range(2):
                bias_scr[u, h] = _moba_select_bias(qs[u][h], km_ref[...], blocks[u])

        def scores(u, h, j):
            start = pl.multiple_of(j * tq, tq)
            return lax.dot_general(k_ref[pl.ds(start, tq), :], qs[u][h], _NT,
                                   preferred_element_type=F32)

        def values_t(u, h, j):
            return vt_ref[j, h * HEAD_DIM:(h + 1) * HEAD_DIM, :]

        def finalize(u, accs):
            heads = [a[:HEAD_DIM] * (1.0 / a[HEAD_DIM:HEAD_DIM + 1]) for a in accs]
            o_ref[rows[u], :] = jnp.concatenate(heads, axis=0).T.astype(o_ref.dtype)

        _flash_pipeline(ti, tq, scores, values_t,
                        lambda u, h, j: bias_scr[u, h, pl.ds(j, 1), :], finalize, s_scr, acc_scr)
        return carry

    lax.fori_loop(0, q_ref.shape[0] // (2 * tq), query_tile, 0)


def _moba_attention(zq, zkv, vt, kmean, *, batch, seq):
    t = zq.shape[0]
    tq = MOBA_BLOCK
    nq = seq // tq
    ng = N_SELF_GROUPS
    return pl.pallas_call(
        _moba_attn_kernel,
        out_shape=jax.ShapeDtypeStruct((t, SELF_WIDTH), BF16),
        grid=(batch, ng),
        in_specs=[
            pl.BlockSpec((seq, LANES), lambda b, h: (b, h)),
            pl.BlockSpec((seq, LANES), lambda b, h: (b, h)),
            pl.BlockSpec((seq // KV_CHUNK, LANES, KV_CHUNK), lambda b, h: (b, h, 0)),
            pl.BlockSpec((None, nq, LANES), lambda b, h: (b, 0, h)),
        ],
        out_specs=pl.BlockSpec((seq, LANES), lambda b, h: (b, h)),
        scratch_shapes=[pltpu.VMEM((2, 2, 2, tq, tq), F32),
                        pltpu.VMEM((2, 2, HEAD_DIM + SUM_ROWS, tq), F32),
                        pltpu.VMEM((2, 2, nq, tq), F32)],
        compiler_params=_params("parallel", "parallel"),
        name="moba_attn",
    )(zq, zkv, vt, kmean)


def _attn_out_kernel(x_ref, os_ref, qm_ref, mk_ref, mvt_ref, w_ref, o_ref):
    mem_len = mk_ref.shape[0]
    mk = mk_ref[...]
    lane = lax.broadcasted_iota(jnp.int32, mk.shape, 1)
    zero = jnp.zeros_like(mk)
    mk_heads = jnp.concatenate(
        [jnp.where((lane >= h * HEAD_DIM) & (lane < (h + 1) * HEAD_DIM), mk, zero)
         for h in range(N_MEM_HEADS)], axis=0)
    s = lax.dot_general(mk_heads, qm_ref[...], _NT, preferred_element_type=F32)
    ones_rows = jnp.ones((SUM_ROWS, mem_len), BF16)
    heads = []
    for h in range(N_MEM_HEADS):
        sh = s[h * mem_len:(h + 1) * mem_len]
        p = jnp.exp2((sh - jnp.max(sh, axis=0, keepdims=True)).astype(BF16))
        vt_ones = jnp.concatenate([mvt_ref[h * HEAD_DIM:(h + 1) * HEAD_DIM, :], ones_rows], axis=0)
        oh = jnp.dot(vt_ones, p, preferred_element_type=F32)
        heads.append(oh[:HEAD_DIM] * (1.0 / oh[HEAD_DIM:HEAD_DIM + 1]))
    o_mem = jnp.concatenate(heads, axis=0).T.astype(BF16)
    y = jnp.dot(os_ref[...], w_ref[:SELF_WIDTH, :], preferred_element_type=F32)
    y = y + jnp.dot(o_mem, w_ref[SELF_WIDTH:, :], preferred_element_type=F32)
    o_ref[...] = x_ref[...] + y


def _attn_out(x2d, o_self, zq, qm_block, memkv, memvt, w_out, *, seq, mem_len, tm, name):
    t, d = x2d.shape
    per_batch = seq // tm
    return pl.pallas_call(
        _attn_out_kernel,
        out_shape=jax.ShapeDtypeStruct((t, d), F32),
        grid=(t // tm,),
        in_specs=[
            pl.BlockSpec((tm, d), lambda i: (i, 0)),
            pl.BlockSpec((tm, SELF_WIDTH), lambda i: (i, 0)),
            pl.BlockSpec((tm, MEM_WIDTH), lambda i: (i, qm_block)),
            pl.BlockSpec((mem_len, MEM_WIDTH), lambda i: (i // per_batch, 0)),
            pl.BlockSpec((None, MEM_WIDTH, mem_len), lambda i: (i // per_batch, 0, 0)),
            pl.BlockSpec((d, d), lambda i: (0, 0)),
        ],
        out_specs=pl.BlockSpec((tm, d), lambda i: (i, 0)),
        compiler_params=_params("parallel"),
        name=name,
    )(x2d, o_self, zq, memkv, memvt, w_out)


def _mlp_kernel(x_ref, g_ref, wu_ref, wd_ref, gf_ref, o_ref, u_scr, *, final_norm, tf):
    x = x_ref[...]
    ms = jnp.mean(x * x, axis=-1, keepdims=True)
    h = (x * lax.rsqrt(ms + EPS) * g_ref[...]).astype(BF16)
    for c0 in range(0, wu_ref.shape[1], tf):
        u = jnp.maximum(jnp.dot(h, wu_ref[:, c0:c0 + tf], preferred_element_type=F32), 0.0)
        u_scr[:, c0:c0 + tf] = (u * u).astype(BF16)
    y = x + jnp.dot(u_scr[...], wd_ref[...], preferred_element_type=F32)
    if final_norm:
        ms = jnp.mean(y * y, axis=-1, keepdims=True)
        y = y * lax.rsqrt(ms + EPS) * gf_ref[...]
    o_ref[...] = y


def _mlp(x2d, g, w_up, w_down, g_final, *, final_norm, tm, tf, name):
    t, d = x2d.shape
    dff = w_up.shape[1]
    kern = functools.partial(_mlp_kernel, final_norm=final_norm, tf=tf)
    resident = dict(pipeline_mode=pl.Buffered(1))
    return pl.pallas_call(
        kern,
        out_shape=jax.ShapeDtypeStruct((t, d), F32),
        grid=(t // tm,),
        in_specs=[
            pl.BlockSpec((tm, d), lambda i: (i, 0)),
            pl.BlockSpec((1, d), lambda i: (0, 0)),
            pl.BlockSpec((d, dff), lambda i: (0, 0), **resident),
            pl.BlockSpec((dff, d), lambda i: (0, 0), **resident),
            pl.BlockSpec((1, d), lambda i: (0, 0)),
        ],
        out_specs=pl.BlockSpec((tm, d), lambda i: (i, 0)),
        scratch_shapes=[pltpu.VMEM((tm, dff), BF16)],
        compiler_params=_params("parallel"),
        name=name,
    )(x2d, g.reshape(1, d), w_up, w_down, g_final.reshape(1, d))


def _rope_tables(seq):
    half = HEAD_DIM // 2
    inv = 1.0 / (ROPE_THETA ** (jnp.arange(half, dtype=F32) / half))
    ang = jnp.arange(seq, dtype=F32)[:, None] * inv[None, :]
    cos, sin = jnp.cos(ang), jnp.sin(ang)
    reps = LANES // half
    cos_t = jnp.tile(cos, (1, reps))
    sign = jnp.tile(jnp.concatenate([-jnp.ones((half,), F32), jnp.ones((half,), F32)]), LANES // HEAD_DIM)
    sin_t = jnp.tile(sin, (1, reps)) * sign[None, :]
    return cos_t, sin_t


def kernel(x, mem, a_norm_attn, a_w_in, a_lambda, a_subln, a_mem_norm, a_w_mem_kv, a_w_out, a_norm_mlp, a_w_up, a_w_down, kv_norm, w_kv, b_norm_attn, b_w_in, b_mem_norm, b_w_mem_kv, b_w_out, b_norm_mlp, b_w_up, b_w_down, final_norm):
    batch, seq, d = x.shape
    mem_len = mem.shape[1]
    t = batch * seq
    x2d = x.reshape(t, d)
    mem2d = mem.reshape(batch * mem_len, d)
    cos_t, sin_t = _rope_tables(seq)
    bf = lambda w: w.astype(BF16)
    tm = 512
    ng = N_SELF_GROUPS
    q_scale = QK_SCALE * LOG2E
    mem_groups = MEM_WIDTH // LANES

    def memkv(g, w, name):
        return _proj(mem2d, g, bf(w), cos_t, sin_t, n_rope=0, scales={}, with_kmean=False,
                     vt_start=mem_groups, vt_width=MEM_WIDTH, seq=mem_len, tm=mem_len, name=name)

    lam_init = 0.8 - 0.6 * math.exp(-0.3 * 0)
    za, vt_a = _proj(x2d, a_norm_attn[0], bf(a_w_in[0]), cos_t, sin_t, n_rope=2 * ng,
                     scales={j: q_scale for j in (*range(ng), 3 * ng, 3 * ng + 1)},
                     with_kmean=False, vt_start=2 * ng, seq=seq, tm=tm, name="a_proj")
    mkv_a, mvt_a = memkv(a_mem_norm[0], a_w_mem_kv[0], "a_memkv")
    o_self = _diff_attention(za, vt_a, a_lambda[0], a_subln[0], batch=batch, seq=seq,
                             lam_init=lam_init)
    x2d = _attn_out(x2d, o_self, za, 3 * SELF_WIDTH // MEM_WIDTH, mkv_a, mvt_a, bf(a_w_out[0]),
                    seq=seq, mem_len=mem_len, tm=tm, name="a_attn_out")
    x2d = _mlp(x2d, a_norm_mlp[0], bf(a_w_up[0]), bf(a_w_down[0]), final_norm,
               final_norm=False, tm=512, tf=512, name="a_mlp")

    zkv, kmean, vt_b = _proj(x2d, kv_norm, bf(w_kv), cos_t, sin_t, n_rope=ng, scales={},
                             with_kmean=True, vt_start=ng, seq=seq, tm=tm, name="b_kvproj")
    kmean = kmean.reshape(batch, seq // MOBA_BLOCK, SELF_WIDTH)
    zb, = _proj(x2d, b_norm_attn[0], bf(b_w_in[0]), cos_t, sin_t, n_rope=ng,
                scales={j: q_scale for j in range(ng + mem_groups)}, with_kmean=False, vt_start=None,
                seq=seq, tm=tm, name="b_qproj")
    mkv_b, mvt_b = memkv(b_mem_norm[0], b_w_mem_kv[0], "b_memkv")
    o_self = _moba_attention(zb, zkv, vt_b, kmean, batch=batch, seq=seq)
    x2d = _attn_out(x2d, o_self, zb, SELF_WIDTH // MEM_WIDTH, mkv_b, mvt_b, bf(b_w_out[0]),
                    seq=seq, mem_len=mem_len, tm=tm, name="b_attn_out")
    x2d = _mlp(x2d, b_norm_mlp[0], bf(b_w_up[0]), bf(b_w_down[0]), final_norm,
               final_norm=True, tm=512, tf=512, name="b_mlp")
    return x2d.reshape(batch, seq, d)
```

```python
import functools
import math

import jax
import jax.numpy as jnp
from jax import lax
from jax.experimental import pallas as pl
from jax.experimental.pallas import tpu as pltpu

D_MODEL = 1024
HEAD_DIM = 64
SELF_WIDTH = 768
MEM_WIDTH = 256
N_MEM_HEADS = 4
D_FF = 4096
MOBA_BLOCK = 256
MOBA_TOPK = 3
ROPE_THETA = 10000.0
EPS = 1e-6
NEG = -1e30

LANES = 128
N_SELF_GROUPS = SELF_WIDTH // LANES
QK_SCALE = HEAD_DIM ** -0.5
LOG2E = math.log2(math.e)
KV_CHUNK = 256
SUM_ROWS = 16

F32 = jnp.float32
BF16 = jnp.bfloat16

_VMEM_LIMIT = 56 * 1024 * 1024


def _params(*sem):
    return pltpu.CompilerParams(dimension_semantics=sem, vmem_limit_bytes=_VMEM_LIMIT)


def _proj_kernel(x_ref, g_ref, w_ref, cos_ref, sin_ref, o_ref, *extra_refs,
                 n_rope, scales, col_chunk, with_kmean, vt_start):
    tm = x_ref.shape[0]
    n_out = w_ref.shape[1]
    extra = list(extra_refs)
    km_ref = extra.pop(0) if with_kmean else None
    vt_ref = extra.pop(0) if vt_start is not None else None
    x = x_ref[...]
    ms = jnp.mean(x * x, axis=-1, keepdims=True)
    h = (x * lax.rsqrt(ms + EPS) * g_ref[...]).astype(BF16)
    lane = lax.broadcasted_iota(jnp.int32, (tm, LANES), 1)
    first_half = (lane & (HEAD_DIM - 1)) < HEAD_DIM // 2
    if n_rope:
        cos = cos_ref[...]
        sin = sin_ref[...]
    for c0 in range(0, n_out, col_chunk):
        z = jnp.dot(h, w_ref[:, c0:c0 + col_chunk], preferred_element_type=F32)
        for jj in range(col_chunk // LANES):
            j = c0 // LANES + jj
            blk = z[:, jj * LANES:(jj + 1) * LANES]
            if j < n_rope:
                swap = jnp.where(first_half, pltpu.roll(blk, LANES - 32, 1), pltpu.roll(blk, 32, 1))
                blk = blk * cos + swap * sin
                if km_ref is not None:
                    km_ref[0, :, j * LANES:(j + 1) * LANES] = jnp.mean(
                        blk.reshape(tm // MOBA_BLOCK, MOBA_BLOCK, LANES), axis=1)
            if j in scales:
                blk = blk * scales[j]
            o_ref[:, j * LANES:(j + 1) * LANES] = blk.astype(o_ref.dtype)
            if vt_ref is not None and vt_start <= j < vt_start + vt_ref.shape[1] // LANES:
                g = j - vt_start
                for c in range(tm // KV_CHUNK):
                    vt_ref[c, g * LANES:(g + 1) * LANES, :] = (
                        blk[c * KV_CHUNK:(c + 1) * KV_CHUNK, :].T.astype(BF16))


def _proj(x2d, g, w, cos_t, sin_t, *, n_rope, scales, with_kmean, vt_start, seq, tm, name,
          vt_width=SELF_WIDTH):
    t, d = x2d.shape
    n_out = w.shape[1]
    n_pos_blocks = seq // tm
    col_chunk = 512 if n_out % 512 == 0 else 256
    out_shape = [jax.ShapeDtypeStruct((t, n_out), BF16)]
    out_specs = [pl.BlockSpec((tm, n_out), lambda i: (i, 0))]
    if with_kmean:
        out_shape.append(jax.ShapeDtypeStruct((t // tm, tm // MOBA_BLOCK, n_rope * LANES), F32))
        out_specs.append(pl.BlockSpec((1, tm // MOBA_BLOCK, n_rope * LANES), lambda i: (i, 0, 0)))
    if vt_start is not None:
        out_shape.append(jax.ShapeDtypeStruct((t // KV_CHUNK, vt_width, KV_CHUNK), BF16))
        out_specs.append(pl.BlockSpec((tm // KV_CHUNK, vt_width, KV_CHUNK), lambda i: (i, 0, 0)))
    kern = functools.partial(_proj_kernel, n_rope=n_rope, scales=dict(scales), col_chunk=col_chunk,
                             with_kmean=with_kmean, vt_start=vt_start)
    return pl.pallas_call(
        kern,
        out_shape=out_shape,
        grid=(t // tm,),
        in_specs=[
            pl.BlockSpec((tm, d), lambda i: (i, 0)),
            pl.BlockSpec((1, d), lambda i: (0, 0)),
            pl.BlockSpec((d, n_out), lambda i: (0, 0)),
            pl.BlockSpec((tm, LANES), lambda i: (i % n_pos_blocks, 0)),
            pl.BlockSpec((tm, LANES), lambda i: (i % n_pos_blocks, 0)),
        ],
        out_specs=out_specs,
        compiler_params=_params("parallel"),
        name=name,
    )(x2d, g.reshape(1, d), w, cos_t, sin_t)


_NT = (((1,), (1,)), ((), ()))


def _split_maps(q):
    lane = lax.broadcasted_iota(jnp.int32, q.shape, 1)
    zero = jnp.zeros_like(q)
    return [jnp.where(lane < HEAD_DIM, q, zero), jnp.where(lane >= HEAD_DIM, q, zero)]


def _causal_mask(st):
    key = lax.broadcasted_iota(jnp.int32, st.shape, 0)
    qry = lax.broadcasted_iota(jnp.int32, st.shape, 1)
    return jnp.where(key <= qry, st, NEG)


def _flash_pipeline(n_groups, n_past, tq, scores_fn, vt_fn, bias_fn, finalize_fn, s_scr, acc_scr):
    streams = [(g, h) for g in range(n_groups) for h in range(2)]

    def produce(j, buf):
        col_max = []
        for g, h in streams:
            st = scores_fn(g, h, j)
            s_scr[buf, g, h] = st
            col_max.append(jnp.max(st, axis=0, keepdims=True))
        return tuple(col_max)

    ones_rows = jnp.ones((SUM_ROWS, tq), BF16)

    def consume(j, buf, stats, col_max, diagonal):
        out, accs = [], []
        for i, (g, h) in enumerate(streams):
            st = s_scr[buf, g, h]
            if diagonal:
                st = _causal_mask(st)
                m_cur = jnp.max(st, axis=0, keepdims=True)
            else:
                m_cur = col_max[i]
            if bias_fn is not None:
                bias = bias_fn(g, h, j)
                m_cur = m_cur + bias
            m_next = jnp.maximum(stats[i], m_cur)
            alpha = jnp.exp2(stats[i] - m_next)
            shift = m_next if bias_fn is None else m_next - bias
            p = jnp.exp2((st - shift).astype(BF16))
            acc = alpha * acc_scr[g, h] + jnp.dot(
                jnp.concatenate([vt_fn(g, h, j), ones_rows], axis=0), p, preferred_element_type=F32)
            if diagonal:
                accs.append(acc)
                if h == 1:
                    finalize_fn(g, accs[-2:])
            else:
                acc_scr[g, h] = acc
            out.append(m_next)
        return tuple(out)

    def step(j, buf, carry):
        nxt = produce(j + 1, 1 - buf)
        return consume(j, buf, carry[0], carry[1], False), nxt

    def pair(i, carry):
        return step(2 * i + 1, 1, step(2 * i, 0, carry))

    def quad(i, carry):
        return pair(2 * i + 1, pair(2 * i, carry))

    def octet(i, carry):
        return quad(2 * i + 1, quad(2 * i, carry))

    acc_scr[...] = jnp.zeros(acc_scr.shape, F32)
    init = tuple(jnp.full((1, tq), NEG, F32) for _ in streams)
    n_octets = n_past // 8
    carry = lax.fori_loop(0, n_octets, octet, (init, produce(0, 0)))
    carry = lax.fori_loop(2 * n_octets, n_past // 4, quad, carry)
    carry = lax.fori_loop(2 * (n_past // 4), n_past // 2, pair, carry)

    def odd_tail(carry):
        stats, col_max = step(n_past - 1, 0, carry)
        return consume(n_past, 1, stats, col_max, True)

    def even_tail(carry):
        return consume(n_past, 0, carry[0], carry[1], True)

    return lax.cond(n_past % 2 == 1, odd_tail, even_tail, carry)


def _diff_attn_kernel(lam_ref, g_ref, q_ref, k_ref, vt_ref, o_ref, s_scr, acc_scr, *, lam_init):
    tq = KV_CHUNK
    n_groups = q_ref.shape[1] // LANES
    cols = [slice(g * LANES, (g + 1) * LANES) for g in range(n_groups)]
    lp = lam_ref[...]
    lam = (jnp.exp(jnp.sum(lp[0:1] * lp[1:2], axis=1, keepdims=True))
           - jnp.exp(jnp.sum(lp[2:3] * lp[3:4], axis=1, keepdims=True)) + lam_init)

    def query_tile(qi, carry):
        rows = pl.ds(pl.multiple_of(qi * tq, tq), tq)
        qs = [_split_maps(q_ref[rows, c]) for c in cols]

        def scores(g, h, j):
            start = pl.multiple_of(j * tq, tq)
            return lax.dot_general(k_ref[pl.ds(start, tq), cols[g]], qs[g][h], _NT,
                                   preferred_element_type=F32)

        def finalize(g, accs):
            o1, o2 = [a[:LANES] * (1.0 / a[LANES:LANES + 1]) for a in accs]
            od_t = o1 - lam * o2
            ms = jnp.mean(od_t * od_t, axis=0, keepdims=True)
            y = (od_t * lax.rsqrt(ms + EPS)).T * g_ref[...]
            o_ref[rows, cols[g]] = (y * (1.0 - lam_init)).astype(o_ref.dtype)

        _flash_pipeline(n_groups, qi, tq, scores, lambda g, h, j: vt_ref[j, cols[g], :], None,
                        finalize, s_scr, acc_scr)
        return carry

    lax.fori_loop(0, q_ref.shape[0] // tq, query_tile, 0)


def _diff_attention(z, vt, lam_p, subln, *, batch, seq, n_groups, lam_init):
    t = z.shape[0]
    tq = KV_CHUNK
    nq = seq // tq
    w = n_groups * LANES
    n_steps = N_SELF_GROUPS // n_groups
    kern = functools.partial(_diff_attn_kernel, lam_init=lam_init)
    return pl.pallas_call(
        kern,
        out_shape=jax.ShapeDtypeStruct((t, SELF_WIDTH), BF16),
        grid=(batch, n_steps),
        in_specs=[
            pl.BlockSpec((4, HEAD_DIM), lambda b, h: (0, 0)),
            pl.BlockSpec((1, LANES), lambda b, h: (0, 0)),
            pl.BlockSpec((seq, w), lambda b, h: (b, h)),
            pl.BlockSpec((seq, w), lambda b, h: (b, n_steps + h)),
            pl.BlockSpec((seq // KV_CHUNK, w, KV_CHUNK), lambda b, h: (b, h, 0)),
        ],
        out_specs=pl.BlockSpec((seq, w), lambda b, h: (b, h)),
        scratch_shapes=[pltpu.VMEM((2, n_groups, 2, tq, tq), F32),
                        pltpu.VMEM((n_groups, 2, LANES + SUM_ROWS, tq), F32)],
        compiler_params=_params("parallel", "parallel"),
        name="diff_attn",
    )(lam_p, subln.reshape(1, LANES), z, z, vt)


def _moba_select_bias(qs, km, qi):
    n_blk = km.shape[0]
    parts, rest = [], km
    for _ in range(3):
        part = rest.astype(BF16)
        parts.append(part)
        rest = rest - part.astype(F32)
    terms = lax.dot_general(jnp.concatenate(parts, axis=0), qs, _NT, preferred_element_type=F32)
    gate = (terms[2 * n_blk:] + terms[n_blk:2 * n_blk]) + terms[:n_blk]
    blk = lax.broadcasted_iota(jnp.int32, gate.shape, 0)
    past = blk < qi
    gm = jnp.where(past, gate, NEG)
    sel = jnp.zeros(gate.shape, F32)
    for _ in range(MOBA_TOPK):
        mx = jnp.max(gm, axis=0, keepdims=True)
        first = jnp.min(jnp.where(gm == mx, blk, n_blk), axis=0, keepdims=True)
        pick = blk == first
        sel = jnp.where(pick, 1.0, sel)
        gm = jnp.where(pick, -jnp.inf, gm)
    return jnp.where((past & (sel > 0.5)) | (blk == qi), 0.0, 2.0 * NEG)


def _moba_attn_kernel(q_ref, k_ref, vt_ref, km_ref, o_ref, s_scr, acc_scr, bias_scr):
    tq = MOBA_BLOCK
    n_groups = q_ref.shape[1] // LANES
    cols = [slice(g * LANES, (g + 1) * LANES) for g in range(n_groups)]

    def query_tile(qi, carry):
        rows = pl.ds(pl.multiple_of(qi * tq, tq), tq)
        qs = [_split_maps(q_ref[rows, c]) for c in cols]
        for g, c in enumerate(cols):
            for h in range(2):
                bias_scr[g, h] = _moba_select_bias(qs[g][h], km_ref[:, c], qi)

        def scores(g, h, j):
            start = pl.multiple_of(j * tq, tq)
            return lax.dot_general(k_ref[pl.ds(start, tq), cols[g]], qs[g][h], _NT,
                                   preferred_element_type=F32)

        def values_t(g, h, j):
            return vt_ref[j, pl.ds(g * LANES + h * HEAD_DIM, HEAD_DIM), :]

        def finalize(g, accs):
            heads = [a[:HEAD_DIM] * (1.0 / a[HEAD_DIM:HEAD_DIM + 1]) for a in accs]
            o_ref[rows, cols[g]] = jnp.concatenate(heads, axis=0).T.astype(o_ref.dtype)

        _flash_pipeline(n_groups, qi, tq, scores, values_t,
                        lambda g, h, j: bias_scr[g, h, pl.ds(j, 1), :], finalize, s_scr, acc_scr)
        return carry

    lax.fori_loop(0, q_ref.shape[0] // tq, query_tile, 0)


def _moba_attention(zq, zkv, vt, kmean, *, batch, seq, n_groups):
    t = zq.shape[0]
    tq = MOBA_BLOCK
    nq = seq // tq
    w = n_groups * LANES
    n_steps = N_SELF_GROUPS // n_groups
    return pl.pallas_call(
        _moba_attn_kernel,
        out_shape=jax.ShapeDtypeStruct((t, SELF_WIDTH), BF16),
        grid=(batch, n_steps),
        in_specs=[
            pl.BlockSpec((seq, w), lambda b, h: (b, h)),
            pl.BlockSpec((seq, w), lambda b, h: (b, h)),
            pl.BlockSpec((seq // KV_CHUNK, w, KV_CHUNK), lambda b, h: (b, h, 0)),
            pl.BlockSpec((None, nq, w), lambda b, h: (b, 0, h)),
        ],
        out_specs=pl.BlockSpec((seq, w), lambda b, h: (b, h)),
        scratch_shapes=[pltpu.VMEM((2, n_groups, 2, tq, tq), F32),
                        pltpu.VMEM((n_groups, 2, HEAD_DIM + SUM_ROWS, tq), F32),
                        pltpu.VMEM((n_groups, 2, nq, tq), F32)],
        compiler_params=_params("parallel", "parallel"),
        name="moba_attn",
    )(zq, zkv, vt, kmean)


def _attn_out_kernel(x_ref, os_ref, qm_ref, mk_ref, mvt_ref, w_ref, o_ref):
    mem_len = mk_ref.shape[0]
    mk = mk_ref[...]
    lane = lax.broadcasted_iota(jnp.int32, mk.shape, 1)
    zero = jnp.zeros_like(mk)
    mk_heads = jnp.concatenate(
        [jnp.where((lane >= h * HEAD_DIM) & (lane < (h + 1) * HEAD_DIM), mk, zero)
         for h in range(N_MEM_HEADS)], axis=0)
    s = lax.dot_general(mk_heads, qm_ref[...], _NT, preferred_element_type=F32)
    ones_rows = jnp.ones((SUM_ROWS, mem_len), BF16)
    heads = []
    for h in range(N_MEM_HEADS):
        sh = s[h * mem_len:(h + 1) * mem_len]
        p = jnp.exp2((sh - jnp.max(sh, axis=0, keepdims=True)).astype(BF16))
        vt_ones = jnp.concatenate([mvt_ref[h * HEAD_DIM:(h + 1) * HEAD_DIM, :], ones_rows], axis=0)
        oh = jnp.dot(vt_ones, p, preferred_element_type=F32)
        heads.append(oh[:HEAD_DIM] * (1.0 / oh[HEAD_DIM:HEAD_DIM + 1]))
    o_mem = jnp.concatenate(heads, axis=0).T.astype(BF16)
    y = jnp.dot(os_ref[...], w_ref[:SELF_WIDTH, :], preferred_element_type=F32)
    y = y + jnp.dot(o_mem, w_ref[SELF_WIDTH:, :], preferred_element_type=F32)
    o_ref[...] = x_ref[...] + y


def _attn_out(x2d, o_self, zq, qm_block, memkv, memvt, w_out, *, seq, mem_len, tm, name):
    t, d = x2d.shape
    per_batch = seq // tm
    return pl.pallas_call(
        _attn_out_kernel,
        out_shape=jax.ShapeDtypeStruct((t, d), F32),
        grid=(t // tm,),
        in_specs=[
            pl.BlockSpec((tm, d), lambda i: (i, 0)),
            pl.BlockSpec((tm, SELF_WIDTH), lambda i: (i, 0)),
            pl.BlockSpec((tm, MEM_WIDTH), lambda i: (i, qm_block)),
            pl.BlockSpec((mem_len, MEM_WIDTH), lambda i: (i // per_batch, 0)),
            pl.BlockSpec((None, MEM_WIDTH, mem_len), lambda i: (i // per_batch, 0, 0)),
            pl.BlockSpec((d, d), lambda i: (0, 0)),
        ],
        out_specs=pl.BlockSpec((tm, d), lambda i: (i, 0)),
        compiler_params=_params("parallel"),
        name=name,
    )(x2d, o_self, zq, memkv, memvt, w_out)


def _mlp_kernel(x_ref, g_ref, wu_ref, wd_ref, gf_ref, o_ref, u_scr, *, final_norm, tf):
    x = x_ref[...]
    ms = jnp.mean(x * x, axis=-1, keepdims=True)
    h = (x * lax.rsqrt(ms + EPS) * g_ref[...]).astype(BF16)
    for c0 in range(0, wu_ref.shape[1], tf):
        u = jnp.maximum(jnp.dot(h, wu_ref[:, c0:c0 + tf], preferred_element_type=F32), 0.0)
        u_scr[:, c0:c0 + tf] = (u * u).astype(BF16)
    y = x + jnp.dot(u_scr[...], wd_ref[...], preferred_element_type=F32)
    if final_norm:
        ms = jnp.mean(y * y, axis=-1, keepdims=True)
        y = y * lax.rsqrt(ms + EPS) * gf_ref[...]
    o_ref[...] = y


def _mlp(x2d, g, w_up, w_down, g_final, *, final_norm, tm, tf, name):
    t, d = x2d.shape
    dff = w_up.shape[1]
    kern = functools.partial(_mlp_kernel, final_norm=final_norm, tf=tf)
    resident = dict(pipeline_mode=pl.Buffered(1))
    return pl.pallas_call(
        kern,
        out_shape=jax.ShapeDtypeStruct((t, d), F32),
        grid=(t // tm,),
        in_specs=[
            pl.BlockSpec((tm, d), lambda i: (i, 0)),
            pl.BlockSpec((1, d), lambda i: (0, 0)),
            pl.BlockSpec((d, dff), lambda i: (0, 0), **resident),
            pl.BlockSpec((dff, d), lambda i: (0, 0), **resident),
            pl.BlockSpec((1, d), lambda i: (0, 0)),
        ],
        out_specs=pl.BlockSpec((tm, d), lambda i: (i, 0)),
        scratch_shapes=[pltpu.VMEM((tm, dff), BF16)],
        compiler_params=_params("parallel"),
        name=name,
    )(x2d, g.reshape(1, d), w_up, w_down, g_final.reshape(1, d))


def _rope_tables(seq):
    half = HEAD_DIM // 2
    inv = 1.0 / (ROPE_THETA ** (jnp.arange(half, dtype=F32) / half))
    ang = jnp.arange(seq, dtype=F32)[:, None] * inv[None, :]
    cos, sin = jnp.cos(ang), jnp.sin(ang)
    reps = LANES // half
    cos_t = jnp.tile(cos, (1, reps))
    sign = jnp.tile(jnp.concatenate([-jnp.ones((half,), F32), jnp.ones((half,), F32)]), LANES // HEAD_DIM)
    sin_t = jnp.tile(sin, (1, reps)) * sign[None, :]
    return cos_t, sin_t


def kernel(x, mem, a_norm_attn, a_w_in, a_lambda, a_subln, a_mem_norm, a_w_mem_kv, a_w_out, a_norm_mlp, a_w_up, a_w_down, kv_norm, w_kv, b_norm_attn, b_w_in, b_mem_norm, b_w_mem_kv, b_w_out, b_norm_mlp, b_w_up, b_w_down, final_norm):
    batch, seq, d = x.shape
    mem_len = mem.shape[1]
    t = batch * seq
    x2d = x.reshape(t, d)
    mem2d = mem.reshape(batch * mem_len, d)
    cos_t, sin_t = _rope_tables(seq)
    bf = lambda w: w.astype(BF16)
    tm = 512
    ng = N_SELF_GROUPS
    q_scale = QK_SCALE * LOG2E
    mem_groups = MEM_WIDTH // LANES

    def memkv(g, w, name):
        return _proj(mem2d, g, bf(w), cos_t, sin_t, n_rope=0, scales={}, with_kmean=False,
                     vt_start=mem_groups, vt_width=MEM_WIDTH, seq=mem_len, tm=mem_len, name=name)

    lam_init = 0.8 - 0.6 * math.exp(-0.3 * 0)
    za, vt_a = _proj(x2d, a_norm_attn[0], bf(a_w_in[0]), cos_t, sin_t, n_rope=2 * ng,
                     scales={j: q_scale for j in (*range(ng), 3 * ng, 3 * ng + 1)},
                     with_kmean=False, vt_start=2 * ng, seq=seq, tm=2 * tm, name="a_proj")
    mkv_a, mvt_a = memkv(a_mem_norm[0], a_w_mem_kv[0], "a_memkv")
    o_self = _diff_attention(za, vt_a, a_lambda[0], a_subln[0], batch=batch, seq=seq,
                             n_groups=2, lam_init=lam_init)
    x2d = _attn_out(x2d, o_self, za, 3 * SELF_WIDTH // MEM_WIDTH, mkv_a, mvt_a, bf(a_w_out[0]),
                    seq=seq, mem_len=mem_len, tm=2 * tm, name="a_attn_out")
    x2d = _mlp(x2d, a_norm_mlp[0], bf(a_w_up[0]), bf(a_w_down[0]), final_norm,
               final_norm=False, tm=512, tf=512, name="a_mlp")

    zkv, kmean, vt_b = _proj(x2d, kv_norm, bf(w_kv), cos_t, sin_t, n_rope=ng, scales={},
                             with_kmean=True, vt_start=ng, seq=seq, tm=2 * tm, name="b_kvproj")
    kmean = kmean.reshape(batch, seq // MOBA_BLOCK, SELF_WIDTH)
    zb, = _proj(x2d, b_norm_attn[0], bf(b_w_in[0]), cos_t, sin_t, n_rope=ng,
                scales={j: q_scale for j in range(ng + mem_groups)}, with_kmean=False, vt_start=None,
                seq=seq, tm=2 * tm, name="b_qproj")
    mkv_b, mvt_b = memkv(b_mem_norm[0], b_w_mem_kv[0], "b_memkv")
    o_self = _moba_attention(zb, zkv, vt_b, kmean, batch=batch, seq=seq, n_groups=2)
    x2d = _attn_out(x2d, o_self, zb, SELF_WIDTH // MEM_WIDTH, mkv_b, mvt_b, bf(b_w_out[0]),
                    seq=seq, mem_len=mem_len, tm=2 * tm, name="b_attn_out")
    x2d = _mlp(x2d, b_norm_mlp[0], bf(b_w_up[0]), bf(b_w_down[0]), final_norm,
               final_norm=True, tm=512, tf=512, name="b_mlp")
    return x2d.reshape(batch, seq, d)
```

```python
import functools
import math

import jax
import jax.numpy as jnp
from jax import lax
from jax.experimental import pallas as pl
from jax.experimental.pallas import tpu as pltpu

D_MODEL = 1024
HEAD_DIM = 64
SELF_WIDTH = 768
MEM_WIDTH = 256
N_MEM_HEADS = 4
D_FF = 4096
MOBA_BLOCK = 256
MOBA_TOPK = 3
ROPE_THETA = 10000.0
EPS = 1e-6
NEG = -1e30

LANES = 128
N_SELF_GROUPS = SELF_WIDTH // LANES
QK_SCALE = HEAD_DIM ** -0.5
LOG2E = math.log2(math.e)
KV_CHUNK = 256
SUM_ROWS = 16

F32 = jnp.float32
BF16 = jnp.bfloat16

_VMEM_LIMIT = 56 * 1024 * 1024


def _params(*sem):
    return pltpu.CompilerParams(dimension_semantics=sem, vmem_limit_bytes=_VMEM_LIMIT)


def _proj_kernel(x_ref, g_ref, w_ref, cos_ref, sin_ref, o_ref, *extra_refs,
                 n_rope, scales, col_chunk, with_kmean, vt_start):
    tm = x_ref.shape[0]
    n_out = w_ref.shape[1]
    extra = list(extra_refs)
    km_ref = extra.pop(0) if with_kmean else None
    vt_ref = extra.pop(0) if vt_start is not None else None
    x = x_ref[...]
    ms = jnp.mean(x * x, axis=-1, keepdims=True)
    h = (x * lax.rsqrt(ms + EPS) * g_ref[...]).astype(BF16)
    lane = lax.broadcasted_iota(jnp.int32, (tm, LANES), 1)
    first_half = (lane & (HEAD_DIM - 1)) < HEAD_DIM // 2
    if n_rope:
        cos = cos_ref[...]
        sin = sin_ref[...]
    for c0 in range(0, n_out, col_chunk):
        z = jnp.dot(h, w_ref[:, c0:c0 + col_chunk], preferred_element_type=F32)
        for jj in range(col_chunk // LANES):
            j = c0 // LANES + jj
            blk = z[:, jj * LANES:(jj + 1) * LANES]
            if j < n_rope:
                swap = jnp.where(first_half, pltpu.roll(blk, LANES - 32, 1), pltpu.roll(blk, 32, 1))
                blk = blk * cos + swap * sin
                if km_ref is not None:
                    km_ref[0, :, j * LANES:(j + 1) * LANES] = jnp.mean(
                        blk.reshape(tm // MOBA_BLOCK, MOBA_BLOCK, LANES), axis=1)
            if j in scales:
                blk = blk * scales[j]
            o_ref[:, j * LANES:(j + 1) * LANES] = blk.astype(o_ref.dtype)
            if vt_ref is not None and vt_start <= j < vt_start + vt_ref.shape[1] // LANES:
                g = j - vt_start
                for c in range(tm // KV_CHUNK):
                    vt_ref[c, g * LANES:(g + 1) * LANES, :] = (
                        blk[c * KV_CHUNK:(c + 1) * KV_CHUNK, :].T.astype(BF16))


def _proj(x2d, g, w, cos_t, sin_t, *, n_rope, scales, with_kmean, vt_start, seq, tm, name,
          vt_width=SELF_WIDTH):
    t, d = x2d.shape
    n_out = w.shape[1]
    n_pos_blocks = seq // tm
    col_chunk = 512 if n_out % 512 == 0 else 256
    out_shape = [jax.ShapeDtypeStruct((t, n_out), BF16)]
    out_specs = [pl.BlockSpec((tm, n_out), lambda i: (i, 0))]
    if with_kmean:
        out_shape.append(jax.ShapeDtypeStruct((t // tm, tm // MOBA_BLOCK, n_rope * LANES), F32))
        out_specs.append(pl.BlockSpec((1, tm // MOBA_BLOCK, n_rope * LANES), lambda i: (i, 0, 0)))
    if vt_start is not None:
        out_shape.append(jax.ShapeDtypeStruct((t // KV_CHUNK, vt_width, KV_CHUNK), BF16))
        out_specs.append(pl.BlockSpec((tm // KV_CHUNK, vt_width, KV_CHUNK), lambda i: (i, 0, 0)))
    kern = functools.partial(_proj_kernel, n_rope=n_rope, scales=dict(scales), col_chunk=col_chunk,
                             with_kmean=with_kmean, vt_start=vt_start)
    return pl.pallas_call(
        kern,
        out_shape=out_shape,
        grid=(t // tm,),
        in_specs=[
            pl.BlockSpec((tm, d), lambda i: (i, 0)),
            pl.BlockSpec((1, d), lambda i: (0, 0)),
            pl.BlockSpec((d, n_out), lambda i: (0, 0)),
            pl.BlockSpec((tm, LANES), lambda i: (i % n_pos_blocks, 0)),
            pl.BlockSpec((tm, LANES), lambda i: (i % n_pos_blocks, 0)),
        ],
        out_specs=out_specs,
        compiler_params=_params("parallel"),
        name=name,
    )(x2d, g.reshape(1, d), w, cos_t, sin_t)


_NT = (((1,), (1,)), ((), ()))


def _split_maps(q):
    lane = lax.broadcasted_iota(jnp.int32, q.shape, 1)
    zero = jnp.zeros_like(q)
    return [jnp.where(lane < HEAD_DIM, q, zero), jnp.where(lane >= HEAD_DIM, q, zero)]


def _causal_mask(st):
    key = lax.broadcasted_iota(jnp.int32, st.shape, 0)
    qry = lax.broadcasted_iota(jnp.int32, st.shape, 1)
    return jnp.where(key <= qry, st, NEG)


def _flash_pipeline(n_groups, n_past, tq, scores_fn, vt_fn, bias_fn, finalize_fn, s_scr, acc_scr):
    streams = [(g, h) for g in range(n_groups) for h in range(2)]

    def produce(j, buf, only=None):
        col_max = []
        for i, (g, h) in enumerate(streams):
            if only is not None and i != only:
                continue
            st = scores_fn(g, h, j)
            s_scr[buf, g, h] = st
            col_max.append(jnp.max(st, axis=0, keepdims=True))
        return tuple(col_max)

    ones_rows = jnp.ones((SUM_ROWS, tq), BF16)

    def consume(j, buf, stats, col_max, diagonal, only=None):
        out, accs = [], []
        for i, (g, h) in enumerate(streams):
            if only is not None and i != only:
                continue
            st = s_scr[buf, g, h]
            if diagonal:
                st = _causal_mask(st)
                m_cur = jnp.max(st, axis=0, keepdims=True)
            else:
                m_cur = col_max[i]
            if bias_fn is not None:
                bias = bias_fn(g, h, j)
                m_cur = m_cur + bias
            m_next = jnp.maximum(stats[i], m_cur)
            alpha = jnp.exp2(stats[i] - m_next)
            shift = m_next if bias_fn is None else m_next - bias
            p = jnp.exp2((st - shift).astype(BF16))
            acc = alpha * acc_scr[g, h] + jnp.dot(
                jnp.concatenate([vt_fn(g, h, j), ones_rows], axis=0), p, preferred_element_type=F32)
            if diagonal:
                accs.append(acc)
                if h == 1:
                    finalize_fn(g, accs[-2:])
            else:
                acc_scr[g, h] = acc
            out.append(m_next)
        return tuple(out)

    def step(j, buf, carry):
        stats, nxt = [], []
        for i in range(len(streams)):
            nxt += produce(j + 1, 1 - buf, only=i)
            stats += consume(j, buf, carry[0], carry[1], False, only=i)
        return tuple(stats), tuple(nxt)

    def pair(i, carry):
        return step(2 * i + 1, 1, step(2 * i, 0, carry))

    def quad(i, carry):
        return pair(2 * i + 1, pair(2 * i, carry))

    def octet(i, carry):
        return quad(2 * i + 1, quad(2 * i, carry))

    acc_scr[...] = jnp.zeros(acc_scr.shape, F32)
    init = tuple(jnp.full((1, tq), NEG, F32) for _ in streams)
    n_octets = n_past // 8
    carry = lax.fori_loop(0, n_octets, octet, (init, produce(0, 0)))
    carry = lax.fori_loop(2 * n_octets, n_past // 4, quad, carry)
    carry = lax.fori_loop(2 * (n_past // 4), n_past // 2, pair, carry)

    def odd_tail(carry):
        stats, col_max = step(n_past - 1, 0, carry)
        return consume(n_past, 1, stats, col_max, True)

    def even_tail(carry):
        return consume(n_past, 0, carry[0], carry[1], True)

    return lax.cond(n_past % 2 == 1, odd_tail, even_tail, carry)


def _diff_attn_kernel(lam_ref, g_ref, q_ref, k_ref, vt_ref, o_ref, s_scr, acc_scr, *, lam_init):
    tq = KV_CHUNK
    n_groups = q_ref.shape[1] // LANES
    cols = [slice(g * LANES, (g + 1) * LANES) for g in range(n_groups)]
    lp = lam_ref[...]
    lam = (jnp.exp(jnp.sum(lp[0:1] * lp[1:2], axis=1, keepdims=True))
           - jnp.exp(jnp.sum(lp[2:3] * lp[3:4], axis=1, keepdims=True)) + lam_init)

    def query_tile(qi, carry):
        rows = pl.ds(pl.multiple_of(qi * tq, tq), tq)
        qs = [_split_maps(q_ref[rows, c]) for c in cols]

        def scores(g, h, j):
            start = pl.multiple_of(j * tq, tq)
            return lax.dot_general(k_ref[pl.ds(start, tq), cols[g]], qs[g][h], _NT,
                                   preferred_element_type=F32)

        def finalize(g, accs):
            o1, o2 = [a[:LANES] * (1.0 / a[LANES:LANES + 1]) for a in accs]
            od_t = o1 - lam * o2
            ms = jnp.mean(od_t * od_t, axis=0, keepdims=True)
            y = (od_t * lax.rsqrt(ms + EPS)).T * g_ref[...]
            o_ref[rows, cols[g]] = (y * (1.0 - lam_init)).astype(o_ref.dtype)

        _flash_pipeline(n_groups, qi, tq, scores, lambda g, h, j: vt_ref[j, cols[g], :], None,
                        finalize, s_scr, acc_scr)
        return carry

    lax.fori_loop(0, q_ref.shape[0] // tq, query_tile, 0)


def _diff_attention(z, vt, lam_p, subln, *, batch, seq, n_groups, lam_init):
    t = z.shape[0]
    tq = KV_CHUNK
    nq = seq // tq
    w = n_groups * LANES
    n_steps = N_SELF_GROUPS // n_groups
    kern = functools.partial(_diff_attn_kernel, lam_init=lam_init)
    return pl.pallas_call(
        kern,
        out_shape=jax.ShapeDtypeStruct((t, SELF_WIDTH), BF16),
        grid=(batch, n_steps),
        in_specs=[
            pl.BlockSpec((4, HEAD_DIM), lambda b, h: (0, 0)),
            pl.BlockSpec((1, LANES), lambda b, h: (0, 0)),
            pl.BlockSpec((seq, w), lambda b, h: (b, h)),
            pl.BlockSpec((seq, w), lambda b, h: (b, n_steps + h)),
            pl.BlockSpec((seq // KV_CHUNK, w, KV_CHUNK), lambda b, h: (b, h, 0)),
        ],
        out_specs=pl.BlockSpec((seq, w), lambda b, h: (b, h)),
        scratch_shapes=[pltpu.VMEM((2, n_groups, 2, tq, tq), F32),
                        pltpu.VMEM((n_groups, 2, LANES + SUM_ROWS, tq), F32)],
        compiler_params=_params("parallel", "parallel"),
        name="diff_attn",
    )(lam_p, subln.reshape(1, LANES), z, z, vt)


def _moba_select_bias(qs, km, qi):
    n_blk = km.shape[0]
    parts, rest = [], km
    for _ in range(3):
        part = rest.astype(BF16)
        parts.append(part)
        rest = rest - part.astype(F32)
    terms = lax.dot_general(jnp.concatenate(parts, axis=0), qs, _NT, preferred_element_type=F32)
    gate = (terms[2 * n_blk:] + terms[n_blk:2 * n_blk]) + terms[:n_blk]
    blk = lax.broadcasted_iota(jnp.int32, gate.shape, 0)
    past = blk < qi
    gm = jnp.where(past, gate, NEG)
    sel = jnp.zeros(gate.shape, F32)
    for _ in range(MOBA_TOPK):
        mx = jnp.max(gm, axis=0, keepdims=True)
        first = jnp.min(jnp.where(gm == mx, blk, n_blk), axis=0, keepdims=True)
        pick = blk == first
        sel = jnp.where(pick, 1.0, sel)
        gm = jnp.where(pick, -jnp.inf, gm)
    return jnp.where((past & (sel > 0.5)) | (blk == qi), 0.0, 2.0 * NEG)


def _moba_attn_kernel(q_ref, k_ref, vt_ref, km_ref, o_ref, s_scr, acc_scr, bias_scr):
    tq = MOBA_BLOCK
    n_groups = q_ref.shape[1] // LANES
    cols = [slice(g * LANES, (g + 1) * LANES) for g in range(n_groups)]

    def query_tile(qi, carry):
        rows = pl.ds(pl.multiple_of(qi * tq, tq), tq)
        qs = [_split_maps(q_ref[rows, c]) for c in cols]
        for g, c in enumerate(cols):
            for h in range(2):
                bias_scr[g, h] = _moba_select_bias(qs[g][h], km_ref[:, c], qi)

        def scores(g, h, j):
            start = pl.multiple_of(j * tq, tq)
            return lax.dot_general(k_ref[pl.ds(start, tq), cols[g]], qs[g][h], _NT,
                                   preferred_element_type=F32)

        def values_t(g, h, j):
            return vt_ref[j, pl.ds(g * LANES + h * HEAD_DIM, HEAD_DIM), :]

        def finalize(g, accs):
            heads = [a[:HEAD_DIM] * (1.0 / a[HEAD_DIM:HEAD_DIM + 1]) for a in accs]
            o_ref[rows, cols[g]] = jnp.concatenate(heads, axis=0).T.astype(o_ref.dtype)

        _flash_pipeline(n_groups, qi, tq, scores, values_t,
                        lambda g, h, j: bias_scr[g, h, pl.ds(j, 1), :], finalize, s_scr, acc_scr)
        return carry

    lax.fori_loop(0, q_ref.shape[0] // tq, query_tile, 0)


def _moba_attention(zq, zkv, vt, kmean, *, batch, seq, n_groups):
    t = zq.shape[0]
    tq = MOBA_BLOCK
    nq = seq // tq
    w = n_groups * LANES
    n_steps = N_SELF_GROUPS // n_groups
    return pl.pallas_call(
        _moba_attn_kernel,
        out_shape=jax.ShapeDtypeStruct((t, SELF_WIDTH), BF16),
        grid=(batch, n_steps),
        in_specs=[
            pl.BlockSpec((seq, w), lambda b, h: (b, h)),
            pl.BlockSpec((seq, w), lambda b, h: (b, h)),
            pl.BlockSpec((seq // KV_CHUNK, w, KV_CHUNK), lambda b, h: (b, h, 0)),
            pl.BlockSpec((None, nq, w), lambda b, h: (b, 0, h)),
        ],
        out_specs=pl.BlockSpec((seq, w), lambda b, h: (b, h)),
        scratch_shapes=[pltpu.VMEM((2, n_groups, 2, tq, tq), F32),
                        pltpu.VMEM((n_groups, 2, HEAD_DIM + SUM_ROWS, tq), F32),
                        pltpu.VMEM((n_groups, 2, nq, tq), F32)],
        compiler_params=_params("parallel", "parallel"),
        name="moba_attn",
    )(zq, zkv, vt, kmean)


def _attn_out_kernel(x_ref, os_ref, qm_ref, mk_ref, mvt_ref, w_ref, o_ref):
    mem_len = mk_ref.shape[0]
    mk = mk_ref[...]
    lane = lax.broadcasted_iota(jnp.int32, mk.shape, 1)
    zero = jnp.zeros_like(mk)
    mk_heads = jnp.concatenate(
        [jnp.where((lane >= h * HEAD_DIM) & (lane < (h + 1) * HEAD_DIM), mk, zero)
         for h in range(N_MEM_HEADS)], axis=0)
    s = lax.dot_general(mk_heads, qm_ref[...], _NT, preferred_element_type=F32)
    ones_rows = jnp.ones((SUM_ROWS, mem_len), BF16)
    heads = []
    for h in range(N_MEM_HEADS):
        sh = s[h * mem_len:(h + 1) * mem_len]
        p = jnp.exp2((sh - jnp.max(sh, axis=0, keepdims=True)).astype(BF16))
        vt_ones = jnp.concatenate([mvt_ref[h * HEAD_DIM:(h + 1) * HEAD_DIM, :], ones_rows], axis=0)
        oh = jnp.dot(vt_ones, p, preferred_element_type=F32)
        heads.append(oh[:HEAD_DIM] * (1.0 / oh[HEAD_DIM:HEAD_DIM + 1]))
    o_mem = jnp.concatenate(heads, axis=0).T.astype(BF16)
    y = jnp.dot(os_ref[...], w_ref[:SELF_WIDTH, :], preferred_element_type=F32)
    y = y + jnp.dot(o_mem, w_ref[SELF_WIDTH:, :], preferred_element_type=F32)
    o_ref[...] = x_ref[...] + y


def _attn_out(x2d, o_self, zq, qm_block, memkv, memvt, w_out, *, seq, mem_len, tm, name):
    t, d = x2d.shape
    per_batch = seq // tm
    return pl.pallas_call(
        _attn_out_kernel,
        out_shape=jax.ShapeDtypeStruct((t, d), F32),
        grid=(t // tm,),
        in_specs=[
            pl.BlockSpec((tm, d), lambda i: (i, 0)),
            pl.BlockSpec((tm, SELF_WIDTH), lambda i: (i, 0)),
            pl.BlockSpec((tm, MEM_WIDTH), lambda i: (i, qm_block)),
            pl.BlockSpec((mem_len, MEM_WIDTH), lambda i: (i // per_batch, 0)),
            pl.BlockSpec((None, MEM_WIDTH, mem_len), lambda i: (i // per_batch, 0, 0)),
            pl.BlockSpec((d, d), lambda i: (0, 0)),
        ],
        out_specs=pl.BlockSpec((tm, d), lambda i: (i, 0)),
        compiler_params=_params("parallel"),
        name=name,
    )(x2d, o_self, zq, memkv, memvt, w_out)


def _mlp_kernel(x_ref, g_ref, wu_ref, wd_ref, gf_ref, o_ref, u_scr, *, final_norm, tf):
    x = x_ref[...]
    ms = jnp.mean(x * x, axis=-1, keepdims=True)
    h = (x * lax.rsqrt(ms + EPS) * g_ref[...]).astype(BF16)
    for c0 in range(0, wu_ref.shape[1], tf):
        u = jnp.maximum(jnp.dot(h, wu_ref[:, c0:c0 + tf], preferred_element_type=F32), 0.0)
        u_scr[:, c0:c0 + tf] = (u * u).astype(BF16)
    y = x + jnp.dot(u_scr[...], wd_ref[...], preferred_element_type=F32)
    if final_norm:
        ms = jnp.mean(y * y, axis=-1, keepdims=True)
        y = y * lax.rsqrt(ms + EPS) * gf_ref[...]
    o_ref[...] = y


def _mlp(x2d, g, w_up, w_down, g_final, *, final_norm, tm, tf, name):
    t, d = x2d.shape
    dff = w_up.shape[1]
    kern = functools.partial(_mlp_kernel, final_norm=final_norm, tf=tf)
    resident = dict(pipeline_mode=pl.Buffered(1))
    return pl.pallas_call(
        kern,
        out_shape=jax.ShapeDtypeStruct((t, d), F32),
        grid=(t // tm,),
        in_specs=[
            pl.BlockSpec((tm, d), lambda i: (i, 0)),
            pl.BlockSpec((1, d), lambda i: (0, 0)),
            pl.BlockSpec((d, dff), lambda i: (0, 0), **resident),
            pl.BlockSpec((dff, d), lambda i: (0, 0), **resident),
            pl.BlockSpec((1, d), lambda i: (0, 0)),
        ],
        out_specs=pl.BlockSpec((tm, d), lambda i: (i, 0)),
        scratch_shapes=[pltpu.VMEM((tm, dff), BF16)],
        compiler_params=_params("parallel"),
        name=name,
    )(x2d, g.reshape(1, d), w_up, w_down, g_final.reshape(1, d))


def _rope_tables(seq):
    half = HEAD_DIM // 2
    inv = 1.0 / (ROPE_THETA ** (jnp.arange(half, dtype=F32) / half))
    ang = jnp.arange(seq, dtype=F32)[:, None] * inv[None, :]
    cos, sin = jnp.cos(ang), jnp.sin(ang)
    reps = LANES // half
    cos_t = jnp.tile(cos, (1, reps))
    sign = jnp.tile(jnp.concatenate([-jnp.ones((half,), F32), jnp.ones((half,), F32)]), LANES // HEAD_DIM)
    sin_t = jnp.tile(sin, (1, reps)) * sign[None, :]
    return cos_t, sin_t


def kernel(x, mem, a_norm_attn, a_w_in, a_lambda, a_subln, a_mem_norm, a_w_mem_kv, a_w_out, a_norm_mlp, a_w_up, a_w_down, kv_norm, w_kv, b_norm_attn, b_w_in, b_mem_norm, b_w_mem_kv, b_w_out, b_norm_mlp, b_w_up, b_w_down, final_norm):
    batch, seq, d = x.shape
    mem_len = mem.shape[1]
    t = batch * seq
    x2d = x.reshape(t, d)
    mem2d = mem.reshape(batch * mem_len, d)
    cos_t, sin_t = _rope_tables(seq)
    bf = lambda w: w.astype(BF16)
    tm = 512
    ng = N_SELF_GROUPS
    q_scale = QK_SCALE * LOG2E
    mem_groups = MEM_WIDTH // LANES

    def memkv(g, w, name):
        return _proj(mem2d, g, bf(w), cos_t, sin_t, n_rope=0, scales={}, with_kmean=False,
                     vt_start=mem_groups, vt_width=MEM_WIDTH, seq=mem_len, tm=mem_len, name=name)

    lam_init = 0.8 - 0.6 * math.exp(-0.3 * 0)
    za, vt_a = _proj(x2d, a_norm_attn[0], bf(a_w_in[0]), cos_t, sin_t, n_rope=2 * ng,
                     scales={j: q_scale for j in (*range(ng), 3 * ng, 3 * ng + 1)},
                     with_kmean=False, vt_start=2 * ng, seq=seq, tm=2 * tm, name="a_proj")
    mkv_a, mvt_a = memkv(a_mem_norm[0], a_w_mem_kv[0], "a_memkv")
    o_self = _diff_attention(za, vt_a, a_lambda[0], a_subln[0], batch=batch, seq=seq,
                             n_groups=2, lam_init=lam_init)
    x2d = _attn_out(x2d, o_self, za, 3 * SELF_WIDTH // MEM_WIDTH, mkv_a, mvt_a, bf(a_w_out[0]),
                    seq=seq, mem_len=mem_len, tm=2 * tm, name="a_attn_out")
    x2d = _mlp(x2d, a_norm_mlp[0], bf(a_w_up[0]), bf(a_w_down[0]), final_norm,
               final_norm=False, tm=512, tf=512, name="a_mlp")

    zkv, kmean, vt_b = _proj(x2d, kv_norm, bf(w_kv), cos_t, sin_t, n_rope=ng, scales={},
                             with_kmean=True, vt_start=ng, seq=seq, tm=2 * tm, name="b_kvproj")
    kmean = kmean.reshape(batch, seq // MOBA_BLOCK, SELF_WIDTH)
    zb, = _proj(x2d, b_norm_attn[0], bf(b_w_in[0]), cos_t, sin_t, n_rope=ng,
                scales={j: q_scale for j in range(ng + mem_groups)}, with_kmean=False, vt_start=None,
                seq=seq, tm=2 * tm, name="b_qproj")
    mkv_b, mvt_b = memkv(b_mem_norm[0], b_w_mem_kv[0], "b_memkv")
    o_self = _moba_attention(zb, zkv, vt_b, kmean, batch=batch, seq=seq, n_groups=2)
    x2d = _attn_out(x2d, o_self, zb, SELF_WIDTH // MEM_WIDTH, mkv_b, mvt_b, bf(b_w_out[0]),
                    seq=seq, mem_len=mem_len, tm=2 * tm, name="b_attn_out")
    x2d = _mlp(x2d, b_norm_mlp[0], bf(b_w_up[0]), bf(b_w_down[0]), final_norm,
               final_norm=True, tm=512, tf=512, name="b_mlp")
    return x2d.reshape(batch, seq, d)
```

```python
import functools
import math

import jax
import jax.numpy as jnp
from jax import lax
from jax.experimental import pallas as pl
from jax.experimental.pallas import tpu as pltpu

D_MODEL = 1024
HEAD_DIM = 64
SELF_WIDTH = 768
MEM_WIDTH = 256
N_MEM_HEADS = 4
D_FF = 4096
MOBA_BLOCK = 256
MOBA_TOPK = 3
ROPE_THETA = 10000.0
EPS = 1e-6
NEG = -1e30

LANES = 128
N_SELF_GROUPS = SELF_WIDTH // LANES
QK_SCALE = HEAD_DIM ** -0.5
LOG2E = math.log2(math.e)
KV_CHUNK = 256
SUM_ROWS = 16

F32 = jnp.float32
BF16 = jnp.bfloat16

_VMEM_LIMIT = 56 * 1024 * 1024


def _params(*sem):
    return pltpu.CompilerParams(dimension_semantics=sem, vmem_limit_bytes=_VMEM_LIMIT)


def _proj_kernel(x_ref, g_ref, w_ref, cos_ref, sin_ref, o_ref, *extra_refs,
                 n_rope, scales, col_chunk, with_kmean, vt_start):
    tm = x_ref.shape[0]
    n_out = w_ref.shape[1]
    extra = list(extra_refs)
    km_ref = extra.pop(0) if with_kmean else None
    vt_ref = extra.pop(0) if vt_start is not None else None
    x = x_ref[...]
    ms = jnp.mean(x * x, axis=-1, keepdims=True)
    h = (x * lax.rsqrt(ms + EPS) * g_ref[...]).astype(BF16)
    lane = lax.broadcasted_iota(jnp.int32, (tm, LANES), 1)
    first_half = (lane & (HEAD_DIM - 1)) < HEAD_DIM // 2
    if n_rope:
        cos = cos_ref[...]
        sin = sin_ref[...]
    for c0 in range(0, n_out, col_chunk):
        z = jnp.dot(h, w_ref[:, c0:c0 + col_chunk], preferred_element_type=F32)
        for jj in range(col_chunk // LANES):
            j = c0 // LANES + jj
            blk = z[:, jj * LANES:(jj + 1) * LANES]
            if j < n_rope:
                swap = jnp.where(first_half, pltpu.roll(blk, LANES - 32, 1), pltpu.roll(blk, 32, 1))
                blk = blk * cos + swap * sin
                if km_ref is not None:
                    km_ref[0, :, j * LANES:(j + 1) * LANES] = jnp.mean(
                        blk.reshape(tm // MOBA_BLOCK, MOBA_BLOCK, LANES), axis=1)
            if j in scales:
                blk = blk * scales[j]
            o_ref[:, j * LANES:(j + 1) * LANES] = blk.astype(o_ref.dtype)
            if vt_ref is not None and vt_start <= j < vt_start + vt_ref.shape[1] // LANES:
                g = j - vt_start
                for c in range(tm // KV_CHUNK):
                    vt_ref[c, g * LANES:(g + 1) * LANES, :] = (
                        blk[c * KV_CHUNK:(c + 1) * KV_CHUNK, :].T.astype(BF16))


def _proj(x2d, g, w, cos_t, sin_t, *, n_rope, scales, with_kmean, vt_start, seq, tm, name,
          vt_width=SELF_WIDTH):
    t, d = x2d.shape
    n_out = w.shape[1]
    n_pos_blocks = seq // tm
    col_chunk = 512 if n_out % 512 == 0 else 256
    out_shape = [jax.ShapeDtypeStruct((t, n_out), BF16)]
    out_specs = [pl.BlockSpec((tm, n_out), lambda i: (i, 0))]
    if with_kmean:
        out_shape.append(jax.ShapeDtypeStruct((t // tm, tm // MOBA_BLOCK, n_rope * LANES), F32))
        out_specs.append(pl.BlockSpec((1, tm // MOBA_BLOCK, n_rope * LANES), lambda i: (i, 0, 0)))
    if vt_start is not None:
        out_shape.append(jax.ShapeDtypeStruct((t // KV_CHUNK, vt_width, KV_CHUNK), BF16))
        out_specs.append(pl.BlockSpec((tm // KV_CHUNK, vt_width, KV_CHUNK), lambda i: (i, 0, 0)))
    kern = functools.partial(_proj_kernel, n_rope=n_rope, scales=dict(scales), col_chunk=col_chunk,
                             with_kmean=with_kmean, vt_start=vt_start)
    return pl.pallas_call(
        kern,
        out_shape=out_shape,
        grid=(t // tm,),
        in_specs=[
            pl.BlockSpec((tm, d), lambda i: (i, 0)),
            pl.BlockSpec((1, d), lambda i: (0, 0)),
            pl.BlockSpec((d, n_out), lambda i: (0, 0)),
            pl.BlockSpec((tm, LANES), lambda i: (i % n_pos_blocks, 0)),
            pl.BlockSpec((tm, LANES), lambda i: (i % n_pos_blocks, 0)),
        ],
        out_specs=out_specs,
        compiler_params=_params("parallel"),
        name=name,
    )(x2d, g.reshape(1, d), w, cos_t, sin_t)


_NT = (((1,), (1,)), ((), ()))


def _split_maps(q):
    lane = lax.broadcasted_iota(jnp.int32, q.shape, 1)
    zero = jnp.zeros_like(q)
    return [jnp.where(lane < HEAD_DIM, q, zero), jnp.where(lane >= HEAD_DIM, q, zero)]


def _causal_mask(st):
    key = lax.broadcasted_iota(jnp.int32, st.shape, 0)
    qry = lax.broadcasted_iota(jnp.int32, st.shape, 1)
    return jnp.where(key <= qry, st, NEG)


def _flash_pipeline(n_groups, n_past, tq, scores_fn, vt_fn, bias_fn, finalize_fn, s_scr, acc_scr):
    streams = [(g, h) for g in range(n_groups) for h in range(2)]

    def produce(j, buf, only=None):
        col_max = []
        for i, (g, h) in enumerate(streams):
            if only is not None and i != only:
                continue
            st = scores_fn(g, h, j)
            s_scr[buf, g, h] = st
            col_max.append(jnp.max(st, axis=0, keepdims=True))
        return tuple(col_max)

    ones_rows = jnp.ones((SUM_ROWS, tq), BF16)

    def consume(j, buf, stats, col_max, diagonal, only=None):
        out, accs = [], []
        for i, (g, h) in enumerate(streams):
            if only is not None and i != only:
                continue
            st = s_scr[buf, g, h]
            if diagonal:
                st = _causal_mask(st)
                m_cur = jnp.max(st, axis=0, keepdims=True)
            else:
                m_cur = col_max[i]
            if bias_fn is not None:
                bias = bias_fn(g, h, j)
                m_cur = m_cur + bias
            m_next = jnp.maximum(stats[i], m_cur)
            alpha = jnp.exp2(stats[i] - m_next)
            shift = m_next if bias_fn is None else m_next - bias
            p = jnp.exp2((st - shift).astype(BF16))
            acc = alpha * acc_scr[g, h] + jnp.dot(
                jnp.concatenate([vt_fn(g, h, j), ones_rows], axis=0), p, preferred_element_type=F32)
            if diagonal:
                accs.append(acc)
                if h == 1:
                    finalize_fn(g, accs[-2:])
            else:
                acc_scr[g, h] = acc
            out.append(m_next)
        return tuple(out)

    def step(j, buf, carry):
        stats, nxt = [], []
        for i in range(len(streams)):
            nxt += produce(j + 1, 1 - buf, only=i)
            stats += consume(j, buf, carry[0], carry[1], False, only=i)
        return tuple(stats), tuple(nxt)

    def pair(i, carry):
        return step(2 * i + 1, 1, step(2 * i, 0, carry))

    def quad(i, carry):
        return pair(2 * i + 1, pair(2 * i, carry))

    def octet(i, carry):
        return quad(2 * i + 1, quad(2 * i, carry))

    acc_scr[...] = jnp.zeros(acc_scr.shape, F32)
    init = tuple(jnp.full((1, tq), NEG, F32) for _ in streams)
    n_octets = n_past // 8
    carry = lax.fori_loop(0, n_octets, octet, (init, produce(0, 0)))
    carry = lax.fori_loop(2 * n_octets, n_past // 4, quad, carry)
    carry = lax.fori_loop(2 * (n_past // 4), n_past // 2, pair, carry)

    def odd_tail(carry):
        stats, col_max = step(n_past - 1, 0, carry)
        return consume(n_past, 1, stats, col_max, True)

    def even_tail(carry):
        return consume(n_past, 0, carry[0], carry[1], True)

    return lax.cond(n_past % 2 == 1, odd_tail, even_tail, carry)


def _diff_attn_kernel(lam_ref, g_ref, q_ref, k_ref, vt_ref, o_ref, s_scr, acc_scr, *, lam_init):
    tq = KV_CHUNK
    n_groups = q_ref.shape[1] // LANES
    cols = [slice(g * LANES, (g + 1) * LANES) for g in range(n_groups)]
    lp = lam_ref[...]
    lam = (jnp.exp(jnp.sum(lp[0:1] * lp[1:2], axis=1, keepdims=True))
           - jnp.exp(jnp.sum(lp[2:3] * lp[3:4], axis=1, keepdims=True)) + lam_init)

    def query_tile(qi, carry):
        rows = pl.ds(pl.multiple_of(qi * tq, tq), tq)
        qs = [_split_maps(q_ref[rows, c]) for c in cols]

        def scores(g, h, j):
            start = pl.multiple_of(j * tq, tq)
            return lax.dot_general(k_ref[pl.ds(start, tq), cols[g]], qs[g][h], _NT,
                                   preferred_element_type=F32)

        def finalize(g, accs):
            o1, o2 = [a[:LANES] * (1.0 / a[LANES:LANES + 1]) for a in accs]
            od_t = o1 - lam * o2
            ms = jnp.mean(od_t * od_t, axis=0, keepdims=True)
            y = (od_t * lax.rsqrt(ms + EPS)).T * g_ref[...]
            o_ref[rows, cols[g]] = (y * (1.0 - lam_init)).astype(o_ref.dtype)

        _flash_pipeline(n_groups, qi, tq, scores, lambda g, h, j: vt_ref[j, cols[g], :], None,
                        finalize, s_scr, acc_scr)
        return carry

    lax.fori_loop(0, q_ref.shape[0] // tq, query_tile, 0)


def _diff_attention(z, vt, lam_p, subln, *, batch, seq, n_groups, lam_init):
    t = z.shape[0]
    tq = KV_CHUNK
    nq = seq // tq
    w = n_groups * LANES
    n_steps = N_SELF_GROUPS // n_groups
    kern = functools.partial(_diff_attn_kernel, lam_init=lam_init)
    return pl.pallas_call(
        kern,
        out_shape=jax.ShapeDtypeStruct((t, SELF_WIDTH), BF16),
        grid=(batch, n_steps),
        in_specs=[
            pl.BlockSpec((4, HEAD_DIM), lambda b, h: (0, 0)),
            pl.BlockSpec((1, LANES), lambda b, h: (0, 0)),
            pl.BlockSpec((seq, w), lambda b, h: (b, h)),
            pl.BlockSpec((seq, w), lambda b, h: (b, n_steps + h)),
            pl.BlockSpec((seq // KV_CHUNK, w, KV_CHUNK), lambda b, h: (b, h, 0)),
        ],
        out_specs=pl.BlockSpec((seq, w), lambda b, h: (b, h)),
        scratch_shapes=[pltpu.VMEM((2, n_groups, 2, tq, tq), F32),
                        pltpu.VMEM((n_groups, 2, LANES + SUM_ROWS, tq), F32)],
        compiler_params=_params("parallel", "parallel"),
        name="diff_attn",
    )(lam_p, subln.reshape(1, LANES), z, z, vt)


def _moba_select_bias(qs, km, qi):
    n_blk = km.shape[0]
    parts, rest = [], km
    for _ in range(3):
        part = rest.astype(BF16)
        parts.append(part)
        rest = rest - part.astype(F32)
    terms = lax.dot_general(jnp.concatenate(parts, axis=0), qs, _NT, preferred_element_type=F32)
    gate = (terms[2 * n_blk:] + terms[n_blk:2 * n_blk]) + terms[:n_blk]
    blk = lax.broadcasted_iota(jnp.int32, gate.shape, 0)
    past = blk < qi
    gm = jnp.where(past, gate, NEG)
    sel = jnp.zeros(gate.shape, F32)
    for _ in range(MOBA_TOPK):
        mx = jnp.max(gm, axis=0, keepdims=True)
        first = jnp.min(jnp.where(gm == mx, blk, n_blk), axis=0, keepdims=True)
        pick = blk == first
        sel = jnp.where(pick, 1.0, sel)
        gm = jnp.where(pick, -jnp.inf, gm)
    return jnp.where((past & (sel > 0.5)) | (blk == qi), 0.0, 2.0 * NEG)


def _moba_attn_kernel(q_ref, k_ref, vt_ref, km_ref, o_ref, s_scr, acc_scr, bias_scr):
    tq = MOBA_BLOCK
    n_groups = q_ref.shape[1] // LANES
    cols = [slice(g * LANES, (g + 1) * LANES) for g in range(n_groups)]

    def query_tile(qi, carry):
        rows = pl.ds(pl.multiple_of(qi * tq, tq), tq)
        qs = [_split_maps(q_ref[rows, c]) for c in cols]
        for g, c in enumerate(cols):
            for h in range(2):
                bias_scr[g, h] = _moba_select_bias(qs[g][h], km_ref[:, c], qi)

        def scores(g, h, j):
            start = pl.multiple_of(j * tq, tq)
            return lax.dot_general(k_ref[pl.ds(start, tq), cols[g]], qs[g][h], _NT,
                                   preferred_element_type=F32)

        def values_t(g, h, j):
            return vt_ref[j, pl.ds(g * LANES + h * HEAD_DIM, HEAD_DIM), :]

        def finalize(g, accs):
            heads = [a[:HEAD_DIM] * (1.0 / a[HEAD_DIM:HEAD_DIM + 1]) for a in accs]
            o_ref[rows, cols[g]] = jnp.concatenate(heads, axis=0).T.astype(o_ref.dtype)

        _flash_pipeline(n_groups, qi, tq, scores, values_t,
                        lambda g, h, j: bias_scr[g, h, pl.ds(j, 1), :], finalize, s_scr, acc_scr)
        return carry

    lax.fori_loop(0, q_ref.shape[0] // tq, query_tile, 0)


def _moba_attention(zq, zkv, vt, kmean, *, batch, seq, n_groups):
    t = zq.shape[0]
    tq = MOBA_BLOCK
    nq = seq // tq
    w = n_groups * LANES
    n_steps = N_SELF_GROUPS // n_groups
    return pl.pallas_call(
        _moba_attn_kernel,
        out_shape=jax.ShapeDtypeStruct((t, SELF_WIDTH), BF16),
        grid=(batch, n_steps),
        in_specs=[
            pl.BlockSpec((seq, w), lambda b, h: (b, h)),
            pl.BlockSpec((seq, w), lambda b, h: (b, h)),
            pl.BlockSpec((seq // KV_CHUNK, w, KV_CHUNK), lambda b, h: (b, h, 0)),
            pl.BlockSpec((None, nq, w), lambda b, h: (b, 0, h)),
        ],
        out_specs=pl.BlockSpec((seq, w), lambda b, h: (b, h)),
        scratch_shapes=[pltpu.VMEM((2, n_groups, 2, tq, tq), F32),
                        pltpu.VMEM((n_groups, 2, HEAD_DIM + SUM_ROWS, tq), F32),
                        pltpu.VMEM((n_groups, 2, nq, tq), F32)],
        compiler_params=_params("parallel", "parallel"),
        name="moba_attn",
    )(zq, zkv, vt, kmean)


def _attn_out_kernel(x_ref, os_ref, qm_ref, mk_ref, mvt_ref, w_ref, o_ref):
    mem_len = mk_ref.shape[0]
    mk = mk_ref[...]
    lane = lax.broadcasted_iota(jnp.int32, mk.shape, 1)
    zero = jnp.zeros_like(mk)
    mk_heads = jnp.concatenate(
        [jnp.where((lane >= h * HEAD_DIM) & (lane < (h + 1) * HEAD_DIM), mk, zero)
         for h in range(N_MEM_HEADS)], axis=0)
    s = lax.dot_general(mk_heads, qm_ref[...], _NT, preferred_element_type=F32)
    ones_rows = jnp.ones((SUM_ROWS, mem_len), BF16)
    heads = []
    for h in range(N_MEM_HEADS):
        sh = s[h * mem_len:(h + 1) * mem_len]
        p = jnp.exp2((sh - jnp.max(sh, axis=0, keepdims=True)).astype(BF16))
        vt_ones = jnp.concatenate([mvt_ref[h * HEAD_DIM:(h + 1) * HEAD_DIM, :], ones_rows], axis=0)
        oh = jnp.dot(vt_ones, p, preferred_element_type=F32)
        heads.append(oh[:HEAD_DIM] * (1.0 / oh[HEAD_DIM:HEAD_DIM + 1]))
    o_mem = jnp.concatenate(heads, axis=0).T.astype(BF16)
    y = jnp.dot(os_ref[...], w_ref[:SELF_WIDTH, :], preferred_element_type=F32)
    y = y + jnp.dot(o_mem, w_ref[SELF_WIDTH:, :], preferred_element_type=F32)
    o_ref[...] = x_ref[...] + y


def _attn_out(x2d, o_self, zq, qm_block, memkv, memvt, w_out, *, seq, mem_len, tm, name):
    t, d = x2d.shape
    per_batch = seq // tm
    return pl.pallas_call(
        _attn_out_kernel,
        out_shape=jax.ShapeDtypeStruct((t, d), F32),
        grid=(t // tm,),
        in_specs=[
            pl.BlockSpec((tm, d), lambda i: (i, 0)),
            pl.BlockSpec((tm, SELF_WIDTH), lambda i: (i, 0)),
            pl.BlockSpec((tm, MEM_WIDTH), lambda i: (i, qm_block)),
            pl.BlockSpec((mem_len, MEM_WIDTH), lambda i: (i // per_batch, 0)),
            pl.BlockSpec((None, MEM_WIDTH, mem_len), lambda i: (i // per_batch, 0, 0)),
            pl.BlockSpec((d, d), lambda i: (0, 0)),
        ],
        out_specs=pl.BlockSpec((tm, d), lambda i: (i, 0)),
        compiler_params=_params("parallel"),
        name=name,
    )(x2d, o_self, zq, memkv, memvt, w_out)


def _mlp_kernel(x_ref, g_ref, wu_ref, wd_ref, gf_ref, o_ref, u_scr, *, final_norm, tf):
    x = x_ref[...]
    ms = jnp.mean(x * x, axis=-1, keepdims=True)
    h = (x * lax.rsqrt(ms + EPS) * g_ref[...]).astype(BF16)
    for c0 in range(0, wu_ref.shape[1], tf):
        u = jnp.maximum(jnp.dot(h, wu_ref[:, c0:c0 + tf], preferred_element_type=F32), 0.0)
        u_scr[:, c0:c0 + tf] = (u * u).astype(BF16)
    y = x + jnp.dot(u_scr[...], wd_ref[...], preferred_element_type=F32)
    if final_norm:
        ms = jnp.mean(y * y, axis=-1, keepdims=True)
        y = y * lax.rsqrt(ms + EPS) * gf_ref[...]
    o_ref[...] = y


def _mlp(x2d, g, w_up, w_down, g_final, *, final_norm, tm, tf, name):
    t, d = x2d.shape
    dff = w_up.shape[1]
    kern = functools.partial(_mlp_kernel, final_norm=final_norm, tf=tf)
    resident = dict(pipeline_mode=pl.Buffered(1))
    return pl.pallas_call(
        kern,
        out_shape=jax.ShapeDtypeStruct((t, d), F32),
        grid=(t // tm,),
        in_specs=[
            pl.BlockSpec((tm, d), lambda i: (i, 0)),
            pl.BlockSpec((1, d), lambda i: (0, 0)),
            pl.BlockSpec((d, dff), lambda i: (0, 0), **resident),
            pl.BlockSpec((dff, d), lambda i: (0, 0), **resident),
            pl.BlockSpec((1, d), lambda i: (0, 0)),
        ],
        out_specs=pl.BlockSpec((tm, d), lambda i: (i, 0)),
        scratch_shapes=[pltpu.VMEM((tm, dff), BF16)],
        compiler_params=_params("parallel"),
        name=name,
    )(x2d, g.reshape(1, d), w_up, w_down, g_final.reshape(1, d))


def _rope_tables(seq):
    half = HEAD_DIM // 2
    inv = 1.0 / (ROPE_THETA ** (jnp.arange(half, dtype=F32) / half))
    ang = jnp.arange(seq, dtype=F32)[:, None] * inv[None, :]
    cos, sin = jnp.cos(ang), jnp.sin(ang)
    reps = LANES // half
    cos_t = jnp.tile(cos, (1, reps))
    sign = jnp.tile(jnp.concatenate([-jnp.ones((half,), F32), jnp.ones((half,), F32)]), LANES // HEAD_DIM)
    sin_t = jnp.tile(sin, (1, reps)) * sign[None, :]
    return cos_t, sin_t


def kernel(x, mem, a_norm_attn, a_w_in, a_lambda, a_subln, a_mem_norm, a_w_mem_kv, a_w_out, a_norm_mlp, a_w_up, a_w_down, kv_norm, w_kv, b_norm_attn, b_w_in, b_mem_norm, b_w_mem_kv, b_w_out, b_norm_mlp, b_w_up, b_w_down, final_norm):
    batch, seq, d = x.shape
    mem_len = mem.shape[1]
    t = batch * seq
    x2d = x.reshape(t, d)
    mem2d = mem.reshape(batch * mem_len, d)
    cos_t, sin_t = _rope_tables(seq)
    bf = lambda w: w.astype(BF16)
    tm = 512
    ng = N_SELF_GROUPS
    q_scale = QK_SCALE * LOG2E
    mem_groups = MEM_WIDTH // LANES

    def memkv(g, w, name):
        return _proj(mem2d, g, bf(w), cos_t, sin_t, n_rope=0, scales={}, with_kmean=False,
                     vt_start=mem_groups, vt_width=MEM_WIDTH, seq=mem_len, tm=mem_len, name=name)

    lam_init = 0.8 - 0.6 * math.exp(-0.3 * 0)
    za, vt_a = _proj(x2d, a_norm_attn[0], bf(a_w_in[0]), cos_t, sin_t, n_rope=2 * ng,
                     scales={j: q_scale for j in (*range(ng), 3 * ng, 3 * ng + 1)},
                     with_kmean=False, vt_start=2 * ng, seq=seq, tm=2 * tm, name="a_proj")
    mkv_a, mvt_a = memkv(a_mem_norm[0], a_w_mem_kv[0], "a_memkv")
    o_self = _diff_attention(za, vt_a, a_lambda[0], a_subln[0], batch=batch, seq=seq,
                             n_groups=3, lam_init=lam_init)
    x2d = _attn_out(x2d, o_self, za, 3 * SELF_WIDTH // MEM_WIDTH, mkv_a, mvt_a, bf(a_w_out[0]),
                    seq=seq, mem_len=mem_len, tm=2 * tm, name="a_attn_out")
    x2d = _mlp(x2d, a_norm_mlp[0], bf(a_w_up[0]), bf(a_w_down[0]), final_norm,
               final_norm=False, tm=512, tf=512, name="a_mlp")

    zkv, kmean, vt_b = _proj(x2d, kv_norm, bf(w_kv), cos_t, sin_t, n_rope=ng, scales={},
                             with_kmean=True, vt_start=ng, seq=seq, tm=2 * tm, name="b_kvproj")
    kmean = kmean.reshape(batch, seq // MOBA_BLOCK, SELF_WIDTH)
    zb, = _proj(x2d, b_norm_attn[0], bf(b_w_in[0]), cos_t, sin_t, n_rope=ng,
                scales={j: q_scale for j in range(ng + mem_groups)}, with_kmean=False, vt_start=None,
                seq=seq, tm=2 * tm, name="b_qproj")
    mkv_b, mvt_b = memkv(b_mem_norm[0], b_w_mem_kv[0], "b_memkv")
    o_self = _moba_attention(zb, zkv, vt_b, kmean, batch=batch, seq=seq, n_groups=3)
    x2d = _attn_out(x2d, o_self, zb, SELF_WIDTH // MEM_WIDTH, mkv_b, mvt_b, bf(b_w_out[0]),
                    seq=seq, mem_len=mem_len, tm=2 * tm, name="b_attn_out")
    x2d = _mlp(x2d, b_norm_mlp[0], bf(b_w_up[0]), bf(b_w_down[0]), final_norm,
               final_norm=True, tm=512, tf=512, name="b_mlp")
    return x2d.reshape(batch, seq, d)
```

```python
import functools
import math

import jax
import jax.numpy as jnp
from jax import lax
from jax.experimental import pallas as pl
from jax.experimental.pallas import tpu as pltpu

D_MODEL = 1024
HEAD_DIM = 64
SELF_WIDTH = 768
MEM_WIDTH = 256
N_MEM_HEADS = 4
D_FF = 4096
MOBA_BLOCK = 256
MOBA_TOPK = 3
ROPE_THETA = 10000.0
EPS = 1e-6
NEG = -1e30

LANES = 128
N_SELF_GROUPS = SELF_WIDTH // LANES
QK_SCALE = HEAD_DIM ** -0.5
LOG2E = math.log2(math.e)
KV_CHUNK = 256
SUM_ROWS = 16

F32 = jnp.float32
BF16 = jnp.bfloat16

_VMEM_LIMIT = 56 * 1024 * 1024


def _params(*sem):
    return pltpu.CompilerParams(dimension_semantics=sem, vmem_limit_bytes=_VMEM_LIMIT)


def _proj_kernel(x_ref, g_ref, w_ref, cos_ref, sin_ref, o_ref, *extra_refs,
                 n_rope, scales, col_chunk, with_kmean, vt_start):
    tm = x_ref.shape[0]
    n_out = w_ref.shape[1]
    extra = list(extra_refs)
    km_ref = extra.pop(0) if with_kmean else None
    vt_ref = extra.pop(0) if vt_start is not None else None
    x = x_ref[...]
    ms = jnp.mean(x * x, axis=-1, keepdims=True)
    h = (x * lax.rsqrt(ms + EPS) * g_ref[...]).astype(BF16)
    lane = lax.broadcasted_iota(jnp.int32, (tm, LANES), 1)
    first_half = (lane & (HEAD_DIM - 1)) < HEAD_DIM // 2
    if n_rope:
        cos = cos_ref[...]
        sin = sin_ref[...]
    for c0 in range(0, n_out, col_chunk):
        z = jnp.dot(h, w_ref[:, c0:c0 + col_chunk], preferred_element_type=F32)
        for jj in range(col_chunk // LANES):
            j = c0 // LANES + jj
            blk = z[:, jj * LANES:(jj + 1) * LANES]
            if j < n_rope:
                swap = jnp.where(first_half, pltpu.roll(blk, LANES - 32, 1), pltpu.roll(blk, 32, 1))
                blk = blk * cos + swap * sin
                if km_ref is not None:
                    km_ref[0, :, j * LANES:(j + 1) * LANES] = jnp.mean(
                        blk.reshape(tm // MOBA_BLOCK, MOBA_BLOCK, LANES), axis=1)
            if j in scales:
                blk = blk * scales[j]
            o_ref[:, j * LANES:(j + 1) * LANES] = blk.astype(o_ref.dtype)
            if vt_ref is not None and vt_start <= j < vt_start + vt_ref.shape[1] // LANES:
                g = j - vt_start
                for c in range(tm // KV_CHUNK):
                    vt_ref[c, g * LANES:(g + 1) * LANES, :] = (
                        blk[c * KV_CHUNK:(c + 1) * KV_CHUNK, :].T.astype(BF16))


def _proj(x2d, g, w, cos_t, sin_t, *, n_rope, scales, with_kmean, vt_start, seq, tm, name,
          vt_width=SELF_WIDTH):
    t, d = x2d.shape
    n_out = w.shape[1]
    n_pos_blocks = seq // tm
    col_chunk = 512 if n_out % 512 == 0 else 256
    out_shape = [jax.ShapeDtypeStruct((t, n_out), BF16)]
    out_specs = [pl.BlockSpec((tm, n_out), lambda i: (i, 0))]
    if with_kmean:
        out_shape.append(jax.ShapeDtypeStruct((t // tm, tm // MOBA_BLOCK, n_rope * LANES), F32))
        out_specs.append(pl.BlockSpec((1, tm // MOBA_BLOCK, n_rope * LANES), lambda i: (i, 0, 0)))
    if vt_start is not None:
        out_shape.append(jax.ShapeDtypeStruct((t // KV_CHUNK, vt_width, KV_CHUNK), BF16))
        out_specs.append(pl.BlockSpec((tm // KV_CHUNK, vt_width, KV_CHUNK), lambda i: (i, 0, 0)))
    kern = functools.partial(_proj_kernel, n_rope=n_rope, scales=dict(scales), col_chunk=col_chunk,
                             with_kmean=with_kmean, vt_start=vt_start)
    return pl.pallas_call(
        kern,
        out_shape=out_shape,
        grid=(t // tm,),
        in_specs=[
            pl.BlockSpec((tm, d), lambda i: (i, 0)),
            pl.BlockSpec((1, d), lambda i: (0, 0)),
            pl.BlockSpec((d, n_out), lambda i: (0, 0)),
            pl.BlockSpec((tm, LANES), lambda i: (i % n_pos_blocks, 0)),
            pl.BlockSpec((tm, LANES), lambda i: (i % n_pos_blocks, 0)),
        ],
        out_specs=out_specs,
        compiler_params=_params("parallel"),
        name=name,
    )(x2d, g.reshape(1, d), w, cos_t, sin_t)


_NT = (((1,), (1,)), ((), ()))


def _split_maps(q):
    lane = lax.broadcasted_iota(jnp.int32, q.shape, 1)
    zero = jnp.zeros_like(q)
    return [jnp.where(lane < HEAD_DIM, q, zero), jnp.where(lane >= HEAD_DIM, q, zero)]


def _causal_mask(st):
    key = lax.broadcasted_iota(jnp.int32, st.shape, 0)
    qry = lax.broadcasted_iota(jnp.int32, st.shape, 1)
    return jnp.where(key <= qry, st, NEG)


def _flash_pipeline(n_groups, n_past, tq, scores_fn, vt_fn, bias_fn, finalize_fn, s_scr, acc_scr):
    streams = [(g, h) for g in range(n_groups) for h in range(2)]

    def produce(j, buf, only=None):
        col_max = []
        for i, (g, h) in enumerate(streams):
            if only is not None and i != only:
                continue
            st = scores_fn(g, h, j)
            s_scr[buf, g, h] = st
            col_max.append(jnp.max(st, axis=0, keepdims=True))
        return tuple(col_max)

    ones_rows = jnp.ones((SUM_ROWS, tq), BF16)

    def consume(j, buf, stats, col_max, diagonal, only=None):
        out, accs = [], []
        for i, (g, h) in enumerate(streams):
            if only is not None and i != only:
                continue
            st = s_scr[buf, g, h]
            if diagonal:
                st = _causal_mask(st)
                m_cur = jnp.max(st, axis=0, keepdims=True)
            else:
                m_cur = col_max[i]
            if bias_fn is not None:
                bias = bias_fn(g, h, j)
                m_cur = m_cur + bias
            m_next = jnp.maximum(stats[i], m_cur)
            alpha = jnp.exp2(stats[i] - m_next)
            shift = m_next if bias_fn is None else m_next - bias
            p = jnp.exp2((st - shift).astype(BF16))
            acc = alpha * acc_scr[g, h] + jnp.dot(
                jnp.concatenate([vt_fn(g, h, j), ones_rows], axis=0), p, preferred_element_type=F32)
            if diagonal:
                accs.append(acc)
                if h == 1:
                    finalize_fn(g, accs[-2:])
            else:
                acc_scr[g, h] = acc
            out.append(m_next)
        return tuple(out)

    def step(j, buf, carry):
        stats, nxt = [], []
        for i in range(len(streams)):
            nxt += produce(j + 1, 1 - buf, only=i)
            stats += consume(j, buf, carry[0], carry[1], False, only=i)
        return tuple(stats), tuple(nxt)

    def pair(i, carry):
        return step(2 * i + 1, 1, step(2 * i, 0, carry))

    def quad(i, carry):
        return pair(2 * i + 1, pair(2 * i, carry))

    def octet(i, carry):
        return quad(2 * i + 1, quad(2 * i, carry))

    acc_scr[...] = jnp.zeros(acc_scr.shape, F32)
    init = tuple(jnp.full((1, tq), NEG, F32) for _ in streams)
    n_octets = n_past // 8
    carry = lax.fori_loop(0, n_octets, octet, (init, produce(0, 0)))
    carry = lax.fori_loop(2 * n_octets, n_past // 4, quad, carry)
    carry = lax.fori_loop(2 * (n_past // 4), n_past // 2, pair, carry)

    def odd_tail(carry):
        stats, col_max = step(n_past - 1, 0, carry)
        return consume(n_past, 1, stats, col_max, True)

    def even_tail(carry):
        return consume(n_past, 0, carry[0], carry[1], True)

    return lax.cond(n_past % 2 == 1, odd_tail, even_tail, carry)


def _diff_attn_kernel(lam_ref, g_ref, q_ref, k_ref, vt_ref, o_ref, s_scr, acc_scr, *, lam_init):
    tq = KV_CHUNK
    n_groups = q_ref.shape[1] // LANES
    cols = [slice(g * LANES, (g + 1) * LANES) for g in range(n_groups)]
    lp = lam_ref[...]
    lam = (jnp.exp(jnp.sum(lp[0:1] * lp[1:2], axis=1, keepdims=True))
           - jnp.exp(jnp.sum(lp[2:3] * lp[3:4], axis=1, keepdims=True)) + lam_init)

    tiles = q_ref.shape[0] // tq
    first_tile = pl.program_id(2) * tiles

    def query_tile(local, carry):
        qi = first_tile + local
        rows = pl.ds(pl.multiple_of(local * tq, tq), tq)
        qs = [_split_maps(q_ref[rows, c]) for c in cols]

        def scores(g, h, j):
            start = pl.multiple_of(j * tq, tq)
            return lax.dot_general(k_ref[pl.ds(start, tq), cols[g]], qs[g][h], _NT,
                                   preferred_element_type=F32)

        def finalize(g, accs):
            o1, o2 = [a[:LANES] * (1.0 / a[LANES:LANES + 1]) for a in accs]
            od_t = o1 - lam * o2
            ms = jnp.mean(od_t * od_t, axis=0, keepdims=True)
            y = (od_t * lax.rsqrt(ms + EPS)).T * g_ref[...]
            o_ref[rows, cols[g]] = (y * (1.0 - lam_init)).astype(o_ref.dtype)

        _flash_pipeline(n_groups, qi, tq, scores, lambda g, h, j: vt_ref[j, cols[g], :], None,
                        finalize, s_scr, acc_scr)
        return carry

    lax.fori_loop(0, tiles, query_tile, 0)


def _diff_attention(z, vt, lam_p, subln, *, batch, seq, n_groups, tiles_per_step, lam_init):
    t = z.shape[0]
    tq = KV_CHUNK
    w = n_groups * LANES
    n_steps = N_SELF_GROUPS // n_groups
    rows = tiles_per_step * tq
    per_batch = seq // rows
    kern = functools.partial(_diff_attn_kernel, lam_init=lam_init)
    resident = dict(pipeline_mode=pl.Buffered(1))
    return pl.pallas_call(
        kern,
        out_shape=jax.ShapeDtypeStruct((t, SELF_WIDTH), BF16),
        grid=(batch, n_steps, per_batch),
        in_specs=[
            pl.BlockSpec((4, HEAD_DIM), lambda b, h, i: (0, 0)),
            pl.BlockSpec((1, LANES), lambda b, h, i: (0, 0)),
            pl.BlockSpec((rows, w), lambda b, h, i: (b * per_batch + i, h)),
            pl.BlockSpec((seq, w), lambda b, h, i: (b, n_steps + h), **resident),
            pl.BlockSpec((seq // KV_CHUNK, w, KV_CHUNK), lambda b, h, i: (b, h, 0), **resident),
        ],
        out_specs=pl.BlockSpec((rows, w), lambda b, h, i: (b * per_batch + i, h)),
        scratch_shapes=[pltpu.VMEM((2, n_groups, 2, tq, tq), F32),
                        pltpu.VMEM((n_groups, 2, LANES + SUM_ROWS, tq), F32)],
        compiler_params=_params("parallel", "parallel", "parallel"),
        name="diff_attn",
    )(lam_p, subln.reshape(1, LANES), z, z, vt)


def _moba_select_bias(qs, km, qi):
    n_blk = km.shape[0]
    parts, rest = [], km
    for _ in range(3):
        part = rest.astype(BF16)
        parts.append(part)
        rest = rest - part.astype(F32)
    terms = lax.dot_general(jnp.concatenate(parts, axis=0), qs, _NT, preferred_element_type=F32)
    gate = (terms[2 * n_blk:] + terms[n_blk:2 * n_blk]) + terms[:n_blk]
    blk = lax.broadcasted_iota(jnp.int32, gate.shape, 0)
    past = blk < qi
    gm = jnp.where(past, gate, NEG)
    sel = jnp.zeros(gate.shape, F32)
    for _ in range(MOBA_TOPK):
        mx = jnp.max(gm, axis=0, keepdims=True)
        first = jnp.min(jnp.where(gm == mx, blk, n_blk), axis=0, keepdims=True)
        pick = blk == first
        sel = jnp.where(pick, 1.0, sel)
        gm = jnp.where(pick, -jnp.inf, gm)
    return jnp.where((past & (sel > 0.5)) | (blk == qi), 0.0, 2.0 * NEG)


def _moba_attn_kernel(q_ref, k_ref, vt_ref, km_ref, o_ref, s_scr, acc_scr, bias_scr):
    tq = MOBA_BLOCK
    n_groups = q_ref.shape[1] // LANES
    cols = [slice(g * LANES, (g + 1) * LANES) for g in range(n_groups)]

    tiles = q_ref.shape[0] // tq
    first_tile = pl.program_id(2) * tiles

    def query_tile(local, carry):
        qi = first_tile + local
        rows = pl.ds(pl.multiple_of(local * tq, tq), tq)
        qs = [_split_maps(q_ref[rows, c]) for c in cols]
        for g, c in enumerate(cols):
            for h in range(2):
                bias_scr[g, h] = _moba_select_bias(qs[g][h], km_ref[:, c], qi)

        def scores(g, h, j):
            start = pl.multiple_of(j * tq, tq)
            return lax.dot_general(k_ref[pl.ds(start, tq), cols[g]], qs[g][h], _NT,
                                   preferred_element_type=F32)

        def values_t(g, h, j):
            return vt_ref[j, pl.ds(g * LANES + h * HEAD_DIM, HEAD_DIM), :]

        def finalize(g, accs):
            heads = [a[:HEAD_DIM] * (1.0 / a[HEAD_DIM:HEAD_DIM + 1]) for a in accs]
            o_ref[rows, cols[g]] = jnp.concatenate(heads, axis=0).T.astype(o_ref.dtype)

        _flash_pipeline(n_groups, qi, tq, scores, values_t,
                        lambda g, h, j: bias_scr[g, h, pl.ds(j, 1), :], finalize, s_scr, acc_scr)
        return carry

    lax.fori_loop(0, tiles, query_tile, 0)


def _moba_attention(zq, zkv, vt, kmean, *, batch, seq, n_groups, tiles_per_step):
    t = zq.shape[0]
    tq = MOBA_BLOCK
    nq = seq // tq
    w = n_groups * LANES
    n_steps = N_SELF_GROUPS // n_groups
    rows = tiles_per_step * tq
    per_batch = seq // rows
    resident = dict(pipeline_mode=pl.Buffered(1))
    return pl.pallas_call(
        _moba_attn_kernel,
        out_shape=jax.ShapeDtypeStruct((t, SELF_WIDTH), BF16),
        grid=(batch, n_steps, per_batch),
        in_specs=[
            pl.BlockSpec((rows, w), lambda b, h, i: (b * per_batch + i, h)),
            pl.BlockSpec((seq, w), lambda b, h, i: (b, h), **resident),
            pl.BlockSpec((seq // KV_CHUNK, w, KV_CHUNK), lambda b, h, i: (b, h, 0), **resident),
            pl.BlockSpec((None, nq, w), lambda b, h, i: (b, 0, h)),
        ],
        out_specs=pl.BlockSpec((rows, w), lambda b, h, i: (b * per_batch + i, h)),
        scratch_shapes=[pltpu.VMEM((2, n_groups, 2, tq, tq), F32),
                        pltpu.VMEM((n_groups, 2, HEAD_DIM + SUM_ROWS, tq), F32),
                        pltpu.VMEM((n_groups, 2, nq, tq), F32)],
        compiler_params=_params("parallel", "parallel", "parallel"),
        name="moba_attn",
    )(zq, zkv, vt, kmean)


def _attn_out_kernel(x_ref, os_ref, qm_ref, mk_ref, mvt_ref, w_ref, o_ref):
    mem_len = mk_ref.shape[0]
    mk = mk_ref[...]
    lane = lax.broadcasted_iota(jnp.int32, mk.shape, 1)
    zero = jnp.zeros_like(mk)
    mk_heads = jnp.concatenate(
        [jnp.where((lane >= h * HEAD_DIM) & (lane < (h + 1) * HEAD_DIM), mk, zero)
         for h in range(N_MEM_HEADS)], axis=0)
    s = lax.dot_general(mk_heads, qm_ref[...], _NT, preferred_element_type=F32)
    ones_rows = jnp.ones((SUM_ROWS, mem_len), BF16)
    heads = []
    for h in range(N_MEM_HEADS):
        sh = s[h * mem_len:(h + 1) * mem_len]
        p = jnp.exp2((sh - jnp.max(sh, axis=0, keepdims=True)).astype(BF16))
        vt_ones = jnp.concatenate([mvt_ref[h * HEAD_DIM:(h + 1) * HEAD_DIM, :], ones_rows], axis=0)
        oh = jnp.dot(vt_ones, p, preferred_element_type=F32)
        heads.append(oh[:HEAD_DIM] * (1.0 / oh[HEAD_DIM:HEAD_DIM + 1]))
    o_mem = jnp.concatenate(heads, axis=0).T.astype(BF16)
    y = jnp.dot(os_ref[...], w_ref[:SELF_WIDTH, :], preferred_element_type=F32)
    y = y + jnp.dot(o_mem, w_ref[SELF_WIDTH:, :], preferred_element_type=F32)
    o_ref[...] = x_ref[...] + y


def _attn_out(x2d, o_self, zq, qm_block, memkv, memvt, w_out, *, seq, mem_len, tm, name):
    t, d = x2d.shape
    per_batch = seq // tm
    return pl.pallas_call(
        _attn_out_kernel,
        out_shape=jax.ShapeDtypeStruct((t, d), F32),
        grid=(t // tm,),
        in_specs=[
            pl.BlockSpec((tm, d), lambda i: (i, 0)),
            pl.BlockSpec((tm, SELF_WIDTH), lambda i: (i, 0)),
            pl.BlockSpec((tm, MEM_WIDTH), lambda i: (i, qm_block)),
            pl.BlockSpec((mem_len, MEM_WIDTH), lambda i: (i // per_batch, 0)),
            pl.BlockSpec((None, MEM_WIDTH, mem_len), lambda i: (i // per_batch, 0, 0)),
            pl.BlockSpec((d, d), lambda i: (0, 0)),
        ],
        out_specs=pl.BlockSpec((tm, d), lambda i: (i, 0)),
        compiler_params=_params("parallel"),
        name=name,
    )(x2d, o_self, zq, memkv, memvt, w_out)


def _mlp_kernel(x_ref, g_ref, wu_ref, wd_ref, gf_ref, o_ref, u_scr, *, final_norm, tf):
    x = x_ref[...]
    ms = jnp.mean(x * x, axis=-1, keepdims=True)
    h = (x * lax.rsqrt(ms + EPS) * g_ref[...]).astype(BF16)
    for c0 in range(0, wu_ref.shape[1], tf):
        u = jnp.maximum(jnp.dot(h, wu_ref[:, c0:c0 + tf], preferred_element_type=F32), 0.0)
        u_scr[:, c0:c0 + tf] = (u * u).astype(BF16)
    y = x + jnp.dot(u_scr[...], wd_ref[...], preferred_element_type=F32)
    if final_norm:
        ms = jnp.mean(y * y, axis=-1, keepdims=True)
        y = y * lax.rsqrt(ms + EPS) * gf_ref[...]
    o_ref[...] = y


def _mlp(x2d, g, w_up, w_down, g_final, *, final_norm, tm, tf, name):
    t, d = x2d.shape
    dff = w_up.shape[1]
    kern = functools.partial(_mlp_kernel, final_norm=final_norm, tf=tf)
    resident = dict(pipeline_mode=pl.Buffered(1))
    return pl.pallas_call(
        kern,
        out_shape=jax.ShapeDtypeStruct((t, d), F32),
        grid=(t // tm,),
        in_specs=[
            pl.BlockSpec((tm, d), lambda i: (i, 0)),
            pl.BlockSpec((1, d), lambda i: (0, 0)),
            pl.BlockSpec((d, dff), lambda i: (0, 0), **resident),
            pl.BlockSpec((dff, d), lambda i: (0, 0), **resident),
            pl.BlockSpec((1, d), lambda i: (0, 0)),
        ],
        out_specs=pl.BlockSpec((tm, d), lambda i: (i, 0)),
        scratch_shapes=[pltpu.VMEM((tm, dff), BF16)],
        compiler_params=_params("parallel"),
        name=name,
    )(x2d, g.reshape(1, d), w_up, w_down, g_final.reshape(1, d))


def _rope_tables(seq):
    half = HEAD_DIM // 2
    inv = 1.0 / (ROPE_THETA ** (jnp.arange(half, dtype=F32) / half))
    ang = jnp.arange(seq, dtype=F32)[:, None] * inv[None, :]
    cos, sin = jnp.cos(ang), jnp.sin(ang)
    reps = LANES // half
    cos_t = jnp.tile(cos, (1, reps))
    sign = jnp.tile(jnp.concatenate([-jnp.ones((half,), F32), jnp.ones((half,), F32)]), LANES // HEAD_DIM)
    sin_t = jnp.tile(sin, (1, reps)) * sign[None, :]
    return cos_t, sin_t


def kernel(x, mem, a_norm_attn, a_w_in, a_lambda, a_subln, a_mem_norm, a_w_mem_kv, a_w_out, a_norm_mlp, a_w_up, a_w_down, kv_norm, w_kv, b_norm_attn, b_w_in, b_mem_norm, b_w_mem_kv, b_w_out, b_norm_mlp, b_w_up, b_w_down, final_norm):
    batch, seq, d = x.shape
    mem_len = mem.shape[1]
    t = batch * seq
    x2d = x.reshape(t, d)
    mem2d = mem.reshape(batch * mem_len, d)
    cos_t, sin_t = _rope_tables(seq)
    bf = lambda w: w.astype(BF16)
    tm = 512
    ng = N_SELF_GROUPS
    q_scale = QK_SCALE * LOG2E
    mem_groups = MEM_WIDTH // LANES

    def memkv(g, w, name):
        return _proj(mem2d, g, bf(w), cos_t, sin_t, n_rope=0, scales={}, with_kmean=False,
                     vt_start=mem_groups, vt_width=MEM_WIDTH, seq=mem_len, tm=mem_len, name=name)

    lam_init = 0.8 - 0.6 * math.exp(-0.3 * 0)
    za, vt_a = _proj(x2d, a_norm_attn[0], bf(a_w_in[0]), cos_t, sin_t, n_rope=2 * ng,
                     scales={j: q_scale for j in (*range(ng), 3 * ng, 3 * ng + 1)},
                     with_kmean=False, vt_start=2 * ng, seq=seq, tm=2 * tm, name="a_proj")
    mkv_a, mvt_a = memkv(a_mem_norm[0], a_w_mem_kv[0], "a_memkv")
    o_self = _diff_attention(za, vt_a, a_lambda[0], a_subln[0], batch=batch, seq=seq,
                             n_groups=6, tiles_per_step=4, lam_init=lam_init)
    x2d = _attn_out(x2d, o_self, za, 3 * SELF_WIDTH // MEM_WIDTH, mkv_a, mvt_a, bf(a_w_out[0]),
                    seq=seq, mem_len=mem_len, tm=2 * tm, name="a_attn_out")
    x2d = _mlp(x2d, a_norm_mlp[0], bf(a_w_up[0]), bf(a_w_down[0]), final_norm,
               final_norm=False, tm=512, tf=512, name="a_mlp")

    zkv, kmean, vt_b = _proj(x2d, kv_norm, bf(w_kv), cos_t, sin_t, n_rope=ng, scales={},
                             with_kmean=True, vt_start=ng, seq=seq, tm=2 * tm, name="b_kvproj")
    kmean = kmean.reshape(batch, seq // MOBA_BLOCK, SELF_WIDTH)
    zb, = _proj(x2d, b_norm_attn[0], bf(b_w_in[0]), cos_t, sin_t, n_rope=ng,
                scales={j: q_scale for j in range(ng + mem_groups)}, with_kmean=False, vt_start=None,
                seq=seq, tm=2 * tm, name="b_qproj")
    mkv_b, mvt_b = memkv(b_mem_norm[0], b_w_mem_kv[0], "b_memkv")
    o_self = _moba_attention(zb, zkv, vt_b, kmean, batch=batch, seq=seq, n_groups=6,
                             tiles_per_step=4)
    x2d = _attn_out(x2d, o_self, zb, SELF_WIDTH // MEM_WIDTH, mkv_b, mvt_b, bf(b_w_out[0]),
                    seq=seq, mem_len=mem_len, tm=2 * tm, name="b_attn_out")
    x2d = _mlp(x2d, b_norm_mlp[0], bf(b_w_up[0]), bf(b_w_down[0]), final_norm,
               final_norm=True, tm=512, tf=512, name="b_mlp")
    return x2d.reshape(batch, seq, d)
```

```python
import functools
import math

import jax
import jax.numpy as jnp
from jax import lax
from jax.experimental import pallas as pl
from jax.experimental.pallas import tpu as pltpu

D_MODEL = 1024
HEAD_DIM = 64
SELF_WIDTH = 768
MEM_WIDTH = 256
N_MEM_HEADS = 4
D_FF = 4096
MOBA_BLOCK = 256
MOBA_TOPK = 3
ROPE_THETA = 10000.0
EPS = 1e-6
NEG = -1e30

LANES = 128
N_SELF_GROUPS = SELF_WIDTH // LANES
QK_SCALE = HEAD_DIM ** -0.5
LOG2E = math.log2(math.e)
KV_CHUNK = 256
SUM_ROWS = 16

F32 = jnp.float32
BF16 = jnp.bfloat16

_VMEM_LIMIT = 56 * 1024 * 1024


def _params(*sem):
    return pltpu.CompilerParams(dimension_semantics=sem, vmem_limit_bytes=_VMEM_LIMIT)


def _proj_kernel(x_ref, g_ref, w_ref, cos_ref, sin_ref, o_ref, *extra_refs,
                 n_rope, scales, col_chunk, with_kmean, vt_start):
    tm = x_ref.shape[0]
    n_out = w_ref.shape[1]
    extra = list(extra_refs)
    km_ref = extra.pop(0) if with_kmean else None
    vt_ref = extra.pop(0) if vt_start is not None else None
    x = x_ref[...]
    ms = jnp.mean(x * x, axis=-1, keepdims=True)
    h = (x * lax.rsqrt(ms + EPS) * g_ref[...]).astype(BF16)
    lane = lax.broadcasted_iota(jnp.int32, (tm, LANES), 1)
    first_half = (lane & (HEAD_DIM - 1)) < HEAD_DIM // 2
    if n_rope:
        cos = cos_ref[...]
        sin = sin_ref[...]
    for c0 in range(0, n_out, col_chunk):
        z = jnp.dot(h, w_ref[:, c0:c0 + col_chunk], preferred_element_type=F32)
        for jj in range(col_chunk // LANES):
            j = c0 // LANES + jj
            blk = z[:, jj * LANES:(jj + 1) * LANES]
            if j < n_rope:
                swap = jnp.where(first_half, pltpu.roll(blk, LANES - 32, 1), pltpu.roll(blk, 32, 1))
                blk = blk * cos + swap * sin
                if km_ref is not None:
                    km_ref[0, :, j * LANES:(j + 1) * LANES] = jnp.mean(
                        blk.reshape(tm // MOBA_BLOCK, MOBA_BLOCK, LANES), axis=1)
            if j in scales:
                blk = blk * scales[j]
            o_ref[:, j * LANES:(j + 1) * LANES] = blk.astype(o_ref.dtype)
            if vt_ref is not None and vt_start <= j < vt_start + vt_ref.shape[1] // LANES:
                g = j - vt_start
                for c in range(tm // KV_CHUNK):
                    vt_ref[c, g * LANES:(g + 1) * LANES, :] = (
                        blk[c * KV_CHUNK:(c + 1) * KV_CHUNK, :].T.astype(BF16))


def _proj(x2d, g, w, cos_t, sin_t, *, n_rope, scales, with_kmean, vt_start, seq, tm, name,
          vt_width=SELF_WIDTH):
    t, d = x2d.shape
    n_out = w.shape[1]
    n_pos_blocks = seq // tm
    col_chunk = 512 if n_out % 512 == 0 else 256
    out_shape = [jax.ShapeDtypeStruct((t, n_out), BF16)]
    out_specs = [pl.BlockSpec((tm, n_out), lambda i: (i, 0))]
    if with_kmean:
        out_shape.append(jax.ShapeDtypeStruct((t // tm, tm // MOBA_BLOCK, n_rope * LANES), F32))
        out_specs.append(pl.BlockSpec((1, tm // MOBA_BLOCK, n_rope * LANES), lambda i: (i, 0, 0)))
    if vt_start is not None:
        out_shape.append(jax.ShapeDtypeStruct((t // KV_CHUNK, vt_width, KV_CHUNK), BF16))
        out_specs.append(pl.BlockSpec((tm // KV_CHUNK, vt_width, KV_CHUNK), lambda i: (i, 0, 0)))
    kern = functools.partial(_proj_kernel, n_rope=n_rope, scales=dict(scales), col_chunk=col_chunk,
                             with_kmean=with_kmean, vt_start=vt_start)
    return pl.pallas_call(
        kern,
        out_shape=out_shape,
        grid=(t // tm,),
        in_specs=[
            pl.BlockSpec((tm, d), lambda i: (i, 0)),
            pl.BlockSpec((1, d), lambda i: (0, 0)),
            pl.BlockSpec((d, n_out), lambda i: (0, 0)),
            pl.BlockSpec((tm, LANES), lambda i: (i % n_pos_blocks, 0)),
            pl.BlockSpec((tm, LANES), lambda i: (i % n_pos_blocks, 0)),
        ],
        out_specs=out_specs,
        compiler_params=_params("parallel"),
        name=name,
    )(x2d, g.reshape(1, d), w, cos_t, sin_t)


_NT = (((1,), (1,)), ((), ()))


def _split_maps(q):
    lane = lax.broadcasted_iota(jnp.int32, q.shape, 1)
    zero = jnp.zeros_like(q)
    return [jnp.where(lane < HEAD_DIM, q, zero), jnp.where(lane >= HEAD_DIM, q, zero)]


def _causal_mask(st):
    key = lax.broadcasted_iota(jnp.int32, st.shape, 0)
    qry = lax.broadcasted_iota(jnp.int32, st.shape, 1)
    return jnp.where(key <= qry, st, NEG)


FIRST = 2


def _flash_pipeline(n_groups, n_past, tq, scores_fn, next_scores_fn, vt_fn, bias_fn, finalize_fn,
                    first_max, s_scr, acc_scr):
    streams = [(g, h) for g in range(n_groups) for h in range(2)]

    def produce(j, buf, only=None):
        col_max = []
        for i, (g, h) in enumerate(streams):
            if only is not None and i != only:
                continue
            st = scores_fn(g, h, j)
            s_scr[buf, g, h] = st
            col_max.append(jnp.max(st, axis=0, keepdims=True))
        return tuple(col_max)

    ones_rows = jnp.ones((SUM_ROWS, tq), BF16)

    def consume(j, buf, stats, col_max, diagonal, only=None, accs=None):
        out, accs = [], ([] if accs is None else accs)
        for i, (g, h) in enumerate(streams):
            if only is not None and i != only:
                continue
            st = s_scr[buf, g, h]
            if diagonal:
                st = _causal_mask(st)
                m_cur = jnp.max(st, axis=0, keepdims=True)
            else:
                m_cur = col_max[i]
            if bias_fn is not None:
                bias = bias_fn(g, h, j)
                m_cur = m_cur + bias
            m_next = jnp.maximum(stats[i], m_cur)
            alpha = jnp.exp2(stats[i] - m_next)
            shift = m_next if bias_fn is None else m_next - bias
            p = jnp.exp2((st - shift).astype(BF16))
            acc = alpha * acc_scr[g, h] + jnp.dot(
                jnp.concatenate([vt_fn(g, h, j), ones_rows], axis=0), p, preferred_element_type=F32)
            if diagonal:
                accs.append(acc)
                if h == 1:
                    finalize_fn(g, accs[-2:])
            else:
                acc_scr[g, h] = acc
            out.append(m_next)
        return tuple(out)

    def step(j, buf, carry):
        stats, nxt = [], []
        for i in range(len(streams)):
            nxt += produce(j + 1, 1 - buf, only=i)
            stats += consume(j, buf, carry[0], carry[1], False, only=i)
        return tuple(stats), tuple(nxt)

    def pair(i, carry):
        return step(2 * i + 2, 0, step(2 * i + 1, 1, carry))

    def quad(i, carry):
        return pair(2 * i + 1, pair(2 * i, carry))

    def octet(i, carry):
        return quad(2 * i + 1, quad(2 * i, carry))

    def produce_next(i):
        g, h = streams[i]
        st = next_scores_fn(g, h)
        s_scr[FIRST, g, h] = st
        return [jnp.max(st, axis=0, keepdims=True)]

    def last_chunk(j, buf, stats, col_max):
        nxt, accs = [], []
        for i in range(len(streams)):
            nxt += produce_next(i)
            consume(j, buf, stats, col_max, True, only=i, accs=accs)
        return tuple(nxt)

    acc_scr[...] = jnp.zeros(acc_scr.shape, F32)
    init = tuple(jnp.full((1, tq), NEG, F32) for _ in streams)

    def only_diagonal(_):
        nxt, accs = [], []
        for i in range(len(streams)):
            consume(0, FIRST, init, first_max, True, only=i, accs=accs)
            nxt += produce_next(i)
        return tuple(nxt)

    def with_past(_):
        stats, nxt = [], []
        for i in range(len(streams)):
            nxt += produce(1, 1, only=i)
            stats += consume(0, FIRST, init, first_max, False, only=i)
        carry = (tuple(stats), tuple(nxt))
        n_pairs = (n_past - 1) // 2
        carry = lax.fori_loop(0, n_pairs // 4, octet, carry)
        carry = lax.fori_loop(2 * (n_pairs // 4), n_pairs // 2, quad, carry)
        carry = lax.fori_loop(2 * (n_pairs // 2), n_pairs, pair, carry)

        def even_tail(carry):
            stats, col_max = step(n_past - 1, 1, carry)
            return last_chunk(n_past, 0, stats, col_max)

        def odd_tail(carry):
            return last_chunk(n_past, 1, carry[0], carry[1])

        return lax.cond(n_past % 2 == 0, even_tail, odd_tail, carry)

    return lax.cond(n_past == 0, only_diagonal, with_past, 0)


def _qk_scores(k_ref, cols, qs, tq):
    def scores(g, h, j):
        start = j * tq if isinstance(j, int) else pl.multiple_of(j * tq, tq)
        return lax.dot_general(k_ref[pl.ds(start, tq), cols[g]], qs[g][h], _NT,
                               preferred_element_type=F32)
    return scores


def _tile_queries(q_ref, cols, tile, tq):
    start = tile * tq if isinstance(tile, int) else pl.multiple_of(tile * tq, tq)
    return [_split_maps(q_ref[pl.ds(start, tq), c]) for c in cols]


def _tile_scores(q_ref, k_ref, cols, local, tiles, tq):
    qs = _tile_queries(q_ref, cols, local, tq)
    nxt = _qk_scores(k_ref, cols, _tile_queries(q_ref, cols, jnp.minimum(local + 1, tiles - 1), tq), tq)
    return qs, _qk_scores(k_ref, cols, qs, tq), lambda g, h: nxt(g, h, 0)


def _first_chunk(q_ref, k_ref, cols, tq, s_scr):
    scores = _qk_scores(k_ref, cols, _tile_queries(q_ref, cols, 0, tq), tq)
    col_max = []
    for g in range(len(cols)):
        for h in range(2):
            st = scores(g, h, 0)
            s_scr[FIRST, g, h] = st
            col_max.append(jnp.max(st, axis=0, keepdims=True))
    return tuple(col_max)


def _diff_attn_kernel(lam_ref, g_ref, q_ref, k_ref, vt_ref, o_ref, s_scr, acc_scr, *, lam_init):
    tq = KV_CHUNK
    n_groups = q_ref.shape[1] // LANES
    cols = [slice(g * LANES, (g + 1) * LANES) for g in range(n_groups)]
    lp = lam_ref[...]
    lam = (jnp.exp(jnp.sum(lp[0:1] * lp[1:2], axis=1, keepdims=True))
           - jnp.exp(jnp.sum(lp[2:3] * lp[3:4], axis=1, keepdims=True)) + lam_init)

    tiles = q_ref.shape[0] // tq
    first_tile = pl.program_id(2) * tiles

    def query_tile(local, first_max):
        qi = first_tile + local
        rows = pl.ds(pl.multiple_of(local * tq, tq), tq)
        scores, next_scores = _tile_scores(q_ref, k_ref, cols, local, tiles, tq)[1:]

        def finalize(g, accs):
            o1, o2 = [a[:LANES] * (1.0 / a[LANES:LANES + 1]) for a in accs]
            od_t = o1 - lam * o2
            ms = jnp.mean(od_t * od_t, axis=0, keepdims=True)
            y = (od_t * lax.rsqrt(ms + EPS)).T * g_ref[...]
            o_ref[rows, cols[g]] = (y * (1.0 - lam_init)).astype(o_ref.dtype)

        return _flash_pipeline(n_groups, qi, tq, scores, next_scores,
                               lambda g, h, j: vt_ref[j, cols[g], :], None, finalize, first_max,
                               s_scr, acc_scr)

    lax.fori_loop(0, tiles, query_tile, _first_chunk(q_ref, k_ref, cols, tq, s_scr))


def _diff_attention(z, vt, lam_p, subln, *, batch, seq, n_groups, tiles_per_step, lam_init):
    t = z.shape[0]
    tq = KV_CHUNK
    w = n_groups * LANES
    n_steps = N_SELF_GROUPS // n_groups
    rows = tiles_per_step * tq
    per_batch = seq // rows
    kern = functools.partial(_diff_attn_kernel, lam_init=lam_init)
    resident = dict(pipeline_mode=pl.Buffered(1))
    return pl.pallas_call(
        kern,
        out_shape=jax.ShapeDtypeStruct((t, SELF_WIDTH), BF16),
        grid=(batch, n_steps, per_batch),
        in_specs=[
            pl.BlockSpec((4, HEAD_DIM), lambda b, h, i: (0, 0)),
            pl.BlockSpec((1, LANES), lambda b, h, i: (0, 0)),
            pl.BlockSpec((rows, w), lambda b, h, i: (b * per_batch + i, h)),
            pl.BlockSpec((seq, w), lambda b, h, i: (b, n_steps + h), **resident),
            pl.BlockSpec((seq // KV_CHUNK, w, KV_CHUNK), lambda b, h, i: (b, h, 0), **resident),
        ],
        out_specs=pl.BlockSpec((rows, w), lambda b, h, i: (b * per_batch + i, h)),
        scratch_shapes=[pltpu.VMEM((3, n_groups, 2, tq, tq), F32),
                        pltpu.VMEM((n_groups, 2, LANES + SUM_ROWS, tq), F32)],
        compiler_params=_params("parallel", "parallel", "parallel"),
        name="diff_attn",
    )(lam_p, subln.reshape(1, LANES), z, z, vt)


def _moba_select_bias(qs, km, qi):
    n_blk = km.shape[0]
    parts, rest = [], km
    for _ in range(3):
        part = rest.astype(BF16)
        parts.append(part)
        rest = rest - part.astype(F32)
    terms = lax.dot_general(jnp.concatenate(parts, axis=0), qs, _NT, preferred_element_type=F32)
    gate = (terms[2 * n_blk:] + terms[n_blk:2 * n_blk]) + terms[:n_blk]
    blk = lax.broadcasted_iota(jnp.int32, gate.shape, 0)
    past = blk < qi
    gm = jnp.where(past, gate, NEG)
    sel = jnp.zeros(gate.shape, F32)
    for _ in range(MOBA_TOPK):
        mx = jnp.max(gm, axis=0, keepdims=True)
        first = jnp.min(jnp.where(gm == mx, blk, n_blk), axis=0, keepdims=True)
        pick = blk == first
        sel = jnp.where(pick, 1.0, sel)
        gm = jnp.where(pick, -jnp.inf, gm)
    return jnp.where((past & (sel > 0.5)) | (blk == qi), 0.0, 2.0 * NEG)


def _moba_attn_kernel(q_ref, k_ref, vt_ref, km_ref, o_ref, s_scr, acc_scr, bias_scr):
    tq = MOBA_BLOCK
    n_groups = q_ref.shape[1] // LANES
    cols = [slice(g * LANES, (g + 1) * LANES) for g in range(n_groups)]

    tiles = q_ref.shape[0] // tq
    first_tile = pl.program_id(2) * tiles

    def query_tile(local, first_max):
        qi = first_tile + local
        rows = pl.ds(pl.multiple_of(local * tq, tq), tq)
        qs, scores, next_scores = _tile_scores(q_ref, k_ref, cols, local, tiles, tq)
        for g, c in enumerate(cols):
            for h in range(2):
                bias_scr[g, h] = _moba_select_bias(qs[g][h], km_ref[:, c], qi)

        def values_t(g, h, j):
            return vt_ref[j, pl.ds(g * LANES + h * HEAD_DIM, HEAD_DIM), :]

        def finalize(g, accs):
            heads = [a[:HEAD_DIM] * (1.0 / a[HEAD_DIM:HEAD_DIM + 1]) for a in accs]
            o_ref[rows, cols[g]] = jnp.concatenate(heads, axis=0).T.astype(o_ref.dtype)

        return _flash_pipeline(n_groups, qi, tq, scores, next_scores, values_t,
                               lambda g, h, j: bias_scr[g, h, pl.ds(j, 1), :], finalize, first_max,
                               s_scr, acc_scr)

    lax.fori_loop(0, tiles, query_tile, _first_chunk(q_ref, k_ref, cols, tq, s_scr))


def _moba_attention(zq, zkv, vt, kmean, *, batch, seq, n_groups, tiles_per_step):
    t = zq.shape[0]
    tq = MOBA_BLOCK
    nq = seq // tq
    w = n_groups * LANES
    n_steps = N_SELF_GROUPS // n_groups
    rows = tiles_per_step * tq
    per_batch = seq // rows
    resident = dict(pipeline_mode=pl.Buffered(1))
    return pl.pallas_call(
        _moba_attn_kernel,
        out_shape=jax.ShapeDtypeStruct((t, SELF_WIDTH), BF16),
        grid=(batch, n_steps, per_batch),
        in_specs=[
            pl.BlockSpec((rows, w), lambda b, h, i: (b * per_batch + i, h)),
            pl.BlockSpec((seq, w), lambda b, h, i: (b, h), **resident),
            pl.BlockSpec((seq // KV_CHUNK, w, KV_CHUNK), lambda b, h, i: (b, h, 0), **resident),
            pl.BlockSpec((None, nq, w), lambda b, h, i: (b, 0, h)),
        ],
        out_specs=pl.BlockSpec((rows, w), lambda b, h, i: (b * per_batch + i, h)),
        scratch_shapes=[pltpu.VMEM((3, n_groups, 2, tq, tq), F32),
                        pltpu.VMEM((n_groups, 2, HEAD_DIM + SUM_ROWS, tq), F32),
                        pltpu.VMEM((n_groups, 2, nq, tq), F32)],
        compiler_params=_params("parallel", "parallel", "parallel"),
        name="moba_attn",
    )(zq, zkv, vt, kmean)


def _attn_out_kernel(x_ref, os_ref, qm_ref, mk_ref, mvt_ref, w_ref, o_ref):
    mem_len = mk_ref.shape[0]
    mk = mk_ref[...]
    lane = lax.broadcasted_iota(jnp.int32, mk.shape, 1)
    zero = jnp.zeros_like(mk)
    mk_heads = jnp.concatenate(
        [jnp.where((lane >= h * HEAD_DIM) & (lane < (h + 1) * HEAD_DIM), mk, zero)
         for h in range(N_MEM_HEADS)], axis=0)
    s = lax.dot_general(mk_heads, qm_ref[...], _NT, preferred_element_type=F32)
    ones_rows = jnp.ones((SUM_ROWS, mem_len), BF16)
    heads = []
    for h in range(N_MEM_HEADS):
        sh = s[h * mem_len:(h + 1) * mem_len]
        p = jnp.exp2((sh - jnp.max(sh, axis=0, keepdims=True)).astype(BF16))
        vt_ones = jnp.concatenate([mvt_ref[h * HEAD_DIM:(h + 1) * HEAD_DIM, :], ones_rows], axis=0)
        oh = jnp.dot(vt_ones, p, preferred_element_type=F32)
        heads.append(oh[:HEAD_DIM] * (1.0 / oh[HEAD_DIM:HEAD_DIM + 1]))
    o_mem = jnp.concatenate(heads, axis=0).T.astype(BF16)
    y = jnp.dot(os_ref[...], w_ref[:SELF_WIDTH, :], preferred_element_type=F32)
    y = y + jnp.dot(o_mem, w_ref[SELF_WIDTH:, :], preferred_element_type=F32)
    o_ref[...] = x_ref[...] + y


def _attn_out(x2d, o_self, zq, qm_block, memkv, memvt, w_out, *, seq, mem_len, tm, name):
    t, d = x2d.shape
    per_batch = seq // tm
    return pl.pallas_call(
        _attn_out_kernel,
        out_shape=jax.ShapeDtypeStruct((t, d), F32),
        grid=(t // tm,),
        in_specs=[
            pl.BlockSpec((tm, d), lambda i: (i, 0)),
            pl.BlockSpec((tm, SELF_WIDTH), lambda i: (i, 0)),
            pl.BlockSpec((tm, MEM_WIDTH), lambda i: (i, qm_block)),
            pl.BlockSpec((mem_len, MEM_WIDTH), lambda i: (i // per_batch, 0)),
            pl.BlockSpec((None, MEM_WIDTH, mem_len), lambda i: (i // per_batch, 0, 0)),
            pl.BlockSpec((d, d), lambda i: (0, 0)),
        ],
        out_specs=pl.BlockSpec((tm, d), lambda i: (i, 0)),
        compiler_params=_params("parallel"),
        name=name,
    )(x2d, o_self, zq, memkv, memvt, w_out)


def _mlp_kernel(x_ref, g_ref, wu_ref, wd_ref, gf_ref, o_ref, u_scr, *, final_norm, tf):
    x = x_ref[...]
    ms = jnp.mean(x * x, axis=-1, keepdims=True)
    h = (x * lax.rsqrt(ms + EPS) * g_ref[...]).astype(BF16)
    for c0 in range(0, wu_ref.shape[1], tf):
        u = jnp.maximum(jnp.dot(h, wu_ref[:, c0:c0 + tf], preferred_element_type=F32), 0.0)
        u_scr[:, c0:c0 + tf] = (u * u).astype(BF16)
    y = x + jnp.dot(u_scr[...], wd_ref[...], preferred_element_type=F32)
    if final_norm:
        ms = jnp.mean(y * y, axis=-1, keepdims=True)
        y = y * lax.rsqrt(ms + EPS) * gf_ref[...]
    o_ref[...] = y


def _mlp(x2d, g, w_up, w_down, g_final, *, final_norm, tm, tf, name):
    t, d = x2d.shape
    dff = w_up.shape[1]
    kern = functools.partial(_mlp_kernel, final_norm=final_norm, tf=tf)
    resident = dict(pipeline_mode=pl.Buffered(1))
    return pl.pallas_call(
        kern,
        out_shape=jax.ShapeDtypeStruct((t, d), F32),
        grid=(t // tm,),
        in_specs=[
            pl.BlockSpec((tm, d), lambda i: (i, 0)),
            pl.BlockSpec((1, d), lambda i: (0, 0)),
            pl.BlockSpec((d, dff), lambda i: (0, 0), **resident),
            pl.BlockSpec((dff, d), lambda i: (0, 0), **resident),
            pl.BlockSpec((1, d), lambda i: (0, 0)),
        ],
        out_specs=pl.BlockSpec((tm, d), lambda i: (i, 0)),
        scratch_shapes=[pltpu.VMEM((tm, dff), BF16)],
        compiler_params=_params("parallel"),
        name=name,
    )(x2d, g.reshape(1, d), w_up, w_down, g_final.reshape(1, d))


def _rope_tables(seq):
    half = HEAD_DIM // 2
    inv = 1.0 / (ROPE_THETA ** (jnp.arange(half, dtype=F32) / half))
    ang = jnp.arange(seq, dtype=F32)[:, None] * inv[None, :]
    cos, sin = jnp.cos(ang), jnp.sin(ang)
    reps = LANES // half
    cos_t = jnp.tile(cos, (1, reps))
    sign = jnp.tile(jnp.concatenate([-jnp.ones((half,), F32), jnp.ones((half,), F32)]), LANES // HEAD_DIM)
    sin_t = jnp.tile(sin, (1, reps)) * sign[None, :]
    return cos_t, sin_t


def kernel(x, mem, a_norm_attn, a_w_in, a_lambda, a_subln, a_mem_norm, a_w_mem_kv, a_w_out, a_norm_mlp, a_w_up, a_w_down, kv_norm, w_kv, b_norm_attn, b_w_in, b_mem_norm, b_w_mem_kv, b_w_out, b_norm_mlp, b_w_up, b_w_down, final_norm):
    batch, seq, d = x.shape
    mem_len = mem.shape[1]
    t = batch * seq
    x2d = x.reshape(t, d)
    mem2d = mem.reshape(batch * mem_len, d)
    cos_t, sin_t = _rope_tables(seq)
    bf = lambda w: w.astype(BF16)
    tm = 512
    ng = N_SELF_GROUPS
    q_scale = QK_SCALE * LOG2E
    mem_groups = MEM_WIDTH // LANES

    def memkv(g, w, name):
        return _proj(mem2d, g, bf(w), cos_t, sin_t, n_rope=0, scales={}, with_kmean=False,
                     vt_start=mem_groups, vt_width=MEM_WIDTH, seq=mem_len, tm=mem_len, name=name)

    lam_init = 0.8 - 0.6 * math.exp(-0.3 * 0)
    za, vt_a = _proj(x2d, a_norm_attn[0], bf(a_w_in[0]), cos_t, sin_t, n_rope=2 * ng,
                     scales={j: q_scale for j in (*range(ng), 3 * ng, 3 * ng + 1)},
                     with_kmean=False, vt_start=2 * ng, seq=seq, tm=2 * tm, name="a_proj")
    mkv_a, mvt_a = memkv(a_mem_norm[0], a_w_mem_kv[0], "a_memkv")
    o_self = _diff_attention(za, vt_a, a_lambda[0], a_subln[0], batch=batch, seq=seq,
                             n_groups=6, tiles_per_step=8, lam_init=lam_init)
    x2d = _attn_out(x2d, o_self, za, 3 * SELF_WIDTH // MEM_WIDTH, mkv_a, mvt_a, bf(a_w_out[0]),
                    seq=seq, mem_len=mem_len, tm=2 * tm, name="a_attn_out")
    x2d = _mlp(x2d, a_norm_mlp[0], bf(a_w_up[0]), bf(a_w_down[0]), final_norm,
               final_norm=False, tm=512, tf=512, name="a_mlp")

    zkv, kmean, vt_b = _proj(x2d, kv_norm, bf(w_kv), cos_t, sin_t, n_rope=ng, scales={},
                             with_kmean=True, vt_start=ng, seq=seq, tm=2 * tm, name="b_kvproj")
    kmean = kmean.reshape(batch, seq // MOBA_BLOCK, SELF_WIDTH)
    zb, = _proj(x2d, b_norm_attn[0], bf(b_w_in[0]), cos_t, sin_t, n_rope=ng,
                scales={j: q_scale for j in range(ng + mem_groups)}, with_kmean=False, vt_start=None,
                seq=seq, tm=2 * tm, name="b_qproj")
    mkv_b, mvt_b = memkv(b_mem_norm[0], b_w_mem_kv[0], "b_memkv")
    o_self = _moba_attention(zb, zkv, vt_b, kmean, batch=batch, seq=seq, n_groups=6,
                             tiles_per_step=8)
    x2d = _attn_out(x2d, o_self, zb, SELF_WIDTH // MEM_WIDTH, mkv_b, mvt_b, bf(b_w_out[0]),
                    seq=seq, mem_len=mem_len, tm=2 * tm, name="b_attn_out")
    x2d = _mlp(x2d, b_norm_mlp[0], bf(b_w_up[0]), bf(b_w_down[0]), final_norm,
               final_norm=True, tm=512, tf=512, name="b_mlp")
    return x2d.reshape(batch, seq, d)
```

```python
import functools
import math

import jax
import jax.numpy as jnp
from jax import lax
from jax.experimental import pallas as pl
from jax.experimental.pallas import tpu as pltpu

HEAD_DIM = 64
SELF_WIDTH = 768
MEM_WIDTH = 256
N_MEM_HEADS = 4
MOBA_BLOCK = 256
MOBA_TOPK = 3
ROPE_THETA = 10000.0
EPS = 1e-6
NEG = -1e30

LANES = 128
N_SELF_GROUPS = SELF_WIDTH // LANES
QK_SCALE = HEAD_DIM ** -0.5
LOG2E = math.log2(math.e)
KV_CHUNK = 256
SUM_ROWS = 16

F32 = jnp.float32
BF16 = jnp.bfloat16

V7X_VMEM_BYTES = 64 * 1024 * 1024
VMEM_LIMIT_BYTES = V7X_VMEM_BYTES * 7 // 8

PROJ_ROWS = 1024
MLP_ROWS = 512
MLP_FF_CHUNK = 512
ATTN_GROUPS = N_SELF_GROUPS
ATTN_TILES_PER_STEP = 4


def _params(*sem):
    return pltpu.CompilerParams(dimension_semantics=sem, vmem_limit_bytes=VMEM_LIMIT_BYTES)


def _proj_kernel(x_ref, g_ref, w_ref, cos_ref, sin_ref, o_ref, *extra_refs,
                 n_rope, scales, col_chunk, with_kmean, vt_start):
    tm = x_ref.shape[0]
    n_out = w_ref.shape[1]
    extra = list(extra_refs)
    km_ref = extra.pop(0) if with_kmean else None
    vt_ref = extra.pop(0) if vt_start is not None else None
    x = x_ref[...]
    ms = jnp.mean(x * x, axis=-1, keepdims=True)
    h = (x * lax.rsqrt(ms + EPS) * g_ref[...]).astype(BF16)
    lane = lax.broadcasted_iota(jnp.int32, (tm, LANES), 1)
    first_half = (lane & (HEAD_DIM - 1)) < HEAD_DIM // 2
    if n_rope:
        cos = cos_ref[...]
        sin = sin_ref[...]
    for c0 in range(0, n_out, col_chunk):
        z = jnp.dot(h, w_ref[:, c0:c0 + col_chunk], preferred_element_type=F32)
        for jj in range(col_chunk // LANES):
            j = c0 // LANES + jj
            blk = z[:, jj * LANES:(jj + 1) * LANES]
            if j < n_rope:
                swap = jnp.where(first_half, pltpu.roll(blk, LANES - 32, 1), pltpu.roll(blk, 32, 1))
                blk = blk * cos + swap * sin
                if km_ref is not None:
                    km_ref[0, :, j * LANES:(j + 1) * LANES] = jnp.mean(
                        blk.reshape(tm // MOBA_BLOCK, MOBA_BLOCK, LANES), axis=1)
            if j in scales:
                blk = blk * scales[j]
            o_ref[:, j * LANES:(j + 1) * LANES] = blk.astype(o_ref.dtype)
            if vt_ref is not None and vt_start <= j < vt_start + vt_ref.shape[1] // LANES:
                g = j - vt_start
                for c in range(tm // KV_CHUNK):
                    vt_ref[c, g * LANES:(g + 1) * LANES, :] = (
                        blk[c * KV_CHUNK:(c + 1) * KV_CHUNK, :].T.astype(BF16))


def _proj(x2d, g, w, cos_t, sin_t, *, n_rope, scales, with_kmean, vt_start, seq, tm, name,
          vt_width=SELF_WIDTH):
    t, d = x2d.shape
    n_out = w.shape[1]
    n_pos_blocks = seq // tm
    col_chunk = 512 if n_out % 512 == 0 else 256
    out_shape = [jax.ShapeDtypeStruct((t, n_out), BF16)]
    out_specs = [pl.BlockSpec((tm, n_out), lambda i: (i, 0))]
    if with_kmean:
        out_shape.append(jax.ShapeDtypeStruct((t // tm, tm // MOBA_BLOCK, n_rope * LANES), F32))
        out_specs.append(pl.BlockSpec((1, tm // MOBA_BLOCK, n_rope * LANES), lambda i: (i, 0, 0)))
    if vt_start is not None:
        out_shape.append(jax.ShapeDtypeStruct((t // KV_CHUNK, vt_width, KV_CHUNK), BF16))
        out_specs.append(pl.BlockSpec((tm // KV_CHUNK, vt_width, KV_CHUNK), lambda i: (i, 0, 0)))
    kern = functools.partial(_proj_kernel, n_rope=n_rope, scales=dict(scales), col_chunk=col_chunk,
                             with_kmean=with_kmean, vt_start=vt_start)
    return pl.pallas_call(
        kern,
        out_shape=out_shape,
        grid=(t // tm,),
        in_specs=[
            pl.BlockSpec((tm, d), lambda i: (i, 0)),
            pl.BlockSpec((1, d), lambda i: (0, 0)),
            pl.BlockSpec((d, n_out), lambda i: (0, 0)),
            pl.BlockSpec((tm, LANES), lambda i: (i % n_pos_blocks, 0)),
            pl.BlockSpec((tm, LANES), lambda i: (i % n_pos_blocks, 0)),
        ],
        out_specs=out_specs,
        compiler_params=_params("parallel"),
        name=name,
    )(x2d, g.reshape(1, d), w, cos_t, sin_t)


_NT = (((1,), (1,)), ((), ()))


def _split_maps(q):
    lane = lax.broadcasted_iota(jnp.int32, q.shape, 1)
    zero = jnp.zeros_like(q)
    return [jnp.where(lane < HEAD_DIM, q, zero), jnp.where(lane >= HEAD_DIM, q, zero)]


def _causal_mask(st):
    key = lax.broadcasted_iota(jnp.int32, st.shape, 0)
    qry = lax.broadcasted_iota(jnp.int32, st.shape, 1)
    return jnp.where(key <= qry, st, NEG)


def _flash_pipeline(n_groups, n_past, tq, scores_fn, vt_fn, bias_fn, finalize_fn, s_scr, acc_scr):
    streams = [(g, h) for g in range(n_groups) for h in range(2)]

    def produce(j, buf, only=None):
        col_max = []
        for i, (g, h) in enumerate(streams):
            if only is not None and i != only:
                continue
            st = scores_fn(g, h, j)
            s_scr[buf, g, h] = st
            col_max.append(jnp.max(st, axis=0, keepdims=True))
        return tuple(col_max)

    ones_rows = jnp.ones((SUM_ROWS, tq), BF16)

    def consume(j, buf, stats, col_max, diagonal, only=None):
        out, accs = [], []
        for i, (g, h) in enumerate(streams):
            if only is not None and i != only:
                continue
            st = s_scr[buf, g, h]
            if diagonal:
                st = _causal_mask(st)
                m_cur = jnp.max(st, axis=0, keepdims=True)
            else:
                m_cur = col_max[i]
            if bias_fn is not None:
                bias = bias_fn(g, h, j)
                m_cur = m_cur + bias
            m_next = jnp.maximum(stats[i], m_cur)
            alpha = jnp.exp2(stats[i] - m_next)
            shift = m_next if bias_fn is None else m_next - bias
            p = jnp.exp2((st - shift).astype(BF16))
            acc = alpha * acc_scr[g, h] + jnp.dot(
                jnp.concatenate([vt_fn(g, h, j), ones_rows], axis=0), p, preferred_element_type=F32)
            if diagonal:
                accs.append(acc)
                if h == 1:
                    finalize_fn(g, accs[-2:])
            else:
                acc_scr[g, h] = acc
            out.append(m_next)
        return tuple(out)

    def step(j, buf, carry):
        stats, nxt = [], []
        for i in range(len(streams)):
            nxt += produce(j + 1, 1 - buf, only=i)
            stats += consume(j, buf, carry[0], carry[1], False, only=i)
        return tuple(stats), tuple(nxt)

    def pair(i, carry):
        return step(2 * i + 1, 1, step(2 * i, 0, carry))

    def quad(i, carry):
        return pair(2 * i + 1, pair(2 * i, carry))

    def octet(i, carry):
        return quad(2 * i + 1, quad(2 * i, carry))

    acc_scr[...] = jnp.zeros(acc_scr.shape, F32)
    init = tuple(jnp.full((1, tq), NEG, F32) for _ in streams)
    n_octets = n_past // 8
    carry = lax.fori_loop(0, n_octets, octet, (init, produce(0, 0)))
    carry = lax.fori_loop(2 * n_octets, n_past // 4, quad, carry)
    carry = lax.fori_loop(2 * (n_past // 4), n_past // 2, pair, carry)

    def odd_tail(carry):
        stats, col_max = step(n_past - 1, 0, carry)
        return consume(n_past, 1, stats, col_max, True)

    def even_tail(carry):
        return consume(n_past, 0, carry[0], carry[1], True)

    return lax.cond(n_past % 2 == 1, odd_tail, even_tail, carry)


def _diff_attn_kernel(lam_ref, g_ref, q_ref, k_ref, vt_ref, o_ref, s_scr, acc_scr, *, lam_init):
    tq = KV_CHUNK
    n_groups = q_ref.shape[1] // LANES
    cols = [slice(g * LANES, (g + 1) * LANES) for g in range(n_groups)]
    lp = lam_ref[...]
    lam = (jnp.exp(jnp.sum(lp[0:1] * lp[1:2], axis=1, keepdims=True))
           - jnp.exp(jnp.sum(lp[2:3] * lp[3:4], axis=1, keepdims=True)) + lam_init)

    tiles = q_ref.shape[0] // tq
    first_tile = pl.program_id(2) * tiles

    def query_tile(local, carry):
        qi = first_tile + local
        rows = pl.ds(pl.multiple_of(local * tq, tq), tq)
        qs = [_split_maps(q_ref[rows, c]) for c in cols]

        def scores(g, h, j):
            start = pl.multiple_of(j * tq, tq)
            return lax.dot_general(k_ref[pl.ds(start, tq), cols[g]], qs[g][h], _NT,
                                   preferred_element_type=F32)

        def finalize(g, accs):
            o1, o2 = [a[:LANES] * (1.0 / a[LANES:LANES + 1]) for a in accs]
            od_t = o1 - lam * o2
            ms = jnp.mean(od_t * od_t, axis=0, keepdims=True)
            y = (od_t * lax.rsqrt(ms + EPS)).T * g_ref[...]
            o_ref[rows, cols[g]] = (y * (1.0 - lam_init)).astype(o_ref.dtype)

        _flash_pipeline(n_groups, qi, tq, scores, lambda g, h, j: vt_ref[j, cols[g], :], None,
                        finalize, s_scr, acc_scr)
        return carry

    lax.fori_loop(0, tiles, query_tile, 0)


def _diff_attention(z, vt, lam_p, subln, *, batch, seq, lam_init):
    t = z.shape[0]
    tq = KV_CHUNK
    n_groups = ATTN_GROUPS
    w = n_groups * LANES
    n_steps = N_SELF_GROUPS // n_groups
    rows = ATTN_TILES_PER_STEP * tq
    per_batch = seq // rows
    kern = functools.partial(_diff_attn_kernel, lam_init=lam_init)
    resident = dict(pipeline_mode=pl.Buffered(1))
    return pl.pallas_call(
        kern,
        out_shape=jax.ShapeDtypeStruct((t, SELF_WIDTH), BF16),
        grid=(batch, n_steps, per_batch),
        in_specs=[
            pl.BlockSpec((4, HEAD_DIM), lambda b, h, i: (0, 0)),
            pl.BlockSpec((1, LANES), lambda b, h, i: (0, 0)),
            pl.BlockSpec((rows, w), lambda b, h, i: (b * per_batch + i, h)),
            pl.BlockSpec((seq, w), lambda b, h, i: (b, n_steps + h), **resident),
            pl.BlockSpec((seq // KV_CHUNK, w, KV_CHUNK), lambda b, h, i: (b, h, 0), **resident),
        ],
        out_specs=pl.BlockSpec((rows, w), lambda b, h, i: (b * per_batch + i, h)),
        scratch_shapes=[pltpu.VMEM((2, n_groups, 2, tq, tq), F32),
                        pltpu.VMEM((n_groups, 2, LANES + SUM_ROWS, tq), F32)],
        compiler_params=_params("parallel", "parallel", "parallel"),
        name="diff_attn",
    )(lam_p, subln.reshape(1, LANES), z, z, vt)


def _moba_select_bias(qs, km, qi):
    n_blk = km.shape[0]
    parts, rest = [], km
    for _ in range(3):
        part = rest.astype(BF16)
        parts.append(part)
        rest = rest - part.astype(F32)
    terms = lax.dot_general(jnp.concatenate(parts, axis=0), qs, _NT, preferred_element_type=F32)
    gate = (terms[2 * n_blk:] + terms[n_blk:2 * n_blk]) + terms[:n_blk]
    blk = lax.broadcasted_iota(jnp.int32, gate.shape, 0)
    past = blk < qi
    gm = jnp.where(past, gate, NEG)
    sel = jnp.zeros(gate.shape, F32)
    for _ in range(MOBA_TOPK):
        mx = jnp.max(gm, axis=0, keepdims=True)
        first = jnp.min(jnp.where(gm == mx, blk, n_blk), axis=0, keepdims=True)
        pick = blk == first
        sel = jnp.where(pick, 1.0, sel)
        gm = jnp.where(pick, -jnp.inf, gm)
    return jnp.where((past & (sel > 0.5)) | (blk == qi), 0.0, 2.0 * NEG)


def _moba_attn_kernel(q_ref, k_ref, vt_ref, km_ref, o_ref, s_scr, acc_scr, bias_scr):
    tq = MOBA_BLOCK
    n_groups = q_ref.shape[1] // LANES
    cols = [slice(g * LANES, (g + 1) * LANES) for g in range(n_groups)]

    tiles = q_ref.shape[0] // tq
    first_tile = pl.program_id(2) * tiles

    def query_tile(local, carry):
        qi = first_tile + local
        rows = pl.ds(pl.multiple_of(local * tq, tq), tq)
        qs = [_split_maps(q_ref[rows, c]) for c in cols]
        for g, c in enumerate(cols):
            for h in range(2):
                bias_scr[g, h] = _moba_select_bias(qs[g][h], km_ref[:, c], qi)

        def scores(g, h, j):
            start = pl.multiple_of(j * tq, tq)
            return lax.dot_general(k_ref[pl.ds(start, tq), cols[g]], qs[g][h], _NT,
                                   preferred_element_type=F32)

        def values_t(g, h, j):
            return vt_ref[j, pl.ds(g * LANES + h * HEAD_DIM, HEAD_DIM), :]

        def finalize(g, accs):
            heads = [a[:HEAD_DIM] * (1.0 / a[HEAD_DIM:HEAD_DIM + 1]) for a in accs]
            o_ref[rows, cols[g]] = jnp.concatenate(heads, axis=0).T.astype(o_ref.dtype)

        _flash_pipeline(n_groups, qi, tq, scores, values_t,
                        lambda g, h, j: bias_scr[g, h, pl.ds(j, 1), :], finalize, s_scr, acc_scr)
        return carry

    lax.fori_loop(0, tiles, query_tile, 0)


def _moba_attention(zq, zkv, vt, kmean, *, batch, seq):
    t = zq.shape[0]
    tq = MOBA_BLOCK
    nq = seq // tq
    n_groups = ATTN_GROUPS
    w = n_groups * LANES
    n_steps = N_SELF_GROUPS // n_groups
    rows = ATTN_TILES_PER_STEP * tq
    per_batch = seq // rows
    resident = dict(pipeline_mode=pl.Buffered(1))
    return pl.pallas_call(
        _moba_attn_kernel,
        out_shape=jax.ShapeDtypeStruct((t, SELF_WIDTH), BF16),
        grid=(batch, n_steps, per_batch),
        in_specs=[
            pl.BlockSpec((rows, w), lambda b, h, i: (b * per_batch + i, h)),
            pl.BlockSpec((seq, w), lambda b, h, i: (b, h), **resident),
            pl.BlockSpec((seq // KV_CHUNK, w, KV_CHUNK), lambda b, h, i: (b, h, 0), **resident),
            pl.BlockSpec((None, nq, w), lambda b, h, i: (b, 0, h)),
        ],
        out_specs=pl.BlockSpec((rows, w), lambda b, h, i: (b * per_batch + i, h)),
        scratch_shapes=[pltpu.VMEM((2, n_groups, 2, tq, tq), F32),
                        pltpu.VMEM((n_groups, 2, HEAD_DIM + SUM_ROWS, tq), F32),
                        pltpu.VMEM((n_groups, 2, nq, tq), F32)],
        compiler_params=_params("parallel", "parallel", "parallel"),
        name="moba_attn",
    )(zq, zkv, vt, kmean)


def _attn_out_kernel(x_ref, os_ref, qm_ref, mk_ref, mvt_ref, w_ref, o_ref):
    mem_len = mk_ref.shape[0]
    mk = mk_ref[...]
    lane = lax.broadcasted_iota(jnp.int32, mk.shape, 1)
    zero = jnp.zeros_like(mk)
    mk_heads = jnp.concatenate(
        [jnp.where((lane >= h * HEAD_DIM) & (lane < (h + 1) * HEAD_DIM), mk, zero)
         for h in range(N_MEM_HEADS)], axis=0)
    s = lax.dot_general(mk_heads, qm_ref[...], _NT, preferred_element_type=F32)
    ones_rows = jnp.ones((SUM_ROWS, mem_len), BF16)
    heads = []
    for h in range(N_MEM_HEADS):
        sh = s[h * mem_len:(h + 1) * mem_len]
        p = jnp.exp2((sh - jnp.max(sh, axis=0, keepdims=True)).astype(BF16))
        vt_ones = jnp.concatenate([mvt_ref[h * HEAD_DIM:(h + 1) * HEAD_DIM, :], ones_rows], axis=0)
        oh = jnp.dot(vt_ones, p, preferred_element_type=F32)
        heads.append(oh[:HEAD_DIM] * (1.0 / oh[HEAD_DIM:HEAD_DIM + 1]))
    o_mem = jnp.concatenate(heads, axis=0).T.astype(BF16)
    y = jnp.dot(os_ref[...], w_ref[:SELF_WIDTH, :], preferred_element_type=F32)
    y = y + jnp.dot(o_mem, w_ref[SELF_WIDTH:, :], preferred_element_type=F32)
    o_ref[...] = x_ref[...] + y


def _attn_out(x2d, o_self, zq, qm_block, memkv, memvt, w_out, *, seq, mem_len, name):
    t, d = x2d.shape
    tm = PROJ_ROWS
    per_batch = seq // tm
    return pl.pallas_call(
        _attn_out_kernel,
        out_shape=jax.ShapeDtypeStruct((t, d), F32),
        grid=(t // tm,),
        in_specs=[
            pl.BlockSpec((tm, d), lambda i: (i, 0)),
            pl.BlockSpec((tm, SELF_WIDTH), lambda i: (i, 0)),
            pl.BlockSpec((tm, MEM_WIDTH), lambda i: (i, qm_block)),
            pl.BlockSpec((mem_len, MEM_WIDTH), lambda i: (i // per_batch, 0)),
            pl.BlockSpec((None, MEM_WIDTH, mem_len), lambda i: (i // per_batch, 0, 0)),
            pl.BlockSpec((d, d), lambda i: (0, 0)),
        ],
        out_specs=pl.BlockSpec((tm, d), lambda i: (i, 0)),
        compiler_params=_params("parallel"),
        name=name,
    )(x2d, o_self, zq, memkv, memvt, w_out)


def _mlp_kernel(x_ref, g_ref, wu_ref, wd_ref, gf_ref, o_ref, u_scr, *, final_norm, tf):
    x = x_ref[...]
    ms = jnp.mean(x * x, axis=-1, keepdims=True)
    h = (x * lax.rsqrt(ms + EPS) * g_ref[...]).astype(BF16)
    for c0 in range(0, wu_ref.shape[1], tf):
        u = jnp.maximum(jnp.dot(h, wu_ref[:, c0:c0 + tf], preferred_element_type=F32), 0.0)
        u_scr[:, c0:c0 + tf] = (u * u).astype(BF16)
    y = x + jnp.dot(u_scr[...], wd_ref[...], preferred_element_type=F32)
    if final_norm:
        ms = jnp.mean(y * y, axis=-1, keepdims=True)
        y = y * lax.rsqrt(ms + EPS) * gf_ref[...]
    o_ref[...] = y


def _mlp(x2d, g, w_up, w_down, g_final, *, final_norm, name):
    t, d = x2d.shape
    dff = w_up.shape[1]
    tm = MLP_ROWS
    kern = functools.partial(_mlp_kernel, final_norm=final_norm, tf=MLP_FF_CHUNK)
    resident = dict(pipeline_mode=pl.Buffered(1))
    return pl.pallas_call(
        kern,
        out_shape=jax.ShapeDtypeStruct((t, d), F32),
        grid=(t // tm,),
        in_specs=[
            pl.BlockSpec((tm, d), lambda i: (i, 0)),
            pl.BlockSpec((1, d), lambda i: (0, 0)),
            pl.BlockSpec((d, dff), lambda i: (0, 0), **resident),
            pl.BlockSpec((dff, d), lambda i: (0, 0), **resident),
            pl.BlockSpec((1, d), lambda i: (0, 0)),
        ],
        out_specs=pl.BlockSpec((tm, d), lambda i: (i, 0)),
        scratch_shapes=[pltpu.VMEM((tm, dff), BF16)],
        compiler_params=_params("parallel"),
        name=name,
    )(x2d, g.reshape(1, d), w_up, w_down, g_final.reshape(1, d))


def _rope_tables(seq):
    half = HEAD_DIM // 2
    inv = 1.0 / (ROPE_THETA ** (jnp.arange(half, dtype=F32) / half))
    ang = jnp.arange(seq, dtype=F32)[:, None] * inv[None, :]
    cos, sin = jnp.cos(ang), jnp.sin(ang)
    reps = LANES // half
    cos_t = jnp.tile(cos, (1, reps))
    sign = jnp.tile(jnp.concatenate([-jnp.ones((half,), F32), jnp.ones((half,), F32)]), LANES // HEAD_DIM)
    sin_t = jnp.tile(sin, (1, reps)) * sign[None, :]
    return cos_t, sin_t


def kernel(x, mem, a_norm_attn, a_w_in, a_lambda, a_subln, a_mem_norm, a_w_mem_kv, a_w_out, a_norm_mlp, a_w_up, a_w_down, kv_norm, w_kv, b_norm_attn, b_w_in, b_mem_norm, b_w_mem_kv, b_w_out, b_norm_mlp, b_w_up, b_w_down, final_norm):
    batch, seq, d = x.shape
    mem_len = mem.shape[1]
    t = batch * seq
    x2d = x.reshape(t, d)
    mem2d = mem.reshape(batch * mem_len, d)
    cos_t, sin_t = _rope_tables(seq)
    bf = lambda w: w.astype(BF16)
    ng = N_SELF_GROUPS
    q_scale = QK_SCALE * LOG2E
    mem_groups = MEM_WIDTH // LANES

    def memkv(g, w, name):
        return _proj(mem2d, g, bf(w), cos_t, sin_t, n_rope=0, scales={}, with_kmean=False,
                     vt_start=mem_groups, vt_width=MEM_WIDTH, seq=mem_len, tm=mem_len, name=name)

    lam_init = 0.8 - 0.6 * math.exp(-0.3 * 0)
    za, vt_a = _proj(x2d, a_norm_attn[0], bf(a_w_in[0]), cos_t, sin_t, n_rope=2 * ng,
                     scales={j: q_scale for j in (*range(ng), 3 * ng, 3 * ng + 1)},
                     with_kmean=False, vt_start=2 * ng, seq=seq, tm=PROJ_ROWS, name="a_proj")
    mkv_a, mvt_a = memkv(a_mem_norm[0], a_w_mem_kv[0], "a_memkv")
    o_self = _diff_attention(za, vt_a, a_lambda[0], a_subln[0], batch=batch, seq=seq,
                             lam_init=lam_init)
    x2d = _attn_out(x2d, o_self, za, 3 * SELF_WIDTH // MEM_WIDTH, mkv_a, mvt_a, bf(a_w_out[0]),
                    seq=seq, mem_len=mem_len, name="a_attn_out")
    x2d = _mlp(x2d, a_norm_mlp[0], bf(a_w_up[0]), bf(a_w_down[0]), final_norm,
               final_norm=False, name="a_mlp")

    zkv, kmean, vt_b = _proj(x2d, kv_norm, bf(w_kv), cos_t, sin_t, n_rope=ng, scales={},
                             with_kmean=True, vt_start=ng, seq=seq, tm=PROJ_ROWS, name="b_kvproj")
    kmean = kmean.reshape(batch, seq // MOBA_BLOCK, SELF_WIDTH)
    zb, = _proj(x2d, b_norm_attn[0], bf(b_w_in[0]), cos_t, sin_t, n_rope=ng,
                scales={j: q_scale for j in range(ng + mem_groups)}, with_kmean=False, vt_start=None,
                seq=seq, tm=PROJ_ROWS, name="b_qproj")
    mkv_b, mvt_b = memkv(b_mem_norm[0], b_w_mem_kv[0], "b_memkv")
    o_self = _moba_attention(zb, zkv, vt_b, kmean, batch=batch, seq=seq)
    x2d = _attn_out(x2d, o_self, zb, SELF_WIDTH // MEM_WIDTH, mkv_b, mvt_b, bf(b_w_out[0]),
                    seq=seq, mem_len=mem_len, name="b_attn_out")
    x2d = _mlp(x2d, b_norm_mlp[0], bf(b_w_up[0]), bf(b_w_down[0]), final_norm,
               final_norm=True, name="b_mlp")
    return x2d.reshape(batch, seq, d)
```

```python
import functools
import math

import jax
import jax.numpy as jnp
from jax import lax
from jax.experimental import pallas as pl
from jax.experimental.pallas import tpu as pltpu

HEAD_DIM = 64
SELF_WIDTH = 768
MEM_WIDTH = 256
N_MEM_HEADS = 4
MOBA_BLOCK = 256
MOBA_TOPK = 3
ROPE_THETA = 10000.0
EPS = 1e-6
NEG = -1e30

LANES = 128
N_SELF_GROUPS = SELF_WIDTH // LANES
QK_SCALE = HEAD_DIM ** -0.5
LOG2E = math.log2(math.e)
KV_CHUNK = 256
SUM_ROWS = 16

F32 = jnp.float32
BF16 = jnp.bfloat16

V7X_VMEM_BYTES = 64 * 1024 * 1024
VMEM_LIMIT_BYTES = V7X_VMEM_BYTES * 7 // 8

PROJ_ROWS = 1024
MLP_ROWS = 512
MLP_FF_CHUNK = 512
ATTN_GROUPS = N_SELF_GROUPS
ATTN_TILES_PER_STEP = 4


def _params(*sem):
    return pltpu.CompilerParams(dimension_semantics=sem, vmem_limit_bytes=VMEM_LIMIT_BYTES)


def _proj_kernel(x_ref, g_ref, w_ref, cos_ref, sin_ref, o_ref, *extra_refs,
                 n_rope, scales, col_chunk, with_kmean, vt_start):
    tm = x_ref.shape[0]
    n_out = w_ref.shape[1]
    extra = list(extra_refs)
    km_ref = extra.pop(0) if with_kmean else None
    vt_ref = extra.pop(0) if vt_start is not None else None
    x = x_ref[...]
    ms = jnp.mean(x * x, axis=-1, keepdims=True)
    h = (x * lax.rsqrt(ms + EPS) * g_ref[...]).astype(BF16)
    lane = lax.broadcasted_iota(jnp.int32, (tm, LANES), 1)
    first_half = (lane & (HEAD_DIM - 1)) < HEAD_DIM // 2
    if n_rope:
        cos = cos_ref[...]
        sin = sin_ref[...]
    for c0 in range(0, n_out, col_chunk):
        z = jnp.dot(h, w_ref[:, c0:c0 + col_chunk], preferred_element_type=F32)
        for jj in range(col_chunk // LANES):
            j = c0 // LANES + jj
            blk = z[:, jj * LANES:(jj + 1) * LANES]
            if j < n_rope:
                swap = jnp.where(first_half, pltpu.roll(blk, LANES - 32, 1), pltpu.roll(blk, 32, 1))
                blk = blk * cos + swap * sin
                if km_ref is not None:
                    km_ref[0, :, j * LANES:(j + 1) * LANES] = jnp.mean(
                        blk.reshape(tm // MOBA_BLOCK, MOBA_BLOCK, LANES), axis=1)
            if j in scales:
                blk = blk * scales[j]
            o_ref[:, j * LANES:(j + 1) * LANES] = blk.astype(o_ref.dtype)
            if vt_ref is not None and vt_start <= j < vt_start + vt_ref.shape[1] // LANES:
                g = j - vt_start
                for c in range(tm // KV_CHUNK):
                    vt_ref[c, g * LANES:(g + 1) * LANES, :] = (
                        blk[c * KV_CHUNK:(c + 1) * KV_CHUNK, :].T.astype(BF16))


def _proj(x2d, g, w, cos_t, sin_t, *, n_rope, scales, with_kmean, vt_start, seq, tm, name,
          vt_width=SELF_WIDTH):
    t, d = x2d.shape
    n_out = w.shape[1]
    n_pos_blocks = seq // tm
    col_chunk = 512 if n_out % 512 == 0 else 256
    out_shape = [jax.ShapeDtypeStruct((t, n_out), BF16)]
    out_specs = [pl.BlockSpec((tm, n_out), lambda i: (i, 0))]
    if with_kmean:
        out_shape.append(jax.ShapeDtypeStruct((t // tm, tm // MOBA_BLOCK, n_rope * LANES), F32))
        out_specs.append(pl.BlockSpec((1, tm // MOBA_BLOCK, n_rope * LANES), lambda i: (i, 0, 0)))
    if vt_start is not None:
        out_shape.append(jax.ShapeDtypeStruct((t // KV_CHUNK, vt_width, KV_CHUNK), BF16))
        out_specs.append(pl.BlockSpec((tm // KV_CHUNK, vt_width, KV_CHUNK), lambda i: (i, 0, 0)))
    kern = functools.partial(_proj_kernel, n_rope=n_rope, scales=dict(scales), col_chunk=col_chunk,
                             with_kmean=with_kmean, vt_start=vt_start)
    return pl.pallas_call(
        kern,
        out_shape=out_shape,
        grid=(t // tm,),
        in_specs=[
            pl.BlockSpec((tm, d), lambda i: (i, 0)),
            pl.BlockSpec((1, d), lambda i: (0, 0)),
            pl.BlockSpec((d, n_out), lambda i: (0, 0)),
            pl.BlockSpec((tm, LANES), lambda i: (i % n_pos_blocks, 0)),
            pl.BlockSpec((tm, LANES), lambda i: (i % n_pos_blocks, 0)),
        ],
        out_specs=out_specs,
        compiler_params=_params("parallel"),
        name=name,
    )(x2d, g.reshape(1, d), w, cos_t, sin_t)


_NT = (((1,), (1,)), ((), ()))


def _split_maps(q):
    lane = lax.broadcasted_iota(jnp.int32, q.shape, 1)
    zero = jnp.zeros_like(q)
    return [jnp.where(lane < HEAD_DIM, q, zero), jnp.where(lane >= HEAD_DIM, q, zero)]


def _causal_mask(st):
    key = lax.broadcasted_iota(jnp.int32, st.shape, 0)
    qry = lax.broadcasted_iota(jnp.int32, st.shape, 1)
    return jnp.where(key <= qry, st, NEG)


def _flash_pipeline(n_groups, n_past, tq, scores_fn, vt_fn, bias_fn, finalize_fn, s_scr, acc_scr):
    streams = [(g, h) for g in range(n_groups) for h in range(2)]

    def produce(j, buf, only=None):
        col_max = []
        for i, (g, h) in enumerate(streams):
            if only is not None and i != only:
                continue
            st = scores_fn(g, h, j)
            s_scr[buf, g, h] = st
            col_max.append(jnp.max(st, axis=0, keepdims=True))
        return tuple(col_max)

    ones_rows = jnp.ones((SUM_ROWS, tq), BF16)

    def consume(j, buf, stats, col_max, diagonal, only=None):
        out, accs = [], []
        for i, (g, h) in enumerate(streams):
            if only is not None and i != only:
                continue
            st = s_scr[buf, g, h]
            if diagonal:
                st = _causal_mask(st)
                m_cur = jnp.max(st, axis=0, keepdims=True)
            else:
                m_cur = col_max[i]
            if bias_fn is not None:
                bias = bias_fn(g, h, j)
                m_cur = m_cur + bias
            m_next = jnp.maximum(stats[i], m_cur)
            alpha = jnp.exp2(stats[i] - m_next)
            shift = m_next if bias_fn is None else m_next - bias
            p = jnp.exp2((st - shift).astype(BF16))
            acc = alpha * acc_scr[g, h] + jnp.dot(
                jnp.concatenate([vt_fn(g, h, j), ones_rows], axis=0), p, preferred_element_type=F32)
            if diagonal:
                accs.append(acc)
                if h == 1:
                    finalize_fn(g, accs[-2:])
            else:
                acc_scr[g, h] = acc
            out.append(m_next)
        return tuple(out)

    def step(j, buf, carry):
        stats, nxt = [], []
        for i in range(len(streams)):
            nxt += produce(j + 1, 1 - buf, only=i)
            stats += consume(j, buf, carry[0], carry[1], False, only=i)
        return tuple(stats), tuple(nxt)

    def pair(i, carry):
        return step(2 * i + 1, 1, step(2 * i, 0, carry))

    def quad(i, carry):
        return pair(2 * i + 1, pair(2 * i, carry))

    def octet(i, carry):
        return quad(2 * i + 1, quad(2 * i, carry))

    acc_scr[...] = jnp.zeros(acc_scr.shape, F32)
    init = tuple(jnp.full((1, tq), NEG, F32) for _ in streams)
    n_octets = n_past // 8
    carry = lax.fori_loop(0, n_octets, octet, (init, produce(0, 0)))
    carry = lax.fori_loop(2 * n_octets, n_past // 4, quad, carry)
    carry = lax.fori_loop(2 * (n_past // 4), n_past // 2, pair, carry)

    def odd_tail(carry):
        stats, col_max = step(n_past - 1, 0, carry)
        return consume(n_past, 1, stats, col_max, True)

    def even_tail(carry):
        return consume(n_past, 0, carry[0], carry[1], True)

    return lax.cond(n_past % 2 == 1, odd_tail, even_tail, carry)


def _diff_attn_kernel(lam_ref, g_ref, q_ref, k_ref, vt_ref, o_ref, s_scr, acc_scr, *, lam_init):
    tq = KV_CHUNK
    n_groups = q_ref.shape[1] // LANES
    cols = [slice(g * LANES, (g + 1) * LANES) for g in range(n_groups)]
    lp = lam_ref[...]
    lam = (jnp.exp(jnp.sum(lp[0:1] * lp[1:2], axis=1, keepdims=True))
           - jnp.exp(jnp.sum(lp[2:3] * lp[3:4], axis=1, keepdims=True)) + lam_init)

    tiles = q_ref.shape[0] // tq
    first_tile = pl.program_id(2) * tiles

    def query_tile(local, carry):
        qi = first_tile + local
        rows = pl.ds(pl.multiple_of(local * tq, tq), tq)
        qs = [_split_maps(q_ref[rows, c]) for c in cols]

        def scores(g, h, j):
            start = pl.multiple_of(j * tq, tq)
            return lax.dot_general(k_ref[pl.ds(start, tq), cols[g]], qs[g][h], _NT,
                                   preferred_element_type=F32)

        def finalize(g, accs):
            o1, o2 = [a[:LANES] * (1.0 / a[LANES:LANES + 1]) for a in accs]
            od_t = o1 - lam * o2
            ms = jnp.mean(od_t * od_t, axis=0, keepdims=True)
            y = (od_t * lax.rsqrt(ms + EPS)).T * g_ref[...]
            o_ref[rows, cols[g]] = (y * (1.0 - lam_init)).astype(o_ref.dtype)

        _flash_pipeline(n_groups, qi, tq, scores, lambda g, h, j: vt_ref[j, cols[g], :], None,
                        finalize, s_scr, acc_scr)
        return carry

    lax.fori_loop(0, tiles, query_tile, 0)


def _diff_attention(z, vt, lam_p, subln, *, batch, seq, lam_init):
    t = z.shape[0]
    tq = KV_CHUNK
    n_groups = ATTN_GROUPS
    w = n_groups * LANES
    n_steps = N_SELF_GROUPS // n_groups
    rows = ATTN_TILES_PER_STEP * tq
    per_batch = seq // rows
    kern = functools.partial(_diff_attn_kernel, lam_init=lam_init)
    resident = dict(pipeline_mode=pl.Buffered(1))
    return pl.pallas_call(
        kern,
        out_shape=jax.ShapeDtypeStruct((t, SELF_WIDTH), BF16),
        grid=(batch, n_steps, per_batch),
        in_specs=[
            pl.BlockSpec((4, HEAD_DIM), lambda b, h, i: (0, 0)),
            pl.BlockSpec((1, LANES), lambda b, h, i: (0, 0)),
            pl.BlockSpec((rows, w), lambda b, h, i: (b * per_batch + i, h)),
            pl.BlockSpec((seq, w), lambda b, h, i: (b, n_steps + h)),
            pl.BlockSpec((seq // KV_CHUNK, w, KV_CHUNK), lambda b, h, i: (b, h, 0), **resident),
        ],
        out_specs=pl.BlockSpec((rows, w), lambda b, h, i: (b * per_batch + i, h)),
        scratch_shapes=[pltpu.VMEM((2, n_groups, 2, tq, tq), F32),
                        pltpu.VMEM((n_groups, 2, LANES + SUM_ROWS, tq), F32)],
        compiler_params=_params("parallel", "parallel", "parallel"),
        name="diff_attn",
    )(lam_p, subln.reshape(1, LANES), z, z, vt)


def _moba_select_bias(qs, km, qi):
    n_blk = km.shape[0]
    parts, rest = [], km
    for _ in range(3):
        part = rest.astype(BF16)
        parts.append(part)
        rest = rest - part.astype(F32)
    terms = lax.dot_general(jnp.concatenate(parts, axis=0), qs, _NT, preferred_element_type=F32)
    gate = (terms[2 * n_blk:] + terms[n_blk:2 * n_blk]) + terms[:n_blk]
    blk = lax.broadcasted_iota(jnp.int32, gate.shape, 0)
    past = blk < qi
    gm = jnp.where(past, gate, NEG)
    sel = jnp.zeros(gate.shape, F32)
    for _ in range(MOBA_TOPK):
        mx = jnp.max(gm, axis=0, keepdims=True)
        first = jnp.min(jnp.where(gm == mx, blk, n_blk), axis=0, keepdims=True)
        pick = blk == first
        sel = jnp.where(pick, 1.0, sel)
        gm = jnp.where(pick, -jnp.inf, gm)
    return jnp.where((past & (sel > 0.5)) | (blk == qi), 0.0, 2.0 * NEG)


def _moba_attn_kernel(q_ref, k_ref, vt_ref, km_ref, o_ref, s_scr, acc_scr, bias_scr):
    tq = MOBA_BLOCK
    n_groups = q_ref.shape[1] // LANES
    cols = [slice(g * LANES, (g + 1) * LANES) for g in range(n_groups)]

    tiles = q_ref.shape[0] // tq
    first_tile = pl.program_id(2) * tiles

    def query_tile(local, carry):
        qi = first_tile + local
        rows = pl.ds(pl.multiple_of(local * tq, tq), tq)
        qs = [_split_maps(q_ref[rows, c]) for c in cols]
        for g, c in enumerate(cols):
            for h in range(2):
                bias_scr[g, h] = _moba_select_bias(qs[g][h], km_ref[:, c], qi)

        def scores(g, h, j):
            start = pl.multiple_of(j * tq, tq)
            return lax.dot_general(k_ref[pl.ds(start, tq), cols[g]], qs[g][h], _NT,
                                   preferred_element_type=F32)

        def values_t(g, h, j):
            return vt_ref[j, pl.ds(g * LANES + h * HEAD_DIM, HEAD_DIM), :]

        def finalize(g, accs):
            heads = [a[:HEAD_DIM] * (1.0 / a[HEAD_DIM:HEAD_DIM + 1]) for a in accs]
            o_ref[rows, cols[g]] = jnp.concatenate(heads, axis=0).T.astype(o_ref.dtype)

        _flash_pipeline(n_groups, qi, tq, scores, values_t,
                        lambda g, h, j: bias_scr[g, h, pl.ds(j, 1), :], finalize, s_scr, acc_scr)
        return carry

    lax.fori_loop(0, tiles, query_tile, 0)


def _moba_attention(zq, zkv, vt, kmean, *, batch, seq):
    t = zq.shape[0]
    tq = MOBA_BLOCK
    nq = seq // tq
    n_groups = ATTN_GROUPS
    w = n_groups * LANES
    n_steps = N_SELF_GROUPS // n_groups
    rows = ATTN_TILES_PER_STEP * tq
    per_batch = seq // rows
    resident = dict(pipeline_mode=pl.Buffered(1))
    return pl.pallas_call(
        _moba_attn_kernel,
        out_shape=jax.ShapeDtypeStruct((t, SELF_WIDTH), BF16),
        grid=(batch, n_steps, per_batch),
        in_specs=[
            pl.BlockSpec((rows, w), lambda b, h, i: (b * per_batch + i, h)),
            pl.BlockSpec((seq, w), lambda b, h, i: (b, h)),
            pl.BlockSpec((seq // KV_CHUNK, w, KV_CHUNK), lambda b, h, i: (b, h, 0), **resident),
            pl.BlockSpec((None, nq, w), lambda b, h, i: (b, 0, h)),
        ],
        out_specs=pl.BlockSpec((rows, w), lambda b, h, i: (b * per_batch + i, h)),
        scratch_shapes=[pltpu.VMEM((2, n_groups, 2, tq, tq), F32),
                        pltpu.VMEM((n_groups, 2, HEAD_DIM + SUM_ROWS, tq), F32),
                        pltpu.VMEM((n_groups, 2, nq, tq), F32)],
        compiler_params=_params("parallel", "parallel", "parallel"),
        name="moba_attn",
    )(zq, zkv, vt, kmean)


def _attn_out_kernel(x_ref, os_ref, qm_ref, mk_ref, mvt_ref, w_ref, o_ref):
    mem_len = mk_ref.shape[0]
    mk = mk_ref[...]
    lane = lax.broadcasted_iota(jnp.int32, mk.shape, 1)
    zero = jnp.zeros_like(mk)
    mk_heads = jnp.concatenate(
        [jnp.where((lane >= h * HEAD_DIM) & (lane < (h + 1) * HEAD_DIM), mk, zero)
         for h in range(N_MEM_HEADS)], axis=0)
    s = lax.dot_general(mk_heads, qm_ref[...], _NT, preferred_element_type=F32)
    ones_rows = jnp.ones((SUM_ROWS, mem_len), BF16)
    heads = []
    for h in range(N_MEM_HEADS):
        sh = s[h * mem_len:(h + 1) * mem_len]
        p = jnp.exp2((sh - jnp.max(sh, axis=0, keepdims=True)).astype(BF16))
        vt_ones = jnp.concatenate([mvt_ref[h * HEAD_DIM:(h + 1) * HEAD_DIM, :], ones_rows], axis=0)
        oh = jnp.dot(vt_ones, p, preferred_element_type=F32)
        heads.append(oh[:HEAD_DIM] * (1.0 / oh[HEAD_DIM:HEAD_DIM + 1]))
    o_mem = jnp.concatenate(heads, axis=0).T.astype(BF16)
    y = jnp.dot(os_ref[...], w_ref[:SELF_WIDTH, :], preferred_element_type=F32)
    y = y + jnp.dot(o_mem, w_ref[SELF_WIDTH:, :], preferred_element_type=F32)
    o_ref[...] = x_ref[...] + y


def _attn_out(x2d, o_self, zq, qm_block, memkv, memvt, w_out, *, seq, mem_len, name):
    t, d = x2d.shape
    tm = PROJ_ROWS
    per_batch = seq // tm
    return pl.pallas_call(
        _attn_out_kernel,
        out_shape=jax.ShapeDtypeStruct((t, d), F32),
        grid=(t // tm,),
        in_specs=[
            pl.BlockSpec((tm, d), lambda i: (i, 0)),
            pl.BlockSpec((tm, SELF_WIDTH), lambda i: (i, 0)),
            pl.BlockSpec((tm, MEM_WIDTH), lambda i: (i, qm_block)),
            pl.BlockSpec((mem_len, MEM_WIDTH), lambda i: (i // per_batch, 0)),
            pl.BlockSpec((None, MEM_WIDTH, mem_len), lambda i: (i // per_batch, 0, 0)),
            pl.BlockSpec((d, d), lambda i: (0, 0)),
        ],
        out_specs=pl.BlockSpec((tm, d), lambda i: (i, 0)),
        compiler_params=_params("parallel"),
        name=name,
    )(x2d, o_self, zq, memkv, memvt, w_out)


def _mlp_kernel(x_ref, g_ref, wu_ref, wd_ref, gf_ref, o_ref, u_scr, *, final_norm, tf):
    x = x_ref[...]
    ms = jnp.mean(x * x, axis=-1, keepdims=True)
    h = (x * lax.rsqrt(ms + EPS) * g_ref[...]).astype(BF16)
    for c0 in range(0, wu_ref.shape[1], tf):
        u = jnp.maximum(jnp.dot(h, wu_ref[:, c0:c0 + tf], preferred_element_type=F32), 0.0)
        u_scr[:, c0:c0 + tf] = (u * u).astype(BF16)
    y = x + jnp.dot(u_scr[...], wd_ref[...], preferred_element_type=F32)
    if final_norm:
        ms = jnp.mean(y * y, axis=-1, keepdims=True)
        y = y * lax.rsqrt(ms + EPS) * gf_ref[...]
    o_ref[...] = y


def _mlp(x2d, g, w_up, w_down, g_final, *, final_norm, name):
    t, d = x2d.shape
    dff = w_up.shape[1]
    tm = MLP_ROWS
    kern = functools.partial(_mlp_kernel, final_norm=final_norm, tf=MLP_FF_CHUNK)
    resident = dict(pipeline_mode=pl.Buffered(1))
    return pl.pallas_call(
        kern,
        out_shape=jax.ShapeDtypeStruct((t, d), F32),
        grid=(t // tm,),
        in_specs=[
            pl.BlockSpec((tm, d), lambda i: (i, 0)),
            pl.BlockSpec((1, d), lambda i: (0, 0)),
            pl.BlockSpec((d, dff), lambda i: (0, 0), **resident),
            pl.BlockSpec((dff, d), lambda i: (0, 0), **resident),
            pl.BlockSpec((1, d), lambda i: (0, 0)),
        ],
        out_specs=pl.BlockSpec((tm, d), lambda i: (i, 0)),
        scratch_shapes=[pltpu.VMEM((tm, dff), BF16)],
        compiler_params=_params("parallel"),
        name=name,
    )(x2d, g.reshape(1, d), w_up, w_down, g_final.reshape(1, d))


def _rope_tables(seq):
    half = HEAD_DIM // 2
    inv = 1.0 / (ROPE_THETA ** (jnp.arange(half, dtype=F32) / half))
    ang = jnp.arange(seq, dtype=F32)[:, None] * inv[None, :]
    cos, sin = jnp.cos(ang), jnp.sin(ang)
    reps = LANES // half
    cos_t = jnp.tile(cos, (1, reps))
    sign = jnp.tile(jnp.concatenate([-jnp.ones((half,), F32), jnp.ones((half,), F32)]), LANES // HEAD_DIM)
    sin_t = jnp.tile(sin, (1, reps)) * sign[None, :]
    return cos_t, sin_t


def kernel(x, mem, a_norm_attn, a_w_in, a_lambda, a_subln, a_mem_norm, a_w_mem_kv, a_w_out, a_norm_mlp, a_w_up, a_w_down, kv_norm, w_kv, b_norm_attn, b_w_in, b_mem_norm, b_w_mem_kv, b_w_out, b_norm_mlp, b_w_up, b_w_down, final_norm):
    batch, seq, d = x.shape
    mem_len = mem.shape[1]
    t = batch * seq
    x2d = x.reshape(t, d)
    mem2d = mem.reshape(batch * mem_len, d)
    cos_t, sin_t = _rope_tables(seq)
    bf = lambda w: w.astype(BF16)
    ng = N_SELF_GROUPS
    q_scale = QK_SCALE * LOG2E
    mem_groups = MEM_WIDTH // LANES

    def memkv(g, w, name):
        return _proj(mem2d, g, bf(w), cos_t, sin_t, n_rope=0, scales={}, with_kmean=False,
                     vt_start=mem_groups, vt_width=MEM_WIDTH, seq=mem_len, tm=mem_len, name=name)

    lam_init = 0.8 - 0.6 * math.exp(-0.3 * 0)
    za, vt_a = _proj(x2d, a_norm_attn[0], bf(a_w_in[0]), cos_t, sin_t, n_rope=2 * ng,
                     scales={j: q_scale for j in (*range(ng), 3 * ng, 3 * ng + 1)},
                     with_kmean=False, vt_start=2 * ng, seq=seq, tm=PROJ_ROWS, name="a_proj")
    mkv_a, mvt_a = memkv(a_mem_norm[0], a_w_mem_kv[0], "a_memkv")
    o_self = _diff_attention(za, vt_a, a_lambda[0], a_subln[0], batch=batch, seq=seq,
                             lam_init=lam_init)
    x2d = _attn_out(x2d, o_self, za, 3 * SELF_WIDTH // MEM_WIDTH, mkv_a, mvt_a, bf(a_w_out[0]),
                    seq=seq, mem_len=mem_len, name="a_attn_out")
    x2d = _mlp(x2d, a_norm_mlp[0], bf(a_w_up[0]), bf(a_w_down[0]), final_norm,
               final_norm=False, name="a_mlp")

    zkv, kmean, vt_b = _proj(x2d, kv_norm, bf(w_kv), cos_t, sin_t, n_rope=ng, scales={},
                             with_kmean=True, vt_start=ng, seq=seq, tm=PROJ_ROWS, name="b_kvproj")
    kmean = kmean.reshape(batch, seq // MOBA_BLOCK, SELF_WIDTH)
    zb, = _proj(x2d, b_norm_attn[0], bf(b_w_in[0]), cos_t, sin_t, n_rope=ng,
                scales={j: q_scale for j in range(ng + mem_groups)}, with_kmean=False, vt_start=None,
                seq=seq, tm=PROJ_ROWS, name="b_qproj")
    mkv_b, mvt_b = memkv(b_mem_norm[0], b_w_mem_kv[0], "b_memkv")
    o_self = _moba_attention(zb, zkv, vt_b, kmean, batch=batch, seq=seq)
    x2d = _attn_out(x2d, o_self, zb, SELF_WIDTH // MEM_WIDTH, mkv_b, mvt_b, bf(b_w_out[0]),
                    seq=seq, mem_len=mem_len, name="b_attn_out")
    x2d = _mlp(x2d, b_norm_mlp[0], bf(b_w_up[0]), bf(b_w_down[0]), final_norm,
               final_norm=True, name="b_mlp")
    return x2d.reshape(batch, seq, d)
```

```python
import functools
import math

import jax
import jax.numpy as jnp
from jax import lax
from jax.experimental import pallas as pl
from jax.experimental.pallas import tpu as pltpu

HEAD_DIM = 64
SELF_WIDTH = 768
MEM_WIDTH = 256
N_MEM_HEADS = 4
MOBA_BLOCK = 256
MOBA_TOPK = 3
ROPE_THETA = 10000.0
EPS = 1e-6
NEG = -1e30

LANES = 128
N_SELF_GROUPS = SELF_WIDTH // LANES
QK_SCALE = HEAD_DIM ** -0.5
LOG2E = math.log2(math.e)
KV_CHUNK = 256
SUM_ROWS = 16

F32 = jnp.float32
BF16 = jnp.bfloat16

V7X_VMEM_BYTES = 64 * 1024 * 1024
VMEM_LIMIT_BYTES = V7X_VMEM_BYTES * 7 // 8

PROJ_ROWS = 1024
MLP_ROWS = 512
MLP_FF_CHUNK = 512
ATTN_GROUPS = N_SELF_GROUPS
ATTN_TILES_PER_STEP = 4


def _params(*sem):
    return pltpu.CompilerParams(dimension_semantics=sem, vmem_limit_bytes=VMEM_LIMIT_BYTES)


def _proj_kernel(x_ref, g_ref, w_ref, cos_ref, sin_ref, o_ref, *extra_refs,
                 n_rope, scales, col_chunk, with_kmean, vt_start):
    tm = x_ref.shape[0]
    n_out = w_ref.shape[1]
    extra = list(extra_refs)
    km_ref = extra.pop(0) if with_kmean else None
    vt_ref = extra.pop(0) if vt_start is not None else None
    x = x_ref[...]
    ms = jnp.mean(x * x, axis=-1, keepdims=True)
    h = (x * lax.rsqrt(ms + EPS) * g_ref[...]).astype(BF16)
    lane = lax.broadcasted_iota(jnp.int32, (tm, LANES), 1)
    first_half = (lane & (HEAD_DIM - 1)) < HEAD_DIM // 2
    if n_rope:
        cos = cos_ref[...]
        sin = sin_ref[...]
    for c0 in range(0, n_out, col_chunk):
        z = jnp.dot(h, w_ref[:, c0:c0 + col_chunk], preferred_element_type=F32)
        for jj in range(col_chunk // LANES):
            j = c0 // LANES + jj
            blk = z[:, jj * LANES:(jj + 1) * LANES]
            if j < n_rope:
                swap = jnp.where(first_half, pltpu.roll(blk, LANES - 32, 1), pltpu.roll(blk, 32, 1))
                blk = blk * cos + swap * sin
                if km_ref is not None:
                    km_ref[0, :, j * LANES:(j + 1) * LANES] = jnp.mean(
                        blk.reshape(tm // MOBA_BLOCK, MOBA_BLOCK, LANES), axis=1)
            if j in scales:
                blk = blk * scales[j]
            o_ref[:, j * LANES:(j + 1) * LANES] = blk.astype(o_ref.dtype)
            if vt_ref is not None and vt_start <= j < vt_start + vt_ref.shape[1] // LANES:
                g = j - vt_start
                for c in range(tm // KV_CHUNK):
                    vt_ref[c, g * LANES:(g + 1) * LANES, :] = (
                        blk[c * KV_CHUNK:(c + 1) * KV_CHUNK, :].T.astype(BF16))


def _proj(x2d, g, w, cos_t, sin_t, *, n_rope, scales, with_kmean, vt_start, seq, tm, name,
          vt_width=SELF_WIDTH):
    t, d = x2d.shape
    n_out = w.shape[1]
    n_pos_blocks = seq // tm
    col_chunk = 512 if n_out % 512 == 0 else 256
    out_shape = [jax.ShapeDtypeStruct((t, n_out), BF16)]
    out_specs = [pl.BlockSpec((tm, n_out), lambda i: (i, 0))]
    if with_kmean:
        out_shape.append(jax.ShapeDtypeStruct((t // tm, tm // MOBA_BLOCK, n_rope * LANES), F32))
        out_specs.append(pl.BlockSpec((1, tm // MOBA_BLOCK, n_rope * LANES), lambda i: (i, 0, 0)))
    if vt_start is not None:
        out_shape.append(jax.ShapeDtypeStruct((t // KV_CHUNK, vt_width, KV_CHUNK), BF16))
        out_specs.append(pl.BlockSpec((tm // KV_CHUNK, vt_width, KV_CHUNK), lambda i: (i, 0, 0)))
    kern = functools.partial(_proj_kernel, n_rope=n_rope, scales=dict(scales), col_chunk=col_chunk,
                             with_kmean=with_kmean, vt_start=vt_start)
    return pl.pallas_call(
        kern,
        out_shape=out_shape,
        grid=(t // tm,),
        in_specs=[
            pl.BlockSpec((tm, d), lambda i: (i, 0)),
            pl.BlockSpec((1, d), lambda i: (0, 0)),
            pl.BlockSpec((d, n_out), lambda i: (0, 0)),
            pl.BlockSpec((tm, LANES), lambda i: (i % n_pos_blocks, 0)),
            pl.BlockSpec((tm, LANES), lambda i: (i % n_pos_blocks, 0)),
        ],
        out_specs=out_specs,
        compiler_params=_params("parallel"),
        name=name,
    )(x2d, g.reshape(1, d), w, cos_t, sin_t)


_NT = (((1,), (1,)), ((), ()))


def _split_maps(q):
    lane = lax.broadcasted_iota(jnp.int32, q.shape, 1)
    zero = jnp.zeros_like(q)
    return [jnp.where(lane < HEAD_DIM, q, zero), jnp.where(lane >= HEAD_DIM, q, zero)]


def _causal_mask(st):
    key = lax.broadcasted_iota(jnp.int32, st.shape, 0)
    qry = lax.broadcasted_iota(jnp.int32, st.shape, 1)
    return jnp.where(key <= qry, st, NEG)


def _flash_pipeline(n_groups, n_past, tq, scores_fn, vt_fn, bias_fn, finalize_fn, s_scr, acc_scr):
    streams = [(g, h) for g in range(n_groups) for h in range(2)]

    def produce(j, buf, only=None):
        col_max = []
        for i, (g, h) in enumerate(streams):
            if only is not None and i != only:
                continue
            st = scores_fn(g, h, j)
            s_scr[buf, g, h] = st
            col_max.append(jnp.max(st, axis=0, keepdims=True))
        return tuple(col_max)

    ones_rows = jnp.ones((SUM_ROWS, tq), BF16)

    def consume(j, buf, stats, col_max, diagonal, only=None):
        out, accs = [], []
        for i, (g, h) in enumerate(streams):
            if only is not None and i != only:
                continue
            st = s_scr[buf, g, h]
            if diagonal:
                st = _causal_mask(st)
                m_cur = jnp.max(st, axis=0, keepdims=True)
            else:
                m_cur = col_max[i]
            if bias_fn is not None:
                bias = bias_fn(g, h, j)
                m_cur = m_cur + bias
            m_next = jnp.maximum(stats[i], m_cur)
            alpha = jnp.exp2(stats[i] - m_next)
            shift = m_next if bias_fn is None else m_next - bias
            p = jnp.exp2((st - shift).astype(BF16))
            acc = alpha * acc_scr[g, h] + jnp.dot(
                jnp.concatenate([vt_fn(g, h, j), ones_rows], axis=0), p, preferred_element_type=F32)
            if diagonal:
                accs.append(acc)
                if h == 1:
                    finalize_fn(g, accs[-2:])
            else:
                acc_scr[g, h] = acc
            out.append(m_next)
        return tuple(out)

    def step(j, buf, carry):
        stats, nxt = [], []
        for i in range(len(streams)):
            nxt += produce(j + 1, 1 - buf, only=i)
            stats += consume(j, buf, carry[0], carry[1], False, only=i)
        return tuple(stats), tuple(nxt)

    def pair(i, carry):
        return step(2 * i + 1, 1, step(2 * i, 0, carry))

    def quad(i, carry):
        return pair(2 * i + 1, pair(2 * i, carry))

    def octet(i, carry):
        return quad(2 * i + 1, quad(2 * i, carry))

    acc_scr[...] = jnp.zeros(acc_scr.shape, F32)
    init = tuple(jnp.full((1, tq), NEG, F32) for _ in streams)
    carry = lax.fori_loop(0, n_past // 4, quad, (init, produce(0, 0)))
    carry = lax.fori_loop(2 * (n_past // 4), n_past // 2, pair, carry)

    def odd_tail(carry):
        stats, col_max = step(n_past - 1, 0, carry)
        return consume(n_past, 1, stats, col_max, True)

    def even_tail(carry):
        return consume(n_past, 0, carry[0], carry[1], True)

    return lax.cond(n_past % 2 == 1, odd_tail, even_tail, carry)


def _diff_attn_kernel(lam_ref, g_ref, q_ref, k_ref, vt_ref, o_ref, s_scr, acc_scr, *, lam_init):
    tq = KV_CHUNK
    n_groups = q_ref.shape[1] // LANES
    cols = [slice(g * LANES, (g + 1) * LANES) for g in range(n_groups)]
    lp = lam_ref[...]
    lam = (jnp.exp(jnp.sum(lp[0:1] * lp[1:2], axis=1, keepdims=True))
           - jnp.exp(jnp.sum(lp[2:3] * lp[3:4], axis=1, keepdims=True)) + lam_init)

    tiles = q_ref.shape[0] // tq
    first_tile = pl.program_id(2) * tiles

    def query_tile(local, carry):
        qi = first_tile + local
        rows = pl.ds(pl.multiple_of(local * tq, tq), tq)
        qs = [_split_maps(q_ref[rows, c]) for c in cols]

        def scores(g, h, j):
            start = pl.multiple_of(j * tq, tq)
            return lax.dot_general(k_ref[pl.ds(start, tq), cols[g]], qs[g][h], _NT,
                                   preferred_element_type=F32)

        def finalize(g, accs):
            o1, o2 = [a[:LANES] * (1.0 / a[LANES:LANES + 1]) for a in accs]
            od_t = o1 - lam * o2
            ms = jnp.mean(od_t * od_t, axis=0, keepdims=True)
            y = (od_t * lax.rsqrt(ms + EPS)).T * g_ref[...]
            o_ref[rows, cols[g]] = (y * (1.0 - lam_init)).astype(o_ref.dtype)

        _flash_pipeline(n_groups, qi, tq, scores, lambda g, h, j: vt_ref[j, cols[g], :], None,
                        finalize, s_scr, acc_scr)
        return carry

    lax.fori_loop(0, tiles, query_tile, 0)


def _diff_attention(z, vt, lam_p, subln, *, batch, seq, lam_init):
    t = z.shape[0]
    tq = KV_CHUNK
    n_groups = ATTN_GROUPS
    w = n_groups * LANES
    n_steps = N_SELF_GROUPS // n_groups
    rows = ATTN_TILES_PER_STEP * tq
    per_batch = seq // rows
    kern = functools.partial(_diff_attn_kernel, lam_init=lam_init)
    resident = dict(pipeline_mode=pl.Buffered(1))
    return pl.pallas_call(
        kern,
        out_shape=jax.ShapeDtypeStruct((t, SELF_WIDTH), BF16),
        grid=(batch, n_steps, per_batch),
        in_specs=[
            pl.BlockSpec((4, HEAD_DIM), lambda b, h, i: (0, 0)),
            pl.BlockSpec((1, LANES), lambda b, h, i: (0, 0)),
            pl.BlockSpec((rows, w), lambda b, h, i: (b * per_batch + i, h)),
            pl.BlockSpec((seq, w), lambda b, h, i: (b, n_steps + h)),
            pl.BlockSpec((seq // KV_CHUNK, w, KV_CHUNK), lambda b, h, i: (b, h, 0), **resident),
        ],
        out_specs=pl.BlockSpec((rows, w), lambda b, h, i: (b * per_batch + i, h)),
        scratch_shapes=[pltpu.VMEM((2, n_groups, 2, tq, tq), F32),
                        pltpu.VMEM((n_groups, 2, LANES + SUM_ROWS, tq), F32)],
        compiler_params=_params("parallel", "parallel", "parallel"),
        name="diff_attn",
    )(lam_p, subln.reshape(1, LANES), z, z, vt)


def _moba_select_bias(qs, km, qi):
    n_blk = km.shape[0]
    parts, rest = [], km
    for _ in range(3):
        part = rest.astype(BF16)
        parts.append(part)
        rest = rest - part.astype(F32)
    terms = lax.dot_general(jnp.concatenate(parts, axis=0), qs, _NT, preferred_element_type=F32)
    gate = (terms[2 * n_blk:] + terms[n_blk:2 * n_blk]) + terms[:n_blk]
    blk = lax.broadcasted_iota(jnp.int32, gate.shape, 0)
    past = blk < qi
    gm = jnp.where(past, gate, NEG)
    sel = jnp.zeros(gate.shape, F32)
    for _ in range(MOBA_TOPK):
        mx = jnp.max(gm, axis=0, keepdims=True)
        first = jnp.min(jnp.where(gm == mx, blk, n_blk), axis=0, keepdims=True)
        pick = blk == first
        sel = jnp.where(pick, 1.0, sel)
        gm = jnp.where(pick, -jnp.inf, gm)
    return jnp.where((past & (sel > 0.5)) | (blk == qi), 0.0, 2.0 * NEG)


def _moba_attn_kernel(q_ref, k_ref, vt_ref, km_ref, o_ref, s_scr, acc_scr, bias_scr):
    tq = MOBA_BLOCK
    n_groups = q_ref.shape[1] // LANES
    cols = [slice(g * LANES, (g + 1) * LANES) for g in range(n_groups)]

    tiles = q_ref.shape[0] // tq
    first_tile = pl.program_id(2) * tiles

    def query_tile(local, carry):
        qi = first_tile + local
        rows = pl.ds(pl.multiple_of(local * tq, tq), tq)
        qs = [_split_maps(q_ref[rows, c]) for c in cols]
        for g, c in enumerate(cols):
            for h in range(2):
                bias_scr[g, h] = _moba_select_bias(qs[g][h], km_ref[:, c], qi)

        def scores(g, h, j):
            start = pl.multiple_of(j * tq, tq)
            return lax.dot_general(k_ref[pl.ds(start, tq), cols[g]], qs[g][h], _NT,
                                   preferred_element_type=F32)

        def values_t(g, h, j):
            return vt_ref[j, pl.ds(g * LANES + h * HEAD_DIM, HEAD_DIM), :]

        def finalize(g, accs):
            heads = [a[:HEAD_DIM] * (1.0 / a[HEAD_DIM:HEAD_DIM + 1]) for a in accs]
            o_ref[rows, cols[g]] = jnp.concatenate(heads, axis=0).T.astype(o_ref.dtype)

        _flash_pipeline(n_groups, qi, tq, scores, values_t,
                        lambda g, h, j: bias_scr[g, h, pl.ds(j, 1), :], finalize, s_scr, acc_scr)
        return carry

    lax.fori_loop(0, tiles, query_tile, 0)


def _moba_attention(zq, zkv, vt, kmean, *, batch, seq):
    t = zq.shape[0]
    tq = MOBA_BLOCK
    nq = seq // tq
    n_groups = ATTN_GROUPS
    w = n_groups * LANES
    n_steps = N_SELF_GROUPS // n_groups
    rows = ATTN_TILES_PER_STEP * tq
    per_batch = seq // rows
    resident = dict(pipeline_mode=pl.Buffered(1))
    return pl.pallas_call(
        _moba_attn_kernel,
        out_shape=jax.ShapeDtypeStruct((t, SELF_WIDTH), BF16),
        grid=(batch, n_steps, per_batch),
        in_specs=[
            pl.BlockSpec((rows, w), lambda b, h, i: (b * per_batch + i, h)),
            pl.BlockSpec((seq, w), lambda b, h, i: (b, h)),
            pl.BlockSpec((seq // KV_CHUNK, w, KV_CHUNK), lambda b, h, i: (b, h, 0), **resident),
            pl.BlockSpec((None, nq, w), lambda b, h, i: (b, 0, h)),
        ],
        out_specs=pl.BlockSpec((rows, w), lambda b, h, i: (b * per_batch + i, h)),
        scratch_shapes=[pltpu.VMEM((2, n_groups, 2, tq, tq), F32),
                        pltpu.VMEM((n_groups, 2, HEAD_DIM + SUM_ROWS, tq), F32),
                        pltpu.VMEM((n_groups, 2, nq, tq), F32)],
        compiler_params=_params("parallel", "parallel", "parallel"),
        name="moba_attn",
    )(zq, zkv, vt, kmean)


def _attn_out_kernel(x_ref, os_ref, qm_ref, mk_ref, mvt_ref, w_ref, o_ref):
    mem_len = mk_ref.shape[0]
    mk = mk_ref[...]
    lane = lax.broadcasted_iota(jnp.int32, mk.shape, 1)
    zero = jnp.zeros_like(mk)
    mk_heads = jnp.concatenate(
        [jnp.where((lane >= h * HEAD_DIM) & (lane < (h + 1) * HEAD_DIM), mk, zero)
         for h in range(N_MEM_HEADS)], axis=0)
    s = lax.dot_general(mk_heads, qm_ref[...], _NT, preferred_element_type=F32)
    ones_rows = jnp.ones((SUM_ROWS, mem_len), BF16)
    heads = []
    for h in range(N_MEM_HEADS):
        sh = s[h * mem_len:(h + 1) * mem_len]
        p = jnp.exp2((sh - jnp.max(sh, axis=0, keepdims=True)).astype(BF16))
        vt_ones = jnp.concatenate([mvt_ref[h * HEAD_DIM:(h + 1) * HEAD_DIM, :], ones_rows], axis=0)
        oh = jnp.dot(vt_ones, p, preferred_element_type=F32)
        heads.append(oh[:HEAD_DIM] * (1.0 / oh[HEAD_DIM:HEAD_DIM + 1]))
    o_mem = jnp.concatenate(heads, axis=0).T.astype(BF16)
    y = jnp.dot(os_ref[...], w_ref[:SELF_WIDTH, :], preferred_element_type=F32)
    y = y + jnp.dot(o_mem, w_ref[SELF_WIDTH:, :], preferred_element_type=F32)
    o_ref[...] = x_ref[...] + y


def _attn_out(x2d, o_self, zq, qm_block, memkv, memvt, w_out, *, seq, mem_len, name):
    t, d = x2d.shape
    tm = PROJ_ROWS
    per_batch = seq // tm
    return pl.pallas_call(
        _attn_out_kernel,
        out_shape=jax.ShapeDtypeStruct((t, d), F32),
        grid=(t // tm,),
        in_specs=[
            pl.BlockSpec((tm, d), lambda i: (i, 0)),
            pl.BlockSpec((tm, SELF_WIDTH), lambda i: (i, 0)),
            pl.BlockSpec((tm, MEM_WIDTH), lambda i: (i, qm_block)),
            pl.BlockSpec((mem_len, MEM_WIDTH), lambda i: (i // per_batch, 0)),
            pl.BlockSpec((None, MEM_WIDTH, mem_len), lambda i: (i // per_batch, 0, 0)),
            pl.BlockSpec((d, d), lambda i: (0, 0)),
        ],
        out_specs=pl.BlockSpec((tm, d), lambda i: (i, 0)),
        compiler_params=_params("parallel"),
        name=name,
    )(x2d, o_self, zq, memkv, memvt, w_out)


def _mlp_kernel(x_ref, g_ref, wu_ref, wd_ref, gf_ref, o_ref, u_scr, *, final_norm, tf):
    x = x_ref[...]
    ms = jnp.mean(x * x, axis=-1, keepdims=True)
    h = (x * lax.rsqrt(ms + EPS) * g_ref[...]).astype(BF16)
    for c0 in range(0, wu_ref.shape[1], tf):
        u = jnp.maximum(jnp.dot(h, wu_ref[:, c0:c0 + tf], preferred_element_type=F32), 0.0)
        u_scr[:, c0:c0 + tf] = (u * u).astype(BF16)
    y = x + jnp.dot(u_scr[...], wd_ref[...], preferred_element_type=F32)
    if final_norm:
        ms = jnp.mean(y * y, axis=-1, keepdims=True)
        y = y * lax.rsqrt(ms + EPS) * gf_ref[...]
    o_ref[...] = y


def _mlp(x2d, g, w_up, w_down, g_final, *, final_norm, name):
    t, d = x2d.shape
    dff = w_up.shape[1]
    tm = MLP_ROWS
    kern = functools.partial(_mlp_kernel, final_norm=final_norm, tf=MLP_FF_CHUNK)
    resident = dict(pipeline_mode=pl.Buffered(1))
    return pl.pallas_call(
        kern,
        out_shape=jax.ShapeDtypeStruct((t, d), F32),
        grid=(t // tm,),
        in_specs=[
            pl.BlockSpec((tm, d), lambda i: (i, 0)),
            pl.BlockSpec((1, d), lambda i: (0, 0)),
            pl.BlockSpec((d, dff), lambda i: (0, 0), **resident),
            pl.BlockSpec((dff, d), lambda i: (0, 0), **resident),
            pl.BlockSpec((1, d), lambda i: (0, 0)),
        ],
        out_specs=pl.BlockSpec((tm, d), lambda i: (i, 0)),
        scratch_shapes=[pltpu.VMEM((tm, dff), BF16)],
        compiler_params=_params("parallel"),
        name=name,
    )(x2d, g.reshape(1, d), w_up, w_down, g_final.reshape(1, d))


def _rope_tables(seq):
    half = HEAD_DIM // 2
    inv = 1.0 / (ROPE_THETA ** (jnp.arange(half, dtype=F32) / half))
    ang = jnp.arange(seq, dtype=F32)[:, None] * inv[None, :]
    cos, sin = jnp.cos(ang), jnp.sin(ang)
    reps = LANES // half
    cos_t = jnp.tile(cos, (1, reps))
    sign = jnp.tile(jnp.concatenate([-jnp.ones((half,), F32), jnp.ones((half,), F32)]), LANES // HEAD_DIM)
    sin_t = jnp.tile(sin, (1, reps)) * sign[None, :]
    return cos_t, sin_t


def kernel(x, mem, a_norm_attn, a_w_in, a_lambda, a_subln, a_mem_norm, a_w_mem_kv, a_w_out, a_norm_mlp, a_w_up, a_w_down, kv_norm, w_kv, b_norm_attn, b_w_in, b_mem_norm, b_w_mem_kv, b_w_out, b_norm_mlp, b_w_up, b_w_down, final_norm):
    batch, seq, d = x.shape
    mem_len = mem.shape[1]
    t = batch * seq
    x2d = x.reshape(t, d)
    mem2d = mem.reshape(batch * mem_len, d)
    cos_t, sin_t = _rope_tables(seq)
    bf = lambda w: w.astype(BF16)
    ng = N_SELF_GROUPS
    q_scale = QK_SCALE * LOG2E
    mem_groups = MEM_WIDTH // LANES

    def memkv(g, w, name):
        return _proj(mem2d, g, bf(w), cos_t, sin_t, n_rope=0, scales={}, with_kmean=False,
                     vt_start=mem_groups, vt_width=MEM_WIDTH, seq=mem_len, tm=mem_len, name=name)

    lam_init = 0.8 - 0.6 * math.exp(-0.3 * 0)
    za, vt_a = _proj(x2d, a_norm_attn[0], bf(a_w_in[0]), cos_t, sin_t, n_rope=2 * ng,
                     scales={j: q_scale for j in (*range(ng), 3 * ng, 3 * ng + 1)},
                     with_kmean=False, vt_start=2 * ng, seq=seq, tm=PROJ_ROWS, name="a_proj")
    mkv_a, mvt_a = memkv(a_mem_norm[0], a_w_mem_kv[0], "a_memkv")
    o_self = _diff_attention(za, vt_a, a_lambda[0], a_subln[0], batch=batch, seq=seq,
                             lam_init=lam_init)
    x2d = _attn_out(x2d, o_self, za, 3 * SELF_WIDTH // MEM_WIDTH, mkv_a, mvt_a, bf(a_w_out[0]),
                    seq=seq, mem_len=mem_len, name="a_attn_out")
    x2d = _mlp(x2d, a_norm_mlp[0], bf(a_w_up[0]), bf(a_w_down[0]), final_norm,
               final_norm=False, name="a_mlp")

    zkv, kmean, vt_b = _proj(x2d, kv_norm, bf(w_kv), cos_t, sin_t, n_rope=ng, scales={},
                             with_kmean=True, vt_start=ng, seq=seq, tm=PROJ_ROWS, name="b_kvproj")
    kmean = kmean.reshape(batch, seq // MOBA_BLOCK, SELF_WIDTH)
    zb, = _proj(x2d, b_norm_attn[0], bf(b_w_in[0]), cos_t, sin_t, n_rope=ng,
                scales={j: q_scale for j in range(ng + mem_groups)}, with_kmean=False, vt_start=None,
                seq=seq, tm=PROJ_ROWS, name="b_qproj")
    mkv_b, mvt_b = memkv(b_mem_norm[0], b_w_mem_kv[0], "b_memkv")
    o_self = _moba_attention(zb, zkv, vt_b, kmean, batch=batch, seq=seq)
    x2d = _attn_out(x2d, o_self, zb, SELF_WIDTH // MEM_WIDTH, mkv_b, mvt_b, bf(b_w_out[0]),
                    seq=seq, mem_len=mem_len, name="b_attn_out")
    x2d = _mlp(x2d, b_norm_mlp[0], bf(b_w_up[0]), bf(b_w_down[0]), final_norm,
               final_norm=True, name="b_mlp")
    return x2d.reshape(batch, seq, d)
```

```python
import functools
import math

import jax
import jax.numpy as jnp
from jax import lax
from jax.experimental import pallas as pl
from jax.experimental.pallas import tpu as pltpu

HEAD_DIM = 64
SELF_WIDTH = 768
MEM_WIDTH = 256
N_MEM_HEADS = 4
MOBA_BLOCK = 256
MOBA_TOPK = 3
ROPE_THETA = 10000.0
EPS = 1e-6
NEG = -1e30

LANES = 128
N_SELF_GROUPS = SELF_WIDTH // LANES
QK_SCALE = HEAD_DIM ** -0.5
LOG2E = math.log2(math.e)
KV_CHUNK = 256
SUM_ROWS = 16

F32 = jnp.float32
BF16 = jnp.bfloat16

V7X_VMEM_BYTES = 64 * 1024 * 1024
VMEM_LIMIT_BYTES = V7X_VMEM_BYTES * 7 // 8

PROJ_ROWS = 1024
MLP_ROWS = 512
MLP_FF_CHUNK = 512
ATTN_GROUPS = N_SELF_GROUPS
ATTN_TILES_PER_STEP = 4


def _params(*sem):
    return pltpu.CompilerParams(dimension_semantics=sem, vmem_limit_bytes=VMEM_LIMIT_BYTES)


def _proj_kernel(x_ref, g_ref, w_ref, cos_ref, sin_ref, o_ref, *extra_refs,
                 n_rope, scales, col_chunk, with_kmean, vt_start):
    tm = x_ref.shape[0]
    n_out = w_ref.shape[1]
    extra = list(extra_refs)
    km_ref = extra.pop(0) if with_kmean else None
    vt_ref = extra.pop(0) if vt_start is not None else None
    x = x_ref[...]
    ms = jnp.mean(x * x, axis=-1, keepdims=True)
    h = (x * lax.rsqrt(ms + EPS) * g_ref[...]).astype(BF16)
    lane = lax.broadcasted_iota(jnp.int32, (tm, LANES), 1)
    first_half = (lane & (HEAD_DIM - 1)) < HEAD_DIM // 2
    if n_rope:
        cos = cos_ref[...]
        sin = sin_ref[...]
    for c0 in range(0, n_out, col_chunk):
        z = jnp.dot(h, w_ref[:, c0:c0 + col_chunk], preferred_element_type=F32)
        for jj in range(col_chunk // LANES):
            j = c0 // LANES + jj
            blk = z[:, jj * LANES:(jj + 1) * LANES]
            if j < n_rope:
                swap = jnp.where(first_half, pltpu.roll(blk, LANES - 32, 1), pltpu.roll(blk, 32, 1))
                blk = blk * cos + swap * sin
                if km_ref is not None:
                    km_ref[0, :, j * LANES:(j + 1) * LANES] = jnp.mean(
                        blk.reshape(tm // MOBA_BLOCK, MOBA_BLOCK, LANES), axis=1)
            if j in scales:
                blk = blk * scales[j]
            o_ref[:, j * LANES:(j + 1) * LANES] = blk.astype(o_ref.dtype)
            if vt_ref is not None and vt_start <= j < vt_start + vt_ref.shape[1] // LANES:
                g = j - vt_start
                for c in range(tm // KV_CHUNK):
                    vt_ref[c, g * LANES:(g + 1) * LANES, :] = (
                        blk[c * KV_CHUNK:(c + 1) * KV_CHUNK, :].T.astype(BF16))


def _proj(x2d, g, w, cos_t, sin_t, *, n_rope, scales, with_kmean, vt_start, seq, tm, name,
          vt_width=SELF_WIDTH):
    t, d = x2d.shape
    n_out = w.shape[1]
    n_pos_blocks = seq // tm
    col_chunk = 512 if n_out % 512 == 0 else 256
    out_shape = [jax.ShapeDtypeStruct((t, n_out), BF16)]
    out_specs = [pl.BlockSpec((tm, n_out), lambda i: (i, 0))]
    if with_kmean:
        out_shape.append(jax.ShapeDtypeStruct((t // tm, tm // MOBA_BLOCK, n_rope * LANES), F32))
        out_specs.append(pl.BlockSpec((1, tm // MOBA_BLOCK, n_rope * LANES), lambda i: (i, 0, 0)))
    if vt_start is not None:
        out_shape.append(jax.ShapeDtypeStruct((t // KV_CHUNK, vt_width, KV_CHUNK), BF16))
        out_specs.append(pl.BlockSpec((tm // KV_CHUNK, vt_width, KV_CHUNK), lambda i: (i, 0, 0)))
    kern = functools.partial(_proj_kernel, n_rope=n_rope, scales=dict(scales), col_chunk=col_chunk,
                             with_kmean=with_kmean, vt_start=vt_start)
    return pl.pallas_call(
        kern,
        out_shape=out_shape,
        grid=(t // tm,),
        in_specs=[
            pl.BlockSpec((tm, d), lambda i: (i, 0)),
            pl.BlockSpec((1, d), lambda i: (0, 0)),
            pl.BlockSpec((d, n_out), lambda i: (0, 0)),
            pl.BlockSpec((tm, LANES), lambda i: (i % n_pos_blocks, 0)),
            pl.BlockSpec((tm, LANES), lambda i: (i % n_pos_blocks, 0)),
        ],
        out_specs=out_specs,
        compiler_params=_params("parallel"),
        name=name,
    )(x2d, g.reshape(1, d), w, cos_t, sin_t)


_NT = (((1,), (1,)), ((), ()))


def _split_maps(q):
    lane = lax.broadcasted_iota(jnp.int32, q.shape, 1)
    zero = jnp.zeros_like(q)
    return [jnp.where(lane < HEAD_DIM, q, zero), jnp.where(lane >= HEAD_DIM, q, zero)]


def _causal_mask(st):
    key = lax.broadcasted_iota(jnp.int32, st.shape, 0)
    qry = lax.broadcasted_iota(jnp.int32, st.shape, 1)
    return jnp.where(key <= qry, st, NEG)


def _flash_pipeline(n_groups, n_past, tq, scores_fn, vt_fn, bias_fn, finalize_fn, s_scr, acc_scr):
    streams = [(g, h) for g in range(n_groups) for h in range(2)]

    def produce(j, buf, only=None):
        col_max = []
        for i, (g, h) in enumerate(streams):
            if only is not None and i != only:
                continue
            st = scores_fn(g, h, j)
            s_scr[buf, g, h] = st
            col_max.append(jnp.max(st, axis=0, keepdims=True))
        return tuple(col_max)

    ones_rows = jnp.ones((SUM_ROWS, tq), BF16)

    def consume(j, buf, stats, col_max, diagonal, only=None):
        out, accs = [], []
        for i, (g, h) in enumerate(streams):
            if only is not None and i != only:
                continue
            st = s_scr[buf, g, h]
            if diagonal:
                st = _causal_mask(st)
                m_cur = jnp.max(st, axis=0, keepdims=True)
            else:
                m_cur = col_max[i]
            if bias_fn is not None:
                bias = bias_fn(g, h, j)
                m_cur = m_cur + bias
            m_next = jnp.maximum(stats[i], m_cur)
            alpha = jnp.exp2(stats[i] - m_next)
            shift = m_next if bias_fn is None else m_next - bias
            p = jnp.exp2((st - shift).astype(BF16))
            acc = alpha * acc_scr[g, h] + jnp.dot(
                jnp.concatenate([vt_fn(g, h, j), ones_rows], axis=0), p, preferred_element_type=F32)
            if diagonal:
                accs.append(acc)
                if h == 1:
                    finalize_fn(g, accs[-2:])
            else:
                acc_scr[g, h] = acc
            out.append(m_next)
        return tuple(out)

    def step(j, buf, carry):
        stats, nxt = [], []
        for i in range(len(streams)):
            nxt += produce(j + 1, 1 - buf, only=i)
            stats += consume(j, buf, carry[0], carry[1], False, only=i)
        return tuple(stats), tuple(nxt)

    def pair(i, carry):
        return step(2 * i + 1, 1, step(2 * i, 0, carry))

    def quad(i, carry):
        return pair(2 * i + 1, pair(2 * i, carry))

    def octet(i, carry):
        return quad(2 * i + 1, quad(2 * i, carry))

    acc_scr[...] = jnp.zeros(acc_scr.shape, F32)
    init = tuple(jnp.full((1, tq), NEG, F32) for _ in streams)
    def hexadecad(i, carry):
        return octet(2 * i + 1, octet(2 * i, carry))

    carry = lax.fori_loop(0, n_past // 16, hexadecad, (init, produce(0, 0)))
    carry = lax.fori_loop(2 * (n_past // 16), n_past // 8, octet, carry)
    carry = lax.fori_loop(2 * (n_past // 8), n_past // 4, quad, carry)
    carry = lax.fori_loop(2 * (n_past // 4), n_past // 2, pair, carry)

    def odd_tail(carry):
        stats, col_max = step(n_past - 1, 0, carry)
        return consume(n_past, 1, stats, col_max, True)

    def even_tail(carry):
        return consume(n_past, 0, carry[0], carry[1], True)

    return lax.cond(n_past % 2 == 1, odd_tail, even_tail, carry)


def _diff_attn_kernel(lam_ref, g_ref, q_ref, k_ref, vt_ref, o_ref, s_scr, acc_scr, *, lam_init):
    tq = KV_CHUNK
    n_groups = q_ref.shape[1] // LANES
    cols = [slice(g * LANES, (g + 1) * LANES) for g in range(n_groups)]
    lp = lam_ref[...]
    lam = (jnp.exp(jnp.sum(lp[0:1] * lp[1:2], axis=1, keepdims=True))
           - jnp.exp(jnp.sum(lp[2:3] * lp[3:4], axis=1, keepdims=True)) + lam_init)

    tiles = q_ref.shape[0] // tq
    first_tile = pl.program_id(2) * tiles

    def query_tile(local, carry):
        qi = first_tile + local
        rows = pl.ds(pl.multiple_of(local * tq, tq), tq)
        qs = [_split_maps(q_ref[rows, c]) for c in cols]

        def scores(g, h, j):
            start = pl.multiple_of(j * tq, tq)
            return lax.dot_general(k_ref[pl.ds(start, tq), cols[g]], qs[g][h], _NT,
                                   preferred_element_type=F32)

        def finalize(g, accs):
            o1, o2 = [a[:LANES] * (1.0 / a[LANES:LANES + 1]) for a in accs]
            od_t = o1 - lam * o2
            ms = jnp.mean(od_t * od_t, axis=0, keepdims=True)
            y = (od_t * lax.rsqrt(ms + EPS)).T * g_ref[...]
            o_ref[rows, cols[g]] = (y * (1.0 - lam_init)).astype(o_ref.dtype)

        _flash_pipeline(n_groups, qi, tq, scores, lambda g, h, j: vt_ref[j, cols[g], :], None,
                        finalize, s_scr, acc_scr)
        return carry

    lax.fori_loop(0, tiles, query_tile, 0)


def _diff_attention(z, vt, lam_p, subln, *, batch, seq, lam_init):
    t = z.shape[0]
    tq = KV_CHUNK
    n_groups = ATTN_GROUPS
    w = n_groups * LANES
    n_steps = N_SELF_GROUPS // n_groups
    rows = ATTN_TILES_PER_STEP * tq
    per_batch = seq // rows
    kern = functools.partial(_diff_attn_kernel, lam_init=lam_init)
    resident = dict(pipeline_mode=pl.Buffered(1))
    return pl.pallas_call(
        kern,
        out_shape=jax.ShapeDtypeStruct((t, SELF_WIDTH), BF16),
        grid=(batch, n_steps, per_batch),
        in_specs=[
            pl.BlockSpec((4, HEAD_DIM), lambda b, h, i: (0, 0)),
            pl.BlockSpec((1, LANES), lambda b, h, i: (0, 0)),
            pl.BlockSpec((rows, w), lambda b, h, i: (b * per_batch + i, h)),
            pl.BlockSpec((seq, w), lambda b, h, i: (b, n_steps + h)),
            pl.BlockSpec((seq // KV_CHUNK, w, KV_CHUNK), lambda b, h, i: (b, h, 0), **resident),
        ],
        out_specs=pl.BlockSpec((rows, w), lambda b, h, i: (b * per_batch + i, h)),
        scratch_shapes=[pltpu.VMEM((2, n_groups, 2, tq, tq), F32),
                        pltpu.VMEM((n_groups, 2, LANES + SUM_ROWS, tq), F32)],
        compiler_params=_params("parallel", "parallel", "parallel"),
        name="diff_attn",
    )(lam_p, subln.reshape(1, LANES), z, z, vt)


def _moba_select_bias(qs, km, qi):
    n_blk = km.shape[0]
    parts, rest = [], km
    for _ in range(3):
        part = rest.astype(BF16)
        parts.append(part)
        rest = rest - part.astype(F32)
    terms = lax.dot_general(jnp.concatenate(parts, axis=0), qs, _NT, preferred_element_type=F32)
    gate = (terms[2 * n_blk:] + terms[n_blk:2 * n_blk]) + terms[:n_blk]
    blk = lax.broadcasted_iota(jnp.int32, gate.shape, 0)
    past = blk < qi
    gm = jnp.where(past, gate, NEG)
    sel = jnp.zeros(gate.shape, F32)
    for _ in range(MOBA_TOPK):
        mx = jnp.max(gm, axis=0, keepdims=True)
        first = jnp.min(jnp.where(gm == mx, blk, n_blk), axis=0, keepdims=True)
        pick = blk == first
        sel = jnp.where(pick, 1.0, sel)
        gm = jnp.where(pick, -jnp.inf, gm)
    return jnp.where((past & (sel > 0.5)) | (blk == qi), 0.0, 2.0 * NEG)


def _moba_attn_kernel(q_ref, k_ref, vt_ref, km_ref, o_ref, s_scr, acc_scr, bias_scr):
    tq = MOBA_BLOCK
    n_groups = q_ref.shape[1] // LANES
    cols = [slice(g * LANES, (g + 1) * LANES) for g in range(n_groups)]

    tiles = q_ref.shape[0] // tq
    first_tile = pl.program_id(2) * tiles

    def query_tile(local, carry):
        qi = first_tile + local
        rows = pl.ds(pl.multiple_of(local * tq, tq), tq)
        qs = [_split_maps(q_ref[rows, c]) for c in cols]
        for g, c in enumerate(cols):
            for h in range(2):
                bias_scr[g, h] = _moba_select_bias(qs[g][h], km_ref[:, c], qi)

        def scores(g, h, j):
            start = pl.multiple_of(j * tq, tq)
            return lax.dot_general(k_ref[pl.ds(start, tq), cols[g]], qs[g][h], _NT,
                                   preferred_element_type=F32)

        def values_t(g, h, j):
            return vt_ref[j, pl.ds(g * LANES + h * HEAD_DIM, HEAD_DIM), :]

        def finalize(g, accs):
            heads = [a[:HEAD_DIM] * (1.0 / a[HEAD_DIM:HEAD_DIM + 1]) for a in accs]
            o_ref[rows, cols[g]] = jnp.concatenate(heads, axis=0).T.astype(o_ref.dtype)

        _flash_pipeline(n_groups, qi, tq, scores, values_t,
                        lambda g, h, j: bias_scr[g, h, pl.ds(j, 1), :], finalize, s_scr, acc_scr)
        return carry

    lax.fori_loop(0, tiles, query_tile, 0)


def _moba_attention(zq, zkv, vt, kmean, *, batch, seq):
    t = zq.shape[0]
    tq = MOBA_BLOCK
    nq = seq // tq
    n_groups = ATTN_GROUPS
    w = n_groups * LANES
    n_steps = N_SELF_GROUPS // n_groups
    rows = ATTN_TILES_PER_STEP * tq
    per_batch = seq // rows
    resident = dict(pipeline_mode=pl.Buffered(1))
    return pl.pallas_call(
        _moba_attn_kernel,
        out_shape=jax.ShapeDtypeStruct((t, SELF_WIDTH), BF16),
        grid=(batch, n_steps, per_batch),
        in_specs=[
            pl.BlockSpec((rows, w), lambda b, h, i: (b * per_batch + i, h)),
            pl.BlockSpec((seq, w), lambda b, h, i: (b, h)),
            pl.BlockSpec((seq // KV_CHUNK, w, KV_CHUNK), lambda b, h, i: (b, h, 0), **resident),
            pl.BlockSpec((None, nq, w), lambda b, h, i: (b, 0, h)),
        ],
        out_specs=pl.BlockSpec((rows, w), lambda b, h, i: (b * per_batch + i, h)),
        scratch_shapes=[pltpu.VMEM((2, n_groups, 2, tq, tq), F32),
                        pltpu.VMEM((n_groups, 2, HEAD_DIM + SUM_ROWS, tq), F32),
                        pltpu.VMEM((n_groups, 2, nq, tq), F32)],
        compiler_params=_params("parallel", "parallel", "parallel"),
        name="moba_attn",
    )(zq, zkv, vt, kmean)


def _attn_out_kernel(x_ref, os_ref, qm_ref, mk_ref, mvt_ref, w_ref, o_ref):
    mem_len = mk_ref.shape[0]
    mk = mk_ref[...]
    lane = lax.broadcasted_iota(jnp.int32, mk.shape, 1)
    zero = jnp.zeros_like(mk)
    mk_heads = jnp.concatenate(
        [jnp.where((lane >= h * HEAD_DIM) & (lane < (h + 1) * HEAD_DIM), mk, zero)
         for h in range(N_MEM_HEADS)], axis=0)
    s = lax.dot_general(mk_heads, qm_ref[...], _NT, preferred_element_type=F32)
    ones_rows = jnp.ones((SUM_ROWS, mem_len), BF16)
    heads = []
    for h in range(N_MEM_HEADS):
        sh = s[h * mem_len:(h + 1) * mem_len]
        p = jnp.exp2((sh - jnp.max(sh, axis=0, keepdims=True)).astype(BF16))
        vt_ones = jnp.concatenate([mvt_ref[h * HEAD_DIM:(h + 1) * HEAD_DIM, :], ones_rows], axis=0)
        oh = jnp.dot(vt_ones, p, preferred_element_type=F32)
        heads.append(oh[:HEAD_DIM] * (1.0 / oh[HEAD_DIM:HEAD_DIM + 1]))
    o_mem = jnp.concatenate(heads, axis=0).T.astype(BF16)
    y = jnp.dot(os_ref[...], w_ref[:SELF_WIDTH, :], preferred_element_type=F32)
    y = y + jnp.dot(o_mem, w_ref[SELF_WIDTH:, :], preferred_element_type=F32)
    o_ref[...] = x_ref[...] + y


def _attn_out(x2d, o_self, zq, qm_block, memkv, memvt, w_out, *, seq, mem_len, name):
    t, d = x2d.shape
    tm = PROJ_ROWS
    per_batch = seq // tm
    return pl.pallas_call(
        _attn_out_kernel,
        out_shape=jax.ShapeDtypeStruct((t, d), F32),
        grid=(t // tm,),
        in_specs=[
            pl.BlockSpec((tm, d), lambda i: (i, 0)),
            pl.BlockSpec((tm, SELF_WIDTH), lambda i: (i, 0)),
            pl.BlockSpec((tm, MEM_WIDTH), lambda i: (i, qm_block)),
            pl.BlockSpec((mem_len, MEM_WIDTH), lambda i: (i // per_batch, 0)),
            pl.BlockSpec((None, MEM_WIDTH, mem_len), lambda i: (i // per_batch, 0, 0)),
            pl.BlockSpec((d, d), lambda i: (0, 0)),
        ],
        out_specs=pl.BlockSpec((tm, d), lambda i: (i, 0)),
        compiler_params=_params("parallel"),
        name=name,
    )(x2d, o_self, zq, memkv, memvt, w_out)


def _mlp_kernel(x_ref, g_ref, wu_ref, wd_ref, gf_ref, o_ref, u_scr, *, final_norm, tf):
    x = x_ref[...]
    ms = jnp.mean(x * x, axis=-1, keepdims=True)
    h = (x * lax.rsqrt(ms + EPS) * g_ref[...]).astype(BF16)
    for c0 in range(0, wu_ref.shape[1], tf):
        u = jnp.maximum(jnp.dot(h, wu_ref[:, c0:c0 + tf], preferred_element_type=F32), 0.0)
        u_scr[:, c0:c0 + tf] = (u * u).astype(BF16)
    y = x + jnp.dot(u_scr[...], wd_ref[...], preferred_element_type=F32)
    if final_norm:
        ms = jnp.mean(y * y, axis=-1, keepdims=True)
        y = y * lax.rsqrt(ms + EPS) * gf_ref[...]
    o_ref[...] = y


def _mlp(x2d, g, w_up, w_down, g_final, *, final_norm, name):
    t, d = x2d.shape
    dff = w_up.shape[1]
    tm = MLP_ROWS
    kern = functools.partial(_mlp_kernel, final_norm=final_norm, tf=MLP_FF_CHUNK)
    resident = dict(pipeline_mode=pl.Buffered(1))
    return pl.pallas_call(
        kern,
        out_shape=jax.ShapeDtypeStruct((t, d), F32),
        grid=(t // tm,),
        in_specs=[
            pl.BlockSpec((tm, d), lambda i: (i, 0)),
            pl.BlockSpec((1, d), lambda i: (0, 0)),
            pl.BlockSpec((d, dff), lambda i: (0, 0), **resident),
            pl.BlockSpec((dff, d), lambda i: (0, 0), **resident),
            pl.BlockSpec((1, d), lambda i: (0, 0)),
        ],
        out_specs=pl.BlockSpec((tm, d), lambda i: (i, 0)),
        scratch_shapes=[pltpu.VMEM((tm, dff), BF16)],
        compiler_params=_params("parallel"),
        name=name,
    )(x2d, g.reshape(1, d), w_up, w_down, g_final.reshape(1, d))


def _rope_tables(seq):
    half = HEAD_DIM // 2
    inv = 1.0 / (ROPE_THETA ** (jnp.arange(half, dtype=F32) / half))
    ang = jnp.arange(seq, dtype=F32)[:, None] * inv[None, :]
    cos, sin = jnp.cos(ang), jnp.sin(ang)
    reps = LANES // half
    cos_t = jnp.tile(cos, (1, reps))
    sign = jnp.tile(jnp.concatenate([-jnp.ones((half,), F32), jnp.ones((half,), F32)]), LANES // HEAD_DIM)
    sin_t = jnp.tile(sin, (1, reps)) * sign[None, :]
    return cos_t, sin_t


def kernel(x, mem, a_norm_attn, a_w_in, a_lambda, a_subln, a_mem_norm, a_w_mem_kv, a_w_out, a_norm_mlp, a_w_up, a_w_down, kv_norm, w_kv, b_norm_attn, b_w_in, b_mem_norm, b_w_mem_kv, b_w_out, b_norm_mlp, b_w_up, b_w_down, final_norm):
    batch, seq, d = x.shape
    mem_len = mem.shape[1]
    t = batch * seq
    x2d = x.reshape(t, d)
    mem2d = mem.reshape(batch * mem_len, d)
    cos_t, sin_t = _rope_tables(seq)
    bf = lambda w: w.astype(BF16)
    ng = N_SELF_GROUPS
    q_scale = QK_SCALE * LOG2E
    mem_groups = MEM_WIDTH // LANES

    def memkv(g, w, name):
        return _proj(mem2d, g, bf(w), cos_t, sin_t, n_rope=0, scales={}, with_kmean=False,
                     vt_start=mem_groups, vt_width=MEM_WIDTH, seq=mem_len, tm=mem_len, name=name)

    lam_init = 0.8 - 0.6 * math.exp(-0.3 * 0)
    za, vt_a = _proj(x2d, a_norm_attn[0], bf(a_w_in[0]), cos_t, sin_t, n_rope=2 * ng,
                     scales={j: q_scale for j in (*range(ng), 3 * ng, 3 * ng + 1)},
                     with_kmean=False, vt_start=2 * ng, seq=seq, tm=PROJ_ROWS, name="a_proj")
    mkv_a, mvt_a = memkv(a_mem_norm[0], a_w_mem_kv[0], "a_memkv")
    o_self = _diff_attention(za, vt_a, a_lambda[0], a_subln[0], batch=batch, seq=seq,
                             lam_init=lam_init)
    x2d = _attn_out(x2d, o_self, za, 3 * SELF_WIDTH // MEM_WIDTH, mkv_a, mvt_a, bf(a_w_out[0]),
                    seq=seq, mem_len=mem_len, name="a_attn_out")
    x2d = _mlp(x2d, a_norm_mlp[0], bf(a_w_up[0]), bf(a_w_down[0]), final_norm,
               final_norm=False, name="a_mlp")

    zkv, kmean, vt_b = _proj(x2d, kv_norm, bf(w_kv), cos_t, sin_t, n_rope=ng, scales={},
                             with_kmean=True, vt_start=ng, seq=seq, tm=PROJ_ROWS, name="b_kvproj")
    kmean = kmean.reshape(batch, seq // MOBA_BLOCK, SELF_WIDTH)
    zb, = _proj(x2d, b_norm_attn[0], bf(b_w_in[0]), cos_t, sin_t, n_rope=ng,
                scales={j: q_scale for j in range(ng + mem_groups)}, with_kmean=False, vt_start=None,
                seq=seq, tm=PROJ_ROWS, name="b_qproj")
    mkv_b, mvt_b = memkv(b_mem_norm[0], b_w_mem_kv[0], "b_memkv")
    o_self = _moba_attention(zb, zkv, vt_b, kmean, batch=batch, seq=seq)
    x2d = _attn_out(x2d, o_self, zb, SELF_WIDTH // MEM_WIDTH, mkv_b, mvt_b, bf(b_w_out[0]),
                    seq=seq, mem_len=mem_len, name="b_attn_out")
    x2d = _mlp(x2d, b_norm_mlp[0], bf(b_w_up[0]), bf(b_w_down[0]), final_norm,
               final_norm=True, name="b_mlp")
    return x2d.reshape(batch, seq, d)
```

```python
import functools
import math

import jax
import jax.numpy as jnp
from jax import lax
from jax.experimental import pallas as pl
from jax.experimental.pallas import tpu as pltpu

HEAD_DIM = 64
SELF_WIDTH = 768
MEM_WIDTH = 256
N_MEM_HEADS = 4
MOBA_BLOCK = 256
MOBA_TOPK = 3
ROPE_THETA = 10000.0
EPS = 1e-6
NEG = -1e30

LANES = 128
N_SELF_GROUPS = SELF_WIDTH // LANES
QK_SCALE = HEAD_DIM ** -0.5
LOG2E = math.log2(math.e)
KV_CHUNK = 256
SUM_ROWS = 16

F32 = jnp.float32
BF16 = jnp.bfloat16

V7X_VMEM_BYTES = 64 * 1024 * 1024
VMEM_LIMIT_BYTES = V7X_VMEM_BYTES * 7 // 8

PROJ_ROWS = 1024
MLP_ROWS = 512
MLP_FF_CHUNK = 512
ATTN_GROUPS = N_SELF_GROUPS
ATTN_TILES_PER_STEP = 4


def _params(*sem):
    return pltpu.CompilerParams(dimension_semantics=sem, vmem_limit_bytes=VMEM_LIMIT_BYTES)


def _proj_kernel(x_ref, g_ref, w_ref, cos_ref, sin_ref, o_ref, *extra_refs,
                 n_rope, scales, col_chunk, with_kmean, vt_start):
    tm = x_ref.shape[0]
    n_out = w_ref.shape[1]
    extra = list(extra_refs)
    km_ref = extra.pop(0) if with_kmean else None
    vt_ref = extra.pop(0) if vt_start is not None else None
    x = x_ref[...]
    ms = jnp.mean(x * x, axis=-1, keepdims=True)
    h = (x * lax.rsqrt(ms + EPS) * g_ref[...]).astype(BF16)
    lane = lax.broadcasted_iota(jnp.int32, (tm, LANES), 1)
    first_half = (lane & (HEAD_DIM - 1)) < HEAD_DIM // 2
    if n_rope:
        cos = cos_ref[...]
        sin = sin_ref[...]
    for c0 in range(0, n_out, col_chunk):
        z = jnp.dot(h, w_ref[:, c0:c0 + col_chunk], preferred_element_type=F32)
        for jj in range(col_chunk // LANES):
            j = c0 // LANES + jj
            blk = z[:, jj * LANES:(jj + 1) * LANES]
            if j < n_rope:
                swap = jnp.where(first_half, pltpu.roll(blk, LANES - 32, 1), pltpu.roll(blk, 32, 1))
                blk = blk * cos + swap * sin
                if km_ref is not None:
                    km_ref[0, :, j * LANES:(j + 1) * LANES] = jnp.mean(
                        blk.reshape(tm // MOBA_BLOCK, MOBA_BLOCK, LANES), axis=1)
            if j in scales:
                blk = blk * scales[j]
            o_ref[:, j * LANES:(j + 1) * LANES] = blk.astype(o_ref.dtype)
            if vt_ref is not None and vt_start <= j < vt_start + vt_ref.shape[1] // LANES:
                g = j - vt_start
                for c in range(tm // KV_CHUNK):
                    vt_ref[c, g * LANES:(g + 1) * LANES, :] = (
                        blk[c * KV_CHUNK:(c + 1) * KV_CHUNK, :].T.astype(BF16))


def _proj(x2d, g, w, cos_t, sin_t, *, n_rope, scales, with_kmean, vt_start, seq, tm, name,
          vt_width=SELF_WIDTH):
    t, d = x2d.shape
    n_out = w.shape[1]
    n_pos_blocks = seq // tm
    col_chunk = 512 if n_out % 512 == 0 else 256
    out_shape = [jax.ShapeDtypeStruct((t, n_out), BF16)]
    out_specs = [pl.BlockSpec((tm, n_out), lambda i: (i, 0))]
    if with_kmean:
        out_shape.append(jax.ShapeDtypeStruct((t // tm, tm // MOBA_BLOCK, n_rope * LANES), F32))
        out_specs.append(pl.BlockSpec((1, tm // MOBA_BLOCK, n_rope * LANES), lambda i: (i, 0, 0)))
    if vt_start is not None:
        out_shape.append(jax.ShapeDtypeStruct((t // KV_CHUNK, vt_width, KV_CHUNK), BF16))
        out_specs.append(pl.BlockSpec((tm // KV_CHUNK, vt_width, KV_CHUNK), lambda i: (i, 0, 0)))
    kern = functools.partial(_proj_kernel, n_rope=n_rope, scales=dict(scales), col_chunk=col_chunk,
                             with_kmean=with_kmean, vt_start=vt_start)
    return pl.pallas_call(
        kern,
        out_shape=out_shape,
        grid=(t // tm,),
        in_specs=[
            pl.BlockSpec((tm, d), lambda i: (i, 0)),
            pl.BlockSpec((1, d), lambda i: (0, 0)),
            pl.BlockSpec((d, n_out), lambda i: (0, 0)),
            pl.BlockSpec((tm, LANES), lambda i: (i % n_pos_blocks, 0)),
            pl.BlockSpec((tm, LANES), lambda i: (i % n_pos_blocks, 0)),
        ],
        out_specs=out_specs,
        compiler_params=_params("parallel"),
        name=name,
    )(x2d, g.reshape(1, d), w, cos_t, sin_t)


_NT = (((1,), (1,)), ((), ()))


def _split_maps(q):
    lane = lax.broadcasted_iota(jnp.int32, q.shape, 1)
    zero = jnp.zeros_like(q)
    return [jnp.where(lane < HEAD_DIM, q, zero), jnp.where(lane >= HEAD_DIM, q, zero)]


def _causal_mask(st):
    key = lax.broadcasted_iota(jnp.int32, st.shape, 0)
    qry = lax.broadcasted_iota(jnp.int32, st.shape, 1)
    return jnp.where(key <= qry, st, NEG)


def _flash_pipeline(n_groups, n_past, tq, scores_fn, vt_fn, bias_fn, finalize_fn, s_scr, acc_scr):
    streams = [(g, h) for g in range(n_groups) for h in range(2)]

    def produce(j, buf, only=None):
        col_max = []
        for i, (g, h) in enumerate(streams):
            if only is not None and i != only:
                continue
            st = scores_fn(g, h, j)
            s_scr[buf, g, h] = st
            col_max.append(jnp.max(st, axis=0, keepdims=True))
        return tuple(col_max)

    ones_rows = jnp.ones((SUM_ROWS, tq), BF16)

    def consume(j, buf, stats, col_max, diagonal, only=None):
        out, accs = [], []
        for i, (g, h) in enumerate(streams):
            if only is not None and i != only:
                continue
            st = s_scr[buf, g, h]
            if diagonal:
                st = _causal_mask(st)
                m_cur = jnp.max(st, axis=0, keepdims=True)
            else:
                m_cur = col_max[i]
            if bias_fn is not None:
                bias = bias_fn(g, h, j)
                m_cur = m_cur + bias
            m_next = jnp.maximum(stats[i], m_cur)
            alpha = jnp.exp2(stats[i] - m_next)
            shift = m_next if bias_fn is None else m_next - bias
            p = jnp.exp2((st - shift).astype(BF16))
            acc = alpha * acc_scr[g, h] + jnp.dot(
                jnp.concatenate([vt_fn(g, h, j), ones_rows], axis=0), p, preferred_element_type=F32)
            if diagonal:
                accs.append(acc)
                if h == 1:
                    finalize_fn(g, accs[-2:])
            else:
                acc_scr[g, h] = acc
            out.append(m_next)
        return tuple(out)

    def step(j, buf, carry):
        stats, nxt = [], []
        for i in range(len(streams)):
            nxt += produce(j + 1, 1 - buf, only=i)
            stats += consume(j, buf, carry[0], carry[1], False, only=i)
        return tuple(stats), tuple(nxt)

    def pair(i, carry):
        return step(2 * i + 1, 1, step(2 * i, 0, carry))

    def quad(i, carry):
        return pair(2 * i + 1, pair(2 * i, carry))

    def octet(i, carry):
        return quad(2 * i + 1, quad(2 * i, carry))

    acc_scr[...] = jnp.zeros(acc_scr.shape, F32)
    init = tuple(jnp.full((1, tq), NEG, F32) for _ in streams)
    n_pairs = jnp.maximum(n_past - 1, 0) // 2
    carry = lax.fori_loop(0, n_pairs // 4, octet, (init, produce(0, 0)))
    carry = lax.fori_loop(2 * (n_pairs // 4), n_pairs // 2, quad, carry)
    carry = lax.fori_loop(2 * (n_pairs // 2), n_pairs, pair, carry)

    def odd_tail(carry):
        stats, col_max = step(n_past - 1, 0, carry)
        return consume(n_past, 1, stats, col_max, True)

    def even_tail(carry):
        stats, col_max = step(n_past - 1, 1, step(n_past - 2, 0, carry))
        return consume(n_past, 0, stats, col_max, True)

    def only_diagonal(carry):
        return consume(0, 0, carry[0], carry[1], True)

    return lax.cond(n_past == 0, only_diagonal,
                    lambda c: lax.cond(n_past % 2 == 1, odd_tail, even_tail, c), carry)


def _diff_attn_kernel(lam_ref, g_ref, q_ref, k_ref, vt_ref, o_ref, s_scr, acc_scr, *, lam_init):
    tq = KV_CHUNK
    n_groups = q_ref.shape[1] // LANES
    cols = [slice(g * LANES, (g + 1) * LANES) for g in range(n_groups)]
    lp = lam_ref[...]
    lam = (jnp.exp(jnp.sum(lp[0:1] * lp[1:2], axis=1, keepdims=True))
           - jnp.exp(jnp.sum(lp[2:3] * lp[3:4], axis=1, keepdims=True)) + lam_init)

    tiles = q_ref.shape[0] // tq
    first_tile = pl.program_id(2) * tiles

    def query_tile(local, carry):
        qi = first_tile + local
        rows = pl.ds(pl.multiple_of(local * tq, tq), tq)
        qs = [_split_maps(q_ref[rows, c]) for c in cols]

        def scores(g, h, j):
            start = pl.multiple_of(j * tq, tq)
            return lax.dot_general(k_ref[pl.ds(start, tq), cols[g]], qs[g][h], _NT,
                                   preferred_element_type=F32)

        def finalize(g, accs):
            o1, o2 = [a[:LANES] * (1.0 / a[LANES:LANES + 1]) for a in accs]
            od_t = o1 - lam * o2
            ms = jnp.mean(od_t * od_t, axis=0, keepdims=True)
            y = (od_t * lax.rsqrt(ms + EPS)).T * g_ref[...]
            o_ref[rows, cols[g]] = (y * (1.0 - lam_init)).astype(o_ref.dtype)

        _flash_pipeline(n_groups, qi, tq, scores, lambda g, h, j: vt_ref[j, cols[g], :], None,
                        finalize, s_scr, acc_scr)
        return carry

    lax.fori_loop(0, tiles, query_tile, 0)


def _diff_attention(z, vt, lam_p, subln, *, batch, seq, lam_init):
    t = z.shape[0]
    tq = KV_CHUNK
    n_groups = ATTN_GROUPS
    w = n_groups * LANES
    n_steps = N_SELF_GROUPS // n_groups
    rows = ATTN_TILES_PER_STEP * tq
    per_batch = seq // rows
    kern = functools.partial(_diff_attn_kernel, lam_init=lam_init)
    resident = dict(pipeline_mode=pl.Buffered(1))
    return pl.pallas_call(
        kern,
        out_shape=jax.ShapeDtypeStruct((t, SELF_WIDTH), BF16),
        grid=(batch, n_steps, per_batch),
        in_specs=[
            pl.BlockSpec((4, HEAD_DIM), lambda b, h, i: (0, 0)),
            pl.BlockSpec((1, LANES), lambda b, h, i: (0, 0)),
            pl.BlockSpec((rows, w), lambda b, h, i: (b * per_batch + i, h)),
            pl.BlockSpec((seq, w), lambda b, h, i: (b, n_steps + h)),
            pl.BlockSpec((seq // KV_CHUNK, w, KV_CHUNK), lambda b, h, i: (b, h, 0), **resident),
        ],
        out_specs=pl.BlockSpec((rows, w), lambda b, h, i: (b * per_batch + i, h)),
        scratch_shapes=[pltpu.VMEM((2, n_groups, 2, tq, tq), F32),
                        pltpu.VMEM((n_groups, 2, LANES + SUM_ROWS, tq), F32)],
        compiler_params=_params("parallel", "parallel", "parallel"),
        name="diff_attn",
    )(lam_p, subln.reshape(1, LANES), z, z, vt)


def _moba_select_bias(qs, km, qi):
    n_blk = km.shape[0]
    parts, rest = [], km
    for _ in range(3):
        part = rest.astype(BF16)
        parts.append(part)
        rest = rest - part.astype(F32)
    terms = lax.dot_general(jnp.concatenate(parts, axis=0), qs, _NT, preferred_element_type=F32)
    gate = (terms[2 * n_blk:] + terms[n_blk:2 * n_blk]) + terms[:n_blk]
    blk = lax.broadcasted_iota(jnp.int32, gate.shape, 0)
    past = blk < qi
    gm = jnp.where(past, gate, NEG)
    sel = jnp.zeros(gate.shape, F32)
    for _ in range(MOBA_TOPK):
        mx = jnp.max(gm, axis=0, keepdims=True)
        first = jnp.min(jnp.where(gm == mx, blk, n_blk), axis=0, keepdims=True)
        pick = blk == first
        sel = jnp.where(pick, 1.0, sel)
        gm = jnp.where(pick, -jnp.inf, gm)
    return jnp.where((past & (sel > 0.5)) | (blk == qi), 0.0, 2.0 * NEG)


def _moba_attn_kernel(q_ref, k_ref, vt_ref, km_ref, o_ref, s_scr, acc_scr, bias_scr):
    tq = MOBA_BLOCK
    n_groups = q_ref.shape[1] // LANES
    cols = [slice(g * LANES, (g + 1) * LANES) for g in range(n_groups)]

    tiles = q_ref.shape[0] // tq
    first_tile = pl.program_id(2) * tiles

    def query_tile(local, carry):
        qi = first_tile + local
        rows = pl.ds(pl.multiple_of(local * tq, tq), tq)
        qs = [_split_maps(q_ref[rows, c]) for c in cols]
        for g, c in enumerate(cols):
            for h in range(2):
                bias_scr[g, h] = _moba_select_bias(qs[g][h], km_ref[:, c], qi)

        def scores(g, h, j):
            start = pl.multiple_of(j * tq, tq)
            return lax.dot_general(k_ref[pl.ds(start, tq), cols[g]], qs[g][h], _NT,
                                   preferred_element_type=F32)

        def values_t(g, h, j):
            return vt_ref[j, pl.ds(g * LANES + h * HEAD_DIM, HEAD_DIM), :]

        def finalize(g, accs):
            heads = [a[:HEAD_DIM] * (1.0 / a[HEAD_DIM:HEAD_DIM + 1]) for a in accs]
            o_ref[rows, cols[g]] = jnp.concatenate(heads, axis=0).T.astype(o_ref.dtype)

        _flash_pipeline(n_groups, qi, tq, scores, values_t,
                        lambda g, h, j: bias_scr[g, h, pl.ds(j, 1), :], finalize, s_scr, acc_scr)
        return carry

    lax.fori_loop(0, tiles, query_tile, 0)


def _moba_attention(zq, zkv, vt, kmean, *, batch, seq):
    t = zq.shape[0]
    tq = MOBA_BLOCK
    nq = seq // tq
    n_groups = ATTN_GROUPS
    w = n_groups * LANES
    n_steps = N_SELF_GROUPS // n_groups
    rows = ATTN_TILES_PER_STEP * tq
    per_batch = seq // rows
    resident = dict(pipeline_mode=pl.Buffered(1))
    return pl.pallas_call(
        _moba_attn_kernel,
        out_shape=jax.ShapeDtypeStruct((t, SELF_WIDTH), BF16),
        grid=(batch, n_steps, per_batch),
        in_specs=[
            pl.BlockSpec((rows, w), lambda b, h, i: (b * per_batch + i, h)),
            pl.BlockSpec((seq, w), lambda b, h, i: (b, h)),
            pl.BlockSpec((seq // KV_CHUNK, w, KV_CHUNK), lambda b, h, i: (b, h, 0), **resident),
            pl.BlockSpec((None, nq, w), lambda b, h, i: (b, 0, h)),
        ],
        out_specs=pl.BlockSpec((rows, w), lambda b, h, i: (b * per_batch + i, h)),
        scratch_shapes=[pltpu.VMEM((2, n_groups, 2, tq, tq), F32),
                        pltpu.VMEM((n_groups, 2, HEAD_DIM + SUM_ROWS, tq), F32),
                        pltpu.VMEM((n_groups, 2, nq, tq), F32)],
        compiler_params=_params("parallel", "parallel", "parallel"),
        name="moba_attn",
    )(zq, zkv, vt, kmean)


def _attn_out_kernel(x_ref, os_ref, qm_ref, mk_ref, mvt_ref, w_ref, o_ref):
    mem_len = mk_ref.shape[0]
    mk = mk_ref[...]
    lane = lax.broadcasted_iota(jnp.int32, mk.shape, 1)
    zero = jnp.zeros_like(mk)
    mk_heads = jnp.concatenate(
        [jnp.where((lane >= h * HEAD_DIM) & (lane < (h + 1) * HEAD_DIM), mk, zero)
         for h in range(N_MEM_HEADS)], axis=0)
    s = lax.dot_general(mk_heads, qm_ref[...], _NT, preferred_element_type=F32)
    ones_rows = jnp.ones((SUM_ROWS, mem_len), BF16)
    heads = []
    for h in range(N_MEM_HEADS):
        sh = s[h * mem_len:(h + 1) * mem_len]
        p = jnp.exp2((sh - jnp.max(sh, axis=0, keepdims=True)).astype(BF16))
        vt_ones = jnp.concatenate([mvt_ref[h * HEAD_DIM:(h + 1) * HEAD_DIM, :], ones_rows], axis=0)
        oh = jnp.dot(vt_ones, p, preferred_element_type=F32)
        heads.append(oh[:HEAD_DIM] * (1.0 / oh[HEAD_DIM:HEAD_DIM + 1]))
    o_mem = jnp.concatenate(heads, axis=0).T.astype(BF16)
    y = jnp.dot(os_ref[...], w_ref[:SELF_WIDTH, :], preferred_element_type=F32)
    y = y + jnp.dot(o_mem, w_ref[SELF_WIDTH:, :], preferred_element_type=F32)
    o_ref[...] = x_ref[...] + y


def _attn_out(x2d, o_self, zq, qm_block, memkv, memvt, w_out, *, seq, mem_len, name):
    t, d = x2d.shape
    tm = PROJ_ROWS
    per_batch = seq // tm
    return pl.pallas_call(
        _attn_out_kernel,
        out_shape=jax.ShapeDtypeStruct((t, d), F32),
        grid=(t // tm,),
        in_specs=[
            pl.BlockSpec((tm, d), lambda i: (i, 0)),
            pl.BlockSpec((tm, SELF_WIDTH), lambda i: (i, 0)),
            pl.BlockSpec((tm, MEM_WIDTH), lambda i: (i, qm_block)),
            pl.BlockSpec((mem_len, MEM_WIDTH), lambda i: (i // per_batch, 0)),
            pl.BlockSpec((None, MEM_WIDTH, mem_len), lambda i: (i // per_batch, 0, 0)),
            pl.BlockSpec((d, d), lambda i: (0, 0)),
        ],
        out_specs=pl.BlockSpec((tm, d), lambda i: (i, 0)),
        compiler_params=_params("parallel"),
        name=name,
    )(x2d, o_self, zq, memkv, memvt, w_out)


def _mlp_kernel(x_ref, g_ref, wu_ref, wd_ref, gf_ref, o_ref, u_scr, *, final_norm, tf):
    x = x_ref[...]
    ms = jnp.mean(x * x, axis=-1, keepdims=True)
    h = (x * lax.rsqrt(ms + EPS) * g_ref[...]).astype(BF16)
    for c0 in range(0, wu_ref.shape[1], tf):
        u = jnp.maximum(jnp.dot(h, wu_ref[:, c0:c0 + tf], preferred_element_type=F32), 0.0)
        u_scr[:, c0:c0 + tf] = (u * u).astype(BF16)
    y = x + jnp.dot(u_scr[...], wd_ref[...], preferred_element_type=F32)
    if final_norm:
        ms = jnp.mean(y * y, axis=-1, keepdims=True)
        y = y * lax.rsqrt(ms + EPS) * gf_ref[...]
    o_ref[...] = y


def _mlp(x2d, g, w_up, w_down, g_final, *, final_norm, name):
    t, d = x2d.shape
    dff = w_up.shape[1]
    tm = MLP_ROWS
    kern = functools.partial(_mlp_kernel, final_norm=final_norm, tf=MLP_FF_CHUNK)
    resident = dict(pipeline_mode=pl.Buffered(1))
    return pl.pallas_call(
        kern,
        out_shape=jax.ShapeDtypeStruct((t, d), F32),
        grid=(t // tm,),
        in_specs=[
            pl.BlockSpec((tm, d), lambda i: (i, 0)),
            pl.BlockSpec((1, d), lambda i: (0, 0)),
            pl.BlockSpec((d, dff), lambda i: (0, 0), **resident),
            pl.BlockSpec((dff, d), lambda i: (0, 0), **resident),
            pl.BlockSpec((1, d), lambda i: (0, 0)),
        ],
        out_specs=pl.BlockSpec((tm, d), lambda i: (i, 0)),
        scratch_shapes=[pltpu.VMEM((tm, dff), BF16)],
        compiler_params=_params("parallel"),
        name=name,
    )(x2d, g.reshape(1, d), w_up, w_down, g_final.reshape(1, d))


def _rope_tables(seq):
    half = HEAD_DIM // 2
    inv = 1.0 / (ROPE_THETA ** (jnp.arange(half, dtype=F32) / half))
    ang = jnp.arange(seq, dtype=F32)[:, None] * inv[None, :]
    cos, sin = jnp.cos(ang), jnp.sin(ang)
    reps = LANES // half
    cos_t = jnp.tile(cos, (1, reps))
    sign = jnp.tile(jnp.concatenate([-jnp.ones((half,), F32), jnp.ones((half,), F32)]), LANES // HEAD_DIM)
    sin_t = jnp.tile(sin, (1, reps)) * sign[None, :]
    return cos_t, sin_t


def kernel(x, mem, a_norm_attn, a_w_in, a_lambda, a_subln, a_mem_norm, a_w_mem_kv, a_w_out, a_norm_mlp, a_w_up, a_w_down, kv_norm, w_kv, b_norm_attn, b_w_in, b_mem_norm, b_w_mem_kv, b_w_out, b_norm_mlp, b_w_up, b_w_down, final_norm):
    batch, seq, d = x.shape
    mem_len = mem.shape[1]
    t = batch * seq
    x2d = x.reshape(t, d)
    mem2d = mem.reshape(batch * mem_len, d)
    cos_t, sin_t = _rope_tables(seq)
    bf = lambda w: w.astype(BF16)
    ng = N_SELF_GROUPS
    q_scale = QK_SCALE * LOG2E
    mem_groups = MEM_WIDTH // LANES

    def memkv(g, w, name):
        return _proj(mem2d, g, bf(w), cos_t, sin_t, n_rope=0, scales={}, with_kmean=False,
                     vt_start=mem_groups, vt_width=MEM_WIDTH, seq=mem_len, tm=mem_len, name=name)

    lam_init = 0.8 - 0.6 * math.exp(-0.3 * 0)
    za, vt_a = _proj(x2d, a_norm_attn[0], bf(a_w_in[0]), cos_t, sin_t, n_rope=2 * ng,
                     scales={j: q_scale for j in (*range(ng), 3 * ng, 3 * ng + 1)},
                     with_kmean=False, vt_start=2 * ng, seq=seq, tm=PROJ_ROWS, name="a_proj")
    mkv_a, mvt_a = memkv(a_mem_norm[0], a_w_mem_kv[0], "a_memkv")
    o_self = _diff_attention(za, vt_a, a_lambda[0], a_subln[0], batch=batch, seq=seq,
                             lam_init=lam_init)
    x2d = _attn_out(x2d, o_self, za, 3 * SELF_WIDTH // MEM_WIDTH, mkv_a, mvt_a, bf(a_w_out[0]),
                    seq=seq, mem_len=mem_len, name="a_attn_out")
    x2d = _mlp(x2d, a_norm_mlp[0], bf(a_w_up[0]), bf(a_w_down[0]), final_norm,
               final_norm=False, name="a_mlp")

    zkv, kmean, vt_b = _proj(x2d, kv_norm, bf(w_kv), cos_t, sin_t, n_rope=ng, scales={},
                             with_kmean=True, vt_start=ng, seq=seq, tm=PROJ_ROWS, name="b_kvproj")
    kmean = kmean.reshape(batch, seq // MOBA_BLOCK, SELF_WIDTH)
    zb, = _proj(x2d, b_norm_attn[0], bf(b_w_in[0]), cos_t, sin_t, n_rope=ng,
                scales={j: q_scale for j in range(ng + mem_groups)}, with_kmean=False, vt_start=None,
                seq=seq, tm=PROJ_ROWS, name="b_qproj")
    mkv_b, mvt_b = memkv(b_mem_norm[0], b_w_mem_kv[0], "b_memkv")
    o_self = _moba_attention(zb, zkv, vt_b, kmean, batch=batch, seq=seq)
    x2d = _attn_out(x2d, o_self, zb, SELF_WIDTH // MEM_WIDTH, mkv_b, mvt_b, bf(b_w_out[0]),
                    seq=seq, mem_len=mem_len, name="b_attn_out")
    x2d = _mlp(x2d, b_norm_mlp[0], bf(b_w_up[0]), bf(b_w_down[0]), final_norm,
               final_norm=True, name="b_mlp")
    return x2d.reshape(batch, seq, d)
```

```python
import functools
import math

import jax
import jax.numpy as jnp
from jax import lax
from jax.experimental import pallas as pl
from jax.experimental.pallas import tpu as pltpu

HEAD_DIM = 64
SELF_WIDTH = 768
MEM_WIDTH = 256
N_MEM_HEADS = 4
MOBA_BLOCK = 256
MOBA_TOPK = 3
ROPE_THETA = 10000.0
EPS = 1e-6
NEG = -1e30

LANES = 128
N_SELF_GROUPS = SELF_WIDTH // LANES
QK_SCALE = HEAD_DIM ** -0.5
LOG2E = math.log2(math.e)
KV_CHUNK = 256
SUM_ROWS = 16

F32 = jnp.float32
BF16 = jnp.bfloat16

V7X_VMEM_BYTES = 64 * 1024 * 1024
VMEM_LIMIT_BYTES = V7X_VMEM_BYTES * 7 // 8

PROJ_ROWS = 1024
MLP_ROWS = 512
MLP_FF_CHUNK = 512
ATTN_GROUPS = N_SELF_GROUPS
ATTN_TILES_PER_STEP = 4


def _params(*sem):
    return pltpu.CompilerParams(dimension_semantics=sem, vmem_limit_bytes=VMEM_LIMIT_BYTES)


def _proj_kernel(x_ref, g_ref, w_ref, cos_ref, sin_ref, o_ref, *extra_refs,
                 rope, scales, col_chunk, kmean_groups, vt_start, gain_starts):
    tm = x_ref.shape[0]
    n_out = w_ref.shape[1]
    extra = list(extra_refs)
    km_ref = extra.pop(0) if kmean_groups else None
    vt_ref = extra.pop(0) if vt_start is not None else None
    x = x_ref[...]
    ms = jnp.mean(x * x, axis=-1, keepdims=True)
    xn = x * lax.rsqrt(ms + EPS)
    hs = [(xn * g_ref[i:i + 1, :]).astype(BF16) for i in range(len(gain_starts))]
    lane = lax.broadcasted_iota(jnp.int32, (tm, LANES), 1)
    first_half = (lane & (HEAD_DIM - 1)) < HEAD_DIM // 2
    if rope:
        cos = cos_ref[...]
        sin = sin_ref[...]
    for c0 in range(0, n_out, col_chunk):
        h = hs[max(i for i, start in enumerate(gain_starts) if start <= c0)]
        z = jnp.dot(h, w_ref[:, c0:c0 + col_chunk], preferred_element_type=F32)
        for jj in range(col_chunk // LANES):
            j = c0 // LANES + jj
            blk = z[:, jj * LANES:(jj + 1) * LANES]
            if j in rope:
                swap = jnp.where(first_half, pltpu.roll(blk, LANES - 32, 1), pltpu.roll(blk, 32, 1))
                blk = blk * cos + swap * sin
                if j < kmean_groups:
                    km_ref[0, :, j * LANES:(j + 1) * LANES] = jnp.mean(
                        blk.reshape(tm // MOBA_BLOCK, MOBA_BLOCK, LANES), axis=1)
            if j in scales:
                blk = blk * scales[j]
            o_ref[:, j * LANES:(j + 1) * LANES] = blk.astype(o_ref.dtype)
            if vt_ref is not None and vt_start <= j < vt_start + vt_ref.shape[1] // LANES:
                g = j - vt_start
                for c in range(tm // KV_CHUNK):
                    vt_ref[c, g * LANES:(g + 1) * LANES, :] = (
                        blk[c * KV_CHUNK:(c + 1) * KV_CHUNK, :].T.astype(BF16))


def _proj(x2d, gains, w, cos_t, sin_t, *, rope, scales, kmean_groups, vt_start, seq, tm, name,
          vt_width=SELF_WIDTH, gain_starts=(0,)):
    t, d = x2d.shape
    n_out = w.shape[1]
    n_pos_blocks = seq // tm
    col_chunk = 512 if n_out % 512 == 0 else 256
    assert all(start % col_chunk == 0 for start in gain_starts)
    out_shape = [jax.ShapeDtypeStruct((t, n_out), BF16)]
    out_specs = [pl.BlockSpec((tm, n_out), lambda i: (i, 0))]
    if kmean_groups:
        out_shape.append(jax.ShapeDtypeStruct((t // tm, tm // MOBA_BLOCK, kmean_groups * LANES), F32))
        out_specs.append(pl.BlockSpec((1, tm // MOBA_BLOCK, kmean_groups * LANES), lambda i: (i, 0, 0)))
    if vt_start is not None:
        out_shape.append(jax.ShapeDtypeStruct((t // KV_CHUNK, vt_width, KV_CHUNK), BF16))
        out_specs.append(pl.BlockSpec((tm // KV_CHUNK, vt_width, KV_CHUNK), lambda i: (i, 0, 0)))
    kern = functools.partial(_proj_kernel, rope=frozenset(rope), scales=dict(scales),
                             col_chunk=col_chunk, kmean_groups=kmean_groups, vt_start=vt_start,
                             gain_starts=tuple(gain_starts))
    n_gains = gains.shape[0]
    return pl.pallas_call(
        kern,
        out_shape=out_shape,
        grid=(t // tm,),
        in_specs=[
            pl.BlockSpec((tm, d), lambda i: (i, 0)),
            pl.BlockSpec((n_gains, d), lambda i: (0, 0)),
            pl.BlockSpec((d, n_out), lambda i: (0, 0)),
            pl.BlockSpec((tm, LANES), lambda i: (i % n_pos_blocks, 0)),
            pl.BlockSpec((tm, LANES), lambda i: (i % n_pos_blocks, 0)),
        ],
        out_specs=out_specs,
        compiler_params=_params("parallel"),
        name=name,
    )(x2d, gains, w, cos_t, sin_t)


_NT = (((1,), (1,)), ((), ()))


def _split_maps(q):
    lane = lax.broadcasted_iota(jnp.int32, q.shape, 1)
    zero = jnp.zeros_like(q)
    return [jnp.where(lane < HEAD_DIM, q, zero), jnp.where(lane >= HEAD_DIM, q, zero)]


def _causal_mask(st):
    key = lax.broadcasted_iota(jnp.int32, st.shape, 0)
    qry = lax.broadcasted_iota(jnp.int32, st.shape, 1)
    return jnp.where(key <= qry, st, NEG)


def _flash_pipeline(n_groups, n_past, tq, scores_fn, vt_fn, bias_fn, finalize_fn, s_scr, acc_scr):
    streams = [(g, h) for g in range(n_groups) for h in range(2)]

    def produce(j, buf, only=None):
        col_max = []
        for i, (g, h) in enumerate(streams):
            if only is not None and i != only:
                continue
            st = scores_fn(g, h, j)
            s_scr[buf, g, h] = st
            col_max.append(jnp.max(st, axis=0, keepdims=True))
        return tuple(col_max)

    ones_rows = jnp.ones((SUM_ROWS, tq), BF16)

    def consume(j, buf, stats, col_max, diagonal, only=None):
        out, accs = [], []
        for i, (g, h) in enumerate(streams):
            if only is not None and i != only:
                continue
            st = s_scr[buf, g, h]
            if diagonal:
                st = _causal_mask(st)
                m_cur = jnp.max(st, axis=0, keepdims=True)
            else:
                m_cur = col_max[i]
            if bias_fn is not None:
                bias = bias_fn(g, h, j)
                m_cur = m_cur + bias
            m_next = jnp.maximum(stats[i], m_cur)
            alpha = jnp.exp2(stats[i] - m_next)
            shift = m_next if bias_fn is None else m_next - bias
            p = jnp.exp2((st - shift).astype(BF16))
            acc = alpha * acc_scr[g, h] + jnp.dot(
                jnp.concatenate([vt_fn(g, h, j), ones_rows], axis=0), p, preferred_element_type=F32)
            if diagonal:
                accs.append(acc)
                if h == 1:
                    finalize_fn(g, accs[-2:])
            else:
                acc_scr[g, h] = acc
            out.append(m_next)
        return tuple(out)

    def step(j, buf, carry):
        stats, nxt = [], []
        for i in range(len(streams)):
            nxt += produce(j + 1, 1 - buf, only=i)
            stats += consume(j, buf, carry[0], carry[1], False, only=i)
        return tuple(stats), tuple(nxt)

    def pair(i, carry):
        return step(2 * i + 1, 1, step(2 * i, 0, carry))

    def quad(i, carry):
        return pair(2 * i + 1, pair(2 * i, carry))

    def octet(i, carry):
        return quad(2 * i + 1, quad(2 * i, carry))

    acc_scr[...] = jnp.zeros(acc_scr.shape, F32)
    init = tuple(jnp.full((1, tq), NEG, F32) for _ in streams)
    n_octets = n_past // 8
    carry = lax.fori_loop(0, n_octets, octet, (init, produce(0, 0)))
    carry = lax.fori_loop(2 * n_octets, n_past // 4, quad, carry)
    carry = lax.fori_loop(2 * (n_past // 4), n_past // 2, pair, carry)

    def odd_tail(carry):
        stats, col_max = step(n_past - 1, 0, carry)
        return consume(n_past, 1, stats, col_max, True)

    def even_tail(carry):
        return consume(n_past, 0, carry[0], carry[1], True)

    return lax.cond(n_past % 2 == 1, odd_tail, even_tail, carry)


def _diff_attn_kernel(lam_ref, g_ref, q_ref, k_ref, vt_ref, o_ref, s_scr, acc_scr, *, lam_init):
    tq = KV_CHUNK
    n_groups = q_ref.shape[1] // LANES
    cols = [slice(g * LANES, (g + 1) * LANES) for g in range(n_groups)]
    lp = lam_ref[...]
    lam = (jnp.exp(jnp.sum(lp[0:1] * lp[1:2], axis=1, keepdims=True))
           - jnp.exp(jnp.sum(lp[2:3] * lp[3:4], axis=1, keepdims=True)) + lam_init)

    tiles = q_ref.shape[0] // tq
    first_tile = pl.program_id(2) * tiles

    def query_tile(local, carry):
        qi = first_tile + local
        rows = pl.ds(pl.multiple_of(local * tq, tq), tq)
        qs = [_split_maps(q_ref[rows, c]) for c in cols]

        def scores(g, h, j):
            start = pl.multiple_of(j * tq, tq)
            return lax.dot_general(k_ref[pl.ds(start, tq), cols[g]], qs[g][h], _NT,
                                   preferred_element_type=F32)

        def finalize(g, accs):
            o1, o2 = [a[:LANES] * (1.0 / a[LANES:LANES + 1]) for a in accs]
            od_t = o1 - lam * o2
            ms = jnp.mean(od_t * od_t, axis=0, keepdims=True)
            y = (od_t * lax.rsqrt(ms + EPS)).T * g_ref[...]
            o_ref[rows, cols[g]] = (y * (1.0 - lam_init)).astype(o_ref.dtype)

        _flash_pipeline(n_groups, qi, tq, scores, lambda g, h, j: vt_ref[j, cols[g], :], None,
                        finalize, s_scr, acc_scr)
        return carry

    lax.fori_loop(0, tiles, query_tile, 0)


def _diff_attention(z, vt, lam_p, subln, *, batch, seq, lam_init):
    t = z.shape[0]
    tq = KV_CHUNK
    n_groups = ATTN_GROUPS
    w = n_groups * LANES
    n_steps = N_SELF_GROUPS // n_groups
    rows = ATTN_TILES_PER_STEP * tq
    per_batch = seq // rows
    kern = functools.partial(_diff_attn_kernel, lam_init=lam_init)
    resident = dict(pipeline_mode=pl.Buffered(1))
    return pl.pallas_call(
        kern,
        out_shape=jax.ShapeDtypeStruct((t, SELF_WIDTH), BF16),
        grid=(batch, n_steps, per_batch),
        in_specs=[
            pl.BlockSpec((4, HEAD_DIM), lambda b, h, i: (0, 0)),
            pl.BlockSpec((1, LANES), lambda b, h, i: (0, 0)),
            pl.BlockSpec((rows, w), lambda b, h, i: (b * per_batch + i, h)),
            pl.BlockSpec((seq, w), lambda b, h, i: (b, n_steps + h)),
            pl.BlockSpec((seq // KV_CHUNK, w, KV_CHUNK), lambda b, h, i: (b, h, 0), **resident),
        ],
        out_specs=pl.BlockSpec((rows, w), lambda b, h, i: (b * per_batch + i, h)),
        scratch_shapes=[pltpu.VMEM((2, n_groups, 2, tq, tq), F32),
                        pltpu.VMEM((n_groups, 2, LANES + SUM_ROWS, tq), F32)],
        compiler_params=_params("parallel", "parallel", "parallel"),
        name="diff_attn",
    )(lam_p, subln.reshape(1, LANES), z, z, vt)


def _moba_select_bias(qs, km, qi):
    n_blk = km.shape[0]
    parts, rest = [], km
    for _ in range(3):
        part = rest.astype(BF16)
        parts.append(part)
        rest = rest - part.astype(F32)
    terms = lax.dot_general(jnp.concatenate(parts, axis=0), qs, _NT, preferred_element_type=F32)
    gate = (terms[2 * n_blk:] + terms[n_blk:2 * n_blk]) + terms[:n_blk]
    blk = lax.broadcasted_iota(jnp.int32, gate.shape, 0)
    past = blk < qi
    gm = jnp.where(past, gate, NEG)
    sel = jnp.zeros(gate.shape, F32)
    for _ in range(MOBA_TOPK):
        mx = jnp.max(gm, axis=0, keepdims=True)
        first = jnp.min(jnp.where(gm == mx, blk, n_blk), axis=0, keepdims=True)
        pick = blk == first
        sel = jnp.where(pick, 1.0, sel)
        gm = jnp.where(pick, -jnp.inf, gm)
    return jnp.where((past & (sel > 0.5)) | (blk == qi), 0.0, 2.0 * NEG)


def _moba_attn_kernel(q_ref, k_ref, vt_ref, km_ref, o_ref, s_scr, acc_scr, bias_scr):
    tq = MOBA_BLOCK
    n_groups = q_ref.shape[1] // LANES
    cols = [slice(g * LANES, (g + 1) * LANES) for g in range(n_groups)]

    tiles = q_ref.shape[0] // tq
    first_tile = pl.program_id(2) * tiles

    def query_tile(local, carry):
        qi = first_tile + local
        rows = pl.ds(pl.multiple_of(local * tq, tq), tq)
        qs = [_split_maps(q_ref[rows, c]) for c in cols]
        for g, c in enumerate(cols):
            for h in range(2):
                bias_scr[g, h] = _moba_select_bias(qs[g][h], km_ref[:, c], qi)

        def scores(g, h, j):
            start = pl.multiple_of(j * tq, tq)
            return lax.dot_general(k_ref[pl.ds(start, tq), cols[g]], qs[g][h], _NT,
                                   preferred_element_type=F32)

        def values_t(g, h, j):
            return vt_ref[j, pl.ds(g * LANES + h * HEAD_DIM, HEAD_DIM), :]

        def finalize(g, accs):
            heads = [a[:HEAD_DIM] * (1.0 / a[HEAD_DIM:HEAD_DIM + 1]) for a in accs]
            o_ref[rows, cols[g]] = jnp.concatenate(heads, axis=0).T.astype(o_ref.dtype)

        _flash_pipeline(n_groups, qi, tq, scores, values_t,
                        lambda g, h, j: bias_scr[g, h, pl.ds(j, 1), :], finalize, s_scr, acc_scr)
        return carry

    lax.fori_loop(0, tiles, query_tile, 0)


def _moba_attention(z, vt, kmean, *, batch, seq):
    t = z.shape[0]
    tq = MOBA_BLOCK
    nq = seq // tq
    n_groups = ATTN_GROUPS
    w = n_groups * LANES
    n_steps = N_SELF_GROUPS // n_groups
    q_step = 2 * n_steps
    rows = ATTN_TILES_PER_STEP * tq
    per_batch = seq // rows
    resident = dict(pipeline_mode=pl.Buffered(1))
    return pl.pallas_call(
        _moba_attn_kernel,
        out_shape=jax.ShapeDtypeStruct((t, SELF_WIDTH), BF16),
        grid=(batch, n_steps, per_batch),
        in_specs=[
            pl.BlockSpec((rows, w), lambda b, h, i: (b * per_batch + i, q_step + h)),
            pl.BlockSpec((seq, w), lambda b, h, i: (b, h)),
            pl.BlockSpec((seq // KV_CHUNK, w, KV_CHUNK), lambda b, h, i: (b, h, 0), **resident),
            pl.BlockSpec((None, nq, w), lambda b, h, i: (b, 0, h)),
        ],
        out_specs=pl.BlockSpec((rows, w), lambda b, h, i: (b * per_batch + i, h)),
        scratch_shapes=[pltpu.VMEM((2, n_groups, 2, tq, tq), F32),
                        pltpu.VMEM((n_groups, 2, HEAD_DIM + SUM_ROWS, tq), F32),
                        pltpu.VMEM((n_groups, 2, nq, tq), F32)],
        compiler_params=_params("parallel", "parallel", "parallel"),
        name="moba_attn",
    )(z, z, vt, kmean)


def _attn_out_kernel(x_ref, os_ref, qm_ref, mk_ref, mvt_ref, w_ref, o_ref):
    mem_len = mk_ref.shape[0]
    mk = mk_ref[...]
    lane = lax.broadcasted_iota(jnp.int32, mk.shape, 1)
    zero = jnp.zeros_like(mk)
    mk_heads = jnp.concatenate(
        [jnp.where((lane >= h * HEAD_DIM) & (lane < (h + 1) * HEAD_DIM), mk, zero)
         for h in range(N_MEM_HEADS)], axis=0)
    s = lax.dot_general(mk_heads, qm_ref[...], _NT, preferred_element_type=F32)
    ones_rows = jnp.ones((SUM_ROWS, mem_len), BF16)
    heads = []
    for h in range(N_MEM_HEADS):
        sh = s[h * mem_len:(h + 1) * mem_len]
        p = jnp.exp2((sh - jnp.max(sh, axis=0, keepdims=True)).astype(BF16))
        vt_ones = jnp.concatenate([mvt_ref[h * HEAD_DIM:(h + 1) * HEAD_DIM, :], ones_rows], axis=0)
        oh = jnp.dot(vt_ones, p, preferred_element_type=F32)
        heads.append(oh[:HEAD_DIM] * (1.0 / oh[HEAD_DIM:HEAD_DIM + 1]))
    o_mem = jnp.concatenate(heads, axis=0).T.astype(BF16)
    y = jnp.dot(os_ref[...], w_ref[:SELF_WIDTH, :], preferred_element_type=F32)
    y = y + jnp.dot(o_mem, w_ref[SELF_WIDTH:, :], preferred_element_type=F32)
    o_ref[...] = x_ref[...] + y


def _attn_out(x2d, o_self, zq, qm_block, memkv, memvt, w_out, *, seq, mem_len, name):
    t, d = x2d.shape
    tm = PROJ_ROWS
    per_batch = seq // tm
    return pl.pallas_call(
        _attn_out_kernel,
        out_shape=jax.ShapeDtypeStruct((t, d), F32),
        grid=(t // tm,),
        in_specs=[
            pl.BlockSpec((tm, d), lambda i: (i, 0)),
            pl.BlockSpec((tm, SELF_WIDTH), lambda i: (i, 0)),
            pl.BlockSpec((tm, MEM_WIDTH), lambda i: (i, qm_block)),
            pl.BlockSpec((mem_len, MEM_WIDTH), lambda i: (i // per_batch, 0)),
            pl.BlockSpec((None, MEM_WIDTH, mem_len), lambda i: (i // per_batch, 0, 0)),
            pl.BlockSpec((d, d), lambda i: (0, 0)),
        ],
        out_specs=pl.BlockSpec((tm, d), lambda i: (i, 0)),
        compiler_params=_params("parallel"),
        name=name,
    )(x2d, o_self, zq, memkv, memvt, w_out)


def _mlp_kernel(x_ref, g_ref, wu_ref, wd_ref, gf_ref, o_ref, u_scr, *, final_norm, tf):
    x = x_ref[...]
    ms = jnp.mean(x * x, axis=-1, keepdims=True)
    h = (x * lax.rsqrt(ms + EPS) * g_ref[...]).astype(BF16)
    for c0 in range(0, wu_ref.shape[1], tf):
        u = jnp.maximum(jnp.dot(h, wu_ref[:, c0:c0 + tf], preferred_element_type=F32), 0.0)
        u_scr[:, c0:c0 + tf] = (u * u).astype(BF16)
    y = x + jnp.dot(u_scr[...], wd_ref[...], preferred_element_type=F32)
    if final_norm:
        ms = jnp.mean(y * y, axis=-1, keepdims=True)
        y = y * lax.rsqrt(ms + EPS) * gf_ref[...]
    o_ref[...] = y


def _mlp(x2d, g, w_up, w_down, g_final, *, final_norm, name):
    t, d = x2d.shape
    dff = w_up.shape[1]
    tm = MLP_ROWS
    kern = functools.partial(_mlp_kernel, final_norm=final_norm, tf=MLP_FF_CHUNK)
    resident = dict(pipeline_mode=pl.Buffered(1))
    return pl.pallas_call(
        kern,
        out_shape=jax.ShapeDtypeStruct((t, d), F32),
        grid=(t // tm,),
        in_specs=[
            pl.BlockSpec((tm, d), lambda i: (i, 0)),
            pl.BlockSpec((1, d), lambda i: (0, 0)),
            pl.BlockSpec((d, dff), lambda i: (0, 0), **resident),
            pl.BlockSpec((dff, d), lambda i: (0, 0), **resident),
            pl.BlockSpec((1, d), lambda i: (0, 0)),
        ],
        out_specs=pl.BlockSpec((tm, d), lambda i: (i, 0)),
        scratch_shapes=[pltpu.VMEM((tm, dff), BF16)],
        compiler_params=_params("parallel"),
        name=name,
    )(x2d, g.reshape(1, d), w_up, w_down, g_final.reshape(1, d))


def _rope_tables(seq):
    half = HEAD_DIM // 2
    inv = 1.0 / (ROPE_THETA ** (jnp.arange(half, dtype=F32) / half))
    ang = jnp.arange(seq, dtype=F32)[:, None] * inv[None, :]
    cos, sin = jnp.cos(ang), jnp.sin(ang)
    reps = LANES // half
    cos_t = jnp.tile(cos, (1, reps))
    sign = jnp.tile(jnp.concatenate([-jnp.ones((half,), F32), jnp.ones((half,), F32)]), LANES // HEAD_DIM)
    sin_t = jnp.tile(sin, (1, reps)) * sign[None, :]
    return cos_t, sin_t


def kernel(x, mem, a_norm_attn, a_w_in, a_lambda, a_subln, a_mem_norm, a_w_mem_kv, a_w_out, a_norm_mlp, a_w_up, a_w_down, kv_norm, w_kv, b_norm_attn, b_w_in, b_mem_norm, b_w_mem_kv, b_w_out, b_norm_mlp, b_w_up, b_w_down, final_norm):
    batch, seq, d = x.shape
    mem_len = mem.shape[1]
    t = batch * seq
    x2d = x.reshape(t, d)
    mem2d = mem.reshape(batch * mem_len, d)
    cos_t, sin_t = _rope_tables(seq)
    bf = lambda w: w.astype(BF16)
    ng = N_SELF_GROUPS
    q_scale = QK_SCALE * LOG2E
    mem_groups = MEM_WIDTH // LANES

    qm_block = 3 * SELF_WIDTH // MEM_WIDTH

    def memkv(g, w, name):
        return _proj(mem2d, g, bf(w), cos_t, sin_t, rope=(), scales={}, kmean_groups=0,
                     vt_start=mem_groups, vt_width=MEM_WIDTH, seq=mem_len, tm=mem_len, name=name)

    lam_init = 0.8 - 0.6 * math.exp(-0.3 * 0)
    za, vt_a = _proj(x2d, a_norm_attn, bf(a_w_in[0]), cos_t, sin_t, rope=range(2 * ng),
                     scales={j: q_scale for j in (*range(ng), 3 * ng, 3 * ng + 1)},
                     kmean_groups=0, vt_start=2 * ng, seq=seq, tm=PROJ_ROWS, name="a_proj")
    mkv_a, mvt_a = memkv(a_mem_norm, a_w_mem_kv[0], "a_memkv")
    o_self = _diff_attention(za, vt_a, a_lambda[0], a_subln[0], batch=batch, seq=seq,
                             lam_init=lam_init)
    x2d = _attn_out(x2d, o_self, za, qm_block, mkv_a, mvt_a, bf(a_w_out[0]),
                    seq=seq, mem_len=mem_len, name="a_attn_out")
    x2d = _mlp(x2d, a_norm_mlp[0], bf(a_w_up[0]), bf(a_w_down[0]), final_norm,
               final_norm=False, name="a_mlp")

    zb, kmean, vt_b = _proj(x2d, jnp.stack([kv_norm, b_norm_attn[0]]),
                            bf(jnp.concatenate([w_kv, b_w_in[0]], axis=1)), cos_t, sin_t,
                            rope=(*range(ng), *range(2 * ng, 3 * ng)),
                            scales={j: q_scale for j in range(2 * ng, 3 * ng + mem_groups)},
                            kmean_groups=ng, vt_start=ng, gain_starts=(0, 2 * SELF_WIDTH),
                            seq=seq, tm=PROJ_ROWS, name="b_proj")
    kmean = kmean.reshape(batch, seq // MOBA_BLOCK, SELF_WIDTH)
    mkv_b, mvt_b = memkv(b_mem_norm, b_w_mem_kv[0], "b_memkv")
    o_self = _moba_attention(zb, vt_b, kmean, batch=batch, seq=seq)
    x2d = _attn_out(x2d, o_self, zb, qm_block, mkv_b, mvt_b, bf(b_w_out[0]),
                    seq=seq, mem_len=mem_len, name="b_attn_out")
    x2d = _mlp(x2d, b_norm_mlp[0], bf(b_w_up[0]), bf(b_w_down[0]), final_norm,
               final_norm=True, name="b_mlp")
    return x2d.reshape(batch, seq, d)
```

```python
import functools
import math

import jax
import jax.numpy as jnp
from jax import lax
from jax.experimental import pallas as pl
from jax.experimental.pallas import tpu as pltpu

HEAD_DIM = 64
SELF_WIDTH = 768
MEM_WIDTH = 256
N_MEM_HEADS = 4
MOBA_BLOCK = 256
MOBA_TOPK = 3
ROPE_THETA = 10000.0
EPS = 1e-6
NEG = -1e30

LANES = 128
N_SELF_GROUPS = SELF_WIDTH // LANES
QK_SCALE = HEAD_DIM ** -0.5
LOG2E = math.log2(math.e)
KV_CHUNK = 256
SUM_ROWS = 16

F32 = jnp.float32
BF16 = jnp.bfloat16

V7X_VMEM_BYTES = 64 * 1024 * 1024
VMEM_LIMIT_BYTES = V7X_VMEM_BYTES * 7 // 8

PROJ_ROWS = 1024
OUT_ROWS = 2048
MLP_ROWS = 1024
MLP_FF_CHUNK = 512
ATTN_GROUPS = N_SELF_GROUPS
ATTN_TILES_PER_STEP = 4


def _params(*sem):
    return pltpu.CompilerParams(dimension_semantics=sem, vmem_limit_bytes=VMEM_LIMIT_BYTES)


def _proj_kernel(x_ref, g_ref, w_ref, cos_ref, sin_ref, o_ref, *extra_refs,
                 rope, scales, col_chunk, kmean_groups, vt_start, gain_starts):
    tm = x_ref.shape[0]
    n_out = w_ref.shape[1]
    extra = list(extra_refs)
    km_ref = extra.pop(0) if kmean_groups else None
    vt_ref = extra.pop(0) if vt_start is not None else None
    x = x_ref[...]
    ms = jnp.mean(x * x, axis=-1, keepdims=True)
    xn = x * lax.rsqrt(ms + EPS)
    hs = [(xn * g_ref[i:i + 1, :]).astype(BF16) for i in range(len(gain_starts))]
    lane = lax.broadcasted_iota(jnp.int32, (tm, LANES), 1)
    first_half = (lane & (HEAD_DIM - 1)) < HEAD_DIM // 2
    if rope:
        cos = cos_ref[...]
        sin = sin_ref[...]
    for c0 in range(0, n_out, col_chunk):
        h = hs[max(i for i, start in enumerate(gain_starts) if start <= c0)]
        z = jnp.dot(h, w_ref[:, c0:c0 + col_chunk], preferred_element_type=F32)
        for jj in range(col_chunk // LANES):
            j = c0 // LANES + jj
            blk = z[:, jj * LANES:(jj + 1) * LANES]
            if j in rope:
                swap = jnp.where(first_half, pltpu.roll(blk, LANES - 32, 1), pltpu.roll(blk, 32, 1))
                blk = blk * cos + swap * sin
                if j < kmean_groups:
                    km_ref[0, :, j * LANES:(j + 1) * LANES] = jnp.mean(
                        blk.reshape(tm // MOBA_BLOCK, MOBA_BLOCK, LANES), axis=1)
            if j in scales:
                blk = blk * scales[j]
            o_ref[:, j * LANES:(j + 1) * LANES] = blk.astype(o_ref.dtype)
            if vt_ref is not None and vt_start <= j < vt_start + vt_ref.shape[1] // LANES:
                g = j - vt_start
                for c in range(tm // KV_CHUNK):
                    vt_ref[c, g * LANES:(g + 1) * LANES, :] = (
                        blk[c * KV_CHUNK:(c + 1) * KV_CHUNK, :].T.astype(BF16))


def _proj(x2d, gains, w, cos_t, sin_t, *, rope, scales, kmean_groups, vt_start, seq, tm, name,
          vt_width=SELF_WIDTH, gain_starts=(0,)):
    t, d = x2d.shape
    n_out = w.shape[1]
    n_pos_blocks = seq // tm
    col_chunk = 512 if n_out % 512 == 0 else 256
    assert all(start % col_chunk == 0 for start in gain_starts)
    out_shape = [jax.ShapeDtypeStruct((t, n_out), BF16)]
    out_specs = [pl.BlockSpec((tm, n_out), lambda i: (i, 0))]
    if kmean_groups:
        out_shape.append(jax.ShapeDtypeStruct((t // tm, tm // MOBA_BLOCK, kmean_groups * LANES), F32))
        out_specs.append(pl.BlockSpec((1, tm // MOBA_BLOCK, kmean_groups * LANES), lambda i: (i, 0, 0)))
    if vt_start is not None:
        out_shape.append(jax.ShapeDtypeStruct((t // KV_CHUNK, vt_width, KV_CHUNK), BF16))
        out_specs.append(pl.BlockSpec((tm // KV_CHUNK, vt_width, KV_CHUNK), lambda i: (i, 0, 0)))
    kern = functools.partial(_proj_kernel, rope=frozenset(rope), scales=dict(scales),
                             col_chunk=col_chunk, kmean_groups=kmean_groups, vt_start=vt_start,
                             gain_starts=tuple(gain_starts))
    n_gains = gains.shape[0]
    return pl.pallas_call(
        kern,
        out_shape=out_shape,
        grid=(t // tm,),
        in_specs=[
            pl.BlockSpec((tm, d), lambda i: (i, 0)),
            pl.BlockSpec((n_gains, d), lambda i: (0, 0)),
            pl.BlockSpec((d, n_out), lambda i: (0, 0)),
            pl.BlockSpec((tm, LANES), lambda i: (i % n_pos_blocks, 0)),
            pl.BlockSpec((tm, LANES), lambda i: (i % n_pos_blocks, 0)),
        ],
        out_specs=out_specs,
        compiler_params=_params("parallel"),
        name=name,
    )(x2d, gains, w, cos_t, sin_t)


_NT = (((1,), (1,)), ((), ()))


def _split_maps(q):
    lane = lax.broadcasted_iota(jnp.int32, q.shape, 1)
    zero = jnp.zeros_like(q)
    return [jnp.where(lane < HEAD_DIM, q, zero), jnp.where(lane >= HEAD_DIM, q, zero)]


def _causal_mask(st):
    key = lax.broadcasted_iota(jnp.int32, st.shape, 0)
    qry = lax.broadcasted_iota(jnp.int32, st.shape, 1)
    return jnp.where(key <= qry, st, NEG)


def _flash_pipeline(n_groups, n_past, tq, scores_fn, vt_fn, bias_fn, finalize_fn, s_scr, acc_scr):
    streams = [(g, h) for g in range(n_groups) for h in range(2)]

    def produce(j, buf, only=None):
        col_max = []
        for i, (g, h) in enumerate(streams):
            if only is not None and i != only:
                continue
            st = scores_fn(g, h, j)
            s_scr[buf, g, h] = st
            col_max.append(jnp.max(st, axis=0, keepdims=True))
        return tuple(col_max)

    ones_rows = jnp.ones((SUM_ROWS, tq), BF16)

    def consume(j, buf, stats, col_max, diagonal, only=None):
        out, accs = [], []
        for i, (g, h) in enumerate(streams):
            if only is not None and i != only:
                continue
            st = s_scr[buf, g, h]
            if diagonal:
                st = _causal_mask(st)
                m_cur = jnp.max(st, axis=0, keepdims=True)
            else:
                m_cur = col_max[i]
            if bias_fn is not None:
                bias = bias_fn(g, h, j)
                m_cur = m_cur + bias
            m_next = jnp.maximum(stats[i], m_cur)
            alpha = jnp.exp2(stats[i] - m_next)
            shift = m_next if bias_fn is None else m_next - bias
            p = jnp.exp2((st - shift).astype(BF16))
            acc = alpha * acc_scr[g, h] + jnp.dot(
                jnp.concatenate([vt_fn(g, h, j), ones_rows], axis=0), p, preferred_element_type=F32)
            if diagonal:
                accs.append(acc)
                if h == 1:
                    finalize_fn(g, accs[-2:])
            else:
                acc_scr[g, h] = acc
            out.append(m_next)
        return tuple(out)

    def step(j, buf, carry):
        stats, nxt = [], []
        for i in range(len(streams)):
            nxt += produce(j + 1, 1 - buf, only=i)
            stats += consume(j, buf, carry[0], carry[1], False, only=i)
        return tuple(stats), tuple(nxt)

    def pair(i, carry):
        return step(2 * i + 1, 1, step(2 * i, 0, carry))

    def quad(i, carry):
        return pair(2 * i + 1, pair(2 * i, carry))

    def octet(i, carry):
        return quad(2 * i + 1, quad(2 * i, carry))

    acc_scr[...] = jnp.zeros(acc_scr.shape, F32)
    init = tuple(jnp.full((1, tq), NEG, F32) for _ in streams)
    n_octets = n_past // 8
    carry = lax.fori_loop(0, n_octets, octet, (init, produce(0, 0)))
    carry = lax.fori_loop(2 * n_octets, n_past // 4, quad, carry)
    carry = lax.fori_loop(2 * (n_past // 4), n_past // 2, pair, carry)

    def odd_tail(carry):
        stats, col_max = step(n_past - 1, 0, carry)
        return consume(n_past, 1, stats, col_max, True)

    def even_tail(carry):
        return consume(n_past, 0, carry[0], carry[1], True)

    return lax.cond(n_past % 2 == 1, odd_tail, even_tail, carry)


def _diff_attn_kernel(lam_ref, g_ref, q_ref, k_ref, vt_ref, o_ref, s_scr, acc_scr, *, lam_init):
    tq = KV_CHUNK
    n_groups = q_ref.shape[1] // LANES
    cols = [slice(g * LANES, (g + 1) * LANES) for g in range(n_groups)]
    lp = lam_ref[...]
    lam = (jnp.exp(jnp.sum(lp[0:1] * lp[1:2], axis=1, keepdims=True))
           - jnp.exp(jnp.sum(lp[2:3] * lp[3:4], axis=1, keepdims=True)) + lam_init)

    tiles = q_ref.shape[0] // tq
    first_tile = pl.program_id(2) * tiles

    def query_tile(local, carry):
        qi = first_tile + local
        rows = pl.ds(pl.multiple_of(local * tq, tq), tq)
        qs = [_split_maps(q_ref[rows, c]) for c in cols]

        def scores(g, h, j):
            start = pl.multiple_of(j * tq, tq)
            return lax.dot_general(k_ref[pl.ds(start, tq), cols[g]], qs[g][h], _NT,
                                   preferred_element_type=F32)

        def finalize(g, accs):
            o1, o2 = [a[:LANES] * (1.0 / a[LANES:LANES + 1]) for a in accs]
            od_t = o1 - lam * o2
            ms = jnp.mean(od_t * od_t, axis=0, keepdims=True)
            y = (od_t * lax.rsqrt(ms + EPS)).T * g_ref[...]
            o_ref[rows, cols[g]] = (y * (1.0 - lam_init)).astype(o_ref.dtype)

        _flash_pipeline(n_groups, qi, tq, scores, lambda g, h, j: vt_ref[j, cols[g], :], None,
                        finalize, s_scr, acc_scr)
        return carry

    lax.fori_loop(0, tiles, query_tile, 0)


def _diff_attention(z, vt, lam_p, subln, *, batch, seq, lam_init):
    t = z.shape[0]
    tq = KV_CHUNK
    n_groups = ATTN_GROUPS
    w = n_groups * LANES
    n_steps = N_SELF_GROUPS // n_groups
    rows = ATTN_TILES_PER_STEP * tq
    per_batch = seq // rows
    kern = functools.partial(_diff_attn_kernel, lam_init=lam_init)
    resident = dict(pipeline_mode=pl.Buffered(1))
    return pl.pallas_call(
        kern,
        out_shape=jax.ShapeDtypeStruct((t, SELF_WIDTH), BF16),
        grid=(batch, n_steps, per_batch),
        in_specs=[
            pl.BlockSpec((4, HEAD_DIM), lambda b, h, i: (0, 0)),
            pl.BlockSpec((1, LANES), lambda b, h, i: (0, 0)),
            pl.BlockSpec((rows, w), lambda b, h, i: (b * per_batch + i, h)),
            pl.BlockSpec((seq, w), lambda b, h, i: (b, n_steps + h)),
            pl.BlockSpec((seq // KV_CHUNK, w, KV_CHUNK), lambda b, h, i: (b, h, 0), **resident),
        ],
        out_specs=pl.BlockSpec((rows, w), lambda b, h, i: (b * per_batch + i, h)),
        scratch_shapes=[pltpu.VMEM((2, n_groups, 2, tq, tq), F32),
                        pltpu.VMEM((n_groups, 2, LANES + SUM_ROWS, tq), F32)],
        compiler_params=_params("parallel", "parallel", "parallel"),
        name="diff_attn",
    )(lam_p, subln.reshape(1, LANES), z, z, vt)


def _moba_select_bias(qs, km, qi):
    n_blk = km.shape[0]
    parts, rest = [], km
    for _ in range(3):
        part = rest.astype(BF16)
        parts.append(part)
        rest = rest - part.astype(F32)
    terms = lax.dot_general(jnp.concatenate(parts, axis=0), qs, _NT, preferred_element_type=F32)
    gate = (terms[2 * n_blk:] + terms[n_blk:2 * n_blk]) + terms[:n_blk]
    blk = lax.broadcasted_iota(jnp.int32, gate.shape, 0)
    past = blk < qi
    gm = jnp.where(past, gate, NEG)
    sel = jnp.zeros(gate.shape, F32)
    for _ in range(MOBA_TOPK):
        mx = jnp.max(gm, axis=0, keepdims=True)
        first = jnp.min(jnp.where(gm == mx, blk, n_blk), axis=0, keepdims=True)
        pick = blk == first
        sel = jnp.where(pick, 1.0, sel)
        gm = jnp.where(pick, -jnp.inf, gm)
    return jnp.where((past & (sel > 0.5)) | (blk == qi), 0.0, 2.0 * NEG)


def _moba_attn_kernel(q_ref, k_ref, vt_ref, km_ref, o_ref, s_scr, acc_scr, bias_scr):
    tq = MOBA_BLOCK
    n_groups = q_ref.shape[1] // LANES
    cols = [slice(g * LANES, (g + 1) * LANES) for g in range(n_groups)]

    tiles = q_ref.shape[0] // tq
    first_tile = pl.program_id(2) * tiles

    def query_tile(local, carry):
        qi = first_tile + local
        rows = pl.ds(pl.multiple_of(local * tq, tq), tq)
        qs = [_split_maps(q_ref[rows, c]) for c in cols]
        for g, c in enumerate(cols):
            for h in range(2):
                bias_scr[g, h] = _moba_select_bias(qs[g][h], km_ref[:, c], qi)

        def scores(g, h, j):
            start = pl.multiple_of(j * tq, tq)
            return lax.dot_general(k_ref[pl.ds(start, tq), cols[g]], qs[g][h], _NT,
                                   preferred_element_type=F32)

        def values_t(g, h, j):
            return vt_ref[j, pl.ds(g * LANES + h * HEAD_DIM, HEAD_DIM), :]

        def finalize(g, accs):
            heads = [a[:HEAD_DIM] * (1.0 / a[HEAD_DIM:HEAD_DIM + 1]) for a in accs]
            o_ref[rows, cols[g]] = jnp.concatenate(heads, axis=0).T.astype(o_ref.dtype)

        _flash_pipeline(n_groups, qi, tq, scores, values_t,
                        lambda g, h, j: bias_scr[g, h, pl.ds(j, 1), :], finalize, s_scr, acc_scr)
        return carry

    lax.fori_loop(0, tiles, query_tile, 0)


def _moba_attention(z, vt, kmean, *, batch, seq):
    t = z.shape[0]
    tq = MOBA_BLOCK
    nq = seq // tq
    n_groups = ATTN_GROUPS
    w = n_groups * LANES
    n_steps = N_SELF_GROUPS // n_groups
    q_step = 2 * n_steps
    rows = ATTN_TILES_PER_STEP * tq
    per_batch = seq // rows
    resident = dict(pipeline_mode=pl.Buffered(1))
    return pl.pallas_call(
        _moba_attn_kernel,
        out_shape=jax.ShapeDtypeStruct((t, SELF_WIDTH), BF16),
        grid=(batch, n_steps, per_batch),
        in_specs=[
            pl.BlockSpec((rows, w), lambda b, h, i: (b * per_batch + i, q_step + h)),
            pl.BlockSpec((seq, w), lambda b, h, i: (b, h)),
            pl.BlockSpec((seq // KV_CHUNK, w, KV_CHUNK), lambda b, h, i: (b, h, 0), **resident),
            pl.BlockSpec((None, nq, w), lambda b, h, i: (b, 0, h)),
        ],
        out_specs=pl.BlockSpec((rows, w), lambda b, h, i: (b * per_batch + i, h)),
        scratch_shapes=[pltpu.VMEM((2, n_groups, 2, tq, tq), F32),
                        pltpu.VMEM((n_groups, 2, HEAD_DIM + SUM_ROWS, tq), F32),
                        pltpu.VMEM((n_groups, 2, nq, tq), F32)],
        compiler_params=_params("parallel", "parallel", "parallel"),
        name="moba_attn",
    )(z, z, vt, kmean)


def _attn_out_kernel(x_ref, os_ref, qm_ref, mk_ref, mvt_ref, w_ref, o_ref):
    mem_len = mk_ref.shape[0]
    mk = mk_ref[...]
    lane = lax.broadcasted_iota(jnp.int32, mk.shape, 1)
    zero = jnp.zeros_like(mk)
    mk_heads = jnp.concatenate(
        [jnp.where((lane >= h * HEAD_DIM) & (lane < (h + 1) * HEAD_DIM), mk, zero)
         for h in range(N_MEM_HEADS)], axis=0)
    s = lax.dot_general(mk_heads, qm_ref[...], _NT, preferred_element_type=F32)
    ones_rows = jnp.ones((SUM_ROWS, mem_len), BF16)
    heads = []
    for h in range(N_MEM_HEADS):
        sh = s[h * mem_len:(h + 1) * mem_len]
        p = jnp.exp2((sh - jnp.max(sh, axis=0, keepdims=True)).astype(BF16))
        vt_ones = jnp.concatenate([mvt_ref[h * HEAD_DIM:(h + 1) * HEAD_DIM, :], ones_rows], axis=0)
        oh = jnp.dot(vt_ones, p, preferred_element_type=F32)
        heads.append(oh[:HEAD_DIM] * (1.0 / oh[HEAD_DIM:HEAD_DIM + 1]))
    o_mem = jnp.concatenate(heads, axis=0).T.astype(BF16)
    y = jnp.dot(os_ref[...], w_ref[:SELF_WIDTH, :], preferred_element_type=F32)
    y = y + jnp.dot(o_mem, w_ref[SELF_WIDTH:, :], preferred_element_type=F32)
    o_ref[...] = x_ref[...] + y


def _attn_out(x2d, o_self, zq, qm_block, memkv, memvt, w_out, *, seq, mem_len, name):
    t, d = x2d.shape
    tm = OUT_ROWS
    per_batch = seq // tm
    return pl.pallas_call(
        _attn_out_kernel,
        out_shape=jax.ShapeDtypeStruct((t, d), F32),
        grid=(t // tm,),
        in_specs=[
            pl.BlockSpec((tm, d), lambda i: (i, 0)),
            pl.BlockSpec((tm, SELF_WIDTH), lambda i: (i, 0)),
            pl.BlockSpec((tm, MEM_WIDTH), lambda i: (i, qm_block)),
            pl.BlockSpec((mem_len, MEM_WIDTH), lambda i: (i // per_batch, 0)),
            pl.BlockSpec((None, MEM_WIDTH, mem_len), lambda i: (i // per_batch, 0, 0)),
            pl.BlockSpec((d, d), lambda i: (0, 0)),
        ],
        out_specs=pl.BlockSpec((tm, d), lambda i: (i, 0)),
        compiler_params=_params("parallel"),
        name=name,
    )(x2d, o_self, zq, memkv, memvt, w_out)


def _mlp_kernel(x_ref, g_ref, wu_ref, wd_ref, gf_ref, o_ref, u_scr, *, final_norm, tf):
    x = x_ref[...]
    ms = jnp.mean(x * x, axis=-1, keepdims=True)
    h = (x * lax.rsqrt(ms + EPS) * g_ref[...]).astype(BF16)
    for c0 in range(0, wu_ref.shape[1], tf):
        u = jnp.maximum(jnp.dot(h, wu_ref[:, c0:c0 + tf], preferred_element_type=F32), 0.0)
        u_scr[:, c0:c0 + tf] = (u * u).astype(BF16)
    y = x + jnp.dot(u_scr[...], wd_ref[...], preferred_element_type=F32)
    if final_norm:
        ms = jnp.mean(y * y, axis=-1, keepdims=True)
        y = y * lax.rsqrt(ms + EPS) * gf_ref[...]
    o_ref[...] = y


def _mlp(x2d, g, w_up, w_down, g_final, *, final_norm, name):
    t, d = x2d.shape
    dff = w_up.shape[1]
    tm = MLP_ROWS
    kern = functools.partial(_mlp_kernel, final_norm=final_norm, tf=MLP_FF_CHUNK)
    resident = dict(pipeline_mode=pl.Buffered(1))
    return pl.pallas_call(
        kern,
        out_shape=jax.ShapeDtypeStruct((t, d), F32),
        grid=(t // tm,),
        in_specs=[
            pl.BlockSpec((tm, d), lambda i: (i, 0)),
            pl.BlockSpec((1, d), lambda i: (0, 0)),
            pl.BlockSpec((d, dff), lambda i: (0, 0), **resident),
            pl.BlockSpec((dff, d), lambda i: (0, 0), **resident),
            pl.BlockSpec((1, d), lambda i: (0, 0)),
        ],
        out_specs=pl.BlockSpec((tm, d), lambda i: (i, 0)),
        scratch_shapes=[pltpu.VMEM((tm, dff), BF16)],
        compiler_params=_params("parallel"),
        name=name,
    )(x2d, g.reshape(1, d), w_up, w_down, g_final.reshape(1, d))


def _rope_tables(seq):
    half = HEAD_DIM // 2
    inv = 1.0 / (ROPE_THETA ** (jnp.arange(half, dtype=F32) / half))
    ang = jnp.arange(seq, dtype=F32)[:, None] * inv[None, :]
    cos, sin = jnp.cos(ang), jnp.sin(ang)
    reps = LANES // half
    cos_t = jnp.tile(cos, (1, reps))
    sign = jnp.tile(jnp.concatenate([-jnp.ones((half,), F32), jnp.ones((half,), F32)]), LANES // HEAD_DIM)
    sin_t = jnp.tile(sin, (1, reps)) * sign[None, :]
    return cos_t, sin_t


def kernel(x, mem, a_norm_attn, a_w_in, a_lambda, a_subln, a_mem_norm, a_w_mem_kv, a_w_out, a_norm_mlp, a_w_up, a_w_down, kv_norm, w_kv, b_norm_attn, b_w_in, b_mem_norm, b_w_mem_kv, b_w_out, b_norm_mlp, b_w_up, b_w_down, final_norm):
    batch, seq, d = x.shape
    mem_len = mem.shape[1]
    t = batch * seq
    x2d = x.reshape(t, d)
    mem2d = mem.reshape(batch * mem_len, d)
    cos_t, sin_t = _rope_tables(seq)
    bf = lambda w: w.astype(BF16)
    ng = N_SELF_GROUPS
    q_scale = QK_SCALE * LOG2E
    mem_groups = MEM_WIDTH // LANES

    qm_block = 3 * SELF_WIDTH // MEM_WIDTH

    def memkv(g, w, name):
        return _proj(mem2d, g, bf(w), cos_t, sin_t, rope=(), scales={}, kmean_groups=0,
                     vt_start=mem_groups, vt_width=MEM_WIDTH, seq=mem_len, tm=mem_len, name=name)

    lam_init = 0.8 - 0.6 * math.exp(-0.3 * 0)
    za, vt_a = _proj(x2d, a_norm_attn, bf(a_w_in[0]), cos_t, sin_t, rope=range(2 * ng),
                     scales={j: q_scale for j in (*range(ng), 3 * ng, 3 * ng + 1)},
                     kmean_groups=0, vt_start=2 * ng, seq=seq, tm=PROJ_ROWS, name="a_proj")
    mkv_a, mvt_a = memkv(a_mem_norm, a_w_mem_kv[0], "a_memkv")
    o_self = _diff_attention(za, vt_a, a_lambda[0], a_subln[0], batch=batch, seq=seq,
                             lam_init=lam_init)
    x2d = _attn_out(x2d, o_self, za, qm_block, mkv_a, mvt_a, bf(a_w_out[0]),
                    seq=seq, mem_len=mem_len, name="a_attn_out")
    x2d = _mlp(x2d, a_norm_mlp[0], bf(a_w_up[0]), bf(a_w_down[0]), final_norm,
               final_norm=False, name="a_mlp")

    zb, kmean, vt_b = _proj(x2d, jnp.stack([kv_norm, b_norm_attn[0]]),
                            bf(jnp.concatenate([w_kv, b_w_in[0]], axis=1)), cos_t, sin_t,
                            rope=(*range(ng), *range(2 * ng, 3 * ng)),
                            scales={j: q_scale for j in range(2 * ng, 3 * ng + mem_groups)},
                            kmean_groups=ng, vt_start=ng, gain_starts=(0, 2 * SELF_WIDTH),
                            seq=seq, tm=PROJ_ROWS, name="b_proj")
    kmean = kmean.reshape(batch, seq // MOBA_BLOCK, SELF_WIDTH)
    mkv_b, mvt_b = memkv(b_mem_norm, b_w_mem_kv[0], "b_memkv")
    o_self = _moba_attention(zb, vt_b, kmean, batch=batch, seq=seq)
    x2d = _attn_out(x2d, o_self, zb, qm_block, mkv_b, mvt_b, bf(b_w_out[0]),
                    seq=seq, mem_len=mem_len, name="b_attn_out")
    x2d = _mlp(x2d, b_norm_mlp[0], bf(b_w_up[0]), bf(b_w_down[0]), final_norm,
               final_norm=True, name="b_mlp")
    return x2d.reshape(batch, seq, d)
```
